```python
import math
import jax, jax.numpy as jnp
from jax import lax
import numpy as np


D_MODEL = 1024
BATCH = 4
SEQ = 4096
DEPTH = 1
DEC_BATCH = 4
DEC_SEQ = 8192
PAST_LEN = 128

D_MIX = D_MODEL
D_HYENA = D_MIX // 2
D_RET = D_MIX - D_HYENA
HYENA_ORDER = 2
N_RET_HEADS = 4
RET_HEAD_DIM = D_RET // N_RET_HEADS
RET_CHUNK = 128
D_FF = 2816
FILT_EMB = 33
FILT_BANDS = (FILT_EMB - 1) // 2
FILT_HIDDEN = 64
ROPE_BASE = 10000.0
NORM_EPS = 1e-6
HYENA_TARGET = 1e-2
FAST_DECAY_PCT = 0.3
SLOW_DECAY_PCT = 1.5
D_IN = (HYENA_ORDER + 1) * D_HYENA + 4 * D_RET

kernel_name = 'hybrid_hyena_retention_encoder'


def rmsnorm(x, g):
    xf = x.astype(jnp.float32)
    y = xf * lax.rsqrt(jnp.mean(xf * xf, axis=-1, keepdims=True) + NORM_EPS)
    return (y * g.astype(jnp.float32)).astype(x.dtype)


def swiglu(h, w1, w3, w2):
    return (jax.nn.silu(h @ w1) * (h @ w3)) @ w2


def short_conv(u, w, b):
    L = u.shape[1]
    up = jnp.pad(u, ((0, 0), (1, 1), (0, 0)))
    return up[:, :L] * w[0] + up[:, 1:L + 1] * w[1] + up[:, 2:] * w[2] + b


def hyena_filters(L, w1, b1, w2, b2, w3, b3, w4, freq):
    f32 = jnp.float32
    t = jnp.linspace(0.0, 1.0, L, dtype=f32)[:, None]
    w = 2.0 * math.pi * jnp.arange(L, dtype=f32)[:, None] / L
    fb = jnp.linspace(1e-4, FILT_BANDS - 1, FILT_BANDS, dtype=f32)[None, :]
    z = jnp.concatenate([t, jnp.cos(fb * w), -jnp.sin(fb * w)], axis=-1)
    fr = freq.astype(f32)
    h = jnp.sin(fr * (z @ w1.astype(f32) + b1.astype(f32)))
    h = jnp.sin(fr * (h @ w2.astype(f32) + b2.astype(f32)))
    h = jnp.sin(fr * (h @ w3.astype(f32) + b3.astype(f32)))
    h = h @ w4.astype(f32)
    min_decay = math.log(HYENA_TARGET) / SLOW_DECAY_PCT
    max_decay = math.log(HYENA_TARGET) / FAST_DECAY_PCT
    deltas = jnp.linspace(min_decay, max_decay, D_HYENA, dtype=f32)
    window = jnp.exp(-t * jnp.abs(deltas)[None, :])
    h = h.reshape(L, HYENA_ORDER, 2, D_HYENA) * window[:, None, None, :]
    h = h / jnp.sum(jnp.abs(h), axis=0, keepdims=True)
    return jnp.transpose(h, (1, 2, 0, 3))


def bidir_long_conv(z, hf, hb):
    L = z.shape[1]
    k = jnp.concatenate([hf, jnp.zeros((1, hf.shape[1]), hf.dtype), hb[1:][::-1]], axis=0)
    Z = jnp.fft.rfft(z, n=2 * L, axis=1)
    K = jnp.fft.rfft(k, n=2 * L, axis=0)
    return jnp.fft.irfft(Z * K[None], n=2 * L, axis=1)[:, :L]


def hyena_mixer(u, sw, sb, fw1, fb1, fw2, fb2, fw3, fb3, fw4, freq, bias):
    L = u.shape[1]
    u = short_conv(u, sw, sb).astype(jnp.float32)
    v = u[..., :D_HYENA]
    gates = (u[..., D_HYENA:2 * D_HYENA], u[..., 2 * D_HYENA:])
    hs = hyena_filters(L, fw1, fb1, fw2, fb2, fw3, fb3, fw4, freq)
    bias = bias.astype(jnp.float32)
    z = v
    for n in range(HYENA_ORDER):
        z = gates[n] * (bidir_long_conv(z, hs[n, 0], hs[n, 1]) + bias[n] * z)
    return z


def rotary(x):
    L, d = x.shape[2], x.shape[3]
    inv = 1.0 / (ROPE_BASE ** (jnp.arange(0, d, 2, dtype=jnp.float32) / d))
    ang = jnp.arange(L, dtype=jnp.float32)[:, None] * inv[None, :]
    c, s = jnp.cos(ang), jnp.sin(ang)
    x1, x2 = x[..., :d // 2], x[..., d // 2:]
    return jnp.concatenate([x1 * c - x2 * s, x1 * s + x2 * c], axis=-1)


def retention_one_dir(q, k, v, lg, inclusive):
    B, H, L, d = q.shape
    C = RET_CHUNK
    n = L // C
    idx = jnp.arange(C, dtype=jnp.float32)
    diff = idx[:, None] - idx[None, :]
    mask = (diff >= 0) if inclusive else (diff > 0)
    Dm = jnp.where(mask[None], jnp.exp(jnp.maximum(diff, 0.0)[None] * lg[:, None, None]), 0.0)
    qc = q.reshape(B, H, n, C, d)
    kc = k.reshape(B, H, n, C, d)
    vc = v.reshape(B, H, n, C, d)
    s = jnp.einsum('bhncd,bhnjd->bhncj', qc, kc) * Dm[None, :, None]
    intra = jnp.einsum('bhncj,bhnje->bhnce', s, vc)
    wk = jnp.exp((C - 1 - idx)[None, :] * lg[:, None])
    wq = jnp.exp((idx + 1)[None, :] * lg[:, None])
    gC = jnp.exp(C * lg)
    kw = kc * wk[None, :, None, :, None]
    xs = (jnp.moveaxis(kw, 2, 0), jnp.moveaxis(vc, 2, 0))

    def step(R, kv):
        kk, vv = kv
        Rn = R * gC[None, :, None, None] + jnp.einsum('bhcd,bhce->bhde', kk, vv)
        return Rn, R

    _, states = lax.scan(step, jnp.zeros((B, H, d, d), jnp.float32), xs)
    cross = jnp.einsum('bhncd,nbhde->bhnce', qc * wq[None, :, None, :, None], states)
    return (intra + cross).reshape(B, H, L, d)


def retention_mixer(u, lg_f, lg_b):
    B, L, _ = u.shape
    u = u.astype(jnp.float32)

    def heads(t):
        return t.reshape(B, L, N_RET_HEADS, RET_HEAD_DIM).transpose(0, 2, 1, 3)

    q = rotary(heads(u[..., :D_RET]))
    k = rotary(heads(u[..., D_RET:2 * D_RET])) * (RET_HEAD_DIM ** -0.5)
    v = heads(u[..., 2 * D_RET:3 * D_RET])
    g = u[..., 3 * D_RET:]
    lg_f = lg_f.astype(jnp.float32)
    lg_b = lg_b.astype(jnp.float32)
    o_f = retention_one_dir(q, k, v, lg_f, True)
    o_b = jnp.flip(retention_one_dir(jnp.flip(q, 2), jnp.flip(k, 2), jnp.flip(v, 2), lg_b, False), 2)
    o = o_f + o_b
    o = o * lax.rsqrt(jnp.mean(o * o, axis=-1, keepdims=True) + NORM_EPS)
    o = o.transpose(0, 2, 1, 3).reshape(B, L, D_RET)
    return jax.nn.silu(g) * o


def _layer(x, p):
    h = rmsnorm(x, p['ffn1_pre_g'])
    x = x + 0.5 * rmsnorm(swiglu(h, p['ffn1_w1'], p['ffn1_w3'], p['ffn1_w2']), p['ffn1_post_g'])
    h = rmsnorm(x, p['mix_pre_g'])
    u = h @ p['w_in']
    nh = (HYENA_ORDER + 1) * D_HYENA
    y_h = hyena_mixer(u[..., :nh], p['short_w'], p['short_b'], p['filt_w1'], p['filt_b1'],
                      p['filt_w2'], p['filt_b2'], p['filt_w3'], p['filt_b3'], p['filt_w4'],
                      p['filt_freq'], p['hyena_bias'])
    y_r = retention_mixer(u[..., nh:], p['ret_log_decay_f'], p['ret_log_decay_b'])
    y = jnp.concatenate([y_h, y_r], axis=-1).astype(x.dtype) @ p['w_out']
    x = x + rmsnorm(y, p['mix_post_g'])
    h = rmsnorm(x, p['ffn2_pre_g'])
    x = x + 0.5 * rmsnorm(swiglu(h, p['ffn2_w1'], p['ffn2_w3'], p['ffn2_w2']), p['ffn2_post_g'])
    return x


def setup_inputs(seed: int = 0) -> dict:
    key = jax.random.key(seed)
    ks = jax.random.split(key, 32)
    f32 = jnp.float32

    def nrm(k, shape, scale):
        return jax.random.normal(k, shape, f32) * scale

    def gain(k):
        return 1.0 + 0.02 * jax.random.normal(k, (DEPTH, D_MODEL), f32)

    base = jnp.log(1.0 - 2.0 ** (-5.0 - jnp.arange(N_RET_HEADS, dtype=f32)))
    return {
        'x_prompt': jax.random.normal(ks[0], (BATCH, SEQ, D_MODEL), f32),
        'x_sample': jax.random.normal(ks[1], (DEC_BATCH, DEC_SEQ, D_MODEL), f32),
        'ffn1_pre_g': gain(ks[2]),
        'ffn1_w1': nrm(ks[3], (DEPTH, D_MODEL, D_FF), D_MODEL ** -0.5),
        'ffn1_w3': nrm(ks[4], (DEPTH, D_MODEL, D_FF), D_MODEL ** -0.5),
        'ffn1_w2': nrm(ks[5], (DEPTH, D_FF, D_MODEL), D_FF ** -0.5),
        'ffn1_post_g': gain(ks[6]),
        'mix_pre_g': gain(ks[7]),
        'w_in': nrm(ks[8], (DEPTH, D_MODEL, D_IN), D_MODEL ** -0.5),
        'short_w': nrm(ks[9], (DEPTH, 3, (HYENA_ORDER + 1) * D_HYENA), 3 ** -0.5),
        'short_b': nrm(ks[10], (DEPTH, (HYENA_ORDER + 1) * D_HYENA), 0.02),
        'filt_w1': nrm(ks[11], (DEPTH, FILT_EMB, FILT_HIDDEN), FILT_EMB ** -0.5),
        'filt_b1': nrm(ks[12], (DEPTH, FILT_HIDDEN), 0.1),
        'filt_w2': nrm(ks[13], (DEPTH, FILT_HIDDEN, FILT_HIDDEN), FILT_HIDDEN ** -0.5),
        'filt_b2': nrm(ks[14], (DEPTH, FILT_HIDDEN), 0.1),
        'filt_w3': nrm(ks[15], (DEPTH, FILT_HIDDEN, FILT_HIDDEN), FILT_HIDDEN ** -0.5),
        'filt_b3': nrm(ks[16], (DEPTH, FILT_HIDDEN), 0.1),
        'filt_w4': nrm(ks[17], (DEPTH, FILT_HIDDEN, HYENA_ORDER * 2 * D_HYENA), FILT_HIDDEN ** -0.5),
        'filt_freq': 1.0 + 0.1 * jax.random.normal(ks[18], (DEPTH, FILT_HIDDEN), f32),
        'hyena_bias': nrm(ks[19], (DEPTH, HYENA_ORDER, D_HYENA), 0.1),
        'ret_log_decay_f': base[None, :] * (1.0 + 0.02 * jax.random.normal(ks[20], (DEPTH, N_RET_HEADS), f32)),
        'ret_log_decay_b': base[None, :] * (1.0 + 0.02 * jax.random.normal(ks[21], (DEPTH, N_RET_HEADS), f32)),
        'w_out': nrm(ks[22], (DEPTH, D_MIX, D_MODEL), D_MIX ** -0.5),
        'mix_post_g': gain(ks[23]),
        'ffn2_pre_g': gain(ks[24]),
        'ffn2_w1': nrm(ks[25], (DEPTH, D_MODEL, D_FF), D_MODEL ** -0.5),
        'ffn2_w3': nrm(ks[26], (DEPTH, D_MODEL, D_FF), D_MODEL ** -0.5),
        'ffn2_w2': nrm(ks[27], (DEPTH, D_FF, D_MODEL), D_FF ** -0.5),
        'ffn2_post_g': gain(ks[28]),
    }


def reference(x_prompt, x_sample, ffn1_pre_g, ffn1_w1, ffn1_w3, ffn1_w2, ffn1_post_g,
              mix_pre_g, w_in, short_w, short_b, filt_w1, filt_b1, filt_w2, filt_b2,
              filt_w3, filt_b3, filt_w4, filt_freq, hyena_bias, ret_log_decay_f,
              ret_log_decay_b, w_out, mix_post_g, ffn2_pre_g, ffn2_w1, ffn2_w3, ffn2_w2,
              ffn2_post_g):
    def run(x):
        for l in range(DEPTH):
            p = dict(ffn1_pre_g=ffn1_pre_g[l], ffn1_w1=ffn1_w1[l], ffn1_w3=ffn1_w3[l],
                     ffn1_w2=ffn1_w2[l], ffn1_post_g=ffn1_post_g[l], mix_pre_g=mix_pre_g[l],
                     w_in=w_in[l], short_w=short_w[l], short_b=short_b[l],
                     filt_w1=filt_w1[l], filt_b1=filt_b1[l], filt_w2=filt_w2[l],
                     filt_b2=filt_b2[l], filt_w3=filt_w3[l], filt_b3=filt_b3[l],
                     filt_w4=filt_w4[l], filt_freq=filt_freq[l], hyena_bias=hyena_bias[l],
                     ret_log_decay_f=ret_log_decay_f[l], ret_log_decay_b=ret_log_decay_b[l],
                     w_out=w_out[l], mix_post_g=mix_post_g[l], ffn2_pre_g=ffn2_pre_g[l],
                     ffn2_w1=ffn2_w1[l], ffn2_w3=ffn2_w3[l], ffn2_w2=ffn2_w2[l],
                     ffn2_post_g=ffn2_post_g[l])
            x = _layer(x, p)
        return x

    y_prompt = run(x_prompt)
    y_sample = run(x_sample)
    return (y_prompt, y_sample)
```

```python
import functools
import math

import numpy as np
import jax
import jax.numpy as jnp
from jax import lax
from jax.experimental import pallas as pl
from jax.experimental.pallas import tpu as pltpu

F32 = jnp.float32
BF16 = jnp.bfloat16

D_MODEL = 1024
D_HYENA = 512
D_RET = 512
HYENA_ORDER = 2
N_RET_HEADS = 4
RET_HEAD_DIM = 128
D_FF = 2816
FILT_EMB = 33
FILT_BANDS = 16
FILT_HIDDEN = 64
ROPE_BASE = 10000.0
NORM_EPS = 1e-6
HYENA_TARGET = 1e-2
FAST_DECAY_PCT = 0.3
SLOW_DECAY_PCT = 1.5
N_HY_COLS = (HYENA_ORDER + 1) * D_HYENA
D_IN = N_HY_COLS + 4 * D_RET

LANES = 128
VMEM_LIMIT = 56 * 1024 * 1024
FFT_N2 = 32
RET_CHUNK = 128
TOKEN_TILE = 512


def _const_spec(shape):
    nd = len(shape)
    return pl.BlockSpec(shape, lambda *_: (0,) * nd, pipeline_mode=pl.Buffered(1))


def _rms(x, g):
    ms = jnp.mean(x * x, axis=-1, keepdims=True)
    return x * lax.rsqrt(ms + NORM_EPS) * g


def _bdot(a, b):
    return jnp.dot(a, b, preferred_element_type=F32)


def _ffn_core(x, pre_ref, w1_ref, w3_ref, w2_ref, post_ref, o_ref):
    h = _rms(x, pre_ref[...]).astype(BF16)
    a = _bdot(h, w1_ref[...])
    b = _bdot(h, w3_ref[...])
    g = (a * jax.nn.sigmoid(a) * b).astype(BF16)
    y = _bdot(g, w2_ref[...])
    o_ref[...] = x + 0.5 * _rms(y, post_ref[...])


def _ffn_kernel(x_ref, pre_ref, w1_ref, w3_ref, w2_ref, post_ref, o_ref):
    _ffn_core(x_ref[...], pre_ref, w1_ref, w3_ref, w2_ref, post_ref, o_ref)


def _mix_ffn_kernel(x_ref, yh_ref, yr_ref, woh_ref, wor_ref, mg_ref,
                    pre_ref, w1_ref, w3_ref, w2_ref, post_ref, o_ref):
    y = _bdot(yh_ref[...].astype(BF16), woh_ref[...])
    y = y + _bdot(yr_ref[...].astype(BF16), wor_ref[...])
    x = x_ref[...] + _rms(y, mg_ref[...])
    _ffn_core(x, pre_ref, w1_ref, w3_ref, w2_ref, post_ref, o_ref)


def _row_spec(tm, width):
    return pl.BlockSpec((tm, width), lambda i: (i, 0))


def _ffn(x, pre_g, w1, w3, w2, post_g, mix=None):
    t = x.shape[0]
    tm = TOKEN_TILE
    ffn_specs = [_const_spec((1, D_MODEL)), _const_spec((D_MODEL, D_FF)),
                 _const_spec((D_MODEL, D_FF)), _const_spec((D_FF, D_MODEL)),
                 _const_spec((1, D_MODEL))]
    ffn_args = (pre_g, w1, w3, w2, post_g)
    if mix is None:
        body, args = _ffn_kernel, (x,) + ffn_args
        specs = [_row_spec(tm, D_MODEL)] + ffn_specs
    else:
        yh, yr, woh, wor, mg = mix
        body, args = _mix_ffn_kernel, (x, yh, yr, woh, wor, mg) + ffn_args
        specs = [_row_spec(tm, D_MODEL), _row_spec(tm, D_HYENA), _row_spec(tm, D_RET),
                 _const_spec((D_HYENA, D_MODEL)), _const_spec((D_RET, D_MODEL)),
                 _const_spec((1, D_MODEL))] + ffn_specs
    return pl.pallas_call(
        body,
        grid=(t // tm,),
        in_specs=specs,
        out_specs=_row_spec(tm, D_MODEL),
        out_shape=jax.ShapeDtypeStruct((t, D_MODEL), F32),
        compiler_params=pltpu.CompilerParams(
            dimension_semantics=("parallel",), vmem_limit_bytes=VMEM_LIMIT),
        name="ffn_mix" if mix is not None else "ffn",
    )(*args)


def _inproj_kernel(x_ref, g_ref, w_ref, o_ref):
    h = _rms(x_ref[...], g_ref[...]).astype(BF16)
    o_ref[...] = _bdot(h, w_ref[...])


def _inproj(x, g, w):
    t = x.shape[0]
    tm = TOKEN_TILE
    return pl.pallas_call(
        _inproj_kernel,
        grid=(t // tm,),
        in_specs=[_row_spec(tm, D_MODEL), _const_spec((1, D_MODEL)),
                  _const_spec((D_MODEL, D_IN))],
        out_specs=_row_spec(tm, D_IN),
        out_shape=jax.ShapeDtypeStruct((t, D_IN), F32),
        compiler_params=pltpu.CompilerParams(
            dimension_semantics=("parallel",), vmem_limit_bytes=VMEM_LIMIT),
        name="inproj",
    )(x, g, w)


def _ret_kernel(lgf_ref, lgb_ref, q_ref, k_ref, v_ref, g_ref, cc_ref, ss_ref, o_ref, *, seq):
    c = RET_CHUNK
    d = RET_HEAD_DIM
    n_chunks = seq // c
    head = pl.program_id(1)
    lgf = jnp.full((c, d), lgf_ref[head], F32)
    lgb = jnp.full((c, d), lgb_ref[head], F32)
    row = lax.broadcasted_iota(jnp.int32, (c, d), 0).astype(F32)
    col = lax.broadcasted_iota(jnp.int32, (c, d), 1).astype(F32)
    diff = row - col
    dmat = jnp.where(diff >= 0.0, jnp.exp(jnp.maximum(diff, 0.0) * lgf),
                     jnp.exp(jnp.maximum(-diff, 0.0) * lgb))
    wq_f = jnp.exp((row + 1.0) * lgf)
    wk_f = jnp.exp((c - 1.0 - row) * lgf)
    wq_b = jnp.exp((c - row) * lgb)
    wk_b = jnp.exp(row * lgb)
    gc_f = jnp.exp(c * lgf)
    gc_b = jnp.exp(c * lgb)
    scale = d ** -0.5

    def load_qk(r0):
        cc = cc_ref[pl.ds(r0, c), :]
        ss = ss_ref[pl.ds(r0, c), :]
        q = q_ref[pl.ds(r0, c), :]
        k = k_ref[pl.ds(r0, c), :]
        qr = q * cc + pltpu.roll(q, d // 2, axis=1) * ss
        kr = (k * cc + pltpu.roll(k, d // 2, axis=1) * ss) * scale
        return qr, kr

    def state_update(state, gc, kw, vb):
        kv = lax.dot_general(kw.astype(BF16), vb, (((0,), (0,)), ((), ())),
                             preferred_element_type=F32)
        return state * gc + kv

    def fwd(n, state):
        r0 = pl.multiple_of(n * c, c)
        qr, kr = load_qk(r0)
        vb = v_ref[pl.ds(r0, c), :].astype(BF16)
        s = lax.dot_general(qr.astype(BF16), kr.astype(BF16), (((1,), (1,)), ((), ())),
                            preferred_element_type=F32)
        intra = _bdot((s * dmat).astype(BF16), vb)
        cross = _bdot((qr * wq_f).astype(BF16), state.astype(BF16))
        o_ref[pl.ds(r0, c), :] = intra + cross
        return state_update(state, gc_f, kr * wk_f, vb)

    lax.fori_loop(0, n_chunks, fwd, jnp.zeros((d, d), F32))

    def bwd(i, state):
        r0 = pl.multiple_of((n_chunks - 1 - i) * c, c)
        qr, kr = load_qk(r0)
        vb = v_ref[pl.ds(r0, c), :].astype(BF16)
        cross = _bdot((qr * wq_b).astype(BF16), state.astype(BF16))
        o = o_ref[pl.ds(r0, c), :] + cross
        o = o * lax.rsqrt(jnp.mean(o * o, axis=-1, keepdims=True) + NORM_EPS)
        g = g_ref[pl.ds(r0, c), :]
        o_ref[pl.ds(r0, c), :] = g * jax.nn.sigmoid(g) * o
        return state_update(state, gc_b, kr * wk_b, vb)

    lax.fori_loop(0, n_chunks, bwd, jnp.zeros((d, d), F32))


def _rope_tables(seq):
    d = RET_HEAD_DIM
    inv = 1.0 / (ROPE_BASE ** (jnp.arange(0, d, 2, dtype=F32) / d))
    ang = jnp.arange(seq, dtype=F32)[:, None] * inv[None, :]
    c, s = jnp.cos(ang), jnp.sin(ang)
    return jnp.concatenate([c, c], axis=-1), jnp.concatenate([-s, s], axis=-1)


def _retention(u, lg_f, lg_b, batch, seq):
    cc, ss = _rope_tables(seq)
    first = N_HY_COLS // LANES
    heads = N_RET_HEADS

    def col(off):
        return pl.BlockSpec((seq, LANES), lambda b, h, *_: (b, first + off * heads + h))

    grid_spec = pltpu.PrefetchScalarGridSpec(
        num_scalar_prefetch=2,
        grid=(batch, heads),
        in_specs=[col(0), col(1), col(2), col(3),
                  pl.BlockSpec((seq, LANES), lambda b, h, *_: (0, 0), pipeline_mode=pl.Buffered(1)),
                  pl.BlockSpec((seq, LANES), lambda b, h, *_: (0, 0), pipeline_mode=pl.Buffered(1))],
        out_specs=pl.BlockSpec((seq, LANES), lambda b, h, *_: (b, h)),
    )
    return pl.pallas_call(
        functools.partial(_ret_kernel, seq=seq),
        grid_spec=grid_spec,
        out_shape=jax.ShapeDtypeStruct((batch * seq, D_RET), F32),
        compiler_params=pltpu.CompilerParams(
            dimension_semantics=("parallel", "parallel"), vmem_limit_bytes=VMEM_LIMIT),
        name="retention",
    )(lg_f, lg_b, u, u, u, u, cc, ss)


def _hdot(a, b):
    return jnp.dot(a, b, precision=lax.Precision.HIGHEST, preferred_element_type=F32)


@functools.lru_cache(maxsize=None)
def _fft_tables(seq):
    n = 2 * seq
    n2 = FFT_N2
    n1 = n // n2
    h1 = n1 // 2
    f1 = np.arange(h1, dtype=np.float64) + 0.5
    th = 2.0 * np.pi * f1[:, None] * np.arange(n1, dtype=np.float64)[None, :] / n1
    m1_full = np.concatenate([np.cos(th), -np.sin(th)], axis=0)
    m1 = m1_full[:, :h1]
    m1_inv = (2.0 / n) * m1.T
    t2 = np.arange(n2, dtype=np.float64)
    phi = 2.0 * np.pi * (np.arange(n2, dtype=np.float64)[None, :, None] * t2[None, None, :] / n2
                         + f1[:, None, None] * t2[None, None, :] / n)
    gr, gi = np.cos(phi), -np.sin(phi)
    m2 = np.concatenate([np.concatenate([gr, -gi], axis=2),
                         np.concatenate([gi, gr], axis=2)], axis=1)
    m2_inv = np.transpose(m2, (0, 2, 1))
    return dict(n1=n1, h1=h1, m1_full=m1_full, m1=m1, m1_inv=m1_inv, m2=m2, m2_inv=m2_inv)


def _short_conv_kernel(u_ref, w_ref, b_ref, o_ref, *, seq):
    x = u_ref[...]
    rows = lax.broadcasted_iota(jnp.int32, x.shape, 0)
    prev = jnp.where(rows == 0, 0.0, pltpu.roll(x, 1, axis=0))
    nxt = jnp.where(rows == seq - 1, 0.0, pltpu.roll(x, seq - 1, axis=0))
    o_ref[...] = prev * w_ref[0:1, :] + x * w_ref[1:2, :] + nxt * w_ref[2:3, :] + b_ref[...]


def _short_conv(u, w, b, batch, seq):
    nblk = N_HY_COLS // LANES
    return pl.pallas_call(
        functools.partial(_short_conv_kernel, seq=seq),
        grid=(batch, nblk),
        in_specs=[pl.BlockSpec((seq, LANES), lambda i, j: (i, j)),
                  pl.BlockSpec((3, LANES), lambda i, j: (0, j)),
                  pl.BlockSpec((1, LANES), lambda i, j: (0, j))],
        out_specs=pl.BlockSpec((seq, LANES), lambda i, j: (i, j)),
        out_shape=jax.ShapeDtypeStruct((batch * seq, N_HY_COLS), F32),
        compiler_params=pltpu.CompilerParams(
            dimension_semantics=("parallel", "parallel"), vmem_limit_bytes=VMEM_LIMIT),
        name="short_conv",
    )(u, w, b)


FILT_ROWS = 512


def _filter_kernel(zf_ref, zb_ref, w1_ref, b1_ref, w2_ref, b2_ref, w3_ref, b3_ref, fr_ref,
                   w4f_ref, w4b_ref, dl_ref, o_ref, *, seq):
    nblk = seq // FILT_ROWS
    fr = fr_ref[...]
    adl = jnp.abs(dl_ref[...])
    sel = (lax.broadcasted_iota(jnp.int32, (LANES, LANES), 0) == 0).astype(F32)

    def raw(z, w4):
        h = jnp.sin(fr * (_hdot(z, w1_ref[...]) + b1_ref[...]))
        h = jnp.sin(fr * (_hdot(h, w2_ref[...]) + b2_ref[...]))
        h = jnp.sin(fr * (_hdot(h, w3_ref[...]) + b3_ref[...]))
        t = _hdot(z, sel)
        return _hdot(h, w4) * jnp.exp(-t * adl)

    def fill(z_ref, w4_ref, base):
        def step(i, acc):
            r0 = pl.multiple_of(i * FILT_ROWS, FILT_ROWS)
            h = raw(z_ref[pl.ds(r0, FILT_ROWS), :], w4_ref[...])
            o_ref[pl.ds(base + r0, FILT_ROWS), :] = h
            return acc + jnp.sum(jnp.abs(h), axis=0, keepdims=True)
        return lax.fori_loop(0, nblk, step, jnp.zeros((1, LANES), F32))

    sum_f = fill(zf_ref, w4f_ref, 0)
    sum_b = fill(zb_ref, w4b_ref, seq)

    def scale(i, carry):
        r0 = pl.multiple_of(i * FILT_ROWS, FILT_ROWS)
        o_ref[pl.ds(r0, FILT_ROWS), :] = o_ref[pl.ds(r0, FILT_ROWS), :] / sum_f
        hb = o_ref[pl.ds(seq + r0, FILT_ROWS), :] / sum_b
        rows = lax.broadcasted_iota(jnp.int32, hb.shape, 0) + r0
        o_ref[pl.ds(seq + r0, FILT_ROWS), :] = jnp.where(rows == 0, 0.0, -hb)
        return carry

    lax.fori_loop(0, nblk, scale, 0)


def _kspec_kernel(k_ref, m1_ref, m2_ref, o_ref, a_ref, *, n1):
    n2 = FFT_N2
    h1 = n1 // 2

    def stage1(t2, carry):
        kt = k_ref[pl.ds(t2, n1, stride=n2), :]
        a = _hdot(m1_ref[...], kt)
        a_ref[pl.ds(t2, h1, stride=2 * n2), :] = a[:h1]
        a_ref[pl.ds(t2 + n2, h1, stride=2 * n2), :] = a[h1:]
        return carry

    lax.fori_loop(0, n2, stage1, 0)

    def stage2(f1, carry):
        r0 = pl.multiple_of(f1 * 2 * n2, 2 * n2)
        o_ref[pl.ds(r0, 2 * n2), :] = _hdot(m2_ref[f1], a_ref[pl.ds(r0, 2 * n2), :])
        return carry

    lax.fori_loop(0, h1, stage2, 0)


def _filter_spectra(seq, p):
    n = 2 * seq
    tb = _fft_tables(seq)
    t = jnp.linspace(0.0, 1.0, seq, dtype=F32)[:, None]
    w = 2.0 * math.pi * jnp.arange(seq, dtype=F32)[:, None] / seq
    fb = jnp.linspace(1e-4, FILT_BANDS - 1, FILT_BANDS, dtype=F32)[None, :]
    z = jnp.concatenate([t, jnp.cos(fb * w), -jnp.sin(fb * w)], axis=-1)
    zf = jnp.pad(z, ((0, 0), (0, LANES - FILT_EMB)))
    zb = jnp.roll(zf[::-1], 1, axis=0)
    pad_h = LANES - FILT_HIDDEN

    def padw(a, rows):
        return jnp.pad(a, ((0, rows - a.shape[0]), (0, pad_h)))

    def padv(a):
        return jnp.pad(a, (0, pad_h))[None, :]

    w1 = padw(p['filt_w1'], LANES)
    w2 = padw(p['filt_w2'], LANES)
    w3 = padw(p['filt_w3'], LANES)
    w4 = jnp.pad(p['filt_w4'], ((0, pad_h), (0, 0)))
    min_decay = math.log(HYENA_TARGET) / SLOW_DECAY_PCT
    max_decay = math.log(HYENA_TARGET) / FAST_DECAY_PCT
    deltas = jnp.linspace(min_decay, max_decay, D_HYENA, dtype=F32)[None, :]
    tiles = D_HYENA // LANES
    sq = _const_spec((LANES, LANES))
    vec = _const_spec((1, LANES))
    filt = pl.pallas_call(
        functools.partial(_filter_kernel, seq=seq),
        grid=(HYENA_ORDER, tiles),
        in_specs=[_const_spec((seq, LANES)), _const_spec((seq, LANES)),
                  sq, vec, sq, vec, sq, vec, vec,
                  pl.BlockSpec((LANES, LANES), lambda o, j: (0, o * 2 * tiles + j)),
                  pl.BlockSpec((LANES, LANES), lambda o, j: (0, o * 2 * tiles + tiles + j)),
                  pl.BlockSpec((1, LANES), lambda o, j: (0, j))],
        out_specs=pl.BlockSpec((n, LANES), lambda o, j: (0, o * tiles + j)),
        out_shape=jax.ShapeDtypeStruct((n, HYENA_ORDER * D_HYENA), F32),
        compiler_params=pltpu.CompilerParams(
            dimension_semantics=("parallel", "parallel"), vmem_limit_bytes=VMEM_LIMIT),
        name="hyena_filter",
    )(zf, zb, w1, padv(p['filt_b1']), w2, padv(p['filt_b2']), w3, padv(p['filt_b3']),
      padv(p['filt_freq']), w4, w4, deltas)
    n1, h1 = tb['n1'], tb['h1']
    return pl.pallas_call(
        functools.partial(_kspec_kernel, n1=n1),
        grid=(HYENA_ORDER * tiles,),
        in_specs=[pl.BlockSpec((n, LANES), lambda j: (0, j)),
                  _const_spec((n1, n1)), _const_spec((h1, 2 * FFT_N2, 2 * FFT_N2))],
        out_specs=pl.BlockSpec((n, LANES), lambda j: (0, j)),
        out_shape=jax.ShapeDtypeStruct((n, HYENA_ORDER * D_HYENA), F32),
        scratch_shapes=[pltpu.VMEM((n, LANES), F32)],
        compiler_params=pltpu.CompilerParams(
            dimension_semantics=("parallel",), vmem_limit_bytes=VMEM_LIMIT),
        name="hyena_filter_spectrum",
    )(filt, jnp.asarray(tb['m1_full'], F32), jnp.asarray(tb['m2'], F32))


def _long_conv_kernel(z_ref, g_ref, k_ref, bias_ref, m1_ref, m1i_ref, m2_ref, m2i_ref,
                      o_ref, a_ref, *, n1):
    n2 = FFT_N2
    h1 = n1 // 2

    def stage1(t2, carry):
        zt = z_ref[pl.ds(t2, h1, stride=n2), :]
        a = _bdot(m1_ref[...], zt.astype(BF16))
        a_ref[pl.ds(t2, h1, stride=2 * n2), :] = a[:h1]
        a_ref[pl.ds(t2 + n2, h1, stride=2 * n2), :] = a[h1:]
        return carry

    lax.fori_loop(0, n2, stage1, 0)

    def stage2(f1, carry):
        r0 = pl.multiple_of(f1 * 2 * n2, 2 * n2)
        x = _bdot(m2_ref[f1], a_ref[pl.ds(r0, 2 * n2), :].astype(BF16))
        kk = k_ref[pl.ds(r0, 2 * n2), :]
        xr, xi, kr, ki = x[:n2], x[n2:], kk[:n2], kk[n2:]
        y = jnp.concatenate([xr * kr - xi * ki, xr * ki + xi * kr], axis=0)
        a_ref[pl.ds(r0, 2 * n2), :] = _bdot(m2i_ref[f1], y.astype(BF16))
        return carry

    lax.fori_loop(0, h1, stage2, 0)

    bias = bias_ref[...]

    def stage3(t2, carry):
        br = a_ref[pl.ds(t2, h1, stride=2 * n2), :]
        bi = a_ref[pl.ds(t2 + n2, h1, stride=2 * n2), :]
        y = _bdot(m1i_ref[...], jnp.concatenate([br, bi], axis=0).astype(BF16))
        zt = z_ref[pl.ds(t2, h1, stride=n2), :]
        gt = g_ref[pl.ds(t2, h1, stride=n2), :]
        o_ref[pl.ds(t2, h1, stride=n2), :] = gt * (y + bias * zt)
        return carry

    lax.fori_loop(0, n2, stage3, 0)


def _long_conv(z, z_off, gate, g_off, kspec, k_off, bias, batch, seq):
    n = 2 * seq
    tb = _fft_tables(seq)
    n1, h1 = tb['n1'], tb['h1']
    tiles = D_HYENA // LANES
    return pl.pallas_call(
        functools.partial(_long_conv_kernel, n1=n1),
        grid=(tiles, batch),
        in_specs=[pl.BlockSpec((seq, LANES), lambda j, b: (b, z_off + j)),
                  pl.BlockSpec((seq, LANES), lambda j, b: (b, g_off + j)),
                  pl.BlockSpec((n, LANES), lambda j, b: (0, k_off + j), pipeline_mode=pl.Buffered(1)),
                  pl.BlockSpec((1, LANES), lambda j, b: (0, j)),
                  _const_spec((n1, h1)), _const_spec((h1, n1)),
                  _const_spec((h1, 2 * FFT_N2, 2 * FFT_N2)),
                  _const_spec((h1, 2 * FFT_N2, 2 * FFT_N2))],
        out_specs=pl.BlockSpec((seq, LANES), lambda j, b: (b, j)),
        out_shape=jax.ShapeDtypeStruct((batch * seq, D_HYENA), F32),
        scratch_shapes=[pltpu.VMEM((n, LANES), F32)],
        compiler_params=pltpu.CompilerParams(
            dimension_semantics=("parallel", "parallel"), vmem_limit_bytes=VMEM_LIMIT),
        name="hyena_long_conv",
    )(z, gate, kspec, bias, jnp.asarray(tb['m1'], BF16), jnp.asarray(tb['m1_inv'], BF16),
      jnp.asarray(tb['m2'], BF16), jnp.asarray(tb['m2_inv'], BF16))


def _layer(x3, p):
    batch, seq, _ = x3.shape
    x = x3.reshape(batch * seq, D_MODEL)
    x = _ffn(x, p['ffn1_pre_g'], p['ffn1_w1'], p['ffn1_w3'], p['ffn1_w2'], p['ffn1_post_g'])
    u = _inproj(x, p['mix_pre_g'], p['w_in'])
    sc = _short_conv(u, p['short_w'], p['short_b'], batch, seq)
    kspec = _filter_spectra(seq, p)
    tiles = D_HYENA // LANES
    z1 = _long_conv(sc, 0, sc, tiles, kspec, 0, p['hyena_bias'][0:1], batch, seq)
    yh = _long_conv(z1, 0, sc, 2 * tiles, kspec, tiles, p['hyena_bias'][1:2], batch, seq)
    yr = _retention(u, p['ret_log_decay_f'], p['ret_log_decay_b'], batch, seq)
    mix = (yh, yr, p['w_out'][:D_HYENA], p['w_out'][D_HYENA:], p['mix_post_g'])
    x = _ffn(x, p['ffn2_pre_g'], p['ffn2_w1'], p['ffn2_w3'], p['ffn2_w2'], p['ffn2_post_g'], mix=mix)
    return x.reshape(batch, seq, D_MODEL)


_MATRIX_PARAMS = ('ffn1_w1', 'ffn1_w3', 'ffn1_w2', 'w_in', 'w_out', 'ffn2_w1', 'ffn2_w3', 'ffn2_w2')
_GAIN_PARAMS = ('ffn1_pre_g', 'ffn1_post_g', 'mix_pre_g', 'mix_post_g', 'ffn2_pre_g', 'ffn2_post_g')


def kernel(x_prompt, x_sample, ffn1_pre_g, ffn1_w1, ffn1_w3, ffn1_w2, ffn1_post_g, mix_pre_g, w_in, short_w, short_b, filt_w1, filt_b1, filt_w2, filt_b2, filt_w3, filt_b3, filt_w4, filt_freq, hyena_bias, ret_log_decay_f, ret_log_decay_b, w_out, mix_post_g, ffn2_pre_g, ffn2_w1, ffn2_w3, ffn2_w2, ffn2_post_g):
    params = dict(ffn1_pre_g=ffn1_pre_g, ffn1_w1=ffn1_w1, ffn1_w3=ffn1_w3, ffn1_w2=ffn1_w2,
                  ffn1_post_g=ffn1_post_g, mix_pre_g=mix_pre_g, w_in=w_in, short_w=short_w,
                  short_b=short_b, filt_w1=filt_w1, filt_b1=filt_b1, filt_w2=filt_w2,
                  filt_b2=filt_b2, filt_w3=filt_w3, filt_b3=filt_b3, filt_w4=filt_w4,
                  filt_freq=filt_freq, hyena_bias=hyena_bias, ret_log_decay_f=ret_log_decay_f,
                  ret_log_decay_b=ret_log_decay_b, w_out=w_out, mix_post_g=mix_post_g,
                  ffn2_pre_g=ffn2_pre_g, ffn2_w1=ffn2_w1, ffn2_w3=ffn2_w3, ffn2_w2=ffn2_w2,
                  ffn2_post_g=ffn2_post_g)
    depth = ffn1_w1.shape[0]

    def run(x):
        for l in range(depth):
            p = {k: v[l] for k, v in params.items()}
            for k in _MATRIX_PARAMS:
                p[k] = p[k].astype(BF16)
            for k in _GAIN_PARAMS:
                p[k] = p[k][None, :]
            p['short_b'] = p['short_b'][None, :]
            x = _layer(x, p)
        return x

    return (run(x_prompt), run(x_sample))
```

```python
import functools
import math

import numpy as np
import jax
import jax.numpy as jnp
from jax import lax
from jax.experimental import pallas as pl
from jax.experimental.pallas import tpu as pltpu

F32 = jnp.float32
BF16 = jnp.bfloat16

D_MODEL = 1024
D_HYENA = 512
D_RET = 512
HYENA_ORDER = 2
N_RET_HEADS = 4
RET_HEAD_DIM = 128
D_FF = 2816
FILT_EMB = 33
FILT_BANDS = 16
FILT_HIDDEN = 64
ROPE_BASE = 10000.0
NORM_EPS = 1e-6
HYENA_TARGET = 1e-2
FAST_DECAY_PCT = 0.3
SLOW_DECAY_PCT = 1.5
N_HY_COLS = (HYENA_ORDER + 1) * D_HYENA
D_IN = N_HY_COLS + 4 * D_RET

LANES = 128
VMEM_LIMIT = 56 * 1024 * 1024
FFT_N2 = 32
RET_CHUNK = 128
TOKEN_TILE = 512


def _const_spec(shape):
    nd = len(shape)
    return pl.BlockSpec(shape, lambda *_: (0,) * nd, pipeline_mode=pl.Buffered(1))


def _rms(x, g):
    ms = jnp.mean(x * x, axis=-1, keepdims=True)
    return x * lax.rsqrt(ms + NORM_EPS) * g


def _bdot(a, b):
    return jnp.dot(a, b, preferred_element_type=F32)


def _ffn_core(x, pre_ref, w1_ref, w3_ref, w2_ref, post_ref, o_ref):
    h = _rms(x, pre_ref[...]).astype(BF16)
    a = _bdot(h, w1_ref[...])
    b = _bdot(h, w3_ref[...])
    g = (a * jax.nn.sigmoid(a) * b).astype(BF16)
    y = _bdot(g, w2_ref[...])
    o_ref[...] = x + 0.5 * _rms(y, post_ref[...])


def _ffn_kernel(x_ref, pre_ref, w1_ref, w3_ref, w2_ref, post_ref, o_ref):
    _ffn_core(x_ref[...], pre_ref, w1_ref, w3_ref, w2_ref, post_ref, o_ref)


def _mix_ffn_kernel(x_ref, yh_ref, yr_ref, woh_ref, wor_ref, mg_ref,
                    pre_ref, w1_ref, w3_ref, w2_ref, post_ref, o_ref):
    y = _bdot(yh_ref[...].astype(BF16), woh_ref[...])
    y = y + _bdot(yr_ref[...].astype(BF16), wor_ref[...])
    x = x_ref[...] + _rms(y, mg_ref[...])
    _ffn_core(x, pre_ref, w1_ref, w3_ref, w2_ref, post_ref, o_ref)


def _row_spec(tm, width):
    return pl.BlockSpec((tm, width), lambda i: (i, 0))


def _ffn(x, pre_g, w1, w3, w2, post_g, mix=None):
    t = x.shape[0]
    tm = TOKEN_TILE
    ffn_specs = [_const_spec((1, D_MODEL)), _const_spec((D_MODEL, D_FF)),
                 _const_spec((D_MODEL, D_FF)), _const_spec((D_FF, D_MODEL)),
                 _const_spec((1, D_MODEL))]
    ffn_args = (pre_g, w1, w3, w2, post_g)
    if mix is None:
        body, args = _ffn_kernel, (x,) + ffn_args
        specs = [_row_spec(tm, D_MODEL)] + ffn_specs
    else:
        yh, yr, woh, wor, mg = mix
        body, args = _mix_ffn_kernel, (x, yh, yr, woh, wor, mg) + ffn_args
        specs = [_row_spec(tm, D_MODEL), _row_spec(tm, D_HYENA), _row_spec(tm, D_RET),
                 _const_spec((D_HYENA, D_MODEL)), _const_spec((D_RET, D_MODEL)),
                 _const_spec((1, D_MODEL))] + ffn_specs
    return pl.pallas_call(
        body,
        grid=(t // tm,),
        in_specs=specs,
        out_specs=_row_spec(tm, D_MODEL),
        out_shape=jax.ShapeDtypeStruct((t, D_MODEL), F32),
        compiler_params=pltpu.CompilerParams(
            dimension_semantics=("parallel",), vmem_limit_bytes=VMEM_LIMIT),
        name="ffn_mix" if mix is not None else "ffn",
    )(*args)


def _inproj_kernel(x_ref, g_ref, w_ref, o_ref):
    h = _rms(x_ref[...], g_ref[...]).astype(BF16)
    o_ref[...] = _bdot(h, w_ref[...])


def _inproj(x, g, w):
    t = x.shape[0]
    tm = TOKEN_TILE
    return pl.pallas_call(
        _inproj_kernel,
        grid=(t // tm,),
        in_specs=[_row_spec(tm, D_MODEL), _const_spec((1, D_MODEL)),
                  _const_spec((D_MODEL, D_IN))],
        out_specs=_row_spec(tm, D_IN),
        out_shape=jax.ShapeDtypeStruct((t, D_IN), F32),
        compiler_params=pltpu.CompilerParams(
            dimension_semantics=("parallel",), vmem_limit_bytes=VMEM_LIMIT),
        name="inproj",
    )(x, g, w)


def _ret_kernel(lgf_ref, lgb_ref, q_ref, k_ref, v_ref, g_ref, cc_ref, ss_ref, o_ref, *, seq):
    c = RET_CHUNK
    d = RET_HEAD_DIM
    n_chunks = seq // c
    head = pl.program_id(1)
    lgf = jnp.full((c, d), lgf_ref[head], F32)
    lgb = jnp.full((c, d), lgb_ref[head], F32)
    row = lax.broadcasted_iota(jnp.int32, (c, d), 0).astype(F32)
    col = lax.broadcasted_iota(jnp.int32, (c, d), 1).astype(F32)
    diff = row - col
    dmat = jnp.where(diff >= 0.0, jnp.exp(jnp.maximum(diff, 0.0) * lgf),
                     jnp.exp(jnp.maximum(-diff, 0.0) * lgb))
    wq_f = jnp.exp((row + 1.0) * lgf)
    wk_f = jnp.exp((c - 1.0 - row) * lgf)
    wq_b = jnp.exp((c - row) * lgb)
    wk_b = jnp.exp(row * lgb)
    gc_f = jnp.exp(c * lgf)
    gc_b = jnp.exp(c * lgb)
    scale = d ** -0.5

    def load_qk(r0):
        cc = cc_ref[pl.ds(r0, c), :]
        ss = ss_ref[pl.ds(r0, c), :]
        q = q_ref[pl.ds(r0, c), :]
        k = k_ref[pl.ds(r0, c), :]
        qr = q * cc + pltpu.roll(q, d // 2, axis=1) * ss
        kr = (k * cc + pltpu.roll(k, d // 2, axis=1) * ss) * scale
        return qr, kr

    def state_update(state, gc, kw, vb):
        kv = lax.dot_general(kw.astype(BF16), vb, (((0,), (0,)), ((), ())),
                             preferred_element_type=F32)
        return state * gc + kv

    def fwd(n, state):
        r0 = pl.multiple_of(n * c, c)
        qr, kr = load_qk(r0)
        vb = v_ref[pl.ds(r0, c), :].astype(BF16)
        s = lax.dot_general(qr.astype(BF16), kr.astype(BF16), (((1,), (1,)), ((), ())),
                            preferred_element_type=F32)
        intra = _bdot((s * dmat).astype(BF16), vb)
        cross = _bdot((qr * wq_f).astype(BF16), state.astype(BF16))
        o_ref[pl.ds(r0, c), :] = intra + cross
        return state_update(state, gc_f, kr * wk_f, vb)

    lax.fori_loop(0, n_chunks, fwd, jnp.zeros((d, d), F32))

    def bwd(i, state):
        r0 = pl.multiple_of((n_chunks - 1 - i) * c, c)
        qr, kr = load_qk(r0)
        vb = v_ref[pl.ds(r0, c), :].astype(BF16)
        cross = _bdot((qr * wq_b).astype(BF16), state.astype(BF16))
        o = o_ref[pl.ds(r0, c), :] + cross
        o = o * lax.rsqrt(jnp.mean(o * o, axis=-1, keepdims=True) + NORM_EPS)
        g = g_ref[pl.ds(r0, c), :]
        o_ref[pl.ds(r0, c), :] = g * jax.nn.sigmoid(g) * o
        return state_update(state, gc_b, kr * wk_b, vb)

    lax.fori_loop(0, n_chunks, bwd, jnp.zeros((d, d), F32))


def _rope_tables(seq):
    d = RET_HEAD_DIM
    inv = 1.0 / (ROPE_BASE ** (jnp.arange(0, d, 2, dtype=F32) / d))
    ang = jnp.arange(seq, dtype=F32)[:, None] * inv[None, :]
    c, s = jnp.cos(ang), jnp.sin(ang)
    return jnp.concatenate([c, c], axis=-1), jnp.concatenate([-s, s], axis=-1)


def _retention(u, lg_f, lg_b, batch, seq):
    cc, ss = _rope_tables(seq)
    first = N_HY_COLS // LANES
    heads = N_RET_HEADS

    def col(off):
        return pl.BlockSpec((seq, LANES), lambda b, h, *_: (b, first + off * heads + h))

    grid_spec = pltpu.PrefetchScalarGridSpec(
        num_scalar_prefetch=2,
        grid=(batch, heads),
        in_specs=[col(0), col(1), col(2), col(3),
                  pl.BlockSpec((seq, LANES), lambda b, h, *_: (0, 0), pipeline_mode=pl.Buffered(1)),
                  pl.BlockSpec((seq, LANES), lambda b, h, *_: (0, 0), pipeline_mode=pl.Buffered(1))],
        out_specs=pl.BlockSpec((seq, LANES), lambda b, h, *_: (b, h)),
    )
    return pl.pallas_call(
        functools.partial(_ret_kernel, seq=seq),
        grid_spec=grid_spec,
        out_shape=jax.ShapeDtypeStruct((batch * seq, D_RET), F32),
        compiler_params=pltpu.CompilerParams(
            dimension_semantics=("parallel", "parallel"), vmem_limit_bytes=VMEM_LIMIT),
        name="retention",
    )(lg_f, lg_b, u, u, u, u, cc, ss)


def _hdot(a, b):
    return jnp.dot(a, b, precision=lax.Precision.HIGHEST, preferred_element_type=F32)


@functools.lru_cache(maxsize=None)
def _fft_tables(seq):
    n = 2 * seq
    n2 = FFT_N2
    n1 = n // n2
    h1 = n1 // 2
    f1 = np.arange(h1, dtype=np.float64) + 0.5
    th = 2.0 * np.pi * f1[:, None] * np.arange(n1, dtype=np.float64)[None, :] / n1
    m1_full = np.concatenate([np.cos(th), -np.sin(th)], axis=0)
    m1 = m1_full[:, :h1]
    m1_inv = (2.0 / n) * m1.T
    t2 = np.arange(n2, dtype=np.float64)
    phi = 2.0 * np.pi * (np.arange(n2, dtype=np.float64)[None, :, None] * t2[None, None, :] / n2
                         + f1[:, None, None] * t2[None, None, :] / n)
    gr, gi = np.cos(phi), -np.sin(phi)
    m2 = np.concatenate([np.concatenate([gr, -gi], axis=2),
                         np.concatenate([gi, gr], axis=2)], axis=1)
    m2_inv = np.transpose(m2, (0, 2, 1))
    return dict(n1=n1, h1=h1, m1_full=m1_full, m1=m1, m1_inv=m1_inv, m2=m2, m2_inv=m2_inv)


def _short_conv_kernel(u_ref, w_ref, b_ref, o_ref, *, seq):
    x = u_ref[...]
    rows = lax.broadcasted_iota(jnp.int32, x.shape, 0)
    prev = jnp.where(rows == 0, 0.0, pltpu.roll(x, 1, axis=0))
    nxt = jnp.where(rows == seq - 1, 0.0, pltpu.roll(x, seq - 1, axis=0))
    o_ref[...] = prev * w_ref[0:1, :] + x * w_ref[1:2, :] + nxt * w_ref[2:3, :] + b_ref[...]


def _short_conv(u, w, b, batch, seq):
    nblk = N_HY_COLS // LANES
    return pl.pallas_call(
        functools.partial(_short_conv_kernel, seq=seq),
        grid=(batch, nblk),
        in_specs=[pl.BlockSpec((seq, LANES), lambda i, j: (i, j)),
                  pl.BlockSpec((3, LANES), lambda i, j: (0, j)),
                  pl.BlockSpec((1, LANES), lambda i, j: (0, j))],
        out_specs=pl.BlockSpec((seq, LANES), lambda i, j: (i, j)),
        out_shape=jax.ShapeDtypeStruct((batch * seq, N_HY_COLS), F32),
        compiler_params=pltpu.CompilerParams(
            dimension_semantics=("parallel", "parallel"), vmem_limit_bytes=VMEM_LIMIT),
        name="short_conv",
    )(u, w, b)


FILT_ROWS = 512
A_PITCH = 2 * FFT_N2 + 8
T_PITCH = FFT_N2 + 8


def _filter_kernel(z_ref, w1_ref, b1_ref, w2_ref, b2_ref, w3_ref, b3_ref, fr_ref,
                   w4_ref, dl_ref, o_ref, sum_ref, *, nblk):
    i = pl.program_id(0)
    fr = fr_ref[...]
    z = z_ref[...]
    h = jnp.sin(fr * (_hdot(z, w1_ref[...]) + b1_ref[...]))
    h = jnp.sin(fr * (_hdot(h, w2_ref[...]) + b2_ref[...]))
    h = jnp.sin(fr * (_hdot(h, w3_ref[...]) + b3_ref[...]))
    sel = (lax.broadcasted_iota(jnp.int32, (LANES, LANES), 0) == 0).astype(F32)
    t = _hdot(z, sel)
    adl = jnp.abs(dl_ref[...])
    backward = i >= nblk
    for c in range(HYENA_ORDER * D_HYENA // LANES):
        cols = slice(c * LANES, (c + 1) * LANES)
        hc = _hdot(h, w4_ref[0, :, cols]) * jnp.exp(-t * adl[:, cols])
        part = jnp.sum(jnp.abs(hc), axis=0, keepdims=True)

        @pl.when(i % nblk == 0)
        def _():
            sum_ref[0, :, cols] = part

        @pl.when(i % nblk != 0)
        def _():
            sum_ref[0, :, cols] = sum_ref[0, :, cols] + part

        rows = lax.broadcasted_iota(jnp.int32, hc.shape, 0)
        flipped = jnp.where(jnp.logical_and(rows == 0, i == nblk), 0.0, -hc)
        o_ref[:, cols] = jnp.where(backward, flipped, hc)


def _kspec_kernel(k_ref, s_ref, m1_ref, m2_ref, o_ref, a_ref, kp_ref, *, n1):
    n2 = FFT_N2
    h1 = n1 // 2
    inv_f = 1.0 / s_ref[0]
    inv_b = 1.0 / s_ref[1]

    def pad(t1, carry):
        src = pl.multiple_of(t1 * n2, n2)
        dst = pl.multiple_of(t1 * T_PITCH, 8)
        kp_ref[pl.ds(dst, n2), :] = k_ref[pl.ds(src, n2), :] * jnp.where(t1 < h1, inv_f, inv_b)
        return carry

    lax.fori_loop(0, n1, pad, 0, unroll=8)

    def stage1(t2, carry):
        kt = kp_ref[pl.ds(t2, n1, stride=T_PITCH), :]
        a = _hdot(m1_ref[...], kt)
        a_ref[pl.ds(t2, h1, stride=A_PITCH), :] = a[:h1]
        a_ref[pl.ds(t2 + n2, h1, stride=A_PITCH), :] = a[h1:]
        return carry

    lax.fori_loop(0, n2, stage1, 0)

    def stage2(f1, carry):
        src = pl.multiple_of(f1 * A_PITCH, 8)
        dst = pl.multiple_of(f1 * 2 * n2, 2 * n2)
        o_ref[pl.ds(dst, 2 * n2), :] = _hdot(m2_ref[f1], a_ref[pl.ds(src, 2 * n2), :])
        return carry

    lax.fori_loop(0, h1, stage2, 0, unroll=8)


def _filter_spectra(seq, p):
    n = 2 * seq
    tb = _fft_tables(seq)
    t = jnp.linspace(0.0, 1.0, seq, dtype=F32)[:, None]
    w = 2.0 * math.pi * jnp.arange(seq, dtype=F32)[:, None] / seq
    fb = jnp.linspace(1e-4, FILT_BANDS - 1, FILT_BANDS, dtype=F32)[None, :]
    z = jnp.concatenate([t, jnp.cos(fb * w), -jnp.sin(fb * w)], axis=-1)
    zf = jnp.pad(z, ((0, 0), (0, LANES - FILT_EMB)))
    zb = jnp.roll(zf[::-1], 1, axis=0)
    zz = jnp.concatenate([zf, zb], axis=0)
    pad_h = LANES - FILT_HIDDEN

    def padw(a, rows):
        return jnp.pad(a, ((0, rows - a.shape[0]), (0, pad_h)))

    def padv(a):
        return jnp.pad(a, (0, pad_h))[None, :]

    w1 = padw(p['filt_w1'], LANES)
    w2 = padw(p['filt_w2'], LANES)
    w3 = padw(p['filt_w3'], LANES)
    w4 = jnp.pad(p['filt_w4'], ((0, pad_h), (0, 0))).reshape(LANES, HYENA_ORDER, 2, D_HYENA)
    w4 = jnp.transpose(w4, (2, 0, 1, 3)).reshape(2, LANES, HYENA_ORDER * D_HYENA)
    min_decay = math.log(HYENA_TARGET) / SLOW_DECAY_PCT
    max_decay = math.log(HYENA_TARGET) / FAST_DECAY_PCT
    deltas = jnp.linspace(min_decay, max_decay, D_HYENA, dtype=F32)
    deltas = jnp.tile(deltas, HYENA_ORDER)[None, :]
    width = HYENA_ORDER * D_HYENA
    nblk = seq // FILT_ROWS
    sq = _const_spec((LANES, LANES))
    vec = _const_spec((1, LANES))
    filt, sums = pl.pallas_call(
        functools.partial(_filter_kernel, nblk=nblk),
        grid=(2 * nblk,),
        in_specs=[pl.BlockSpec((FILT_ROWS, LANES), lambda i: (i, 0)),
                  sq, vec, sq, vec, sq, vec, vec,
                  pl.BlockSpec((1, LANES, width), lambda i: (i // nblk, 0, 0)),
                  _const_spec((1, width))],
        out_specs=[pl.BlockSpec((FILT_ROWS, width), lambda i: (i, 0)),
                   pl.BlockSpec((1, 1, width), lambda i: (i // nblk, 0, 0))],
        out_shape=[jax.ShapeDtypeStruct((n, width), F32),
                   jax.ShapeDtypeStruct((2, 1, width), F32)],
        compiler_params=pltpu.CompilerParams(
            dimension_semantics=("arbitrary",), vmem_limit_bytes=VMEM_LIMIT),
        name="hyena_filter",
    )(zz, w1, padv(p['filt_b1']), w2, padv(p['filt_b2']), w3, padv(p['filt_b3']),
      padv(p['filt_freq']), w4, deltas)
    n1, h1 = tb['n1'], tb['h1']
    return pl.pallas_call(
        functools.partial(_kspec_kernel, n1=n1),
        grid=(width // LANES,),
        in_specs=[pl.BlockSpec((n, LANES), lambda j: (0, j), pipeline_mode=pl.Buffered(1)),
                  pl.BlockSpec((2, 1, LANES), lambda j: (0, 0, j)),
                  _const_spec((n1, n1)), _const_spec((h1, 2 * FFT_N2, 2 * FFT_N2))],
        out_specs=pl.BlockSpec((n, LANES), lambda j: (0, j)),
        out_shape=jax.ShapeDtypeStruct((n, width), F32),
        scratch_shapes=[pltpu.VMEM((h1 * A_PITCH, LANES), F32),
                        pltpu.VMEM((n1 * T_PITCH, LANES), F32)],
        compiler_params=pltpu.CompilerParams(
            dimension_semantics=("parallel",), vmem_limit_bytes=VMEM_LIMIT),
        name="hyena_filter_spectrum",
    )(filt, sums, jnp.asarray(tb['m1_full'], F32), jnp.asarray(tb['m2'], F32))


def _long_conv_kernel(z_ref, g_ref, k_ref, bias_ref, m1_ref, m1i_ref, m2_ref, m2i_ref,
                      o_ref, a_ref, tp_ref, *, n1):
    n2 = FFT_N2
    h1 = n1 // 2

    def pad(t1, carry):
        src = pl.multiple_of(t1 * n2, n2)
        dst = pl.multiple_of(t1 * T_PITCH, 8)
        tp_ref[pl.ds(dst, n2), :] = z_ref[pl.ds(src, n2), :]
        return carry

    lax.fori_loop(0, h1, pad, 0, unroll=8)

    def stage1(t2, carry):
        zt = tp_ref[pl.ds(t2, h1, stride=T_PITCH), :]
        a = _bdot(m1_ref[...], zt.astype(BF16))
        a_ref[pl.ds(t2, h1, stride=A_PITCH), :] = a[:h1]
        a_ref[pl.ds(t2 + n2, h1, stride=A_PITCH), :] = a[h1:]
        return carry

    lax.fori_loop(0, n2, stage1, 0, unroll=4)

    def stage2(f1, carry):
        ra = pl.multiple_of(f1 * A_PITCH, 8)
        rk = pl.multiple_of(f1 * 2 * n2, 2 * n2)
        x = _bdot(m2_ref[f1], a_ref[pl.ds(ra, 2 * n2), :].astype(BF16))
        kk = k_ref[pl.ds(rk, 2 * n2), :]
        xr, xi, kr, ki = x[:n2], x[n2:], kk[:n2], kk[n2:]
        y = jnp.concatenate([xr * kr - xi * ki, xr * ki + xi * kr], axis=0)
        a_ref[pl.ds(ra, 2 * n2), :] = _bdot(m2i_ref[f1], y.astype(BF16))
        return carry

    lax.fori_loop(0, h1, stage2, 0, unroll=16)

    def stage3(t2, carry):
        br = a_ref[pl.ds(t2, h1, stride=A_PITCH), :]
        bi = a_ref[pl.ds(t2 + n2, h1, stride=A_PITCH), :]
        y = _bdot(m1i_ref[...], jnp.concatenate([br, bi], axis=0).astype(BF16))
        tp_ref[pl.ds(t2, h1, stride=T_PITCH), :] = y
        return carry

    lax.fori_loop(0, n2, stage3, 0, unroll=4)

    bias = bias_ref[...]

    def finish(t1, carry):
        dst = pl.multiple_of(t1 * n2, n2)
        src = pl.multiple_of(t1 * T_PITCH, 8)
        y = tp_ref[pl.ds(src, n2), :]
        o_ref[pl.ds(dst, n2), :] = g_ref[pl.ds(dst, n2), :] * (y + bias * z_ref[pl.ds(dst, n2), :])
        return carry

    lax.fori_loop(0, h1, finish, 0, unroll=8)


def _bf16_const(a):
    return jnp.asarray(a, F32).astype(BF16)


def _long_conv(z, z_off, gate, g_off, kspec, k_off, bias, batch, seq):
    n = 2 * seq
    tb = _fft_tables(seq)
    n1, h1 = tb['n1'], tb['h1']
    tiles = D_HYENA // LANES
    return pl.pallas_call(
        functools.partial(_long_conv_kernel, n1=n1),
        grid=(tiles, batch),
        in_specs=[pl.BlockSpec((seq, LANES), lambda j, b: (b, z_off + j)),
                  pl.BlockSpec((seq, LANES), lambda j, b: (b, g_off + j)),
                  pl.BlockSpec((n, LANES), lambda j, b: (0, k_off + j), pipeline_mode=pl.Buffered(1)),
                  pl.BlockSpec((1, LANES), lambda j, b: (0, j)),
                  _const_spec((n1, h1)), _const_spec((h1, n1)),
                  _const_spec((h1, 2 * FFT_N2, 2 * FFT_N2)),
                  _const_spec((h1, 2 * FFT_N2, 2 * FFT_N2))],
        out_specs=pl.BlockSpec((seq, LANES), lambda j, b: (b, j)),
        out_shape=jax.ShapeDtypeStruct((batch * seq, D_HYENA), F32),
        scratch_shapes=[pltpu.VMEM((h1 * A_PITCH, LANES), F32),
                        pltpu.VMEM((h1 * T_PITCH, LANES), F32)],
        compiler_params=pltpu.CompilerParams(
            dimension_semantics=("parallel", "parallel"), vmem_limit_bytes=VMEM_LIMIT),
        name="hyena_long_conv",
    )(z, gate, kspec, bias, _bf16_const(tb['m1']), _bf16_const(tb['m1_inv']),
      _bf16_const(tb['m2']), _bf16_const(tb['m2_inv']))


def _layer(x3, p):
    batch, seq, _ = x3.shape
    x = x3.reshape(batch * seq, D_MODEL)
    x = _ffn(x, p['ffn1_pre_g'], p['ffn1_w1'], p['ffn1_w3'], p['ffn1_w2'], p['ffn1_post_g'])
    u = _inproj(x, p['mix_pre_g'], p['w_in'])
    sc = _short_conv(u, p['short_w'], p['short_b'], batch, seq)
    kspec = _filter_spectra(seq, p)
    tiles = D_HYENA // LANES
    z1 = _long_conv(sc, 0, sc, tiles, kspec, 0, p['hyena_bias'][0:1], batch, seq)
    yh = _long_conv(z1, 0, sc, 2 * tiles, kspec, tiles, p['hyena_bias'][1:2], batch, seq)
    yr = _retention(u, p['ret_log_decay_f'], p['ret_log_decay_b'], batch, seq)
    mix = (yh, yr, p['w_out'][:D_HYENA], p['w_out'][D_HYENA:], p['mix_post_g'])
    x = _ffn(x, p['ffn2_pre_g'], p['ffn2_w1'], p['ffn2_w3'], p['ffn2_w2'], p['ffn2_post_g'], mix=mix)
    return x.reshape(batch, seq, D_MODEL)


_MATRIX_PARAMS = ('ffn1_w1', 'ffn1_w3', 'ffn1_w2', 'w_in', 'w_out', 'ffn2_w1', 'ffn2_w3', 'ffn2_w2')
_GAIN_PARAMS = ('ffn1_pre_g', 'ffn1_post_g', 'mix_pre_g', 'mix_post_g', 'ffn2_pre_g', 'ffn2_post_g')


def kernel(x_prompt, x_sample, ffn1_pre_g, ffn1_w1, ffn1_w3, ffn1_w2, ffn1_post_g, mix_pre_g, w_in, short_w, short_b, filt_w1, filt_b1, filt_w2, filt_b2, filt_w3, filt_b3, filt_w4, filt_freq, hyena_bias, ret_log_decay_f, ret_log_decay_b, w_out, mix_post_g, ffn2_pre_g, ffn2_w1, ffn2_w3, ffn2_w2, ffn2_post_g):
    params = dict(ffn1_pre_g=ffn1_pre_g, ffn1_w1=ffn1_w1, ffn1_w3=ffn1_w3, ffn1_w2=ffn1_w2,
                  ffn1_post_g=ffn1_post_g, mix_pre_g=mix_pre_g, w_in=w_in, short_w=short_w,
                  short_b=short_b, filt_w1=filt_w1, filt_b1=filt_b1, filt_w2=filt_w2,
                  filt_b2=filt_b2, filt_w3=filt_w3, filt_b3=filt_b3, filt_w4=filt_w4,
                  filt_freq=filt_freq, hyena_bias=hyena_bias, ret_log_decay_f=ret_log_decay_f,
                  ret_log_decay_b=ret_log_decay_b, w_out=w_out, mix_post_g=mix_post_g,
                  ffn2_pre_g=ffn2_pre_g, ffn2_w1=ffn2_w1, ffn2_w3=ffn2_w3, ffn2_w2=ffn2_w2,
                  ffn2_post_g=ffn2_post_g)
    depth = ffn1_w1.shape[0]

    def run(x):
        for l in range(depth):
            p = {k: v[l] for k, v in params.items()}
            for k in _MATRIX_PARAMS:
                p[k] = p[k].astype(BF16)
            for k in _GAIN_PARAMS:
                p[k] = p[k][None, :]
            p['short_b'] = p['short_b'][None, :]
            x = _layer(x, p)
        return x

    return (run(x_prompt), run(x_sample))
```

```python
import functools
import math

import numpy as np
import jax
import jax.numpy as jnp
from jax import lax
from jax.experimental import pallas as pl
from jax.experimental.pallas import tpu as pltpu

F32 = jnp.float32
BF16 = jnp.bfloat16

D_MODEL = 1024
D_HYENA = 512
D_RET = 512
HYENA_ORDER = 2
N_RET_HEADS = 4
RET_HEAD_DIM = 128
D_FF = 2816
FILT_EMB = 33
FILT_BANDS = 16
FILT_HIDDEN = 64
ROPE_BASE = 10000.0
NORM_EPS = 1e-6
HYENA_TARGET = 1e-2
FAST_DECAY_PCT = 0.3
SLOW_DECAY_PCT = 1.5
N_HY_COLS = (HYENA_ORDER + 1) * D_HYENA
D_IN = N_HY_COLS + 4 * D_RET

LANES = 128
VMEM_LIMIT = 56 * 1024 * 1024
FFT_N2 = 32
RET_CHUNK = 128
RET_GROUP = 8
TOKEN_TILE = 512


def _const_spec(shape):
    nd = len(shape)
    return pl.BlockSpec(shape, lambda *_: (0,) * nd, pipeline_mode=pl.Buffered(1))


def _rms(x, g):
    ms = jnp.mean(x * x, axis=-1, keepdims=True)
    return x * lax.rsqrt(ms + NORM_EPS) * g


def _bdot(a, b):
    return jnp.dot(a, b, preferred_element_type=F32)


def _ffn_core(x, pre_ref, w1_ref, w3_ref, w2_ref, post_ref, o_ref):
    h = _rms(x, pre_ref[...]).astype(BF16)
    a = _bdot(h, w1_ref[...])
    b = _bdot(h, w3_ref[...])
    g = (a * jax.nn.sigmoid(a) * b).astype(BF16)
    y = _bdot(g, w2_ref[...])
    o_ref[...] = x + 0.5 * _rms(y, post_ref[...])


def _ffn_kernel(x_ref, pre_ref, w1_ref, w3_ref, w2_ref, post_ref, o_ref):
    _ffn_core(x_ref[...], pre_ref, w1_ref, w3_ref, w2_ref, post_ref, o_ref)


def _mix_ffn_kernel(x_ref, yh_ref, yr_ref, woh_ref, wor_ref, mg_ref,
                    pre_ref, w1_ref, w3_ref, w2_ref, post_ref, o_ref):
    y = _bdot(yh_ref[...].astype(BF16), woh_ref[...])
    y = y + _bdot(yr_ref[...].astype(BF16), wor_ref[...])
    x = x_ref[...] + _rms(y, mg_ref[...])
    _ffn_core(x, pre_ref, w1_ref, w3_ref, w2_ref, post_ref, o_ref)


def _row_spec(tm, width):
    return pl.BlockSpec((tm, width), lambda i: (i, 0))


def _ffn(x, pre_g, w1, w3, w2, post_g, mix=None):
    t = x.shape[0]
    tm = TOKEN_TILE
    ffn_specs = [_const_spec((1, D_MODEL)), _const_spec((D_MODEL, D_FF)),
                 _const_spec((D_MODEL, D_FF)), _const_spec((D_FF, D_MODEL)),
                 _const_spec((1, D_MODEL))]
    ffn_args = (pre_g, w1, w3, w2, post_g)
    if mix is None:
        body, args = _ffn_kernel, (x,) + ffn_args
        specs = [_row_spec(tm, D_MODEL)] + ffn_specs
    else:
        yh, yr, woh, wor, mg = mix
        body, args = _mix_ffn_kernel, (x, yh, yr, woh, wor, mg) + ffn_args
        specs = [_row_spec(tm, D_MODEL), _row_spec(tm, D_HYENA), _row_spec(tm, D_RET),
                 _const_spec((D_HYENA, D_MODEL)), _const_spec((D_RET, D_MODEL)),
                 _const_spec((1, D_MODEL))] + ffn_specs
    return pl.pallas_call(
        body,
        grid=(t // tm,),
        in_specs=specs,
        out_specs=_row_spec(tm, D_MODEL),
        out_shape=jax.ShapeDtypeStruct((t, D_MODEL), F32),
        compiler_params=pltpu.CompilerParams(
            dimension_semantics=("parallel",), vmem_limit_bytes=VMEM_LIMIT),
        name="ffn_mix" if mix is not None else "ffn",
    )(*args)


def _inproj_kernel(x_ref, g_ref, w_ref, cc_ref, ss_ref, o_ref):
    h = _rms(x_ref[...], g_ref[...]).astype(BF16)
    qk0 = N_HY_COLS
    vg0 = N_HY_COLS + 2 * D_RET
    o_ref[:, :qk0] = _bdot(h, w_ref[:, :qk0])
    o_ref[:, vg0:] = _bdot(h, w_ref[:, vg0:])
    qk = _bdot(h, w_ref[:, qk0:vg0])
    cc = cc_ref[...]
    ss = ss_ref[...]
    d = RET_HEAD_DIM
    for blk in range(2 * N_RET_HEADS):
        x = qk[:, blk * d:(blk + 1) * d]
        r = x * cc + pltpu.roll(x, d // 2, axis=1) * ss
        if blk >= N_RET_HEADS:
            r = r * (d ** -0.5)
        o_ref[:, qk0 + blk * d:qk0 + (blk + 1) * d] = r


def _inproj(x, g, w, seq):
    t = x.shape[0]
    tm = TOKEN_TILE
    cc, ss = _rope_tables(seq)
    pos_blocks = seq // tm
    rope_spec = pl.BlockSpec((tm, RET_HEAD_DIM), lambda i: (i % pos_blocks, 0))
    return pl.pallas_call(
        _inproj_kernel,
        grid=(t // tm,),
        in_specs=[_row_spec(tm, D_MODEL), _const_spec((1, D_MODEL)),
                  _const_spec((D_MODEL, D_IN)), rope_spec, rope_spec],
        out_specs=_row_spec(tm, D_IN),
        out_shape=jax.ShapeDtypeStruct((t, D_IN), F32),
        compiler_params=pltpu.CompilerParams(
            dimension_semantics=("parallel",), vmem_limit_bytes=VMEM_LIMIT),
        name="inproj",
    )(x, g, w, cc, ss)


def _ret_kernel(lgf_ref, lgb_ref, q_ref, k_ref, v_ref, g_ref, o_ref, qs_ref, kt_ref, *, seq):
    c = RET_CHUNK
    d = RET_HEAD_DIM
    n_chunks = seq // c
    head = pl.program_id(1)
    lgf = jnp.full((c, d), lgf_ref[head], F32)
    lgb = jnp.full((c, d), lgb_ref[head], F32)
    row = lax.broadcasted_iota(jnp.int32, (c, d), 0).astype(F32)
    col = lax.broadcasted_iota(jnp.int32, (c, d), 1).astype(F32)
    diff = row - col
    dmat = jnp.where(diff >= 0.0, jnp.exp(jnp.maximum(diff, 0.0) * lgf),
                     jnp.exp(jnp.maximum(-diff, 0.0) * lgb))
    wq_f = jnp.exp((row + 1.0) * lgf)
    wk_f = jnp.exp((c - 1.0 - row) * lgf)
    wq_b = jnp.exp((c - row) * lgb)
    wk_b = jnp.exp(row * lgb)
    gc_f = jnp.exp(c * lgf)
    gc_b = jnp.exp(c * lgb)

    grp = RET_GROUP
    n_groups = n_chunks // grp

    def fwd(gi, state):
        rows = [pl.ds(pl.multiple_of((gi * grp + j) * c, c), c) for j in range(grp)]
        qbs, kts, vs = [], [], []
        for r in rows:
            qb = q_ref[r, :].astype(BF16)
            ktb = k_ref[r, :].T.astype(BF16)
            qs_ref[r, :] = qb
            kt_ref[r, :] = ktb
            qbs.append(qb)
            kts.append(ktb)
            vs.append(v_ref[r, :])
        scores = [_bdot(qb, ktb) for qb, ktb in zip(qbs, kts)]
        kvs = [_bdot(ktb, (v * wk_f).astype(BF16)) for ktb, v in zip(kts, vs)]
        states = []
        for kv in kvs:
            states.append(state)
            state = state * gc_f + kv
        intras = [_bdot((s * dmat).astype(BF16), v.astype(BF16)) for s, v in zip(scores, vs)]
        crosses = [_bdot(qb, st.astype(BF16)) for qb, st in zip(qbs, states)]
        for r, intra, cross in zip(rows, intras, crosses):
            o_ref[r, :] = intra + wq_f * cross
        return state

    lax.fori_loop(0, n_groups, fwd, jnp.zeros((d, d), F32))

    def bwd(gi, state):
        rows = [pl.ds(pl.multiple_of((n_chunks - 1 - gi * grp - j) * c, c), c) for j in range(grp)]
        kvs = [_bdot(kt_ref[r, :], (v_ref[r, :] * wk_b).astype(BF16)) for r in rows]
        states = []
        for kv in kvs:
            states.append(state)
            state = state * gc_b + kv
        crosses = [_bdot(qs_ref[r, :], st.astype(BF16)) for r, st in zip(rows, states)]
        outs = [o_ref[r, :] + wq_b * cross for r, cross in zip(rows, crosses)]
        means = [jnp.mean(o * o, axis=-1, keepdims=True) for o in outs]
        norms = [lax.rsqrt(m + NORM_EPS) for m in means]
        gates = [g_ref[r, :] for r in rows]
        gates = [g * jax.nn.sigmoid(g) for g in gates]
        for r, o, nrm, g in zip(rows, outs, norms, gates):
            o_ref[r, :] = g * (o * nrm)
        return state

    lax.fori_loop(0, n_groups, bwd, jnp.zeros((d, d), F32))


def _rope_tables(seq):
    d = RET_HEAD_DIM
    inv = 1.0 / (ROPE_BASE ** (jnp.arange(0, d, 2, dtype=F32) / d))
    ang = jnp.arange(seq, dtype=F32)[:, None] * inv[None, :]
    c, s = jnp.cos(ang), jnp.sin(ang)
    return jnp.concatenate([c, c], axis=-1), jnp.concatenate([-s, s], axis=-1)


def _retention(u, lg_f, lg_b, batch, seq):
    assert RET_CHUNK == RET_HEAD_DIM
    first = N_HY_COLS // LANES
    heads = N_RET_HEADS

    def col(off):
        return pl.BlockSpec((seq, LANES), lambda b, h, *_: (b, first + off * heads + h))

    grid_spec = pltpu.PrefetchScalarGridSpec(
        num_scalar_prefetch=2,
        grid=(batch, heads),
        in_specs=[col(0), col(1), col(2), col(3)],
        out_specs=pl.BlockSpec((seq, LANES), lambda b, h, *_: (b, h)),
        scratch_shapes=[pltpu.VMEM((seq, RET_HEAD_DIM), BF16), pltpu.VMEM((seq, RET_CHUNK), BF16)],
    )
    return pl.pallas_call(
        functools.partial(_ret_kernel, seq=seq),
        grid_spec=grid_spec,
        out_shape=jax.ShapeDtypeStruct((batch * seq, D_RET), F32),
        compiler_params=pltpu.CompilerParams(
            dimension_semantics=("parallel", "parallel"), vmem_limit_bytes=VMEM_LIMIT),
        name="retention",
    )(lg_f, lg_b, u, u, u, u)


def _split(x):
    hi = x.astype(BF16)
    return hi, (x - hi.astype(F32)).astype(BF16)


def _dot3(a, b):
    a_hi, a_lo = a
    b_hi, b_lo = b
    return _bdot(a_hi, b_hi) + _bdot(a_hi, b_lo) + _bdot(a_lo, b_hi)


@functools.lru_cache(maxsize=None)
def _fft_tables(seq):
    n = 2 * seq
    n2 = FFT_N2
    n1 = n // n2
    h1 = n1 // 2
    f1 = np.arange(h1, dtype=np.float64) + 0.5
    th = 2.0 * np.pi * f1[:, None] * np.arange(n1, dtype=np.float64)[None, :] / n1
    m1_full = np.concatenate([np.cos(th), -np.sin(th)], axis=0)
    m1 = m1_full[:, :h1]
    m1_inv = (2.0 / n) * m1.T
    t2 = np.arange(n2, dtype=np.float64)
    phi = 2.0 * np.pi * (np.arange(n2, dtype=np.float64)[None, :, None] * t2[None, None, :] / n2
                         + f1[:, None, None] * t2[None, None, :] / n)
    gr, gi = np.cos(phi), -np.sin(phi)
    m2 = np.concatenate([np.concatenate([gr, -gi], axis=2),
                         np.concatenate([gi, gr], axis=2)], axis=1)
    m2_inv = np.transpose(m2, (0, 2, 1))
    return dict(n1=n1, h1=h1, m1_full=m1_full, m1=m1, m1_inv=m1_inv, m2=m2, m2_inv=m2_inv)


def _short_conv_kernel(u_ref, w_ref, b_ref, o_ref, *, seq):
    x = u_ref[...]
    rows = lax.broadcasted_iota(jnp.int32, x.shape, 0)
    prev = jnp.where(rows == 0, 0.0, pltpu.roll(x, 1, axis=0))
    nxt = jnp.where(rows == seq - 1, 0.0, pltpu.roll(x, seq - 1, axis=0))
    o_ref[...] = prev * w_ref[0:1, :] + x * w_ref[1:2, :] + nxt * w_ref[2:3, :] + b_ref[...]


def _short_conv(u, w, b, batch, seq):
    nblk = N_HY_COLS // LANES
    return pl.pallas_call(
        functools.partial(_short_conv_kernel, seq=seq),
        grid=(batch, nblk),
        in_specs=[pl.BlockSpec((seq, LANES), lambda i, j: (i, j)),
                  pl.BlockSpec((3, LANES), lambda i, j: (0, j)),
                  pl.BlockSpec((1, LANES), lambda i, j: (0, j))],
        out_specs=pl.BlockSpec((seq, LANES), lambda i, j: (i, j)),
        out_shape=jax.ShapeDtypeStruct((batch * seq, N_HY_COLS), F32),
        compiler_params=pltpu.CompilerParams(
            dimension_semantics=("parallel", "parallel"), vmem_limit_bytes=VMEM_LIMIT),
        name="short_conv",
    )(u, w, b)


FILT_ROWS = 512
A_PITCH = 2 * FFT_N2 + 8
T_PITCH = FFT_N2 + 8


def _filter_kernel(z_ref, w1_ref, b1_ref, w2_ref, b2_ref, w3_ref, b3_ref, fr_ref,
                   w4_ref, dl_ref, o_ref, sum_ref, *, nblk):
    i = pl.program_id(0)
    fr = fr_ref[...]
    z = z_ref[...]
    h = jnp.sin(fr * (_dot3(_split(z), _split(w1_ref[...])) + b1_ref[...]))
    h = jnp.sin(fr * (_dot3(_split(h), _split(w2_ref[...])) + b2_ref[...]))
    h = jnp.sin(fr * (_dot3(_split(h), _split(w3_ref[...])) + b3_ref[...]))
    hs = _split(h)
    t = jnp.broadcast_to(z[:, 0:1], z.shape)
    adl = jnp.abs(dl_ref[...])
    backward = i >= nblk
    for c in range(HYENA_ORDER * D_HYENA // LANES):
        cols = slice(c * LANES, (c + 1) * LANES)
        hc = _dot3(hs, _split(w4_ref[0, :, cols])) * jnp.exp(-t * adl[:, cols])
        part = jnp.sum(jnp.abs(hc), axis=0, keepdims=True)

        @pl.when(i % nblk == 0)
        def _():
            sum_ref[0, :, cols] = part

        @pl.when(i % nblk != 0)
        def _():
            sum_ref[0, :, cols] = sum_ref[0, :, cols] + part

        rows = lax.broadcasted_iota(jnp.int32, hc.shape, 0)
        flipped = jnp.where(jnp.logical_and(rows == 0, i == nblk), 0.0, -hc)
        o_ref[:, cols] = jnp.where(backward, flipped, hc)


def _kspec_kernel(k_ref, s_ref, m1h_ref, m1l_ref, m2h_ref, m2l_ref, o_ref, a_ref, kp_ref, *, n1):
    n2 = FFT_N2
    h1 = n1 // 2
    inv_f = 1.0 / s_ref[0]
    inv_b = 1.0 / s_ref[1]

    def pad(t1, carry):
        src = pl.multiple_of(t1 * n2, n2)
        dst = pl.multiple_of(t1 * T_PITCH, 8)
        kp_ref[pl.ds(dst, n2), :] = k_ref[pl.ds(src, n2), :] * jnp.where(t1 < h1, inv_f, inv_b)
        return carry

    lax.fori_loop(0, n1, pad, 0, unroll=8)

    def stage1(t2, carry):
        kt = kp_ref[pl.ds(t2, n1, stride=T_PITCH), :]
        a = _dot3((m1h_ref[...], m1l_ref[...]), _split(kt))
        a_ref[pl.ds(t2, h1, stride=A_PITCH), :] = a[:h1]
        a_ref[pl.ds(t2 + n2, h1, stride=A_PITCH), :] = a[h1:]
        return carry

    lax.fori_loop(0, n2, stage1, 0)

    def stage2(f1, carry):
        src = pl.multiple_of(f1 * A_PITCH, 8)
        dst = pl.multiple_of(f1 * 2 * n2, 2 * n2)
        o_ref[pl.ds(dst, 2 * n2), :] = _dot3((m2h_ref[f1], m2l_ref[f1]),
                                             _split(a_ref[pl.ds(src, 2 * n2), :]))
        return carry

    lax.fori_loop(0, h1, stage2, 0, unroll=8)


def _filter_spectra(seq, p):
    n = 2 * seq
    tb = _fft_tables(seq)
    t = jnp.linspace(0.0, 1.0, seq, dtype=F32)[:, None]
    w = 2.0 * math.pi * jnp.arange(seq, dtype=F32)[:, None] / seq
    fb = jnp.linspace(1e-4, FILT_BANDS - 1, FILT_BANDS, dtype=F32)[None, :]
    z = jnp.concatenate([t, jnp.cos(fb * w), -jnp.sin(fb * w)], axis=-1)
    zf = jnp.pad(z, ((0, 0), (0, LANES - FILT_EMB)))
    zb = jnp.roll(zf[::-1], 1, axis=0)
    zz = jnp.concatenate([zf, zb], axis=0)
    pad_h = LANES - FILT_HIDDEN

    def padw(a, rows):
        return jnp.pad(a, ((0, rows - a.shape[0]), (0, pad_h)))

    def padv(a):
        return jnp.pad(a, (0, pad_h))[None, :]

    w1 = padw(p['filt_w1'], LANES)
    w2 = padw(p['filt_w2'], LANES)
    w3 = padw(p['filt_w3'], LANES)
    w4 = jnp.pad(p['filt_w4'], ((0, pad_h), (0, 0))).reshape(LANES, HYENA_ORDER, 2, D_HYENA)
    w4 = jnp.transpose(w4, (2, 0, 1, 3)).reshape(2, LANES, HYENA_ORDER * D_HYENA)
    min_decay = math.log(HYENA_TARGET) / SLOW_DECAY_PCT
    max_decay = math.log(HYENA_TARGET) / FAST_DECAY_PCT
    deltas = jnp.linspace(min_decay, max_decay, D_HYENA, dtype=F32)
    deltas = jnp.tile(deltas, HYENA_ORDER)[None, :]
    width = HYENA_ORDER * D_HYENA
    nblk = seq // FILT_ROWS
    sq = _const_spec((LANES, LANES))
    vec = _const_spec((1, LANES))
    filt, sums = pl.pallas_call(
        functools.partial(_filter_kernel, nblk=nblk),
        grid=(2 * nblk,),
        in_specs=[pl.BlockSpec((FILT_ROWS, LANES), lambda i: (i, 0)),
                  sq, vec, sq, vec, sq, vec, vec,
                  pl.BlockSpec((1, LANES, width), lambda i: (i // nblk, 0, 0)),
                  _const_spec((1, width))],
        out_specs=[pl.BlockSpec((FILT_ROWS, width), lambda i: (i, 0)),
                   pl.BlockSpec((1, 1, width), lambda i: (i // nblk, 0, 0))],
        out_shape=[jax.ShapeDtypeStruct((n, width), F32),
                   jax.ShapeDtypeStruct((2, 1, width), F32)],
        compiler_params=pltpu.CompilerParams(
            dimension_semantics=("arbitrary",), vmem_limit_bytes=VMEM_LIMIT),
        name="hyena_filter",
    )(zz, w1, padv(p['filt_b1']), w2, padv(p['filt_b2']), w3, padv(p['filt_b3']),
      padv(p['filt_freq']), w4, deltas)
    n1, h1 = tb['n1'], tb['h1']
    m1h, m1l = _split(jnp.asarray(tb['m1_full'], F32))
    m2h, m2l = _split(jnp.asarray(tb['m2'], F32))
    m1_spec = _const_spec((n1, n1))
    m2_spec = _const_spec((h1, 2 * FFT_N2, 2 * FFT_N2))
    return pl.pallas_call(
        functools.partial(_kspec_kernel, n1=n1),
        grid=(width // LANES,),
        in_specs=[pl.BlockSpec((n, LANES), lambda j: (0, j), pipeline_mode=pl.Buffered(1)),
                  pl.BlockSpec((2, 1, LANES), lambda j: (0, 0, j)),
                  m1_spec, m1_spec, m2_spec, m2_spec],
        out_specs=pl.BlockSpec((n, LANES), lambda j: (0, j)),
        out_shape=jax.ShapeDtypeStruct((n, width), F32),
        scratch_shapes=[pltpu.VMEM((h1 * A_PITCH, LANES), F32),
                        pltpu.VMEM((n1 * T_PITCH, LANES), F32)],
        compiler_params=pltpu.CompilerParams(
            dimension_semantics=("parallel",), vmem_limit_bytes=VMEM_LIMIT),
        name="hyena_filter_spectrum",
    )(filt, sums, m1h, m1l, m2h, m2l)


def _long_conv_kernel(z_ref, g_ref, k_ref, bias_ref, m1_ref, m1i_ref, m2_ref, m2i_ref,
                      o_ref, a_ref, tp_ref, *, n1):
    n2 = FFT_N2
    h1 = n1 // 2

    def pad(t1, carry):
        src = pl.multiple_of(t1 * n2, n2)
        dst = pl.multiple_of(t1 * T_PITCH, 8)
        tp_ref[pl.ds(dst, n2), :] = z_ref[pl.ds(src, n2), :]
        return carry

    lax.fori_loop(0, h1, pad, 0, unroll=8)

    def stage1(t2, carry):
        zt = tp_ref[pl.ds(t2, h1, stride=T_PITCH), :]
        a = _bdot(m1_ref[...], zt.astype(BF16))
        a_ref[pl.ds(t2, h1, stride=A_PITCH), :] = a[:h1]
        a_ref[pl.ds(t2 + n2, h1, stride=A_PITCH), :] = a[h1:]
        return carry

    lax.fori_loop(0, n2, stage1, 0, unroll=4)

    def stage2(f1, carry):
        ra = pl.multiple_of(f1 * A_PITCH, 8)
        rk = pl.multiple_of(f1 * 2 * n2, 2 * n2)
        x = _bdot(m2_ref[f1], a_ref[pl.ds(ra, 2 * n2), :].astype(BF16))
        kk = k_ref[pl.ds(rk, 2 * n2), :]
        xr, xi, kr, ki = x[:n2], x[n2:], kk[:n2], kk[n2:]
        y = jnp.concatenate([xr * kr - xi * ki, xr * ki + xi * kr], axis=0)
        a_ref[pl.ds(ra, 2 * n2), :] = _bdot(m2i_ref[f1], y.astype(BF16))
        return carry

    lax.fori_loop(0, h1, stage2, 0, unroll=16)

    def stage3(t2, carry):
        br = a_ref[pl.ds(t2, h1, stride=A_PITCH), :]
        bi = a_ref[pl.ds(t2 + n2, h1, stride=A_PITCH), :]
        y = _bdot(m1i_ref[...], jnp.concatenate([br, bi], axis=0).astype(BF16))
        tp_ref[pl.ds(t2, h1, stride=T_PITCH), :] = y
        return carry

    lax.fori_loop(0, n2, stage3, 0, unroll=4)

    bias = bias_ref[...]

    def finish(t1, carry):
        dst = pl.multiple_of(t1 * n2, n2)
        src = pl.multiple_of(t1 * T_PITCH, 8)
        y = tp_ref[pl.ds(src, n2), :]
        o_ref[pl.ds(dst, n2), :] = g_ref[pl.ds(dst, n2), :] * (y + bias * z_ref[pl.ds(dst, n2), :])
        return carry

    lax.fori_loop(0, h1, finish, 0, unroll=8)


def _bf16_const(a):
    return jnp.asarray(a, F32).astype(BF16)


def _long_conv(z, z_off, gate, g_off, kspec, k_off, bias, batch, seq):
    n = 2 * seq
    tb = _fft_tables(seq)
    n1, h1 = tb['n1'], tb['h1']
    tiles = D_HYENA // LANES
    return pl.pallas_call(
        functools.partial(_long_conv_kernel, n1=n1),
        grid=(tiles, batch),
        in_specs=[pl.BlockSpec((seq, LANES), lambda j, b: (b, z_off + j)),
                  pl.BlockSpec((seq, LANES), lambda j, b: (b, g_off + j)),
                  pl.BlockSpec((n, LANES), lambda j, b: (0, k_off + j), pipeline_mode=pl.Buffered(1)),
                  pl.BlockSpec((1, LANES), lambda j, b: (0, j)),
                  _const_spec((n1, h1)), _const_spec((h1, n1)),
                  _const_spec((h1, 2 * FFT_N2, 2 * FFT_N2)),
                  _const_spec((h1, 2 * FFT_N2, 2 * FFT_N2))],
        out_specs=pl.BlockSpec((seq, LANES), lambda j, b: (b, j)),
        out_shape=jax.ShapeDtypeStruct((batch * seq, D_HYENA), F32),
        scratch_shapes=[pltpu.VMEM((h1 * A_PITCH, LANES), F32),
                        pltpu.VMEM((h1 * T_PITCH, LANES), F32)],
        compiler_params=pltpu.CompilerParams(
            dimension_semantics=("parallel", "parallel"), vmem_limit_bytes=VMEM_LIMIT),
        name="hyena_long_conv",
    )(z, gate, kspec, bias, _bf16_const(tb['m1']), _bf16_const(tb['m1_inv']),
      _bf16_const(tb['m2']), _bf16_const(tb['m2_inv']))


def _layer(x3, p):
    batch, seq, _ = x3.shape
    x = x3.reshape(batch * seq, D_MODEL)
    x = _ffn(x, p['ffn1_pre_g'], p['ffn1_w1'], p['ffn1_w3'], p['ffn1_w2'], p['ffn1_post_g'])
    u = _inproj(x, p['mix_pre_g'], p['w_in'], seq)
    sc = _short_conv(u, p['short_w'], p['short_b'], batch, seq)
    kspec = _filter_spectra(seq, p)
    tiles = D_HYENA // LANES
    z1 = _long_conv(sc, 0, sc, tiles, kspec, 0, p['hyena_bias'][0:1], batch, seq)
    yh = _long_conv(z1, 0, sc, 2 * tiles, kspec, tiles, p['hyena_bias'][1:2], batch, seq)
    yr = _retention(u, p['ret_log_decay_f'], p['ret_log_decay_b'], batch, seq)
    mix = (yh, yr, p['w_out'][:D_HYENA], p['w_out'][D_HYENA:], p['mix_post_g'])
    x = _ffn(x, p['ffn2_pre_g'], p['ffn2_w1'], p['ffn2_w3'], p['ffn2_w2'], p['ffn2_post_g'], mix=mix)
    return x.reshape(batch, seq, D_MODEL)


_MATRIX_PARAMS = ('ffn1_w1', 'ffn1_w3', 'ffn1_w2', 'w_in', 'w_out', 'ffn2_w1', 'ffn2_w3', 'ffn2_w2')
_GAIN_PARAMS = ('ffn1_pre_g', 'ffn1_post_g', 'mix_pre_g', 'mix_post_g', 'ffn2_pre_g', 'ffn2_post_g')


def kernel(x_prompt, x_sample, ffn1_pre_g, ffn1_w1, ffn1_w3, ffn1_w2, ffn1_post_g, mix_pre_g, w_in, short_w, short_b, filt_w1, filt_b1, filt_w2, filt_b2, filt_w3, filt_b3, filt_w4, filt_freq, hyena_bias, ret_log_decay_f, ret_log_decay_b, w_out, mix_post_g, ffn2_pre_g, ffn2_w1, ffn2_w3, ffn2_w2, ffn2_post_g):
    params = dict(ffn1_pre_g=ffn1_pre_g, ffn1_w1=ffn1_w1, ffn1_w3=ffn1_w3, ffn1_w2=ffn1_w2,
                  ffn1_post_g=ffn1_post_g, mix_pre_g=mix_pre_g, w_in=w_in, short_w=short_w,
                  short_b=short_b, filt_w1=filt_w1, filt_b1=filt_b1, filt_w2=filt_w2,
                  filt_b2=filt_b2, filt_w3=filt_w3, filt_b3=filt_b3, filt_w4=filt_w4,
                  filt_freq=filt_freq, hyena_bias=hyena_bias, ret_log_decay_f=ret_log_decay_f,
                  ret_log_decay_b=ret_log_decay_b, w_out=w_out, mix_post_g=mix_post_g,
                  ffn2_pre_g=ffn2_pre_g, ffn2_w1=ffn2_w1, ffn2_w3=ffn2_w3, ffn2_w2=ffn2_w2,
                  ffn2_post_g=ffn2_post_g)
    depth = ffn1_w1.shape[0]

    def run(x):
        for l in range(depth):
            p = {k: v[l] for k, v in params.items()}
            for k in _MATRIX_PARAMS:
                p[k] = p[k].astype(BF16)
            for k in _GAIN_PARAMS:
                p[k] = p[k][None, :]
            p['short_b'] = p['short_b'][None, :]
            x = _layer(x, p)
        return x

    return (run(x_prompt), run(x_sample))
```

```python
import functools
import math

import numpy as np
import jax
import jax.numpy as jnp
from jax import lax
from jax.experimental import pallas as pl
from jax.experimental.pallas import tpu as pltpu

F32 = jnp.float32
BF16 = jnp.bfloat16

D_MODEL = 1024
D_HYENA = 512
D_RET = 512
HYENA_ORDER = 2
N_RET_HEADS = 4
RET_HEAD_DIM = 128
D_FF = 2816
FILT_EMB = 33
FILT_BANDS = 16
FILT_HIDDEN = 64
ROPE_BASE = 10000.0
NORM_EPS = 1e-6
HYENA_TARGET = 1e-2
FAST_DECAY_PCT = 0.3
SLOW_DECAY_PCT = 1.5
N_HY_COLS = (HYENA_ORDER + 1) * D_HYENA
D_IN = N_HY_COLS + 4 * D_RET

LANES = 128
VMEM_LIMIT = 56 * 1024 * 1024
FFT_N2 = 32
RET_CHUNK = 128
RET_GROUP = 8
TOKEN_TILE = 512


def _const_spec(shape):
    nd = len(shape)
    return pl.BlockSpec(shape, lambda *_: (0,) * nd, pipeline_mode=pl.Buffered(1))


def _rms(x, g):
    ms = jnp.mean(x * x, axis=-1, keepdims=True)
    return x * lax.rsqrt(ms + NORM_EPS) * g


def _bdot(a, b):
    return jnp.dot(a, b, preferred_element_type=F32)


def _ffn_core(x, pre_ref, w1_ref, w3_ref, w2_ref, post_ref, o_ref):
    h = _rms(x, pre_ref[...]).astype(BF16)
    a = _bdot(h, w1_ref[...])
    b = _bdot(h, w3_ref[...])
    g = (a * jax.nn.sigmoid(a) * b).astype(BF16)
    y = _bdot(g, w2_ref[...])
    o_ref[...] = x + 0.5 * _rms(y, post_ref[...])


def _ffn_kernel(x_ref, pre_ref, w1_ref, w3_ref, w2_ref, post_ref, o_ref):
    _ffn_core(x_ref[...], pre_ref, w1_ref, w3_ref, w2_ref, post_ref, o_ref)


def _mix_ffn_kernel(x_ref, yh_ref, yr_ref, woh_ref, wor_ref, mg_ref,
                    pre_ref, w1_ref, w3_ref, w2_ref, post_ref, o_ref):
    y = _bdot(yh_ref[...].astype(BF16), woh_ref[...])
    y = y + _bdot(yr_ref[...].astype(BF16), wor_ref[...])
    x = x_ref[...] + _rms(y, mg_ref[...])
    _ffn_core(x, pre_ref, w1_ref, w3_ref, w2_ref, post_ref, o_ref)


def _row_spec(tm, width):
    return pl.BlockSpec((tm, width), lambda i: (i, 0))


def _ffn(x, pre_g, w1, w3, w2, post_g, mix=None):
    t = x.shape[0]
    tm = TOKEN_TILE
    ffn_specs = [_const_spec((1, D_MODEL)), _const_spec((D_MODEL, D_FF)),
                 _const_spec((D_MODEL, D_FF)), _const_spec((D_FF, D_MODEL)),
                 _const_spec((1, D_MODEL))]
    ffn_args = (pre_g, w1, w3, w2, post_g)
    if mix is None:
        body, args = _ffn_kernel, (x,) + ffn_args
        specs = [_row_spec(tm, D_MODEL)] + ffn_specs
    else:
        yh, yr, woh, wor, mg = mix
        body, args = _mix_ffn_kernel, (x, yh, yr, woh, wor, mg) + ffn_args
        specs = [_row_spec(tm, D_MODEL), _row_spec(tm, D_HYENA), _row_spec(tm, D_RET),
                 _const_spec((D_HYENA, D_MODEL)), _const_spec((D_RET, D_MODEL)),
                 _const_spec((1, D_MODEL))] + ffn_specs
    return pl.pallas_call(
        body,
        grid=(t // tm,),
        in_specs=specs,
        out_specs=_row_spec(tm, D_MODEL),
        out_shape=jax.ShapeDtypeStruct((t, D_MODEL), F32),
        compiler_params=pltpu.CompilerParams(
            dimension_semantics=("parallel",), vmem_limit_bytes=VMEM_LIMIT),
        name="ffn_mix" if mix is not None else "ffn",
    )(*args)


def _inproj_kernel(x_ref, g_ref, w_ref, cc_ref, ss_ref, o_ref):
    h = _rms(x_ref[...], g_ref[...]).astype(BF16)
    qk0 = N_HY_COLS
    vg0 = N_HY_COLS + 2 * D_RET
    o_ref[:, :qk0] = _bdot(h, w_ref[:, :qk0])
    o_ref[:, vg0:] = _bdot(h, w_ref[:, vg0:])
    qk = _bdot(h, w_ref[:, qk0:vg0])
    cc = cc_ref[...]
    ss = ss_ref[...]
    d = RET_HEAD_DIM
    for blk in range(2 * N_RET_HEADS):
        x = qk[:, blk * d:(blk + 1) * d]
        r = x * cc + pltpu.roll(x, d // 2, axis=1) * ss
        if blk >= N_RET_HEADS:
            r = r * (d ** -0.5)
        o_ref[:, qk0 + blk * d:qk0 + (blk + 1) * d] = r


def _inproj(x, g, w, seq):
    t = x.shape[0]
    tm = TOKEN_TILE
    cc, ss = _rope_tables(seq)
    pos_blocks = seq // tm
    rope_spec = pl.BlockSpec((tm, RET_HEAD_DIM), lambda i: (i % pos_blocks, 0))
    return pl.pallas_call(
        _inproj_kernel,
        grid=(t // tm,),
        in_specs=[_row_spec(tm, D_MODEL), _const_spec((1, D_MODEL)),
                  _const_spec((D_MODEL, D_IN)), rope_spec, rope_spec],
        out_specs=_row_spec(tm, D_IN),
        out_shape=jax.ShapeDtypeStruct((t, D_IN), F32),
        compiler_params=pltpu.CompilerParams(
            dimension_semantics=("parallel",), vmem_limit_bytes=VMEM_LIMIT),
        name="inproj",
    )(x, g, w, cc, ss)


def _ret_kernel(lgf_ref, lgb_ref, q_ref, k_ref, v_ref, g_ref, o_ref, qs_ref, kt_ref, *, seq):
    c = RET_CHUNK
    d = RET_HEAD_DIM
    n_chunks = seq // c
    head = pl.program_id(1)
    lgf = jnp.full((c, d), lgf_ref[head], F32)
    lgb = jnp.full((c, d), lgb_ref[head], F32)
    row = lax.broadcasted_iota(jnp.int32, (c, d), 0).astype(F32)
    col = lax.broadcasted_iota(jnp.int32, (c, d), 1).astype(F32)
    diff = row - col
    dmat = jnp.where(diff >= 0.0, jnp.exp(jnp.maximum(diff, 0.0) * lgf),
                     jnp.exp(jnp.maximum(-diff, 0.0) * lgb))
    wq_f = jnp.exp((row + 1.0) * lgf)
    wk_f = jnp.exp((c - 1.0 - row) * lgf)
    wq_b = jnp.exp((c - row) * lgb)
    wk_b = jnp.exp(row * lgb)
    gc_f = jnp.exp(c * lgf)
    gc_b = jnp.exp(c * lgb)

    grp = RET_GROUP
    n_groups = n_chunks // grp

    def fwd(gi, state):
        rows = [pl.ds(pl.multiple_of((gi * grp + j) * c, c), c) for j in range(grp)]
        qbs, kts, vs = [], [], []
        for r in rows:
            qb = q_ref[r, :].astype(BF16)
            ktb = k_ref[r, :].T.astype(BF16)
            qs_ref[r, :] = qb
            kt_ref[r, :] = ktb
            qbs.append(qb)
            kts.append(ktb)
            vs.append(v_ref[r, :])
        scores = [_bdot(qb, ktb) for qb, ktb in zip(qbs, kts)]
        kvs = [_bdot(ktb, (v * wk_f).astype(BF16)) for ktb, v in zip(kts, vs)]
        states = []
        for kv in kvs:
            states.append(state)
            state = state * gc_f + kv
        intras = [_bdot((s * dmat).astype(BF16), v.astype(BF16)) for s, v in zip(scores, vs)]
        crosses = [_bdot(qb, st.astype(BF16)) for qb, st in zip(qbs, states)]
        for r, intra, cross in zip(rows, intras, crosses):
            o_ref[r, :] = intra + wq_f * cross
        return state

    lax.fori_loop(0, n_groups, fwd, jnp.zeros((d, d), F32))

    def bwd(gi, state):
        rows = [pl.ds(pl.multiple_of((n_chunks - 1 - gi * grp - j) * c, c), c) for j in range(grp)]
        kvs = [_bdot(kt_ref[r, :], (v_ref[r, :] * wk_b).astype(BF16)) for r in rows]
        states = []
        for kv in kvs:
            states.append(state)
            state = state * gc_b + kv
        crosses = [_bdot(qs_ref[r, :], st.astype(BF16)) for r, st in zip(rows, states)]
        outs = [o_ref[r, :] + wq_b * cross for r, cross in zip(rows, crosses)]
        means = [jnp.mean(o * o, axis=-1, keepdims=True) for o in outs]
        norms = [lax.rsqrt(m + NORM_EPS) for m in means]
        gates = [g_ref[r, :] for r in rows]
        gates = [g * jax.nn.sigmoid(g) for g in gates]
        for r, o, nrm, g in zip(rows, outs, norms, gates):
            o_ref[r, :] = g * (o * nrm)
        return state

    lax.fori_loop(0, n_groups, bwd, jnp.zeros((d, d), F32))


def _rope_tables(seq):
    d = RET_HEAD_DIM
    inv = 1.0 / (ROPE_BASE ** (jnp.arange(0, d, 2, dtype=F32) / d))
    ang = jnp.arange(seq, dtype=F32)[:, None] * inv[None, :]
    c, s = jnp.cos(ang), jnp.sin(ang)
    return jnp.concatenate([c, c], axis=-1), jnp.concatenate([-s, s], axis=-1)


def _retention(u, lg_f, lg_b, batch, seq):
    assert RET_CHUNK == RET_HEAD_DIM
    first = N_HY_COLS // LANES
    heads = N_RET_HEADS

    def col(off):
        return pl.BlockSpec((seq, LANES), lambda b, h, *_: (b, first + off * heads + h))

    grid_spec = pltpu.PrefetchScalarGridSpec(
        num_scalar_prefetch=2,
        grid=(batch, heads),
        in_specs=[col(0), col(1), col(2), col(3)],
        out_specs=pl.BlockSpec((seq, LANES), lambda b, h, *_: (b, h)),
        scratch_shapes=[pltpu.VMEM((seq, RET_HEAD_DIM), BF16), pltpu.VMEM((seq, RET_CHUNK), BF16)],
    )
    return pl.pallas_call(
        functools.partial(_ret_kernel, seq=seq),
        grid_spec=grid_spec,
        out_shape=jax.ShapeDtypeStruct((batch * seq, D_RET), F32),
        compiler_params=pltpu.CompilerParams(
            dimension_semantics=("parallel", "parallel"), vmem_limit_bytes=VMEM_LIMIT),
        name="retention",
    )(lg_f, lg_b, u, u, u, u)


def _split(x):
    hi = x.astype(BF16)
    return hi, (x - hi.astype(F32)).astype(BF16)


def _dot3(a, b):
    a_hi, a_lo = a
    b_hi, b_lo = b
    return _bdot(a_hi, b_hi) + _bdot(a_hi, b_lo) + _bdot(a_lo, b_hi)


@functools.lru_cache(maxsize=None)
def _fft_tables(seq):
    n = 2 * seq
    n2 = FFT_N2
    n1 = n // n2
    h1 = n1 // 2
    f1 = np.arange(h1, dtype=np.float64) + 0.5
    th = 2.0 * np.pi * f1[:, None] * np.arange(n1, dtype=np.float64)[None, :] / n1
    m1_full = np.concatenate([np.cos(th), -np.sin(th)], axis=0)
    m1 = m1_full[:, :h1]
    m1_inv = (2.0 / n) * m1.T
    t2 = np.arange(n2, dtype=np.float64)
    phi = 2.0 * np.pi * (np.arange(n2, dtype=np.float64)[None, :, None] * t2[None, None, :] / n2
                         + f1[:, None, None] * t2[None, None, :] / n)
    gr, gi = np.cos(phi), -np.sin(phi)
    m2 = np.concatenate([np.concatenate([gr, -gi], axis=2),
                         np.concatenate([gi, gr], axis=2)], axis=1)
    m2_inv = np.transpose(m2, (0, 2, 1))
    return dict(n1=n1, h1=h1, m1_full=m1_full, m1=m1, m1_inv=m1_inv, m2=m2, m2_inv=m2_inv)


FILT_ROWS = 512
A_PITCH = 2 * FFT_N2 + 8
T_PITCH = FFT_N2 + 8


def _filter_kernel(z_ref, w1_ref, b1_ref, w2_ref, b2_ref, w3_ref, b3_ref, fr_ref,
                   w4_ref, dl_ref, o_ref, sum_ref, *, nblk):
    i = pl.program_id(0)
    fr = fr_ref[...]
    z = z_ref[...]
    h = jnp.sin(fr * (_dot3(_split(z), _split(w1_ref[...])) + b1_ref[...]))
    h = jnp.sin(fr * (_dot3(_split(h), _split(w2_ref[...])) + b2_ref[...]))
    h = jnp.sin(fr * (_dot3(_split(h), _split(w3_ref[...])) + b3_ref[...]))
    hs = _split(h)
    t = jnp.broadcast_to(z[:, 0:1], z.shape)
    adl = jnp.abs(dl_ref[...])
    backward = i >= nblk
    for c in range(HYENA_ORDER * D_HYENA // LANES):
        cols = slice(c * LANES, (c + 1) * LANES)
        hc = _dot3(hs, _split(w4_ref[0, :, cols])) * jnp.exp(-t * adl[:, cols])
        part = jnp.sum(jnp.abs(hc), axis=0, keepdims=True)

        @pl.when(i % nblk == 0)
        def _():
            sum_ref[0, :, cols] = part

        @pl.when(i % nblk != 0)
        def _():
            sum_ref[0, :, cols] = sum_ref[0, :, cols] + part

        rows = lax.broadcasted_iota(jnp.int32, hc.shape, 0)
        flipped = jnp.where(jnp.logical_and(rows == 0, i == nblk), 0.0, -hc)
        o_ref[:, cols] = jnp.where(backward, flipped, hc)


def _kspec_kernel(k_ref, s_ref, m1_ref, m2_ref, o_ref, a_ref, kp_ref, *, n1):
    n2 = FFT_N2
    h1 = n1 // 2
    inv_f = 1.0 / s_ref[0]
    inv_b = 1.0 / s_ref[1]

    def pad(t1, carry):
        src = pl.multiple_of(t1 * n2, n2)
        dst = pl.multiple_of(t1 * T_PITCH, 8)
        kp_ref[pl.ds(dst, n2), :] = k_ref[pl.ds(src, n2), :] * jnp.where(t1 < h1, inv_f, inv_b)
        return carry

    lax.fori_loop(0, n1, pad, 0, unroll=8)

    def stage1(t2, carry):
        kt = kp_ref[pl.ds(t2, n1, stride=T_PITCH), :]
        a = _bdot(m1_ref[...], kt.astype(BF16))
        a_ref[pl.ds(t2, h1, stride=A_PITCH), :] = a[:h1]
        a_ref[pl.ds(t2 + n2, h1, stride=A_PITCH), :] = a[h1:]
        return carry

    lax.fori_loop(0, n2, stage1, 0, unroll=4)

    def stage2(f1, carry):
        src = pl.multiple_of(f1 * A_PITCH, 8)
        dst = pl.multiple_of(f1 * 2 * n2, 2 * n2)
        o_ref[pl.ds(dst, 2 * n2), :] = _bdot(m2_ref[f1], a_ref[pl.ds(src, 2 * n2), :].astype(BF16))
        return carry

    lax.fori_loop(0, h1, stage2, 0, unroll=16)


def _filter_spectra(seq, p):
    n = 2 * seq
    tb = _fft_tables(seq)
    fwd_pos = jnp.arange(seq, dtype=jnp.int32)
    pos = jnp.concatenate([fwd_pos, (seq - fwd_pos) % seq])
    t = jnp.linspace(0.0, 1.0, seq, dtype=F32)[pos][:, None]
    w = 2.0 * math.pi * pos.astype(F32)[:, None] / seq
    fb = jnp.linspace(1e-4, FILT_BANDS - 1, FILT_BANDS, dtype=F32)[None, :]
    z = jnp.concatenate([t, jnp.cos(fb * w), -jnp.sin(fb * w)], axis=-1)
    zz = jnp.pad(z, ((0, 0), (0, LANES - FILT_EMB)))
    pad_h = LANES - FILT_HIDDEN

    def padw(a, rows):
        return jnp.pad(a, ((0, rows - a.shape[0]), (0, pad_h)))

    def padv(a):
        return jnp.pad(a, (0, pad_h))[None, :]

    w1 = padw(p['filt_w1'], LANES)
    w2 = padw(p['filt_w2'], LANES)
    w3 = padw(p['filt_w3'], LANES)
    w4 = jnp.pad(p['filt_w4'], ((0, pad_h), (0, 0))).reshape(LANES, HYENA_ORDER, 2, D_HYENA)
    w4 = jnp.transpose(w4, (2, 0, 1, 3)).reshape(2, LANES, HYENA_ORDER * D_HYENA)
    min_decay = math.log(HYENA_TARGET) / SLOW_DECAY_PCT
    max_decay = math.log(HYENA_TARGET) / FAST_DECAY_PCT
    deltas = jnp.linspace(min_decay, max_decay, D_HYENA, dtype=F32)
    deltas = jnp.tile(deltas, HYENA_ORDER)[None, :]
    width = HYENA_ORDER * D_HYENA
    nblk = seq // FILT_ROWS
    sq = _const_spec((LANES, LANES))
    vec = _const_spec((1, LANES))
    filt, sums = pl.pallas_call(
        functools.partial(_filter_kernel, nblk=nblk),
        grid=(2 * nblk,),
        in_specs=[pl.BlockSpec((FILT_ROWS, LANES), lambda i: (i, 0)),
                  sq, vec, sq, vec, sq, vec, vec,
                  pl.BlockSpec((1, LANES, width), lambda i: (i // nblk, 0, 0)),
                  _const_spec((1, width))],
        out_specs=[pl.BlockSpec((FILT_ROWS, width), lambda i: (i, 0)),
                   pl.BlockSpec((1, 1, width), lambda i: (i // nblk, 0, 0))],
        out_shape=[jax.ShapeDtypeStruct((n, width), F32),
                   jax.ShapeDtypeStruct((2, 1, width), F32)],
        compiler_params=pltpu.CompilerParams(
            dimension_semantics=("arbitrary",), vmem_limit_bytes=VMEM_LIMIT),
        name="hyena_filter",
    )(zz, w1, padv(p['filt_b1']), w2, padv(p['filt_b2']), w3, padv(p['filt_b3']),
      padv(p['filt_freq']), w4, deltas)
    n1, h1 = tb['n1'], tb['h1']
    return pl.pallas_call(
        functools.partial(_kspec_kernel, n1=n1),
        grid=(width // LANES,),
        in_specs=[pl.BlockSpec((n, LANES), lambda j: (0, j), pipeline_mode=pl.Buffered(1)),
                  pl.BlockSpec((2, 1, LANES), lambda j: (0, 0, j)),
                  _const_spec((n1, n1)), _const_spec((h1, 2 * FFT_N2, 2 * FFT_N2))],
        out_specs=pl.BlockSpec((n, LANES), lambda j: (0, j)),
        out_shape=jax.ShapeDtypeStruct((n, width), F32),
        scratch_shapes=[pltpu.VMEM((h1 * A_PITCH, LANES), F32),
                        pltpu.VMEM((n1 * T_PITCH, LANES), F32)],
        compiler_params=pltpu.CompilerParams(
            dimension_semantics=("parallel",), vmem_limit_bytes=VMEM_LIMIT),
        name="hyena_filter_spectrum",
    )(filt, sums, _bf16_const(tb['m1_full']), _bf16_const(tb['m2']))


CHUNK_UNROLL = 8


def _for_chunks(count, body):
    body(0, True, False)

    def step(t1, carry):
        body(t1, False, False)
        return carry

    lax.fori_loop(1, count - CHUNK_UNROLL + 1, step, 0, unroll=CHUNK_UNROLL)
    for t1 in range(count - CHUNK_UNROLL + 1, count):
        body(t1, False, t1 == count - 1)


def _short_conv_rows(ref, w_ref, b_ref, r0, nrows, first, last):
    x = ref[pl.ds(r0, nrows), :]
    rid = lax.broadcasted_iota(jnp.int32, x.shape, 0)
    if first:
        prev = jnp.where(rid == 0, 0.0, pltpu.roll(x, 1, axis=0))
    else:
        prev = ref[pl.ds(r0 - 1, nrows), :]
    if last:
        nxt = jnp.where(rid == nrows - 1, 0.0, pltpu.roll(x, nrows - 1, axis=0))
    else:
        nxt = ref[pl.ds(r0 + 1, nrows), :]
    return prev * w_ref[0:1, :] + x * w_ref[1:2, :] + nxt * w_ref[2:3, :] + b_ref[...]


def _long_conv_kernel(*refs, n1, conv_z, conv_g):
    refs = list(refs)
    z_ref = refs.pop(0)
    zw_ref, zb_ref = (refs.pop(0), refs.pop(0)) if conv_z else (None, None)
    g_ref = refs.pop(0)
    gw_ref, gb_ref = (refs.pop(0), refs.pop(0)) if conv_g else (None, None)
    k_ref, bias_ref, m1_ref, m1i_ref, m2_ref, m2i_ref, o_ref, a_ref, tp_ref = refs
    n2 = FFT_N2
    h1 = n1 // 2

    def rows_of(ref, w_ref, b_ref, conv, t1, first, last):
        r0 = t1 * n2 if isinstance(t1, int) else pl.multiple_of(t1 * n2, n2)
        if conv:
            return _short_conv_rows(ref, w_ref, b_ref, r0, n2, first, last)
        return ref[pl.ds(r0, n2), :]

    def z_rows(t1, first, last):
        return rows_of(z_ref, zw_ref, zb_ref, conv_z, t1, first, last)

    def g_rows(t1, first, last):
        return rows_of(g_ref, gw_ref, gb_ref, conv_g, t1, first, last)

    def pitched(t1):
        r0 = t1 * T_PITCH
        return pl.ds(r0 if isinstance(t1, int) else pl.multiple_of(r0, 8), n2)

    def pad(t1, first, last):
        tp_ref[pitched(t1), :] = z_rows(t1, first, last)

    _for_chunks(h1, pad)

    def stage1(t2, carry):
        zt = tp_ref[pl.ds(t2, h1, stride=T_PITCH), :]
        a = _bdot(m1_ref[...], zt.astype(BF16))
        a_ref[pl.ds(t2, h1, stride=A_PITCH), :] = a[:h1]
        a_ref[pl.ds(t2 + n2, h1, stride=A_PITCH), :] = a[h1:]
        return carry

    lax.fori_loop(0, n2, stage1, 0, unroll=4)

    def stage2(f1, carry):
        ra = pl.multiple_of(f1 * A_PITCH, 8)
        rk = pl.multiple_of(f1 * 2 * n2, 2 * n2)
        x = _bdot(m2_ref[f1], a_ref[pl.ds(ra, 2 * n2), :].astype(BF16))
        kk = k_ref[pl.ds(rk, 2 * n2), :]
        xr, xi, kr, ki = x[:n2], x[n2:], kk[:n2], kk[n2:]
        y = jnp.concatenate([xr * kr - xi * ki, xr * ki + xi * kr], axis=0)
        a_ref[pl.ds(ra, 2 * n2), :] = _bdot(m2i_ref[f1], y.astype(BF16))
        return carry

    lax.fori_loop(0, h1, stage2, 0, unroll=16)

    def stage3(t2, carry):
        br = a_ref[pl.ds(t2, h1, stride=A_PITCH), :]
        bi = a_ref[pl.ds(t2 + n2, h1, stride=A_PITCH), :]
        y = _bdot(m1i_ref[...], jnp.concatenate([br, bi], axis=0).astype(BF16))
        tp_ref[pl.ds(t2, h1, stride=T_PITCH), :] = y
        return carry

    lax.fori_loop(0, n2, stage3, 0, unroll=4)

    bias = bias_ref[...]

    def finish(t1, first, last):
        r0 = t1 * n2 if isinstance(t1, int) else pl.multiple_of(t1 * n2, n2)
        y = tp_ref[pitched(t1), :]
        o_ref[pl.ds(r0, n2), :] = g_rows(t1, first, last) * (y + bias * z_rows(t1, first, last))

    _for_chunks(h1, finish)


def _bf16_const(a):
    return jnp.asarray(a, F32).astype(BF16)


def _long_conv(z, z_off, gate, g_off, kspec, k_off, bias, batch, seq, short=None, conv_z=False):
    n = 2 * seq
    tb = _fft_tables(seq)
    n1, h1 = tb['n1'], tb['h1']
    tiles = D_HYENA // LANES
    conv_g = short is not None

    def operand(arr, off, conv):
        specs = [pl.BlockSpec((seq, LANES), lambda j, b: (b, off + j))]
        args = [arr]
        if conv:
            specs += [pl.BlockSpec((3, LANES), lambda j, b: (0, off + j)),
                      pl.BlockSpec((1, LANES), lambda j, b: (0, off + j))]
            args += list(short)
        return specs, args

    z_specs, z_args = operand(z, z_off, conv_z)
    g_specs, g_args = operand(gate, g_off, conv_g)
    return pl.pallas_call(
        functools.partial(_long_conv_kernel, n1=n1, conv_z=conv_z, conv_g=conv_g),
        grid=(tiles, batch),
        in_specs=z_specs + g_specs + [
                  pl.BlockSpec((n, LANES), lambda j, b: (0, k_off + j), pipeline_mode=pl.Buffered(1)),
                  pl.BlockSpec((1, LANES), lambda j, b: (0, j)),
                  _const_spec((n1, h1)), _const_spec((h1, n1)),
                  _const_spec((h1, 2 * FFT_N2, 2 * FFT_N2)),
                  _const_spec((h1, 2 * FFT_N2, 2 * FFT_N2))],
        out_specs=pl.BlockSpec((seq, LANES), lambda j, b: (b, j)),
        out_shape=jax.ShapeDtypeStruct((batch * seq, D_HYENA), F32),
        scratch_shapes=[pltpu.VMEM((h1 * A_PITCH, LANES), F32),
                        pltpu.VMEM((h1 * T_PITCH, LANES), F32)],
        compiler_params=pltpu.CompilerParams(
            dimension_semantics=("parallel", "parallel"), vmem_limit_bytes=VMEM_LIMIT),
        name="hyena_long_conv",
    )(*z_args, *g_args, kspec, bias, _bf16_const(tb['m1']), _bf16_const(tb['m1_inv']),
      _bf16_const(tb['m2']), _bf16_const(tb['m2_inv']))


def _layer(x3, p):
    batch, seq, _ = x3.shape
    x = x3.reshape(batch * seq, D_MODEL)
    x = _ffn(x, p['ffn1_pre_g'], p['ffn1_w1'], p['ffn1_w3'], p['ffn1_w2'], p['ffn1_post_g'])
    u = _inproj(x, p['mix_pre_g'], p['w_in'], seq)
    kspec = _filter_spectra(seq, p)
    tiles = D_HYENA // LANES
    short = (p['short_w'], p['short_b'])
    z1 = _long_conv(u, 0, u, tiles, kspec, 0, p['hyena_bias'][0:1], batch, seq, short=short, conv_z=True)
    yh = _long_conv(z1, 0, u, 2 * tiles, kspec, tiles, p['hyena_bias'][1:2], batch, seq, short=short)
    yr = _retention(u, p['ret_log_decay_f'], p['ret_log_decay_b'], batch, seq)
    mix = (yh, yr, p['w_out'][:D_HYENA], p['w_out'][D_HYENA:], p['mix_post_g'])
    x = _ffn(x, p['ffn2_pre_g'], p['ffn2_w1'], p['ffn2_w3'], p['ffn2_w2'], p['ffn2_post_g'], mix=mix)
    return x.reshape(batch, seq, D_MODEL)


_MATRIX_PARAMS = ('ffn1_w1', 'ffn1_w3', 'ffn1_w2', 'w_in', 'w_out', 'ffn2_w1', 'ffn2_w3', 'ffn2_w2')
_GAIN_PARAMS = ('ffn1_pre_g', 'ffn1_post_g', 'mix_pre_g', 'mix_post_g', 'ffn2_pre_g', 'ffn2_post_g')


def kernel(x_prompt, x_sample, ffn1_pre_g, ffn1_w1, ffn1_w3, ffn1_w2, ffn1_post_g, mix_pre_g, w_in, short_w, short_b, filt_w1, filt_b1, filt_w2, filt_b2, filt_w3, filt_b3, filt_w4, filt_freq, hyena_bias, ret_log_decay_f, ret_log_decay_b, w_out, mix_post_g, ffn2_pre_g, ffn2_w1, ffn2_w3, ffn2_w2, ffn2_post_g):
    params = dict(ffn1_pre_g=ffn1_pre_g, ffn1_w1=ffn1_w1, ffn1_w3=ffn1_w3, ffn1_w2=ffn1_w2,
                  ffn1_post_g=ffn1_post_g, mix_pre_g=mix_pre_g, w_in=w_in, short_w=short_w,
                  short_b=short_b, filt_w1=filt_w1, filt_b1=filt_b1, filt_w2=filt_w2,
                  filt_b2=filt_b2, filt_w3=filt_w3, filt_b3=filt_b3, filt_w4=filt_w4,
                  filt_freq=filt_freq, hyena_bias=hyena_bias, ret_log_decay_f=ret_log_decay_f,
                  ret_log_decay_b=ret_log_decay_b, w_out=w_out, mix_post_g=mix_post_g,
                  ffn2_pre_g=ffn2_pre_g, ffn2_w1=ffn2_w1, ffn2_w3=ffn2_w3, ffn2_w2=ffn2_w2,
                  ffn2_post_g=ffn2_post_g)
    depth = ffn1_w1.shape[0]

    def run(x):
        for l in range(depth):
            p = {k: v[l] for k, v in params.items()}
            for k in _MATRIX_PARAMS:
                p[k] = p[k].astype(BF16)
            for k in _GAIN_PARAMS:
                p[k] = p[k][None, :]
            p['short_b'] = p['short_b'][None, :]
            x = _layer(x, p)
        return x

    return (run(x_prompt), run(x_sample))
```

```python
import functools
import math

import numpy as np
import jax
import jax.numpy as jnp
from jax import lax
from jax.experimental import pallas as pl
from jax.experimental.pallas import tpu as pltpu

F32 = jnp.float32
BF16 = jnp.bfloat16

D_MODEL = 1024
D_HYENA = 512
D_RET = 512
HYENA_ORDER = 2
N_RET_HEADS = 4
RET_HEAD_DIM = 128
D_FF = 2816
FILT_EMB = 33
FILT_BANDS = 16
FILT_HIDDEN = 64
ROPE_BASE = 10000.0
NORM_EPS = 1e-6
HYENA_TARGET = 1e-2
FAST_DECAY_PCT = 0.3
SLOW_DECAY_PCT = 1.5
N_HY_COLS = (HYENA_ORDER + 1) * D_HYENA
D_IN = N_HY_COLS + 4 * D_RET

LANES = 128
VMEM_LIMIT = 56 * 1024 * 1024
VMEM_HEADROOM = 4 * 1024 * 1024
FFT_N2 = 32
RET_CHUNK = 128
RET_GROUP = 8
TOKEN_TILE = 512


def _const_spec(shape):
    nd = len(shape)
    return pl.BlockSpec(shape, lambda *_: (0,) * nd, pipeline_mode=pl.Buffered(1))


def _rms(x, g):
    ms = jnp.mean(x * x, axis=-1, keepdims=True)
    return x * lax.rsqrt(ms + NORM_EPS) * g


def _bdot(a, b):
    return jnp.dot(a, b, preferred_element_type=F32)


def _ffn_core(x, pre_ref, w1_ref, w3_ref, w2_ref, post_ref, o_ref):
    h = _rms(x, pre_ref[...]).astype(BF16)
    a = _bdot(h, w1_ref[...])
    b = _bdot(h, w3_ref[...])
    g = (a * jax.nn.sigmoid(a) * b).astype(BF16)
    y = _bdot(g, w2_ref[...])
    o_ref[...] = x + 0.5 * _rms(y, post_ref[...])


def _ffn_kernel(x_ref, pre_ref, w1_ref, w3_ref, w2_ref, post_ref, o_ref):
    _ffn_core(x_ref[...], pre_ref, w1_ref, w3_ref, w2_ref, post_ref, o_ref)


def _mix_ffn_kernel(x_ref, yh_ref, yr_ref, woh_ref, wor_ref, mg_ref,
                    pre_ref, w1_ref, w3_ref, w2_ref, post_ref, o_ref):
    y = _bdot(yh_ref[...].astype(BF16), woh_ref[...])
    y = y + _bdot(yr_ref[...].astype(BF16), wor_ref[...])
    x = x_ref[...] + _rms(y, mg_ref[...])
    _ffn_core(x, pre_ref, w1_ref, w3_ref, w2_ref, post_ref, o_ref)


def _row_spec(tm, width):
    return pl.BlockSpec((tm, width), lambda i: (i, 0))


def _ffn(x, pre_g, w1, w3, w2, post_g, mix=None):
    t = x.shape[0]
    tm = TOKEN_TILE
    ffn_specs = [_const_spec((1, D_MODEL)), _const_spec((D_MODEL, D_FF)),
                 _const_spec((D_MODEL, D_FF)), _const_spec((D_FF, D_MODEL)),
                 _const_spec((1, D_MODEL))]
    ffn_args = (pre_g, w1, w3, w2, post_g)
    if mix is None:
        body, args = _ffn_kernel, (x,) + ffn_args
        specs = [_row_spec(tm, D_MODEL)] + ffn_specs
    else:
        yh, yr, woh, wor, mg = mix
        body, args = _mix_ffn_kernel, (x, yh, yr, woh, wor, mg) + ffn_args
        specs = [_row_spec(tm, D_MODEL), _row_spec(tm, D_HYENA), _row_spec(tm, D_RET),
                 _const_spec((D_HYENA, D_MODEL)), _const_spec((D_RET, D_MODEL)),
                 _const_spec((1, D_MODEL))] + ffn_specs
    return pl.pallas_call(
        body,
        grid=(t // tm,),
        in_specs=specs,
        out_specs=_row_spec(tm, D_MODEL),
        out_shape=jax.ShapeDtypeStruct((t, D_MODEL), F32),
        compiler_params=pltpu.CompilerParams(
            dimension_semantics=("parallel",), vmem_limit_bytes=VMEM_LIMIT),
        name="ffn_mix" if mix is not None else "ffn",
    )(*args)


def _inproj_kernel(x_ref, g_ref, w_ref, cc_ref, ss_ref, o_ref):
    h = _rms(x_ref[...], g_ref[...]).astype(BF16)
    qk0 = N_HY_COLS
    vg0 = N_HY_COLS + 2 * D_RET
    o_ref[:, :qk0] = _bdot(h, w_ref[:, :qk0])
    o_ref[:, vg0:] = _bdot(h, w_ref[:, vg0:])
    qk = _bdot(h, w_ref[:, qk0:vg0])
    cc = cc_ref[...]
    ss = ss_ref[...]
    d = RET_HEAD_DIM
    for blk in range(2 * N_RET_HEADS):
        x = qk[:, blk * d:(blk + 1) * d]
        r = x * cc + pltpu.roll(x, d // 2, axis=1) * ss
        if blk >= N_RET_HEADS:
            r = r * (d ** -0.5)
        o_ref[:, qk0 + blk * d:qk0 + (blk + 1) * d] = r


def _inproj(x, g, w, seq):
    t = x.shape[0]
    tm = TOKEN_TILE
    cc, ss = _rope_tables(seq)
    pos_blocks = seq // tm
    rope_spec = pl.BlockSpec((tm, RET_HEAD_DIM), lambda i: (i % pos_blocks, 0))
    return pl.pallas_call(
        _inproj_kernel,
        grid=(t // tm,),
        in_specs=[_row_spec(tm, D_MODEL), _const_spec((1, D_MODEL)),
                  _const_spec((D_MODEL, D_IN)), rope_spec, rope_spec],
        out_specs=_row_spec(tm, D_IN),
        out_shape=jax.ShapeDtypeStruct((t, D_IN), F32),
        compiler_params=pltpu.CompilerParams(
            dimension_semantics=("parallel",), vmem_limit_bytes=VMEM_LIMIT),
        name="inproj",
    )(x, g, w, cc, ss)


def _ret_kernel(lgf_ref, lgb_ref, q_ref, k_ref, v_ref, g_ref, o_ref, qs_ref, kt_ref, *, seq):
    c = RET_CHUNK
    d = RET_HEAD_DIM
    n_chunks = seq // c
    head = pl.program_id(1)
    lgf = jnp.full((c, d), lgf_ref[head], F32)
    lgb = jnp.full((c, d), lgb_ref[head], F32)
    row = lax.broadcasted_iota(jnp.int32, (c, d), 0).astype(F32)
    col = lax.broadcasted_iota(jnp.int32, (c, d), 1).astype(F32)
    diff = row - col
    dmat = jnp.where(diff >= 0.0, jnp.exp(jnp.maximum(diff, 0.0) * lgf),
                     jnp.exp(jnp.maximum(-diff, 0.0) * lgb))
    wq_f = jnp.exp((row + 1.0) * lgf)
    wk_f = jnp.exp((c - 1.0 - row) * lgf)
    wq_b = jnp.exp((c - row) * lgb)
    wk_b = jnp.exp(row * lgb)
    gc_f = jnp.exp(c * lgf)
    gc_b = jnp.exp(c * lgb)

    grp = RET_GROUP
    n_groups = n_chunks // grp

    def fwd(gi, state):
        rows = [pl.ds(pl.multiple_of((gi * grp + j) * c, c), c) for j in range(grp)]
        qbs, kts, vs = [], [], []
        for r in rows:
            qb = q_ref[r, :].astype(BF16)
            ktb = k_ref[r, :].T.astype(BF16)
            qs_ref[r, :] = qb
            kt_ref[r, :] = ktb
            qbs.append(qb)
            kts.append(ktb)
            vs.append(v_ref[r, :])
        scores = [_bdot(qb, ktb) for qb, ktb in zip(qbs, kts)]
        kvs = [_bdot(ktb, (v * wk_f).astype(BF16)) for ktb, v in zip(kts, vs)]
        states = []
        for kv in kvs:
            states.append(state)
            state = state * gc_f + kv
        intras = [_bdot((s * dmat).astype(BF16), v.astype(BF16)) for s, v in zip(scores, vs)]
        crosses = [_bdot(qb, st.astype(BF16)) for qb, st in zip(qbs, states)]
        for r, intra, cross in zip(rows, intras, crosses):
            o_ref[r, :] = intra + wq_f * cross
        return state

    lax.fori_loop(0, n_groups, fwd, jnp.zeros((d, d), F32))

    def bwd(gi, state):
        rows = [pl.ds(pl.multiple_of((n_chunks - 1 - gi * grp - j) * c, c), c) for j in range(grp)]
        kvs = [_bdot(kt_ref[r, :], (v_ref[r, :] * wk_b).astype(BF16)) for r in rows]
        states = []
        for kv in kvs:
            states.append(state)
            state = state * gc_b + kv
        crosses = [_bdot(qs_ref[r, :], st.astype(BF16)) for r, st in zip(rows, states)]
        outs = [o_ref[r, :] + wq_b * cross for r, cross in zip(rows, crosses)]
        means = [jnp.mean(o * o, axis=-1, keepdims=True) for o in outs]
        norms = [lax.rsqrt(m + NORM_EPS) for m in means]
        gates = [g_ref[r, :] for r in rows]
        gates = [g * jax.nn.sigmoid(g) for g in gates]
        for r, o, nrm, g in zip(rows, outs, norms, gates):
            o_ref[r, :] = g * (o * nrm)
        return state

    lax.fori_loop(0, n_groups, bwd, jnp.zeros((d, d), F32))


def _rope_tables(seq):
    d = RET_HEAD_DIM
    inv = 1.0 / (ROPE_BASE ** (jnp.arange(0, d, 2, dtype=F32) / d))
    ang = jnp.arange(seq, dtype=F32)[:, None] * inv[None, :]
    c, s = jnp.cos(ang), jnp.sin(ang)
    return jnp.concatenate([c, c], axis=-1), jnp.concatenate([-s, s], axis=-1)


def _retention(u, lg_f, lg_b, batch, seq):
    assert RET_CHUNK == RET_HEAD_DIM
    first = N_HY_COLS // LANES
    heads = N_RET_HEADS

    def col(off):
        return pl.BlockSpec((seq, LANES), lambda b, h, *_: (b, first + off * heads + h))

    grid_spec = pltpu.PrefetchScalarGridSpec(
        num_scalar_prefetch=2,
        grid=(batch, heads),
        in_specs=[col(0), col(1), col(2), col(3)],
        out_specs=pl.BlockSpec((seq, LANES), lambda b, h, *_: (b, h)),
        scratch_shapes=[pltpu.VMEM((seq, RET_HEAD_DIM), BF16), pltpu.VMEM((seq, RET_CHUNK), BF16)],
    )
    return pl.pallas_call(
        functools.partial(_ret_kernel, seq=seq),
        grid_spec=grid_spec,
        out_shape=jax.ShapeDtypeStruct((batch * seq, D_RET), F32),
        compiler_params=pltpu.CompilerParams(
            dimension_semantics=("parallel", "parallel"), vmem_limit_bytes=VMEM_LIMIT),
        name="retention",
    )(lg_f, lg_b, u, u, u, u)


def _split(x):
    hi = x.astype(BF16)
    return hi, (x - hi.astype(F32)).astype(BF16)


def _dot3(a, b):
    a_hi, a_lo = a
    b_hi, b_lo = b
    return _bdot(a_hi, b_hi) + _bdot(a_hi, b_lo) + _bdot(a_lo, b_hi)


@functools.lru_cache(maxsize=None)
def _fft_tables(seq):
    n = 2 * seq
    n2 = FFT_N2
    n1 = n // n2
    h1 = n1 // 2
    f1 = np.arange(h1, dtype=np.float64) + 0.5
    th = 2.0 * np.pi * f1[:, None] * np.arange(n1, dtype=np.float64)[None, :] / n1
    m1_full = np.concatenate([np.cos(th), -np.sin(th)], axis=0)
    m1 = m1_full[:, :h1]
    m1_inv = (2.0 / n) * m1.T
    t2 = np.arange(n2, dtype=np.float64)
    phi = 2.0 * np.pi * (np.arange(n2, dtype=np.float64)[None, :, None] * t2[None, None, :] / n2
                         + f1[:, None, None] * t2[None, None, :] / n)
    gr, gi = np.cos(phi), -np.sin(phi)
    m2 = np.concatenate([np.concatenate([gr, -gi], axis=2),
                         np.concatenate([gi, gr], axis=2)], axis=1)
    m2_inv = np.transpose(m2, (0, 2, 1))
    return dict(n1=n1, h1=h1, m1_full=m1_full, m1=m1, m1_inv=m1_inv, m2=m2, m2_inv=m2_inv)


FILT_ROWS = 512
FILT_PACK = LANES // 2
A_PITCH = 2 * FFT_N2 + 8
T_PITCH = FFT_N2 + 8


def _filter_kernel(z_ref, w1_ref, b1_ref, w2_ref, b2_ref, w3_ref, b3_ref, fr_ref,
                   w4_ref, dl_ref, o_ref, sum_ref, *, nblk):
    i = pl.program_id(0)
    half = FILT_ROWS // 2
    fr = fr_ref[...]
    z = z_ref[...]
    h = jnp.sin(fr * (_dot3(_split(z), _split(w1_ref[...])) + b1_ref[...]))
    h = jnp.sin(fr * (_dot3(_split(h), _split(w2_ref[...])) + b2_ref[...]))
    h = jnp.sin(fr * (_dot3(_split(h), _split(w3_ref[...])) + b3_ref[...]))
    hs = _split(h)
    adl = jnp.abs(dl_ref[...])
    backward = i >= nblk
    for side in range(2):
        lane = side * FILT_PACK
        t = jnp.broadcast_to(z[:, lane:lane + 1], z.shape)
        out_rows = slice(side * half, (side + 1) * half)
        for c in range(HYENA_ORDER * D_HYENA // LANES):
            cols = slice(c * LANES, (c + 1) * LANES)
            hc = _dot3(hs, _split(w4_ref[0, side, :, cols])) * jnp.exp(-t * adl[:, cols])
            part = jnp.sum(jnp.abs(hc), axis=0, keepdims=True)

            if side == 0:
                @pl.when(i % nblk == 0)
                def _():
                    sum_ref[0, :, cols] = part

                @pl.when(i % nblk != 0)
                def _():
                    sum_ref[0, :, cols] = sum_ref[0, :, cols] + part

                rows = lax.broadcasted_iota(jnp.int32, hc.shape, 0)
                flipped = jnp.where(jnp.logical_and(rows == 0, i == nblk), 0.0, -hc)
            else:
                sum_ref[0, :, cols] = sum_ref[0, :, cols] + part
                flipped = -hc
            o_ref[out_rows, cols] = jnp.where(backward, flipped, hc)


def _kspec_kernel(k_ref, s_ref, m1_ref, m2_ref, o_ref, a_ref, kp_ref, *, n1):
    n2 = FFT_N2
    h1 = n1 // 2
    inv_f = 1.0 / s_ref[0]
    inv_b = 1.0 / s_ref[1]

    def pad(t1, carry):
        src = pl.multiple_of(t1 * n2, n2)
        dst = pl.multiple_of(t1 * T_PITCH, 8)
        kp_ref[pl.ds(dst, n2), :] = k_ref[pl.ds(src, n2), :] * jnp.where(t1 < h1, inv_f, inv_b)
        return carry

    lax.fori_loop(0, n1, pad, 0, unroll=8)

    def stage1(t2, carry):
        kt = kp_ref[pl.ds(t2, n1, stride=T_PITCH), :]
        a = _bdot(m1_ref[...], kt.astype(BF16))
        a_ref[pl.ds(t2, h1, stride=A_PITCH), :] = a[:h1]
        a_ref[pl.ds(t2 + n2, h1, stride=A_PITCH), :] = a[h1:]
        return carry

    lax.fori_loop(0, n2, stage1, 0, unroll=4)

    def stage2(f1, carry):
        src = pl.multiple_of(f1 * A_PITCH, 8)
        dst = pl.multiple_of(f1 * 2 * n2, 2 * n2)
        o_ref[pl.ds(dst, 2 * n2), :] = _bdot(m2_ref[f1], a_ref[pl.ds(src, 2 * n2), :].astype(BF16))
        return carry

    lax.fori_loop(0, h1, stage2, 0, unroll=16)


def _filter_spectra(seq, p):
    n = 2 * seq
    tb = _fft_tables(seq)
    fwd_pos = jnp.arange(seq, dtype=jnp.int32)
    pos = jnp.concatenate([fwd_pos, (seq - fwd_pos) % seq])
    t_fwd = jnp.linspace(0.0, 1.0, seq, dtype=F32)
    t = jnp.concatenate([t_fwd, jnp.roll(t_fwd[::-1], 1)])[:, None]
    w = 2.0 * math.pi * pos.astype(F32)[:, None] / seq
    fb = jnp.linspace(1e-4, FILT_BANDS - 1, FILT_BANDS, dtype=F32)[None, :]
    z = jnp.concatenate([t, jnp.cos(fb * w), -jnp.sin(fb * w)], axis=-1)
    nblk = seq // FILT_ROWS
    half = FILT_ROWS // 2
    pack = FILT_PACK
    zz = jnp.pad(z, ((0, 0), (0, pack - FILT_EMB))).reshape(2 * nblk, 2, half, pack)
    zz = jnp.transpose(zz, (0, 2, 1, 3)).reshape(n // 2, LANES)

    def padw(a):
        a = jnp.pad(a, ((0, pack - a.shape[0]), (0, pack - a.shape[1])))
        zero = jnp.zeros_like(a)
        return jnp.concatenate([jnp.concatenate([a, zero], axis=1),
                                jnp.concatenate([zero, a], axis=1)], axis=0)

    def padv(a):
        return jnp.tile(jnp.pad(a, (0, pack - a.shape[0])), 2)[None, :]

    w1 = padw(p['filt_w1'])
    w2 = padw(p['filt_w2'])
    w3 = padw(p['filt_w3'])
    w4 = jnp.pad(p['filt_w4'], ((0, pack - FILT_HIDDEN), (0, 0))).reshape(pack, HYENA_ORDER, 2, D_HYENA)
    w4 = jnp.transpose(w4, (2, 0, 1, 3)).reshape(2, pack, HYENA_ORDER * D_HYENA)
    w4 = jnp.stack([jnp.pad(w4, ((0, 0), (0, pack), (0, 0))),
                    jnp.pad(w4, ((0, 0), (pack, 0), (0, 0)))], axis=1)
    min_decay = math.log(HYENA_TARGET) / SLOW_DECAY_PCT
    max_decay = math.log(HYENA_TARGET) / FAST_DECAY_PCT
    deltas = jnp.linspace(min_decay, max_decay, D_HYENA, dtype=F32)
    deltas = jnp.tile(deltas, HYENA_ORDER)[None, :]
    width = HYENA_ORDER * D_HYENA
    sq = _const_spec((LANES, LANES))
    vec = _const_spec((1, LANES))
    filt, sums = pl.pallas_call(
        functools.partial(_filter_kernel, nblk=nblk),
        grid=(2 * nblk,),
        in_specs=[pl.BlockSpec((half, LANES), lambda i: (i, 0)),
                  sq, vec, sq, vec, sq, vec, vec,
                  pl.BlockSpec((1, 2, LANES, width), lambda i: (i // nblk, 0, 0, 0)),
                  _const_spec((1, width))],
        out_specs=[pl.BlockSpec((FILT_ROWS, width), lambda i: (i, 0)),
                   pl.BlockSpec((1, 1, width), lambda i: (i // nblk, 0, 0))],
        out_shape=[jax.ShapeDtypeStruct((n, width), F32),
                   jax.ShapeDtypeStruct((2, 1, width), F32)],
        compiler_params=pltpu.CompilerParams(
            dimension_semantics=("arbitrary",), vmem_limit_bytes=VMEM_LIMIT),
        name="hyena_filter",
    )(zz, w1, padv(p['filt_b1']), w2, padv(p['filt_b2']), w3, padv(p['filt_b3']),
      padv(p['filt_freq']), w4, deltas)
    n1, h1 = tb['n1'], tb['h1']
    return pl.pallas_call(
        functools.partial(_kspec_kernel, n1=n1),
        grid=(width // LANES,),
        in_specs=[pl.BlockSpec((n, LANES), lambda j: (0, j), pipeline_mode=pl.Buffered(1)),
                  pl.BlockSpec((2, 1, LANES), lambda j: (0, 0, j)),
                  _const_spec((n1, n1)), _const_spec((h1, 2 * FFT_N2, 2 * FFT_N2))],
        out_specs=pl.BlockSpec((n, LANES), lambda j: (0, j)),
        out_shape=jax.ShapeDtypeStruct((n, width), F32),
        scratch_shapes=[pltpu.VMEM((h1 * A_PITCH, LANES), F32),
                        pltpu.VMEM((n1 * T_PITCH, LANES), F32)],
        compiler_params=pltpu.CompilerParams(
            dimension_semantics=("parallel",), vmem_limit_bytes=VMEM_LIMIT),
        name="hyena_filter_spectrum",
    )(filt, sums, _bf16_const(tb['m1_full']), _bf16_const(tb['m2']))


CHUNK_UNROLL = 8


def _for_chunks(count, body):
    body(0, True, False)

    def step(t1, carry):
        body(t1, False, False)
        return carry

    lax.fori_loop(1, count - CHUNK_UNROLL + 1, step, 0, unroll=CHUNK_UNROLL)
    for t1 in range(count - CHUNK_UNROLL + 1, count):
        body(t1, False, t1 == count - 1)


def _short_conv_rows(ref, w_ref, b_ref, r0, nrows, first, last):
    x = ref[pl.ds(r0, nrows), :]
    rid = lax.broadcasted_iota(jnp.int32, x.shape, 0)
    if first:
        prev = jnp.where(rid == 0, 0.0, pltpu.roll(x, 1, axis=0))
    else:
        prev = ref[pl.ds(r0 - 1, nrows), :]
    if last:
        nxt = jnp.where(rid == nrows - 1, 0.0, pltpu.roll(x, nrows - 1, axis=0))
    else:
        nxt = ref[pl.ds(r0 + 1, nrows), :]
    return prev * w_ref[0:1, :] + x * w_ref[1:2, :] + nxt * w_ref[2:3, :] + b_ref[...]


def _long_conv_kernel(*refs, n1, conv_z, conv_g):
    refs = list(refs)
    a_ref, tp_ref = refs[-2:]
    slabs = a_ref.shape[0]

    def take_operand(conv):
        return [tuple(refs.pop(0) for _ in range(3 if conv else 1)) for _ in range(slabs)]

    z_ops = take_operand(conv_z)
    g_ops = take_operand(conv_g)
    k_ref, bias_ref, m1_ref, m1i_ref, m2_ref, m2i_ref, o_ref = refs[:-2]
    n2 = FFT_N2
    h1 = n1 // 2

    def put(ref, idx, x):
        for s in range(slabs):
            ref[s, idx, :] = x[:, s * LANES:(s + 1) * LANES]

    def get(ref, idx):
        return jnp.concatenate([ref[s, idx, :] for s in range(slabs)], axis=1)

    def rows_of(ops, conv, t1, first, last):
        r0 = t1 * n2 if isinstance(t1, int) else pl.multiple_of(t1 * n2, n2)
        if conv:
            parts = [_short_conv_rows(ref, w_ref, b_ref, r0, n2, first, last) for ref, w_ref, b_ref in ops]
        else:
            parts = [ref[pl.ds(r0, n2), :] for ref, in ops]
        return jnp.concatenate(parts, axis=1)

    def z_rows(t1, first, last):
        return rows_of(z_ops, conv_z, t1, first, last)

    def g_rows(t1, first, last):
        return rows_of(g_ops, conv_g, t1, first, last)

    def pitched(t1):
        r0 = t1 * T_PITCH
        return pl.ds(r0 if isinstance(t1, int) else pl.multiple_of(r0, 8), n2)

    def pad(t1, first, last):
        put(tp_ref, pitched(t1), z_rows(t1, first, last))

    _for_chunks(h1, pad)

    def stage1(t2, carry):
        zt = get(tp_ref, pl.ds(t2, h1, stride=T_PITCH))
        a = _bdot(m1_ref[...], zt.astype(BF16))
        put(a_ref, pl.ds(t2, h1, stride=A_PITCH), a[:h1])
        put(a_ref, pl.ds(t2 + n2, h1, stride=A_PITCH), a[h1:])
        return carry

    lax.fori_loop(0, n2, stage1, 0, unroll=4)

    def stage2(f1, carry):
        ra = pl.ds(pl.multiple_of(f1 * A_PITCH, 8), 2 * n2)
        rk = pl.multiple_of(f1 * 2 * n2, 2 * n2)
        x = _bdot(m2_ref[f1], get(a_ref, ra).astype(BF16))
        kk = k_ref[pl.ds(rk, 2 * n2), :]
        xr, xi, kr, ki = x[:n2], x[n2:], kk[:n2], kk[n2:]
        y = jnp.concatenate([xr * kr - xi * ki, xr * ki + xi * kr], axis=0)
        put(a_ref, ra, _bdot(m2i_ref[f1], y.astype(BF16)))
        return carry

    lax.fori_loop(0, h1, stage2, 0, unroll=16)

    def stage3(t2, carry):
        br = get(a_ref, pl.ds(t2, h1, stride=A_PITCH))
        bi = get(a_ref, pl.ds(t2 + n2, h1, stride=A_PITCH))
        y = _bdot(m1i_ref[...], jnp.concatenate([br, bi], axis=0).astype(BF16))
        put(tp_ref, pl.ds(t2, h1, stride=T_PITCH), y)
        return carry

    lax.fori_loop(0, n2, stage3, 0, unroll=4)

    bias = bias_ref[...]

    def finish(t1, first, last):
        r0 = t1 * n2 if isinstance(t1, int) else pl.multiple_of(t1 * n2, n2)
        y = get(tp_ref, pitched(t1))
        o_ref[pl.ds(r0, n2), :] = g_rows(t1, first, last) * (y + bias * z_rows(t1, first, last))

    _for_chunks(h1, finish)


def _bf16_const(a):
    return jnp.asarray(a, F32).astype(BF16)


def _long_conv_slabs(seq):
    n1 = 2 * seq // FFT_N2
    h1 = n1 // 2
    tables = 2 * (2 * n1 * h1 + 2 * h1 * (2 * FFT_N2) ** 2)
    for slabs in (2, 1):
        io = 3 * 2 * seq * 4
        per_lane = io + 2 * seq * 4 + h1 * (A_PITCH + T_PITCH) * 4
        if per_lane * slabs * LANES + tables <= VMEM_LIMIT - VMEM_HEADROOM:
            return slabs
    raise ValueError(f"long conv of length {seq} does not fit VMEM")


def _long_conv(z, z_off, gate, g_off, kspec, k_off, bias, batch, seq, short=None, conv_z=False):
    n = 2 * seq
    tb = _fft_tables(seq)
    n1, h1 = tb['n1'], tb['h1']
    conv_g = short is not None
    slabs = _long_conv_slabs(seq)
    width = slabs * LANES
    assert z_off % slabs == 0 and g_off % slabs == 0 and k_off % slabs == 0

    def operand(arr, off, conv):
        specs, args = [], []
        for s in range(slabs):
            specs.append(pl.BlockSpec((seq, LANES), lambda j, b, s=s: (b, off + j * slabs + s)))
            args.append(arr)
            if conv:
                specs += [pl.BlockSpec((3, LANES), lambda j, b, s=s: (0, off + j * slabs + s)),
                          pl.BlockSpec((1, LANES), lambda j, b, s=s: (0, off + j * slabs + s))]
                args += list(short)
        return specs, args

    z_specs, z_args = operand(z, z_off, conv_z)
    g_specs, g_args = operand(gate, g_off, conv_g)
    k_blk = k_off // slabs
    return pl.pallas_call(
        functools.partial(_long_conv_kernel, n1=n1, conv_z=conv_z, conv_g=conv_g),
        grid=(D_HYENA // width, batch),
        in_specs=z_specs + g_specs + [
                  pl.BlockSpec((n, width), lambda j, b: (0, k_blk + j), pipeline_mode=pl.Buffered(1)),
                  pl.BlockSpec((1, width), lambda j, b: (0, j)),
                  _const_spec((n1, h1)), _const_spec((h1, n1)),
                  _const_spec((h1, 2 * FFT_N2, 2 * FFT_N2)),
                  _const_spec((h1, 2 * FFT_N2, 2 * FFT_N2))],
        out_specs=pl.BlockSpec((seq, width), lambda j, b: (b, j)),
        out_shape=jax.ShapeDtypeStruct((batch * seq, D_HYENA), F32),
        scratch_shapes=[pltpu.VMEM((slabs, h1 * A_PITCH, LANES), F32),
                        pltpu.VMEM((slabs, h1 * T_PITCH, LANES), F32)],
        compiler_params=pltpu.CompilerParams(
            dimension_semantics=("parallel", "parallel"), vmem_limit_bytes=VMEM_LIMIT),
        name="hyena_long_conv",
    )(*z_args, *g_args, kspec, bias, _bf16_const(tb['m1']), _bf16_const(tb['m1_inv']),
      _bf16_const(tb['m2']), _bf16_const(tb['m2_inv']))


def _layer(x3, p):
    batch, seq, _ = x3.shape
    x = x3.reshape(batch * seq, D_MODEL)
    x = _ffn(x, p['ffn1_pre_g'], p['ffn1_w1'], p['ffn1_w3'], p['ffn1_w2'], p['ffn1_post_g'])
    u = _inproj(x, p['mix_pre_g'], p['w_in'], seq)
    kspec = _filter_spectra(seq, p)
    tiles = D_HYENA // LANES
    short = (p['short_w'], p['short_b'])
    z1 = _long_conv(u, 0, u, tiles, kspec, 0, p['hyena_bias'][0:1], batch, seq, short=short, conv_z=True)
    yh = _long_conv(z1, 0, u, 2 * tiles, kspec, tiles, p['hyena_bias'][1:2], batch, seq, short=short)
    yr = _retention(u, p['ret_log_decay_f'], p['ret_log_decay_b'], batch, seq)
    mix = (yh, yr, p['w_out'][:D_HYENA], p['w_out'][D_HYENA:], p['mix_post_g'])
    x = _ffn(x, p['ffn2_pre_g'], p['ffn2_w1'], p['ffn2_w3'], p['ffn2_w2'], p['ffn2_post_g'], mix=mix)
    return x.reshape(batch, seq, D_MODEL)


_MATRIX_PARAMS = ('ffn1_w1', 'ffn1_w3', 'ffn1_w2', 'w_in', 'w_out', 'ffn2_w1', 'ffn2_w3', 'ffn2_w2')
_GAIN_PARAMS = ('ffn1_pre_g', 'ffn1_post_g', 'mix_pre_g', 'mix_post_g', 'ffn2_pre_g', 'ffn2_post_g')


def kernel(x_prompt, x_sample, ffn1_pre_g, ffn1_w1, ffn1_w3, ffn1_w2, ffn1_post_g, mix_pre_g, w_in, short_w, short_b, filt_w1, filt_b1, filt_w2, filt_b2, filt_w3, filt_b3, filt_w4, filt_freq, hyena_bias, ret_log_decay_f, ret_log_decay_b, w_out, mix_post_g, ffn2_pre_g, ffn2_w1, ffn2_w3, ffn2_w2, ffn2_post_g):
    params = dict(ffn1_pre_g=ffn1_pre_g, ffn1_w1=ffn1_w1, ffn1_w3=ffn1_w3, ffn1_w2=ffn1_w2,
                  ffn1_post_g=ffn1_post_g, mix_pre_g=mix_pre_g, w_in=w_in, short_w=short_w,
                  short_b=short_b, filt_w1=filt_w1, filt_b1=filt_b1, filt_w2=filt_w2,
                  filt_b2=filt_b2, filt_w3=filt_w3, filt_b3=filt_b3, filt_w4=filt_w4,
                  filt_freq=filt_freq, hyena_bias=hyena_bias, ret_log_decay_f=ret_log_decay_f,
                  ret_log_decay_b=ret_log_decay_b, w_out=w_out, mix_post_g=mix_post_g,
                  ffn2_pre_g=ffn2_pre_g, ffn2_w1=ffn2_w1, ffn2_w3=ffn2_w3, ffn2_w2=ffn2_w2,
                  ffn2_post_g=ffn2_post_g)
    depth = ffn1_w1.shape[0]

    def run(x):
        for l in range(depth):
            p = {k: v[l] for k, v in params.items()}
            for k in _MATRIX_PARAMS:
                p[k] = p[k].astype(BF16)
            for k in _GAIN_PARAMS:
                p[k] = p[k][None, :]
            p['short_b'] = p['short_b'][None, :]
            x = _layer(x, p)
        return x

    return (run(x_prompt), run(x_sample))
```

```python
import functools
import math

import numpy as np
import jax
import jax.numpy as jnp
from jax import lax
from jax.experimental import pallas as pl
from jax.experimental.pallas import tpu as pltpu

F32 = jnp.float32
BF16 = jnp.bfloat16

D_MODEL = 1024
D_HYENA = 512
D_RET = 512
HYENA_ORDER = 2
N_RET_HEADS = 4
RET_HEAD_DIM = 128
D_FF = 2816
FILT_EMB = 33
FILT_BANDS = 16
FILT_HIDDEN = 64
ROPE_BASE = 10000.0
NORM_EPS = 1e-6
HYENA_TARGET = 1e-2
FAST_DECAY_PCT = 0.3
SLOW_DECAY_PCT = 1.5
N_HY_COLS = (HYENA_ORDER + 1) * D_HYENA
D_IN = N_HY_COLS + 4 * D_RET

LANES = 128
VMEM_LIMIT = 56 * 1024 * 1024
VMEM_HEADROOM = 4 * 1024 * 1024
FFT_N2 = 32
RET_CHUNK = 128
RET_GROUP = 16
TOKEN_TILE = 512


def _const_spec(shape):
    nd = len(shape)
    return pl.BlockSpec(shape, lambda *_: (0,) * nd, pipeline_mode=pl.Buffered(1))


def _rms(x, g):
    ms = jnp.mean(x * x, axis=-1, keepdims=True)
    return x * lax.rsqrt(ms + NORM_EPS) * g


def _bdot(a, b):
    return jnp.dot(a, b, preferred_element_type=F32)


def _ffn_core(x, pre_ref, w1_ref, w3_ref, w2_ref, post_ref, o_ref):
    h = _rms(x, pre_ref[...]).astype(BF16)
    a = _bdot(h, w1_ref[...])
    b = _bdot(h, w3_ref[...])
    g = (a * jax.nn.sigmoid(a) * b).astype(BF16)
    y = _bdot(g, w2_ref[...])
    o_ref[...] = x + 0.5 * _rms(y, post_ref[...])


def _ffn_kernel(x_ref, pre_ref, w1_ref, w3_ref, w2_ref, post_ref, o_ref):
    _ffn_core(x_ref[...], pre_ref, w1_ref, w3_ref, w2_ref, post_ref, o_ref)


def _mix_ffn_kernel(x_ref, yh_ref, yr_ref, woh_ref, wor_ref, mg_ref,
                    pre_ref, w1_ref, w3_ref, w2_ref, post_ref, o_ref):
    y = _bdot(yh_ref[...].astype(BF16), woh_ref[...])
    y = y + _bdot(yr_ref[...].astype(BF16), wor_ref[...])
    x = x_ref[...] + _rms(y, mg_ref[...])
    _ffn_core(x, pre_ref, w1_ref, w3_ref, w2_ref, post_ref, o_ref)


def _row_spec(tm, width):
    return pl.BlockSpec((tm, width), lambda i: (i, 0))


def _ffn(x, pre_g, w1, w3, w2, post_g, mix=None):
    t = x.shape[0]
    tm = TOKEN_TILE
    ffn_specs = [_const_spec((1, D_MODEL)), _const_spec((D_MODEL, D_FF)),
                 _const_spec((D_MODEL, D_FF)), _const_spec((D_FF, D_MODEL)),
                 _const_spec((1, D_MODEL))]
    ffn_args = (pre_g, w1, w3, w2, post_g)
    if mix is None:
        body, args = _ffn_kernel, (x,) + ffn_args
        specs = [_row_spec(tm, D_MODEL)] + ffn_specs
    else:
        yh, yr, woh, wor, mg = mix
        body, args = _mix_ffn_kernel, (x, yh, yr, woh, wor, mg) + ffn_args
        specs = [_row_spec(tm, D_MODEL), _row_spec(tm, D_HYENA), _row_spec(tm, D_RET),
                 _const_spec((D_HYENA, D_MODEL)), _const_spec((D_RET, D_MODEL)),
                 _const_spec((1, D_MODEL))] + ffn_specs
    return pl.pallas_call(
        body,
        grid=(t // tm,),
        in_specs=specs,
        out_specs=_row_spec(tm, D_MODEL),
        out_shape=jax.ShapeDtypeStruct((t, D_MODEL), F32),
        compiler_params=pltpu.CompilerParams(
            dimension_semantics=("parallel",), vmem_limit_bytes=VMEM_LIMIT),
        name="ffn_mix" if mix is not None else "ffn",
    )(*args)


def _inproj_kernel(x_ref, g_ref, w_ref, cc_ref, ss_ref, o_ref):
    h = _rms(x_ref[...], g_ref[...]).astype(BF16)
    qk0 = N_HY_COLS
    vg0 = N_HY_COLS + 2 * D_RET
    o_ref[:, :qk0] = _bdot(h, w_ref[:, :qk0])
    o_ref[:, vg0:] = _bdot(h, w_ref[:, vg0:])
    qk = _bdot(h, w_ref[:, qk0:vg0])
    cc = cc_ref[...]
    ss = ss_ref[...]
    d = RET_HEAD_DIM
    for blk in range(2 * N_RET_HEADS):
        x = qk[:, blk * d:(blk + 1) * d]
        r = x * cc + pltpu.roll(x, d // 2, axis=1) * ss
        if blk >= N_RET_HEADS:
            r = r * (d ** -0.5)
        o_ref[:, qk0 + blk * d:qk0 + (blk + 1) * d] = r


def _inproj(x, g, w, seq):
    t = x.shape[0]
    tm = TOKEN_TILE
    cc, ss = _rope_tables(seq)
    pos_blocks = seq // tm
    rope_spec = pl.BlockSpec((tm, RET_HEAD_DIM), lambda i: (i % pos_blocks, 0))
    return pl.pallas_call(
        _inproj_kernel,
        grid=(t // tm,),
        in_specs=[_row_spec(tm, D_MODEL), _const_spec((1, D_MODEL)),
                  _const_spec((D_MODEL, D_IN)), rope_spec, rope_spec],
        out_specs=_row_spec(tm, D_IN),
        out_shape=jax.ShapeDtypeStruct((t, D_IN), F32),
        compiler_params=pltpu.CompilerParams(
            dimension_semantics=("parallel",), vmem_limit_bytes=VMEM_LIMIT),
        name="inproj",
    )(x, g, w, cc, ss)


def _ret_kernel(lgf_ref, lgb_ref, q_ref, k_ref, v_ref, g_ref, o_ref, qs_ref, kt_ref, *, seq):
    c = RET_CHUNK
    d = RET_HEAD_DIM
    n_chunks = seq // c
    head = pl.program_id(1)
    lgf = jnp.full((c, d), lgf_ref[head], F32)
    lgb = jnp.full((c, d), lgb_ref[head], F32)
    row = lax.broadcasted_iota(jnp.int32, (c, d), 0).astype(F32)
    col = lax.broadcasted_iota(jnp.int32, (c, d), 1).astype(F32)
    diff = row - col
    dmat = jnp.where(diff >= 0.0, jnp.exp(jnp.maximum(diff, 0.0) * lgf),
                     jnp.exp(jnp.maximum(-diff, 0.0) * lgb))
    wq_f = jnp.exp((row + 1.0) * lgf)
    wk_f = jnp.exp((c - 1.0 - row) * lgf)
    wq_b = jnp.exp((c - row) * lgb)
    wk_b = jnp.exp(row * lgb)
    gc_f = jnp.exp(c * lgf)
    gc_b = jnp.exp(c * lgb)

    grp = RET_GROUP
    n_groups = n_chunks // grp

    def fwd(gi, state):
        rows = [pl.ds(pl.multiple_of((gi * grp + j) * c, c), c) for j in range(grp)]
        qbs, kts, vs = [], [], []
        for r in rows:
            qb = q_ref[r, :].astype(BF16)
            ktb = k_ref[r, :].T.astype(BF16)
            qs_ref[r, :] = qb
            kt_ref[r, :] = ktb
            qbs.append(qb)
            kts.append(ktb)
            vs.append(v_ref[r, :])
        scores = [_bdot(qb, ktb) for qb, ktb in zip(qbs, kts)]
        kvs = [_bdot(ktb, (v * wk_f).astype(BF16)) for ktb, v in zip(kts, vs)]
        states = []
        for kv in kvs:
            states.append(state)
            state = state * gc_f + kv
        intras = [_bdot((s * dmat).astype(BF16), v.astype(BF16)) for s, v in zip(scores, vs)]
        crosses = [_bdot(qb, st.astype(BF16)) for qb, st in zip(qbs, states)]
        for r, intra, cross in zip(rows, intras, crosses):
            o_ref[r, :] = intra + wq_f * cross
        return state

    lax.fori_loop(0, n_groups, fwd, jnp.zeros((d, d), F32))

    def bwd(gi, state):
        rows = [pl.ds(pl.multiple_of((n_chunks - 1 - gi * grp - j) * c, c), c) for j in range(grp)]
        kvs = [_bdot(kt_ref[r, :], (v_ref[r, :] * wk_b).astype(BF16)) for r in rows]
        states = []
        for kv in kvs:
            states.append(state)
            state = state * gc_b + kv
        crosses = [_bdot(qs_ref[r, :], st.astype(BF16)) for r, st in zip(rows, states)]
        outs = [o_ref[r, :] + wq_b * cross for r, cross in zip(rows, crosses)]
        means = [jnp.mean(o * o, axis=-1, keepdims=True) for o in outs]
        norms = [lax.rsqrt(m + NORM_EPS) for m in means]
        gates = [g_ref[r, :] for r in rows]
        gates = [g * jax.nn.sigmoid(g) for g in gates]
        for r, o, nrm, g in zip(rows, outs, norms, gates):
            o_ref[r, :] = g * (o * nrm)
        return state

    lax.fori_loop(0, n_groups, bwd, jnp.zeros((d, d), F32))


def _rope_tables(seq):
    d = RET_HEAD_DIM
    inv = 1.0 / (ROPE_BASE ** (jnp.arange(0, d, 2, dtype=F32) / d))
    ang = jnp.arange(seq, dtype=F32)[:, None] * inv[None, :]
    c, s = jnp.cos(ang), jnp.sin(ang)
    return jnp.concatenate([c, c], axis=-1), jnp.concatenate([-s, s], axis=-1)


def _retention(u, lg_f, lg_b, batch, seq):
    assert RET_CHUNK == RET_HEAD_DIM
    first = N_HY_COLS // LANES
    heads = N_RET_HEADS

    def col(off):
        return pl.BlockSpec((seq, LANES), lambda b, h, *_: (b, first + off * heads + h))

    grid_spec = pltpu.PrefetchScalarGridSpec(
        num_scalar_prefetch=2,
        grid=(batch, heads),
        in_specs=[col(0), col(1), col(2), col(3)],
        out_specs=pl.BlockSpec((seq, LANES), lambda b, h, *_: (b, h)),
        scratch_shapes=[pltpu.VMEM((seq, RET_HEAD_DIM), BF16), pltpu.VMEM((seq, RET_CHUNK), BF16)],
    )
    return pl.pallas_call(
        functools.partial(_ret_kernel, seq=seq),
        grid_spec=grid_spec,
        out_shape=jax.ShapeDtypeStruct((batch * seq, D_RET), F32),
        compiler_params=pltpu.CompilerParams(
            dimension_semantics=("parallel", "parallel"), vmem_limit_bytes=VMEM_LIMIT),
        name="retention",
    )(lg_f, lg_b, u, u, u, u)


def _split(x):
    hi = x.astype(BF16)
    return hi, (x - hi.astype(F32)).astype(BF16)


def _dot3(a, b):
    a_hi, a_lo = a
    b_hi, b_lo = b
    return _bdot(a_hi, b_hi) + _bdot(a_hi, b_lo) + _bdot(a_lo, b_hi)


@functools.lru_cache(maxsize=None)
def _fft_tables(seq):
    n = 2 * seq
    n2 = FFT_N2
    n1 = n // n2
    h1 = n1 // 2
    f1 = np.arange(h1, dtype=np.float64) + 0.5
    th = 2.0 * np.pi * f1[:, None] * np.arange(n1, dtype=np.float64)[None, :] / n1
    m1_full = np.concatenate([np.cos(th), -np.sin(th)], axis=0)
    m1 = m1_full[:, :h1]
    m1_inv = (2.0 / n) * m1.T
    t2 = np.arange(n2, dtype=np.float64)
    phi = 2.0 * np.pi * (np.arange(n2, dtype=np.float64)[None, :, None] * t2[None, None, :] / n2
                         + f1[:, None, None] * t2[None, None, :] / n)
    gr, gi = np.cos(phi), -np.sin(phi)
    m2 = np.concatenate([np.concatenate([gr, -gi], axis=2),
                         np.concatenate([gi, gr], axis=2)], axis=1)
    m2_inv = np.transpose(m2, (0, 2, 1))
    return dict(n1=n1, h1=h1, m1_full=m1_full, m1=m1, m1_inv=m1_inv, m2=m2, m2_inv=m2_inv)


FILT_ROWS = 512
FILT_PACK = LANES // 2
PITCH_ALIGN = 4
A_PITCH = 2 * FFT_N2 + PITCH_ALIGN
T_PITCH = FFT_N2 + PITCH_ALIGN


def _filter_kernel(z_ref, w1_ref, b1_ref, w2_ref, b2_ref, w3_ref, b3_ref, fr_ref,
                   w4_ref, dl_ref, o_ref, sum_ref, *, nblk):
    i = pl.program_id(0)
    half = FILT_ROWS // 2
    fr = fr_ref[...]
    z = z_ref[...]
    h = jnp.sin(fr * (_dot3(_split(z), _split(w1_ref[...])) + b1_ref[...]))
    h = jnp.sin(fr * (_dot3(_split(h), _split(w2_ref[...])) + b2_ref[...]))
    h = jnp.sin(fr * (_dot3(_split(h), _split(w3_ref[...])) + b3_ref[...]))
    hs = _split(h)
    adl = jnp.abs(dl_ref[...])
    backward = i >= nblk
    for side in range(2):
        lane = side * FILT_PACK
        t = jnp.broadcast_to(z[:, lane:lane + 1], z.shape)
        out_rows = slice(side * half, (side + 1) * half)
        for c in range(HYENA_ORDER * D_HYENA // LANES):
            cols = slice(c * LANES, (c + 1) * LANES)
            hc = _dot3(hs, _split(w4_ref[0, side, :, cols])) * jnp.exp(-t * adl[:, cols])
            part = jnp.sum(jnp.abs(hc), axis=0, keepdims=True)

            if side == 0:
                @pl.when(i % nblk == 0)
                def _():
                    sum_ref[0, :, cols] = part

                @pl.when(i % nblk != 0)
                def _():
                    sum_ref[0, :, cols] = sum_ref[0, :, cols] + part

                rows = lax.broadcasted_iota(jnp.int32, hc.shape, 0)
                flipped = jnp.where(jnp.logical_and(rows == 0, i == nblk), 0.0, -hc)
            else:
                sum_ref[0, :, cols] = sum_ref[0, :, cols] + part
                flipped = -hc
            o_ref[out_rows, cols] = jnp.where(backward, flipped, hc)


def _kspec_kernel(k_ref, s_ref, m1_ref, m2_ref, o_ref, a_ref, kp_ref, *, n1):
    n2 = FFT_N2
    h1 = n1 // 2
    inv_f = 1.0 / s_ref[0]
    inv_b = 1.0 / s_ref[1]

    def pad(t1, carry):
        src = pl.multiple_of(t1 * n2, n2)
        dst = pl.multiple_of(t1 * T_PITCH, PITCH_ALIGN)
        kp_ref[pl.ds(dst, n2), :] = k_ref[pl.ds(src, n2), :] * jnp.where(t1 < h1, inv_f, inv_b)
        return carry

    lax.fori_loop(0, n1, pad, 0, unroll=8)

    def stage1(t2, carry):
        kt = kp_ref[pl.ds(t2, n1, stride=T_PITCH), :]
        a = _bdot(m1_ref[...], kt.astype(BF16))
        a_ref[pl.ds(t2, h1, stride=A_PITCH), :] = a[:h1]
        a_ref[pl.ds(t2 + n2, h1, stride=A_PITCH), :] = a[h1:]
        return carry

    lax.fori_loop(0, n2, stage1, 0, unroll=4)

    def stage2(f1, carry):
        src = pl.multiple_of(f1 * A_PITCH, PITCH_ALIGN)
        dst = pl.multiple_of(f1 * 2 * n2, 2 * n2)
        o_ref[pl.ds(dst, 2 * n2), :] = _bdot(m2_ref[f1], a_ref[pl.ds(src, 2 * n2), :].astype(BF16))
        return carry

    lax.fori_loop(0, h1, stage2, 0, unroll=16)


def _filter_spectra(seq, p):
    n = 2 * seq
    tb = _fft_tables(seq)
    fwd_pos = jnp.arange(seq, dtype=jnp.int32)
    pos = jnp.concatenate([fwd_pos, (seq - fwd_pos) % seq])
    t_fwd = jnp.linspace(0.0, 1.0, seq, dtype=F32)
    t = jnp.concatenate([t_fwd, jnp.roll(t_fwd[::-1], 1)])[:, None]
    w = 2.0 * math.pi * pos.astype(F32)[:, None] / seq
    fb = jnp.linspace(1e-4, FILT_BANDS - 1, FILT_BANDS, dtype=F32)[None, :]
    z = jnp.concatenate([t, jnp.cos(fb * w), -jnp.sin(fb * w)], axis=-1)
    nblk = seq // FILT_ROWS
    half = FILT_ROWS // 2
    pack = FILT_PACK
    zz = jnp.pad(z, ((0, 0), (0, pack - FILT_EMB))).reshape(2 * nblk, 2, half, pack)
    zz = jnp.transpose(zz, (0, 2, 1, 3)).reshape(n // 2, LANES)

    def padw(a):
        a = jnp.pad(a, ((0, pack - a.shape[0]), (0, pack - a.shape[1])))
        zero = jnp.zeros_like(a)
        return jnp.concatenate([jnp.concatenate([a, zero], axis=1),
                                jnp.concatenate([zero, a], axis=1)], axis=0)

    def padv(a):
        return jnp.tile(jnp.pad(a, (0, pack - a.shape[0])), 2)[None, :]

    w1 = padw(p['filt_w1'])
    w2 = padw(p['filt_w2'])
    w3 = padw(p['filt_w3'])
    w4 = jnp.pad(p['filt_w4'], ((0, pack - FILT_HIDDEN), (0, 0))).reshape(pack, HYENA_ORDER, 2, D_HYENA)
    w4 = jnp.transpose(w4, (2, 0, 1, 3)).reshape(2, pack, HYENA_ORDER * D_HYENA)
    w4 = jnp.stack([jnp.pad(w4, ((0, 0), (0, pack), (0, 0))),
                    jnp.pad(w4, ((0, 0), (pack, 0), (0, 0)))], axis=1)
    min_decay = math.log(HYENA_TARGET) / SLOW_DECAY_PCT
    max_decay = math.log(HYENA_TARGET) / FAST_DECAY_PCT
    deltas = jnp.linspace(min_decay, max_decay, D_HYENA, dtype=F32)
    deltas = jnp.tile(deltas, HYENA_ORDER)[None, :]
    width = HYENA_ORDER * D_HYENA
    sq = _const_spec((LANES, LANES))
    vec = _const_spec((1, LANES))
    filt, sums = pl.pallas_call(
        functools.partial(_filter_kernel, nblk=nblk),
        grid=(2 * nblk,),
        in_specs=[pl.BlockSpec((half, LANES), lambda i: (i, 0)),
                  sq, vec, sq, vec, sq, vec, vec,
                  pl.BlockSpec((1, 2, LANES, width), lambda i: (i // nblk, 0, 0, 0)),
                  _const_spec((1, width))],
        out_specs=[pl.BlockSpec((FILT_ROWS, width), lambda i: (i, 0)),
                   pl.BlockSpec((1, 1, width), lambda i: (i // nblk, 0, 0))],
        out_shape=[jax.ShapeDtypeStruct((n, width), F32),
                   jax.ShapeDtypeStruct((2, 1, width), F32)],
        compiler_params=pltpu.CompilerParams(
            dimension_semantics=("arbitrary",), vmem_limit_bytes=VMEM_LIMIT),
        name="hyena_filter",
    )(zz, w1, padv(p['filt_b1']), w2, padv(p['filt_b2']), w3, padv(p['filt_b3']),
      padv(p['filt_freq']), w4, deltas)
    n1, h1 = tb['n1'], tb['h1']
    return pl.pallas_call(
        functools.partial(_kspec_kernel, n1=n1),
        grid=(width // LANES,),
        in_specs=[pl.BlockSpec((n, LANES), lambda j: (0, j), pipeline_mode=pl.Buffered(1)),
                  pl.BlockSpec((2, 1, LANES), lambda j: (0, 0, j)),
                  _const_spec((n1, n1)), _const_spec((h1, 2 * FFT_N2, 2 * FFT_N2))],
        out_specs=pl.BlockSpec((n, LANES), lambda j: (0, j)),
        out_shape=jax.ShapeDtypeStruct((n, width), F32),
        scratch_shapes=[pltpu.VMEM((h1 * A_PITCH, LANES), F32),
                        pltpu.VMEM((n1 * T_PITCH, LANES), F32)],
        compiler_params=pltpu.CompilerParams(
            dimension_semantics=("parallel",), vmem_limit_bytes=VMEM_LIMIT),
        name="hyena_filter_spectrum",
    )(filt, sums, _bf16_const(tb['m1_full']), _bf16_const(tb['m2']))


CHUNK_UNROLL = 8


def _for_chunks(count, body):
    body(0, True, False)

    def step(t1, carry):
        body(t1, False, False)
        return carry

    lax.fori_loop(1, count - CHUNK_UNROLL + 1, step, 0, unroll=CHUNK_UNROLL)
    for t1 in range(count - CHUNK_UNROLL + 1, count):
        body(t1, False, t1 == count - 1)


def _short_conv_rows(ref, w_ref, b_ref, r0, nrows, first, last):
    x = ref[pl.ds(r0, nrows), :]
    rid = lax.broadcasted_iota(jnp.int32, x.shape, 0)
    if first:
        prev = jnp.where(rid == 0, 0.0, pltpu.roll(x, 1, axis=0))
    else:
        prev = ref[pl.ds(r0 - 1, nrows), :]
    if last:
        nxt = jnp.where(rid == nrows - 1, 0.0, pltpu.roll(x, nrows - 1, axis=0))
    else:
        nxt = ref[pl.ds(r0 + 1, nrows), :]
    return prev * w_ref[0:1, :] + x * w_ref[1:2, :] + nxt * w_ref[2:3, :] + b_ref[...]


def _long_conv_kernel(*refs, n1, conv_z, conv_g):
    refs = list(refs)
    a_ref, tp_ref = refs[-2:]
    slabs = a_ref.shape[0]

    def take_operand(conv):
        return [tuple(refs.pop(0) for _ in range(3 if conv else 1)) for _ in range(slabs)]

    z_ops = take_operand(conv_z)
    g_ops = take_operand(conv_g)
    k_ref, bias_ref, m1_ref, m1i_ref, m2_ref, m2i_ref, o_ref = refs[:-2]
    n2 = FFT_N2
    h1 = n1 // 2

    def put(ref, idx, x):
        for s in range(slabs):
            ref[s, idx, :] = x[:, s * LANES:(s + 1) * LANES]

    def get(ref, idx):
        return jnp.concatenate([ref[s, idx, :] for s in range(slabs)], axis=1)

    def rows_of(ops, conv, t1, first, last):
        r0 = t1 * n2 if isinstance(t1, int) else pl.multiple_of(t1 * n2, n2)
        if conv:
            parts = [_short_conv_rows(ref, w_ref, b_ref, r0, n2, first, last) for ref, w_ref, b_ref in ops]
        else:
            parts = [ref[pl.ds(r0, n2), :] for ref, in ops]
        return jnp.concatenate(parts, axis=1)

    def z_rows(t1, first, last):
        return rows_of(z_ops, conv_z, t1, first, last)

    def g_rows(t1, first, last):
        return rows_of(g_ops, conv_g, t1, first, last)

    def pitched(t1):
        r0 = t1 * T_PITCH
        return pl.ds(r0 if isinstance(t1, int) else pl.multiple_of(r0, PITCH_ALIGN), n2)

    def pad(t1, first, last):
        put(tp_ref, pitched(t1), z_rows(t1, first, last))

    _for_chunks(h1, pad)

    def stage1(t2, carry):
        zt = get(tp_ref, pl.ds(t2, h1, stride=T_PITCH))
        a = _bdot(m1_ref[...], zt.astype(BF16))
        put(a_ref, pl.ds(t2, h1, stride=A_PITCH), a[:h1])
        put(a_ref, pl.ds(t2 + n2, h1, stride=A_PITCH), a[h1:])
        return carry

    lax.fori_loop(0, n2, stage1, 0, unroll=4)

    def stage2(f1, carry):
        ra = pl.ds(pl.multiple_of(f1 * A_PITCH, PITCH_ALIGN), 2 * n2)
        rk = pl.multiple_of(f1 * 2 * n2, 2 * n2)
        x = _bdot(m2_ref[f1], get(a_ref, ra).astype(BF16))
        kk = k_ref[pl.ds(rk, 2 * n2), :]
        xr, xi, kr, ki = x[:n2], x[n2:], kk[:n2], kk[n2:]
        y = jnp.concatenate([xr * kr - xi * ki, xr * ki + xi * kr], axis=0)
        put(a_ref, ra, _bdot(m2i_ref[f1], y.astype(BF16)))
        return carry

    lax.fori_loop(0, h1, stage2, 0, unroll=16)

    def stage3(t2, carry):
        br = get(a_ref, pl.ds(t2, h1, stride=A_PITCH))
        bi = get(a_ref, pl.ds(t2 + n2, h1, stride=A_PITCH))
        y = _bdot(m1i_ref[...], jnp.concatenate([br, bi], axis=0).astype(BF16))
        put(tp_ref, pl.ds(t2, h1, stride=T_PITCH), y)
        return carry

    lax.fori_loop(0, n2, stage3, 0, unroll=4)

    bias = bias_ref[...]

    def finish(t1, first, last):
        r0 = t1 * n2 if isinstance(t1, int) else pl.multiple_of(t1 * n2, n2)
        y = get(tp_ref, pitched(t1))
        o_ref[pl.ds(r0, n2), :] = g_rows(t1, first, last) * (y + bias * z_rows(t1, first, last))

    _for_chunks(h1, finish)


def _bf16_const(a):
    return jnp.asarray(a, F32).astype(BF16)


def _long_conv_slabs(seq):
    n1 = 2 * seq // FFT_N2
    h1 = n1 // 2
    tables = 2 * (2 * n1 * h1 + 2 * h1 * (2 * FFT_N2) ** 2)
    for slabs in (2, 1):
        io = 3 * 2 * seq * 4
        per_lane = io + 2 * seq * 4 + h1 * (A_PITCH + T_PITCH) * 4
        if per_lane * slabs * LANES + tables <= VMEM_LIMIT - VMEM_HEADROOM:
            return slabs
    raise ValueError(f"long conv of length {seq} does not fit VMEM")


def _long_conv(z, z_off, gate, g_off, kspec, k_off, bias, batch, seq, short=None, conv_z=False):
    n = 2 * seq
    tb = _fft_tables(seq)
    n1, h1 = tb['n1'], tb['h1']
    conv_g = short is not None
    slabs = _long_conv_slabs(seq)
    width = slabs * LANES
    assert z_off % slabs == 0 and g_off % slabs == 0 and k_off % slabs == 0

    def operand(arr, off, conv):
        specs, args = [], []
        for s in range(slabs):
            specs.append(pl.BlockSpec((seq, LANES), lambda j, b, s=s: (b, off + j * slabs + s)))
            args.append(arr)
            if conv:
                specs += [pl.BlockSpec((3, LANES), lambda j, b, s=s: (0, off + j * slabs + s)),
                          pl.BlockSpec((1, LANES), lambda j, b, s=s: (0, off + j * slabs + s))]
                args += list(short)
        return specs, args

    z_specs, z_args = operand(z, z_off, conv_z)
    g_specs, g_args = operand(gate, g_off, conv_g)
    k_blk = k_off // slabs
    return pl.pallas_call(
        functools.partial(_long_conv_kernel, n1=n1, conv_z=conv_z, conv_g=conv_g),
        grid=(D_HYENA // width, batch),
        in_specs=z_specs + g_specs + [
                  pl.BlockSpec((n, width), lambda j, b: (0, k_blk + j), pipeline_mode=pl.Buffered(1)),
                  pl.BlockSpec((1, width), lambda j, b: (0, j)),
                  _const_spec((n1, h1)), _const_spec((h1, n1)),
                  _const_spec((h1, 2 * FFT_N2, 2 * FFT_N2)),
                  _const_spec((h1, 2 * FFT_N2, 2 * FFT_N2))],
        out_specs=pl.BlockSpec((seq, width), lambda j, b: (b, j)),
        out_shape=jax.ShapeDtypeStruct((batch * seq, D_HYENA), F32),
        scratch_shapes=[pltpu.VMEM((slabs, h1 * A_PITCH, LANES), F32),
                        pltpu.VMEM((slabs, h1 * T_PITCH, LANES), F32)],
        compiler_params=pltpu.CompilerParams(
            dimension_semantics=("parallel", "parallel"), vmem_limit_bytes=VMEM_LIMIT),
        name="hyena_long_conv",
    )(*z_args, *g_args, kspec, bias, _bf16_const(tb['m1']), _bf16_const(tb['m1_inv']),
      _bf16_const(tb['m2']), _bf16_const(tb['m2_inv']))


def _layer(x3, p):
    batch, seq, _ = x3.shape
    x = x3.reshape(batch * seq, D_MODEL)
    x = _ffn(x, p['ffn1_pre_g'], p['ffn1_w1'], p['ffn1_w3'], p['ffn1_w2'], p['ffn1_post_g'])
    u = _inproj(x, p['mix_pre_g'], p['w_in'], seq)
    kspec = _filter_spectra(seq, p)
    tiles = D_HYENA // LANES
    short = (p['short_w'], p['short_b'])
    z1 = _long_conv(u, 0, u, tiles, kspec, 0, p['hyena_bias'][0:1], batch, seq, short=short, conv_z=True)
    yh = _long_conv(z1, 0, u, 2 * tiles, kspec, tiles, p['hyena_bias'][1:2], batch, seq, short=short)
    yr = _retention(u, p['ret_log_decay_f'], p['ret_log_decay_b'], batch, seq)
    mix = (yh, yr, p['w_out'][:D_HYENA], p['w_out'][D_HYENA:], p['mix_post_g'])
    x = _ffn(x, p['ffn2_pre_g'], p['ffn2_w1'], p['ffn2_w3'], p['ffn2_w2'], p['ffn2_post_g'], mix=mix)
    return x.reshape(batch, seq, D_MODEL)


_MATRIX_PARAMS = ('ffn1_w1', 'ffn1_w3', 'ffn1_w2', 'w_in', 'w_out', 'ffn2_w1', 'ffn2_w3', 'ffn2_w2')
_GAIN_PARAMS = ('ffn1_pre_g', 'ffn1_post_g', 'mix_pre_g', 'mix_post_g', 'ffn2_pre_g', 'ffn2_post_g')


def kernel(x_prompt, x_sample, ffn1_pre_g, ffn1_w1, ffn1_w3, ffn1_w2, ffn1_post_g, mix_pre_g, w_in, short_w, short_b, filt_w1, filt_b1, filt_w2, filt_b2, filt_w3, filt_b3, filt_w4, filt_freq, hyena_bias, ret_log_decay_f, ret_log_decay_b, w_out, mix_post_g, ffn2_pre_g, ffn2_w1, ffn2_w3, ffn2_w2, ffn2_post_g):
    params = dict(ffn1_pre_g=ffn1_pre_g, ffn1_w1=ffn1_w1, ffn1_w3=ffn1_w3, ffn1_w2=ffn1_w2,
                  ffn1_post_g=ffn1_post_g, mix_pre_g=mix_pre_g, w_in=w_in, short_w=short_w,
                  short_b=short_b, filt_w1=filt_w1, filt_b1=filt_b1, filt_w2=filt_w2,
                  filt_b2=filt_b2, filt_w3=filt_w3, filt_b3=filt_b3, filt_w4=filt_w4,
                  filt_freq=filt_freq, hyena_bias=hyena_bias, ret_log_decay_f=ret_log_decay_f,
                  ret_log_decay_b=ret_log_decay_b, w_out=w_out, mix_post_g=mix_post_g,
                  ffn2_pre_g=ffn2_pre_g, ffn2_w1=ffn2_w1, ffn2_w3=ffn2_w3, ffn2_w2=ffn2_w2,
                  ffn2_post_g=ffn2_post_g)
    depth = ffn1_w1.shape[0]

    def run(x):
        for l in range(depth):
            p = {k: v[l] for k, v in params.items()}
            for k in _MATRIX_PARAMS:
                p[k] = p[k].astype(BF16)
            for k in _GAIN_PARAMS:
                p[k] = p[k][None, :]
            p['short_b'] = p['short_b'][None, :]
            x = _layer(x, p)
        return x

    return (run(x_prompt), run(x_sample))
```

```python
import functools
import math

import numpy as np
import jax
import jax.numpy as jnp
from jax import lax
from jax.experimental import pallas as pl
from jax.experimental.pallas import tpu as pltpu

F32 = jnp.float32
BF16 = jnp.bfloat16

D_MODEL = 1024
D_HYENA = 512
D_RET = 512
HYENA_ORDER = 2
N_RET_HEADS = 4
RET_HEAD_DIM = 128
D_FF = 2816
FILT_EMB = 33
FILT_BANDS = 16
FILT_HIDDEN = 64
ROPE_BASE = 10000.0
NORM_EPS = 1e-6
HYENA_TARGET = 1e-2
FAST_DECAY_PCT = 0.3
SLOW_DECAY_PCT = 1.5
N_HY_COLS = (HYENA_ORDER + 1) * D_HYENA
D_IN = N_HY_COLS + 4 * D_RET

LANES = 128
VMEM_LIMIT = 56 * 1024 * 1024
VMEM_HEADROOM = 4 * 1024 * 1024
FFT_N2 = 32
RET_CHUNK = 128
RET_GROUP = 16
TOKEN_TILE = 512


def _const_spec(shape):
    nd = len(shape)
    return pl.BlockSpec(shape, lambda *_: (0,) * nd, pipeline_mode=pl.Buffered(1))


def _rms(x, g):
    ms = jnp.mean(x * x, axis=-1, keepdims=True)
    return x * lax.rsqrt(ms + NORM_EPS) * g


def _bdot(a, b):
    return jnp.dot(a, b, preferred_element_type=F32)


def _ffn_core(x, pre_ref, w1_ref, w3_ref, w2_ref, post_ref, o_ref):
    h = _rms(x, pre_ref[...]).astype(BF16)
    a = _bdot(h, w1_ref[...])
    b = _bdot(h, w3_ref[...])
    g = (a * jax.nn.sigmoid(a) * b).astype(BF16)
    y = _bdot(g, w2_ref[...])
    o_ref[...] = x + 0.5 * _rms(y, post_ref[...])


def _ffn_kernel(x_ref, pre_ref, w1_ref, w3_ref, w2_ref, post_ref, o_ref):
    _ffn_core(x_ref[...], pre_ref, w1_ref, w3_ref, w2_ref, post_ref, o_ref)


def _mix_ffn_kernel(x_ref, yh_ref, yr_ref, woh_ref, wor_ref, mg_ref,
                    pre_ref, w1_ref, w3_ref, w2_ref, post_ref, o_ref):
    y = _bdot(yh_ref[...].astype(BF16), woh_ref[...])
    y = y + _bdot(yr_ref[...].astype(BF16), wor_ref[...])
    x = x_ref[...] + _rms(y, mg_ref[...])
    _ffn_core(x, pre_ref, w1_ref, w3_ref, w2_ref, post_ref, o_ref)


def _row_spec(tm, width):
    return pl.BlockSpec((tm, width), lambda i: (i, 0))


def _ffn(x, pre_g, w1, w3, w2, post_g, mix=None):
    t = x.shape[0]
    tm = TOKEN_TILE
    ffn_specs = [_const_spec((1, D_MODEL)), _const_spec((D_MODEL, D_FF)),
                 _const_spec((D_MODEL, D_FF)), _const_spec((D_FF, D_MODEL)),
                 _const_spec((1, D_MODEL))]
    ffn_args = (pre_g, w1, w3, w2, post_g)
    if mix is None:
        body, args = _ffn_kernel, (x,) + ffn_args
        specs = [_row_spec(tm, D_MODEL)] + ffn_specs
    else:
        yh, yr, woh, wor, mg = mix
        body, args = _mix_ffn_kernel, (x, yh, yr, woh, wor, mg) + ffn_args
        specs = [_row_spec(tm, D_MODEL), _row_spec(tm, D_HYENA), _row_spec(tm, D_RET),
                 _const_spec((D_HYENA, D_MODEL)), _const_spec((D_RET, D_MODEL)),
                 _const_spec((1, D_MODEL))] + ffn_specs
    return pl.pallas_call(
        body,
        grid=(t // tm,),
        in_specs=specs,
        out_specs=_row_spec(tm, D_MODEL),
        out_shape=jax.ShapeDtypeStruct((t, D_MODEL), F32),
        compiler_params=pltpu.CompilerParams(
            dimension_semantics=("parallel",), vmem_limit_bytes=VMEM_LIMIT),
        name="ffn_mix" if mix is not None else "ffn",
    )(*args)


def _inproj_kernel(x_ref, g_ref, w_ref, cc_ref, ss_ref, o_ref):
    h = _rms(x_ref[...], g_ref[...]).astype(BF16)
    qk0 = N_HY_COLS
    vg0 = N_HY_COLS + 2 * D_RET
    o_ref[:, :qk0] = _bdot(h, w_ref[:, :qk0])
    o_ref[:, vg0:] = _bdot(h, w_ref[:, vg0:])
    qk = _bdot(h, w_ref[:, qk0:vg0])
    cc = cc_ref[...]
    ss = ss_ref[...]
    d = RET_HEAD_DIM
    for blk in range(2 * N_RET_HEADS):
        x = qk[:, blk * d:(blk + 1) * d]
        r = x * cc + pltpu.roll(x, d // 2, axis=1) * ss
        if blk >= N_RET_HEADS:
            r = r * (d ** -0.5)
        o_ref[:, qk0 + blk * d:qk0 + (blk + 1) * d] = r


def _inproj(x, g, w, seq):
    t = x.shape[0]
    tm = TOKEN_TILE
    cc, ss = _rope_tables(seq)
    pos_blocks = seq // tm
    rope_spec = pl.BlockSpec((tm, RET_HEAD_DIM), lambda i: (i % pos_blocks, 0))
    return pl.pallas_call(
        _inproj_kernel,
        grid=(t // tm,),
        in_specs=[_row_spec(tm, D_MODEL), _const_spec((1, D_MODEL)),
                  _const_spec((D_MODEL, D_IN)), rope_spec, rope_spec],
        out_specs=_row_spec(tm, D_IN),
        out_shape=jax.ShapeDtypeStruct((t, D_IN), F32),
        compiler_params=pltpu.CompilerParams(
            dimension_semantics=("parallel",), vmem_limit_bytes=VMEM_LIMIT),
        name="inproj",
    )(x, g, w, cc, ss)


def _ret_kernel(lgf_ref, lgb_ref, q_ref, k_ref, v_ref, g_ref, o_ref, qs_ref, kt_ref, *, seq):
    c = RET_CHUNK
    d = RET_HEAD_DIM
    n_chunks = seq // c
    head = pl.program_id(1)
    lgf = jnp.full((c, d), lgf_ref[head], F32)
    lgb = jnp.full((c, d), lgb_ref[head], F32)
    row = lax.broadcasted_iota(jnp.int32, (c, d), 0).astype(F32)
    col = lax.broadcasted_iota(jnp.int32, (c, d), 1).astype(F32)
    diff = row - col
    dmat = jnp.where(diff >= 0.0, jnp.exp(jnp.maximum(diff, 0.0) * lgf),
                     jnp.exp(jnp.maximum(-diff, 0.0) * lgb))
    wq_f = jnp.exp((row + 1.0) * lgf)
    wk_f = jnp.exp((c - 1.0 - row) * lgf)
    wq_b = jnp.exp((c - row) * lgb)
    wk_b = jnp.exp(row * lgb)
    gc_f = jnp.exp(c * lgf)
    gc_b = jnp.exp(c * lgb)

    grp = RET_GROUP
    n_groups = n_chunks // grp

    def fwd(gi, state):
        rows = [pl.ds(pl.multiple_of((gi * grp + j) * c, c), c) for j in range(grp)]
        qbs, kts, vs = [], [], []
        for r in rows:
            qb = q_ref[r, :].astype(BF16)
            ktb = k_ref[r, :].T.astype(BF16)
            qs_ref[r, :] = qb
            kt_ref[r, :] = ktb
            qbs.append(qb)
            kts.append(ktb)
            vs.append(v_ref[r, :])
        scores = [_bdot(qb, ktb) for qb, ktb in zip(qbs, kts)]
        kvs = [_bdot(ktb, (v * wk_f).astype(BF16)) for ktb, v in zip(kts, vs)]
        states = []
        for kv in kvs:
            states.append(state)
            state = state * gc_f + kv
        intras = [_bdot((s * dmat).astype(BF16), v.astype(BF16)) for s, v in zip(scores, vs)]
        crosses = [_bdot(qb, st.astype(BF16)) for qb, st in zip(qbs, states)]
        for r, intra, cross in zip(rows, intras, crosses):
            o_ref[r, :] = intra + wq_f * cross
        return state

    lax.fori_loop(0, n_groups, fwd, jnp.zeros((d, d), F32))

    def bwd(gi, state):
        rows = [pl.ds(pl.multiple_of((n_chunks - 1 - gi * grp - j) * c, c), c) for j in range(grp)]
        kvs = [_bdot(kt_ref[r, :], (v_ref[r, :] * wk_b).astype(BF16)) for r in rows]
        states = []
        for kv in kvs:
            states.append(state)
            state = state * gc_b + kv
        crosses = [_bdot(qs_ref[r, :], st.astype(BF16)) for r, st in zip(rows, states)]
        outs = [o_ref[r, :] + wq_b * cross for r, cross in zip(rows, crosses)]
        means = [jnp.mean(o * o, axis=-1, keepdims=True) for o in outs]
        norms = [lax.rsqrt(m + NORM_EPS) for m in means]
        gates = [g_ref[r, :] for r in rows]
        gates = [g * jax.nn.sigmoid(g) for g in gates]
        for r, o, nrm, g in zip(rows, outs, norms, gates):
            o_ref[r, :] = g * (o * nrm)
        return state

    lax.fori_loop(0, n_groups, bwd, jnp.zeros((d, d), F32))


def _rope_tables(seq):
    d = RET_HEAD_DIM
    inv = 1.0 / (ROPE_BASE ** (jnp.arange(0, d, 2, dtype=F32) / d))
    ang = jnp.arange(seq, dtype=F32)[:, None] * inv[None, :]
    c, s = jnp.cos(ang), jnp.sin(ang)
    return jnp.concatenate([c, c], axis=-1), jnp.concatenate([-s, s], axis=-1)


def _retention(u, lg_f, lg_b, batch, seq):
    assert RET_CHUNK == RET_HEAD_DIM
    first = N_HY_COLS // LANES
    heads = N_RET_HEADS

    def col(off):
        return pl.BlockSpec((seq, LANES), lambda b, h, *_: (b, first + off * heads + h))

    grid_spec = pltpu.PrefetchScalarGridSpec(
        num_scalar_prefetch=2,
        grid=(batch, heads),
        in_specs=[col(0), col(1), col(2), col(3)],
        out_specs=pl.BlockSpec((seq, LANES), lambda b, h, *_: (b, h)),
        scratch_shapes=[pltpu.VMEM((seq, RET_HEAD_DIM), BF16), pltpu.VMEM((seq, RET_CHUNK), BF16)],
    )
    return pl.pallas_call(
        functools.partial(_ret_kernel, seq=seq),
        grid_spec=grid_spec,
        out_shape=jax.ShapeDtypeStruct((batch * seq, D_RET), F32),
        compiler_params=pltpu.CompilerParams(
            dimension_semantics=("parallel", "parallel"), vmem_limit_bytes=VMEM_LIMIT),
        name="retention",
    )(lg_f, lg_b, u, u, u, u)


def _split(x):
    hi = x.astype(BF16)
    return hi, (x - hi.astype(F32)).astype(BF16)


def _dot3(a, b):
    a_hi, a_lo = a
    b_hi, b_lo = b
    return _bdot(a_hi, b_hi) + _bdot(a_hi, b_lo) + _bdot(a_lo, b_hi)


@functools.lru_cache(maxsize=None)
def _fft_tables(seq):
    n = 2 * seq
    n2 = FFT_N2
    n1 = n // n2
    h1 = n1 // 2
    f1 = np.arange(h1, dtype=np.float64) + 0.5
    th = 2.0 * np.pi * f1[:, None] * np.arange(n1, dtype=np.float64)[None, :] / n1
    m1_full = np.concatenate([np.cos(th), -np.sin(th)], axis=0)
    m1 = m1_full[:, :h1]
    m1_inv = (2.0 / n) * m1.T
    t2 = np.arange(n2, dtype=np.float64)
    phi = 2.0 * np.pi * (np.arange(n2, dtype=np.float64)[None, :, None] * t2[None, None, :] / n2
                         + f1[:, None, None] * t2[None, None, :] / n)
    gr, gi = np.cos(phi), -np.sin(phi)
    m2 = np.concatenate([np.concatenate([gr, -gi], axis=2),
                         np.concatenate([gi, gr], axis=2)], axis=1)
    m2_inv = np.transpose(m2, (0, 2, 1))
    return dict(n1=n1, h1=h1, m1_full=m1_full, m1=m1, m1_inv=m1_inv, m2=m2, m2_inv=m2_inv)


FILT_ROWS = 512
FILT_PACK = LANES // 2
PITCH_ALIGN = 4
A_PITCH = 2 * FFT_N2 + PITCH_ALIGN
T_PITCH = FFT_N2 + PITCH_ALIGN


def _filter_kernel(z_ref, w1_ref, b1_ref, w2_ref, b2_ref, w3_ref, b3_ref, fr_ref,
                   w4_ref, dl_ref, o_ref, sum_ref, *, nblk):
    i = pl.program_id(0)
    half = FILT_ROWS // 2
    fr = fr_ref[...]
    z = z_ref[...]
    h = jnp.sin(fr * (_dot3(_split(z), _split(w1_ref[...])) + b1_ref[...]))
    h = jnp.sin(fr * (_dot3(_split(h), _split(w2_ref[...])) + b2_ref[...]))
    h = jnp.sin(fr * (_dot3(_split(h), _split(w3_ref[...])) + b3_ref[...]))
    hs = _split(h)
    adl = jnp.abs(dl_ref[...])
    backward = i >= nblk
    for side in range(2):
        lane = side * FILT_PACK
        t = jnp.broadcast_to(z[:, lane:lane + 1], z.shape)
        out_rows = slice(side * half, (side + 1) * half)
        for c in range(HYENA_ORDER * D_HYENA // LANES):
            cols = slice(c * LANES, (c + 1) * LANES)
            hc = _dot3(hs, _split(w4_ref[0, side, :, cols])) * jnp.exp(-t * adl[:, cols])
            part = jnp.sum(jnp.abs(hc), axis=0, keepdims=True)

            if side == 0:
                @pl.when(i % nblk == 0)
                def _():
                    sum_ref[0, :, cols] = part

                @pl.when(i % nblk != 0)
                def _():
                    sum_ref[0, :, cols] = sum_ref[0, :, cols] + part

                rows = lax.broadcasted_iota(jnp.int32, hc.shape, 0)
                flipped = jnp.where(jnp.logical_and(rows == 0, i == nblk), 0.0, -hc)
            else:
                sum_ref[0, :, cols] = sum_ref[0, :, cols] + part
                flipped = -hc
            o_ref[out_rows, cols] = jnp.where(backward, flipped, hc)


def _kspec_kernel(k_ref, s_ref, m1_ref, m2_ref, o_ref, a_ref, kp_ref, *, n1):
    n2 = FFT_N2
    h1 = n1 // 2
    inv_f = 1.0 / s_ref[0]
    inv_b = 1.0 / s_ref[1]

    def pad(t1, carry):
        src = pl.multiple_of(t1 * n2, n2)
        dst = pl.multiple_of(t1 * T_PITCH, PITCH_ALIGN)
        kp_ref[pl.ds(dst, n2), :] = k_ref[pl.ds(src, n2), :] * jnp.where(t1 < h1, inv_f, inv_b)
        return carry

    lax.fori_loop(0, n1, pad, 0, unroll=8)

    def stage1(t2, carry):
        kt = kp_ref[pl.ds(t2, n1, stride=T_PITCH), :]
        a = _bdot(m1_ref[...], kt.astype(BF16))
        a_ref[pl.ds(t2, h1, stride=A_PITCH), :] = a[:h1]
        a_ref[pl.ds(t2 + n2, h1, stride=A_PITCH), :] = a[h1:]
        return carry

    lax.fori_loop(0, n2, stage1, 0, unroll=4)

    def stage2(f1, carry):
        src = pl.multiple_of(f1 * A_PITCH, PITCH_ALIGN)
        dst = pl.multiple_of(f1 * 2 * n2, 2 * n2)
        spec = _bdot(m2_ref[f1], a_ref[pl.ds(src, 2 * n2), :].astype(BF16))
        o_ref[pl.ds(dst, 2 * n2), :] = spec.astype(o_ref.dtype)
        return carry

    lax.fori_loop(0, h1, stage2, 0, unroll=16)


def _filter_spectra(seq, p):
    n = 2 * seq
    tb = _fft_tables(seq)
    fwd_pos = jnp.arange(seq, dtype=jnp.int32)
    pos = jnp.concatenate([fwd_pos, (seq - fwd_pos) % seq])
    t_fwd = jnp.linspace(0.0, 1.0, seq, dtype=F32)
    t = jnp.concatenate([t_fwd, jnp.roll(t_fwd[::-1], 1)])[:, None]
    w = 2.0 * math.pi * pos.astype(F32)[:, None] / seq
    fb = jnp.linspace(1e-4, FILT_BANDS - 1, FILT_BANDS, dtype=F32)[None, :]
    z = jnp.concatenate([t, jnp.cos(fb * w), -jnp.sin(fb * w)], axis=-1)
    nblk = seq // FILT_ROWS
    half = FILT_ROWS // 2
    pack = FILT_PACK
    zz = jnp.pad(z, ((0, 0), (0, pack - FILT_EMB))).reshape(2 * nblk, 2, half, pack)
    zz = jnp.transpose(zz, (0, 2, 1, 3)).reshape(n // 2, LANES)

    def padw(a):
        a = jnp.pad(a, ((0, pack - a.shape[0]), (0, pack - a.shape[1])))
        zero = jnp.zeros_like(a)
        return jnp.concatenate([jnp.concatenate([a, zero], axis=1),
                                jnp.concatenate([zero, a], axis=1)], axis=0)

    def padv(a):
        return jnp.tile(jnp.pad(a, (0, pack - a.shape[0])), 2)[None, :]

    w1 = padw(p['filt_w1'])
    w2 = padw(p['filt_w2'])
    w3 = padw(p['filt_w3'])
    w4 = jnp.pad(p['filt_w4'], ((0, pack - FILT_HIDDEN), (0, 0))).reshape(pack, HYENA_ORDER, 2, D_HYENA)
    w4 = jnp.transpose(w4, (2, 0, 1, 3)).reshape(2, pack, HYENA_ORDER * D_HYENA)
    w4 = jnp.stack([jnp.pad(w4, ((0, 0), (0, pack), (0, 0))),
                    jnp.pad(w4, ((0, 0), (pack, 0), (0, 0)))], axis=1)
    min_decay = math.log(HYENA_TARGET) / SLOW_DECAY_PCT
    max_decay = math.log(HYENA_TARGET) / FAST_DECAY_PCT
    deltas = jnp.linspace(min_decay, max_decay, D_HYENA, dtype=F32)
    deltas = jnp.tile(deltas, HYENA_ORDER)[None, :]
    width = HYENA_ORDER * D_HYENA
    sq = _const_spec((LANES, LANES))
    vec = _const_spec((1, LANES))
    filt, sums = pl.pallas_call(
        functools.partial(_filter_kernel, nblk=nblk),
        grid=(2 * nblk,),
        in_specs=[pl.BlockSpec((half, LANES), lambda i: (i, 0)),
                  sq, vec, sq, vec, sq, vec, vec,
                  pl.BlockSpec((1, 2, LANES, width), lambda i: (i // nblk, 0, 0, 0)),
                  _const_spec((1, width))],
        out_specs=[pl.BlockSpec((FILT_ROWS, width), lambda i: (i, 0)),
                   pl.BlockSpec((1, 1, width), lambda i: (i // nblk, 0, 0))],
        out_shape=[jax.ShapeDtypeStruct((n, width), F32),
                   jax.ShapeDtypeStruct((2, 1, width), F32)],
        compiler_params=pltpu.CompilerParams(
            dimension_semantics=("arbitrary",), vmem_limit_bytes=VMEM_LIMIT),
        name="hyena_filter",
    )(zz, w1, padv(p['filt_b1']), w2, padv(p['filt_b2']), w3, padv(p['filt_b3']),
      padv(p['filt_freq']), w4, deltas)
    n1, h1 = tb['n1'], tb['h1']
    return pl.pallas_call(
        functools.partial(_kspec_kernel, n1=n1),
        grid=(width // LANES,),
        in_specs=[pl.BlockSpec((n, LANES), lambda j: (0, j)),
                  pl.BlockSpec((2, 1, LANES), lambda j: (0, 0, j)),
                  _const_spec((n1, n1)), _const_spec((h1, 2 * FFT_N2, 2 * FFT_N2))],
        out_specs=pl.BlockSpec((n, LANES), lambda j: (0, j)),
        out_shape=jax.ShapeDtypeStruct((n, width), BF16),
        scratch_shapes=[pltpu.VMEM((h1 * A_PITCH, LANES), F32),
                        pltpu.VMEM((n1 * T_PITCH, LANES), F32)],
        compiler_params=pltpu.CompilerParams(
            dimension_semantics=("parallel",), vmem_limit_bytes=VMEM_LIMIT),
        name="hyena_filter_spectrum",
    )(filt, sums, _bf16_const(tb['m1_full']), _bf16_const(tb['m2']))


CHUNK_UNROLL = 8


def _for_chunks(count, body):
    body(0, True, False)

    def step(t1, carry):
        body(t1, False, False)
        return carry

    lax.fori_loop(1, count - CHUNK_UNROLL + 1, step, 0, unroll=CHUNK_UNROLL)
    for t1 in range(count - CHUNK_UNROLL + 1, count):
        body(t1, False, t1 == count - 1)


def _short_conv_rows(ref, w_ref, b_ref, r0, nrows, first, last):
    x = ref[pl.ds(r0, nrows), :]
    rid = lax.broadcasted_iota(jnp.int32, x.shape, 0)
    if first:
        prev = jnp.where(rid == 0, 0.0, pltpu.roll(x, 1, axis=0))
    else:
        prev = ref[pl.ds(r0 - 1, nrows), :]
    if last:
        nxt = jnp.where(rid == nrows - 1, 0.0, pltpu.roll(x, nrows - 1, axis=0))
    else:
        nxt = ref[pl.ds(r0 + 1, nrows), :]
    return prev * w_ref[0:1, :] + x * w_ref[1:2, :] + nxt * w_ref[2:3, :] + b_ref[...]


def _long_conv_kernel(*refs, n1, conv_z, conv_g):
    refs = list(refs)
    a_ref, tp_ref = refs[-2:]
    slabs = a_ref.shape[0]

    def take_operand(conv):
        return [tuple(refs.pop(0) for _ in range(3 if conv else 1)) for _ in range(slabs)]

    z_ops = take_operand(conv_z)
    g_ops = take_operand(conv_g)
    k_ref, bias_ref, m1_ref, m1i_ref, m2_ref, m2i_ref, o_ref = refs[:-2]
    n2 = FFT_N2
    h1 = n1 // 2

    def put(ref, idx, x):
        for s in range(slabs):
            ref[s, idx, :] = x[:, s * LANES:(s + 1) * LANES]

    def get(ref, idx):
        return jnp.concatenate([ref[s, idx, :] for s in range(slabs)], axis=1)

    def rows_of(ops, conv, t1, first, last):
        r0 = t1 * n2 if isinstance(t1, int) else pl.multiple_of(t1 * n2, n2)
        if conv:
            parts = [_short_conv_rows(ref, w_ref, b_ref, r0, n2, first, last) for ref, w_ref, b_ref in ops]
        else:
            parts = [ref[pl.ds(r0, n2), :] for ref, in ops]
        return jnp.concatenate(parts, axis=1)

    def z_rows(t1, first, last):
        return rows_of(z_ops, conv_z, t1, first, last)

    def g_rows(t1, first, last):
        return rows_of(g_ops, conv_g, t1, first, last)

    def pitched(t1):
        r0 = t1 * T_PITCH
        return pl.ds(r0 if isinstance(t1, int) else pl.multiple_of(r0, PITCH_ALIGN), n2)

    def pad(t1, first, last):
        put(tp_ref, pitched(t1), z_rows(t1, first, last))

    _for_chunks(h1, pad)

    def stage1(t2, carry):
        zt = get(tp_ref, pl.ds(t2, h1, stride=T_PITCH))
        a = _bdot(m1_ref[...], zt.astype(BF16))
        put(a_ref, pl.ds(t2, h1, stride=A_PITCH), a[:h1])
        put(a_ref, pl.ds(t2 + n2, h1, stride=A_PITCH), a[h1:])
        return carry

    lax.fori_loop(0, n2, stage1, 0, unroll=4)

    def stage2(f1, carry):
        ra = pl.ds(pl.multiple_of(f1 * A_PITCH, PITCH_ALIGN), 2 * n2)
        rk = pl.multiple_of(f1 * 2 * n2, 2 * n2)
        x = _bdot(m2_ref[f1], get(a_ref, ra).astype(BF16))
        kk = k_ref[pl.ds(rk, 2 * n2), :].astype(F32)
        xr, xi, kr, ki = x[:n2], x[n2:], kk[:n2], kk[n2:]
        y = jnp.concatenate([xr * kr - xi * ki, xr * ki + xi * kr], axis=0)
        put(a_ref, ra, _bdot(m2i_ref[f1], y.astype(BF16)))
        return carry

    lax.fori_loop(0, h1, stage2, 0, unroll=16)

    def stage3(t2, carry):
        br = get(a_ref, pl.ds(t2, h1, stride=A_PITCH))
        bi = get(a_ref, pl.ds(t2 + n2, h1, stride=A_PITCH))
        y = _bdot(m1i_ref[...], jnp.concatenate([br, bi], axis=0).astype(BF16))
        put(tp_ref, pl.ds(t2, h1, stride=T_PITCH), y)
        return carry

    lax.fori_loop(0, n2, stage3, 0, unroll=4)

    bias = bias_ref[...]

    def finish(t1, first, last):
        r0 = t1 * n2 if isinstance(t1, int) else pl.multiple_of(t1 * n2, n2)
        y = get(tp_ref, pitched(t1))
        o_ref[pl.ds(r0, n2), :] = g_rows(t1, first, last) * (y + bias * z_rows(t1, first, last))

    _for_chunks(h1, finish)


def _bf16_const(a):
    return jnp.asarray(a, F32).astype(BF16)


def _long_conv_slabs(seq):
    n1 = 2 * seq // FFT_N2
    h1 = n1 // 2
    tables = 2 * (2 * n1 * h1 + 2 * h1 * (2 * FFT_N2) ** 2)
    for slabs in (2, 1):
        io = 3 * 2 * seq * 4
        per_lane = io + 2 * 2 * seq * 2 + h1 * (A_PITCH + T_PITCH) * 4
        if per_lane * slabs * LANES + tables <= VMEM_LIMIT - VMEM_HEADROOM:
            return slabs
    raise ValueError(f"long conv of length {seq} does not fit VMEM")


def _long_conv(z, z_off, gate, g_off, kspec, k_off, bias, batch, seq, short=None, conv_z=False):
    n = 2 * seq
    tb = _fft_tables(seq)
    n1, h1 = tb['n1'], tb['h1']
    conv_g = short is not None
    slabs = _long_conv_slabs(seq)
    width = slabs * LANES
    assert z_off % slabs == 0 and g_off % slabs == 0 and k_off % slabs == 0

    def operand(arr, off, conv):
        specs, args = [], []
        for s in range(slabs):
            specs.append(pl.BlockSpec((seq, LANES), lambda j, b, s=s: (b, off + j * slabs + s)))
            args.append(arr)
            if conv:
                specs += [pl.BlockSpec((3, LANES), lambda j, b, s=s: (0, off + j * slabs + s)),
                          pl.BlockSpec((1, LANES), lambda j, b, s=s: (0, off + j * slabs + s))]
                args += list(short)
        return specs, args

    z_specs, z_args = operand(z, z_off, conv_z)
    g_specs, g_args = operand(gate, g_off, conv_g)
    k_blk = k_off // slabs
    return pl.pallas_call(
        functools.partial(_long_conv_kernel, n1=n1, conv_z=conv_z, conv_g=conv_g),
        grid=(D_HYENA // width, batch),
        in_specs=z_specs + g_specs + [
                  pl.BlockSpec((n, width), lambda j, b: (0, k_blk + j)),
                  pl.BlockSpec((1, width), lambda j, b: (0, j)),
                  _const_spec((n1, h1)), _const_spec((h1, n1)),
                  _const_spec((h1, 2 * FFT_N2, 2 * FFT_N2)),
                  _const_spec((h1, 2 * FFT_N2, 2 * FFT_N2))],
        out_specs=pl.BlockSpec((seq, width), lambda j, b: (b, j)),
        out_shape=jax.ShapeDtypeStruct((batch * seq, D_HYENA), F32),
        scratch_shapes=[pltpu.VMEM((slabs, h1 * A_PITCH, LANES), F32),
                        pltpu.VMEM((slabs, h1 * T_PITCH, LANES), F32)],
        compiler_params=pltpu.CompilerParams(
            dimension_semantics=("parallel", "parallel"), vmem_limit_bytes=VMEM_LIMIT),
        name="hyena_long_conv",
    )(*z_args, *g_args, kspec, bias, _bf16_const(tb['m1']), _bf16_const(tb['m1_inv']),
      _bf16_const(tb['m2']), _bf16_const(tb['m2_inv']))


def _layer(x3, p):
    batch, seq, _ = x3.shape
    x = x3.reshape(batch * seq, D_MODEL)
    x = _ffn(x, p['ffn1_pre_g'], p['ffn1_w1'], p['ffn1_w3'], p['ffn1_w2'], p['ffn1_post_g'])
    u = _inproj(x, p['mix_pre_g'], p['w_in'], seq)
    kspec = _filter_spectra(seq, p)
    tiles = D_HYENA // LANES
    short = (p['short_w'], p['short_b'])
    z1 = _long_conv(u, 0, u, tiles, kspec, 0, p['hyena_bias'][0:1], batch, seq, short=short, conv_z=True)
    yh = _long_conv(z1, 0, u, 2 * tiles, kspec, tiles, p['hyena_bias'][1:2], batch, seq, short=short)
    yr = _retention(u, p['ret_log_decay_f'], p['ret_log_decay_b'], batch, seq)
    mix = (yh, yr, p['w_out'][:D_HYENA], p['w_out'][D_HYENA:], p['mix_post_g'])
    x = _ffn(x, p['ffn2_pre_g'], p['ffn2_w1'], p['ffn2_w3'], p['ffn2_w2'], p['ffn2_post_g'], mix=mix)
    return x.reshape(batch, seq, D_MODEL)


_MATRIX_PARAMS = ('ffn1_w1', 'ffn1_w3', 'ffn1_w2', 'w_in', 'w_out', 'ffn2_w1', 'ffn2_w3', 'ffn2_w2')
_GAIN_PARAMS = ('ffn1_pre_g', 'ffn1_post_g', 'mix_pre_g', 'mix_post_g', 'ffn2_pre_g', 'ffn2_post_g')


def kernel(x_prompt, x_sample, ffn1_pre_g, ffn1_w1, ffn1_w3, ffn1_w2, ffn1_post_g, mix_pre_g, w_in, short_w, short_b, filt_w1, filt_b1, filt_w2, filt_b2, filt_w3, filt_b3, filt_w4, filt_freq, hyena_bias, ret_log_decay_f, ret_log_decay_b, w_out, mix_post_g, ffn2_pre_g, ffn2_w1, ffn2_w3, ffn2_w2, ffn2_post_g):
    params = dict(ffn1_pre_g=ffn1_pre_g, ffn1_w1=ffn1_w1, ffn1_w3=ffn1_w3, ffn1_w2=ffn1_w2,
                  ffn1_post_g=ffn1_post_g, mix_pre_g=mix_pre_g, w_in=w_in, short_w=short_w,
                  short_b=short_b, filt_w1=filt_w1, filt_b1=filt_b1, filt_w2=filt_w2,
                  filt_b2=filt_b2, filt_w3=filt_w3, filt_b3=filt_b3, filt_w4=filt_w4,
                  filt_freq=filt_freq, hyena_bias=hyena_bias, ret_log_decay_f=ret_log_decay_f,
                  ret_log_decay_b=ret_log_decay_b, w_out=w_out, mix_post_g=mix_post_g,
                  ffn2_pre_g=ffn2_pre_g, ffn2_w1=ffn2_w1, ffn2_w3=ffn2_w3, ffn2_w2=ffn2_w2,
                  ffn2_post_g=ffn2_post_g)
    depth = ffn1_w1.shape[0]

    def run(x):
        for l in range(depth):
            p = {k: v[l] for k, v in params.items()}
            for k in _MATRIX_PARAMS:
                p[k] = p[k].astype(BF16)
            for k in _GAIN_PARAMS:
                p[k] = p[k][None, :]
            p['short_b'] = p['short_b'][None, :]
            x = _layer(x, p)
        return x

    return (run(x_prompt), run(x_sample))
```

```python
import functools
import math

import numpy as np
import jax
import jax.numpy as jnp
from jax import lax
from jax.experimental import pallas as pl
from jax.experimental.pallas import tpu as pltpu

F32 = jnp.float32
BF16 = jnp.bfloat16

D_MODEL = 1024
D_HYENA = 512
D_RET = 512
HYENA_ORDER = 2
N_RET_HEADS = 4
RET_HEAD_DIM = 128
D_FF = 2816
FILT_EMB = 33
FILT_BANDS = 16
FILT_HIDDEN = 64
ROPE_BASE = 10000.0
NORM_EPS = 1e-6
HYENA_TARGET = 1e-2
FAST_DECAY_PCT = 0.3
SLOW_DECAY_PCT = 1.5
N_HY_COLS = (HYENA_ORDER + 1) * D_HYENA
D_IN = N_HY_COLS + 4 * D_RET

LANES = 128
VMEM_LIMIT = 56 * 1024 * 1024
VMEM_HEADROOM = 4 * 1024 * 1024
FFT_N2 = 32
RET_CHUNK = 128
RET_GROUP = 16
TOKEN_TILE = 512


def _const_spec(shape):
    nd = len(shape)
    return pl.BlockSpec(shape, lambda *_: (0,) * nd, pipeline_mode=pl.Buffered(1))


def _rms(x, g):
    ms = jnp.mean(x * x, axis=-1, keepdims=True)
    return x * lax.rsqrt(ms + NORM_EPS) * g


def _bdot(a, b):
    return jnp.dot(a, b, preferred_element_type=F32)


def _ffn_core(x, pre_ref, w1_ref, w3_ref, w2_ref, post_ref, o_ref):
    h = _rms(x, pre_ref[...]).astype(BF16)
    a = _bdot(h, w1_ref[...])
    b = _bdot(h, w3_ref[...])
    g = (a * jax.nn.sigmoid(a) * b).astype(BF16)
    y = _bdot(g, w2_ref[...])
    o_ref[...] = x + 0.5 * _rms(y, post_ref[...])


def _ffn_kernel(x_ref, pre_ref, w1_ref, w3_ref, w2_ref, post_ref, o_ref):
    _ffn_core(x_ref[...], pre_ref, w1_ref, w3_ref, w2_ref, post_ref, o_ref)


def _mix_ffn_kernel(x_ref, yh_ref, yr_ref, woh_ref, wor_ref, mg_ref,
                    pre_ref, w1_ref, w3_ref, w2_ref, post_ref, o_ref):
    y = _bdot(yh_ref[...].astype(BF16), woh_ref[...])
    y = y + _bdot(yr_ref[...].astype(BF16), wor_ref[...])
    x = x_ref[...] + _rms(y, mg_ref[...])
    _ffn_core(x, pre_ref, w1_ref, w3_ref, w2_ref, post_ref, o_ref)


def _row_spec(tm, width):
    return pl.BlockSpec((tm, width), lambda i: (i, 0))


def _ffn(x, pre_g, w1, w3, w2, post_g, mix=None):
    t = x.shape[0]
    tm = TOKEN_TILE
    ffn_specs = [_const_spec((1, D_MODEL)), _const_spec((D_MODEL, D_FF)),
                 _const_spec((D_MODEL, D_FF)), _const_spec((D_FF, D_MODEL)),
                 _const_spec((1, D_MODEL))]
    ffn_args = (pre_g, w1, w3, w2, post_g)
    if mix is None:
        body, args = _ffn_kernel, (x,) + ffn_args
        specs = [_row_spec(tm, D_MODEL)] + ffn_specs
    else:
        yh, yr, w_out, mg = mix
        body, args = _mix_ffn_kernel, (x, yh, yr, w_out, w_out, mg) + ffn_args
        specs = [_row_spec(tm, D_MODEL), _row_spec(tm, D_HYENA), _row_spec(tm, D_RET),
                 pl.BlockSpec((D_HYENA, D_MODEL), lambda i: (0, 0), pipeline_mode=pl.Buffered(1)),
                 pl.BlockSpec((D_RET, D_MODEL), lambda i: (D_HYENA // D_RET, 0),
                              pipeline_mode=pl.Buffered(1)),
                 _const_spec((1, D_MODEL))] + ffn_specs
    return pl.pallas_call(
        body,
        grid=(t // tm,),
        in_specs=specs,
        out_specs=_row_spec(tm, D_MODEL),
        out_shape=jax.ShapeDtypeStruct((t, D_MODEL), F32),
        compiler_params=pltpu.CompilerParams(
            dimension_semantics=("parallel",), vmem_limit_bytes=VMEM_LIMIT),
        name="ffn_mix" if mix is not None else "ffn",
    )(*args)


def _inproj_kernel(x_ref, g_ref, w_ref, cc_ref, ss_ref, hy_ref, qkv_ref, gate_ref):
    h = _rms(x_ref[...], g_ref[...]).astype(BF16)
    qk0 = N_HY_COLS
    v0 = qk0 + 2 * D_RET
    g0 = v0 + D_RET
    hy_ref[...] = _bdot(h, w_ref[:, :qk0])
    gate_ref[...] = _bdot(h, w_ref[:, g0:])
    qkv_ref[:, 2 * D_RET:] = _bdot(h, w_ref[:, v0:g0]).astype(BF16)
    qk = _bdot(h, w_ref[:, qk0:v0])
    cc = cc_ref[...]
    ss = ss_ref[...]
    d = RET_HEAD_DIM
    for blk in range(2 * N_RET_HEADS):
        x = qk[:, blk * d:(blk + 1) * d]
        r = x * cc + pltpu.roll(x, d // 2, axis=1) * ss
        if blk >= N_RET_HEADS:
            r = r * (d ** -0.5)
        qkv_ref[:, blk * d:(blk + 1) * d] = r.astype(BF16)


def _inproj(x, g, w, seq):
    t = x.shape[0]
    tm = TOKEN_TILE
    cc, ss = _rope_tables(seq)
    pos_blocks = seq // tm
    rope_spec = pl.BlockSpec((tm, RET_HEAD_DIM), lambda i: (i % pos_blocks, 0))
    return pl.pallas_call(
        _inproj_kernel,
        grid=(t // tm,),
        in_specs=[_row_spec(tm, D_MODEL), _const_spec((1, D_MODEL)),
                  _const_spec((D_MODEL, D_IN)), rope_spec, rope_spec],
        out_specs=[_row_spec(tm, N_HY_COLS), _row_spec(tm, 3 * D_RET), _row_spec(tm, D_RET)],
        out_shape=[jax.ShapeDtypeStruct((t, N_HY_COLS), F32),
                   jax.ShapeDtypeStruct((t, 3 * D_RET), BF16),
                   jax.ShapeDtypeStruct((t, D_RET), F32)],
        compiler_params=pltpu.CompilerParams(
            dimension_semantics=("parallel",), vmem_limit_bytes=VMEM_LIMIT),
        name="inproj",
    )(x, g, w, cc, ss)


def _ret_kernel(lgf_ref, lgb_ref, q_ref, k_ref, v_ref, g_ref, o_ref, kt_ref, *, seq):
    c = RET_CHUNK
    d = RET_HEAD_DIM
    n_chunks = seq // c
    head = pl.program_id(1)
    lgf = jnp.full((c, d), lgf_ref[head], F32)
    lgb = jnp.full((c, d), lgb_ref[head], F32)
    row = lax.broadcasted_iota(jnp.int32, (c, d), 0).astype(F32)
    col = lax.broadcasted_iota(jnp.int32, (c, d), 1).astype(F32)
    diff = row - col
    dmat = jnp.where(diff >= 0.0, jnp.exp(jnp.maximum(diff, 0.0) * lgf),
                     jnp.exp(jnp.maximum(-diff, 0.0) * lgb))
    wq_f = jnp.exp((row + 1.0) * lgf)
    wk_f = jnp.exp((c - 1.0 - row) * lgf)
    wq_b = jnp.exp((c - row) * lgb)
    wk_b = jnp.exp(row * lgb)
    gc_f = jnp.exp(c * lgf)
    gc_b = jnp.exp(c * lgb)

    grp = RET_GROUP
    n_groups = n_chunks // grp

    def fwd(gi, state):
        rows = [pl.ds(pl.multiple_of((gi * grp + j) * c, c), c) for j in range(grp)]
        qbs, kts, vs = [], [], []
        for r in rows:
            ktb = k_ref[r, :].astype(F32).T.astype(BF16)
            kt_ref[r, :] = ktb
            qbs.append(q_ref[r, :])
            kts.append(ktb)
            vs.append(v_ref[r, :])
        scores = [_bdot(qb, ktb) for qb, ktb in zip(qbs, kts)]
        kvs = [_bdot(ktb, (v.astype(F32) * wk_f).astype(BF16)) for ktb, v in zip(kts, vs)]
        states = []
        for kv in kvs:
            states.append(state)
            state = state * gc_f + kv
        intras = [_bdot((s * dmat).astype(BF16), v) for s, v in zip(scores, vs)]
        crosses = [_bdot(qb, st.astype(BF16)) for qb, st in zip(qbs, states)]
        for r, intra, cross in zip(rows, intras, crosses):
            o_ref[r, :] = intra + wq_f * cross
        return state

    lax.fori_loop(0, n_groups, fwd, jnp.zeros((d, d), F32))

    def bwd(gi, state):
        rows = [pl.ds(pl.multiple_of((n_chunks - 1 - gi * grp - j) * c, c), c) for j in range(grp)]
        kvs = [_bdot(kt_ref[r, :], (v_ref[r, :].astype(F32) * wk_b).astype(BF16)) for r in rows]
        states = []
        for kv in kvs:
            states.append(state)
            state = state * gc_b + kv
        crosses = [_bdot(q_ref[r, :], st.astype(BF16)) for r, st in zip(rows, states)]
        outs = [o_ref[r, :] + wq_b * cross for r, cross in zip(rows, crosses)]
        means = [jnp.mean(o * o, axis=-1, keepdims=True) for o in outs]
        norms = [lax.rsqrt(m + NORM_EPS) for m in means]
        gates = [g_ref[r, :] for r in rows]
        gates = [g * jax.nn.sigmoid(g) for g in gates]
        for r, o, nrm, g in zip(rows, outs, norms, gates):
            o_ref[r, :] = g * (o * nrm)
        return state

    lax.fori_loop(0, n_groups, bwd, jnp.zeros((d, d), F32))


def _rope_tables(seq):
    d = RET_HEAD_DIM
    inv = 1.0 / (ROPE_BASE ** (jnp.arange(0, d, 2, dtype=F32) / d))
    ang = jnp.arange(seq, dtype=F32)[:, None] * inv[None, :]
    c, s = jnp.cos(ang), jnp.sin(ang)
    return jnp.concatenate([c, c], axis=-1), jnp.concatenate([-s, s], axis=-1)


def _retention(qkv, gate, lg_f, lg_b, batch, seq):
    assert RET_CHUNK == RET_HEAD_DIM
    heads = N_RET_HEADS

    def col(off):
        return pl.BlockSpec((seq, LANES), lambda b, h, *_: (b, off * heads + h))

    grid_spec = pltpu.PrefetchScalarGridSpec(
        num_scalar_prefetch=2,
        grid=(batch, heads),
        in_specs=[col(0), col(1), col(2), col(0)],
        out_specs=pl.BlockSpec((seq, LANES), lambda b, h, *_: (b, h)),
        scratch_shapes=[pltpu.VMEM((seq, RET_CHUNK), BF16)],
    )
    return pl.pallas_call(
        functools.partial(_ret_kernel, seq=seq),
        grid_spec=grid_spec,
        out_shape=jax.ShapeDtypeStruct((batch * seq, D_RET), F32),
        compiler_params=pltpu.CompilerParams(
            dimension_semantics=("parallel", "parallel"), vmem_limit_bytes=VMEM_LIMIT),
        name="retention",
    )(lg_f, lg_b, qkv, qkv, qkv, gate)


def _split(x):
    hi = x.astype(BF16)
    return hi, (x - hi.astype(F32)).astype(BF16)


def _dot3(a, b):
    a_hi, a_lo = a
    b_hi, b_lo = b
    return _bdot(a_hi, b_hi) + _bdot(a_hi, b_lo) + _bdot(a_lo, b_hi)


@functools.lru_cache(maxsize=None)
def _fft_tables(seq):
    n = 2 * seq
    n2 = FFT_N2
    n1 = n // n2
    h1 = n1 // 2
    f1 = np.arange(h1, dtype=np.float64) + 0.5
    th = 2.0 * np.pi * f1[:, None] * np.arange(n1, dtype=np.float64)[None, :] / n1
    m1_full = np.concatenate([np.cos(th), -np.sin(th)], axis=0)
    m1 = m1_full[:, :h1]
    m1_inv = (2.0 / n) * m1.T
    t2 = np.arange(n2, dtype=np.float64)
    phi = 2.0 * np.pi * (np.arange(n2, dtype=np.float64)[None, :, None] * t2[None, None, :] / n2
                         + f1[:, None, None] * t2[None, None, :] / n)
    gr, gi = np.cos(phi), -np.sin(phi)
    m2 = np.concatenate([np.concatenate([gr, -gi], axis=2),
                         np.concatenate([gi, gr], axis=2)], axis=1)
    m2_inv = np.transpose(m2, (0, 2, 1))
    return dict(n1=n1, h1=h1, m1_full=m1_full, m1=m1, m1_inv=m1_inv, m2=m2, m2_inv=m2_inv)


FILT_ROWS = 512
FILT_PACK = LANES // 2
PITCH_ALIGN = 4
A_PITCH = 2 * FFT_N2 + PITCH_ALIGN
T_PITCH = FFT_N2 + PITCH_ALIGN


def _filter_kernel(z_ref, w1_ref, b1_ref, w2_ref, b2_ref, w3_ref, b3_ref, fr_ref,
                   w4_ref, dl_ref, o_ref, sum_ref, *, nblk):
    i = pl.program_id(0)
    half = FILT_ROWS // 2
    fr = fr_ref[...]
    z = z_ref[...]
    h = jnp.sin(fr * (_dot3(_split(z), _split(w1_ref[...])) + b1_ref[...]))
    h = jnp.sin(fr * (_dot3(_split(h), _split(w2_ref[...])) + b2_ref[...]))
    h = jnp.sin(fr * (_dot3(_split(h), _split(w3_ref[...])) + b3_ref[...]))
    hs = _split(h)
    adl = jnp.abs(dl_ref[...])
    backward = i >= nblk
    for side in range(2):
        lane = side * FILT_PACK
        t = jnp.broadcast_to(z[:, lane:lane + 1], z.shape)
        out_rows = slice(side * half, (side + 1) * half)
        for c in range(HYENA_ORDER * D_HYENA // LANES):
            cols = slice(c * LANES, (c + 1) * LANES)
            hc = _dot3(hs, _split(w4_ref[0, side, :, cols])) * jnp.exp(-t * adl[:, cols])
            part = jnp.sum(jnp.abs(hc), axis=0, keepdims=True)

            if side == 0:
                @pl.when(i % nblk == 0)
                def _():
                    sum_ref[0, :, cols] = part

                @pl.when(i % nblk != 0)
                def _():
                    sum_ref[0, :, cols] = sum_ref[0, :, cols] + part

                rows = lax.broadcasted_iota(jnp.int32, hc.shape, 0)
                flipped = jnp.where(jnp.logical_and(rows == 0, i == nblk), 0.0, -hc)
            else:
                sum_ref[0, :, cols] = sum_ref[0, :, cols] + part
                flipped = -hc
            o_ref[out_rows, cols] = jnp.where(backward, flipped, hc)


def _kspec_kernel(k_ref, s_ref, m1_ref, m2_ref, o_ref, a_ref, kp_ref, *, n1):
    n2 = FFT_N2
    h1 = n1 // 2
    inv_f = 1.0 / s_ref[0]
    inv_b = 1.0 / s_ref[1]

    def pad(t1, carry):
        src = pl.multiple_of(t1 * n2, n2)
        dst = pl.multiple_of(t1 * T_PITCH, PITCH_ALIGN)
        kp_ref[pl.ds(dst, n2), :] = k_ref[pl.ds(src, n2), :] * jnp.where(t1 < h1, inv_f, inv_b)
        return carry

    lax.fori_loop(0, n1, pad, 0, unroll=8)

    def stage1(t2, carry):
        kt = kp_ref[pl.ds(t2, n1, stride=T_PITCH), :]
        a = _bdot(m1_ref[...], kt.astype(BF16))
        a_ref[pl.ds(t2, h1, stride=A_PITCH), :] = a[:h1]
        a_ref[pl.ds(t2 + n2, h1, stride=A_PITCH), :] = a[h1:]
        return carry

    lax.fori_loop(0, n2, stage1, 0, unroll=4)

    def stage2(f1, carry):
        src = pl.multiple_of(f1 * A_PITCH, PITCH_ALIGN)
        dst = pl.multiple_of(f1 * 2 * n2, 2 * n2)
        spec = _bdot(m2_ref[f1], a_ref[pl.ds(src, 2 * n2), :].astype(BF16))
        o_ref[pl.ds(dst, 2 * n2), :] = spec.astype(o_ref.dtype)
        return carry

    lax.fori_loop(0, h1, stage2, 0, unroll=16)


def _filter_spectra(seq, p):
    n = 2 * seq
    tb = _fft_tables(seq)
    fwd_pos = jnp.arange(seq, dtype=jnp.int32)
    pos = jnp.concatenate([fwd_pos, (seq - fwd_pos) % seq])
    t_fwd = jnp.linspace(0.0, 1.0, seq, dtype=F32)
    t = jnp.concatenate([t_fwd, jnp.roll(t_fwd[::-1], 1)])[:, None]
    w = 2.0 * math.pi * pos.astype(F32)[:, None] / seq
    fb = jnp.linspace(1e-4, FILT_BANDS - 1, FILT_BANDS, dtype=F32)[None, :]
    z = jnp.concatenate([t, jnp.cos(fb * w), -jnp.sin(fb * w)], axis=-1)
    nblk = seq // FILT_ROWS
    half = FILT_ROWS // 2
    pack = FILT_PACK
    zz = jnp.pad(z, ((0, 0), (0, pack - FILT_EMB))).reshape(2 * nblk, 2, half, pack)
    zz = jnp.transpose(zz, (0, 2, 1, 3)).reshape(n // 2, LANES)

    def padw(a):
        a = jnp.pad(a, ((0, pack - a.shape[0]), (0, pack - a.shape[1])))
        zero = jnp.zeros_like(a)
        return jnp.concatenate([jnp.concatenate([a, zero], axis=1),
                                jnp.concatenate([zero, a], axis=1)], axis=0)

    def padv(a):
        return jnp.tile(jnp.pad(a, (0, pack - a.shape[0])), 2)[None, :]

    w1 = padw(p['filt_w1'])
    w2 = padw(p['filt_w2'])
    w3 = padw(p['filt_w3'])
    w4 = jnp.pad(p['filt_w4'], ((0, pack - FILT_HIDDEN), (0, 0))).reshape(pack, HYENA_ORDER, 2, D_HYENA)
    w4 = jnp.transpose(w4, (2, 0, 1, 3)).reshape(2, pack, HYENA_ORDER * D_HYENA)
    w4 = jnp.stack([jnp.pad(w4, ((0, 0), (0, pack), (0, 0))),
                    jnp.pad(w4, ((0, 0), (pack, 0), (0, 0)))], axis=1)
    min_decay = math.log(HYENA_TARGET) / SLOW_DECAY_PCT
    max_decay = math.log(HYENA_TARGET) / FAST_DECAY_PCT
    deltas = jnp.linspace(min_decay, max_decay, D_HYENA, dtype=F32)
    deltas = jnp.tile(deltas, HYENA_ORDER)[None, :]
    width = HYENA_ORDER * D_HYENA
    sq = _const_spec((LANES, LANES))
    vec = _const_spec((1, LANES))
    filt, sums = pl.pallas_call(
        functools.partial(_filter_kernel, nblk=nblk),
        grid=(2 * nblk,),
        in_specs=[pl.BlockSpec((half, LANES), lambda i: (i, 0)),
                  sq, vec, sq, vec, sq, vec, vec,
                  pl.BlockSpec((1, 2, LANES, width), lambda i: (i // nblk, 0, 0, 0)),
                  _const_spec((1, width))],
        out_specs=[pl.BlockSpec((FILT_ROWS, width), lambda i: (i, 0)),
                   pl.BlockSpec((1, 1, width), lambda i: (i // nblk, 0, 0))],
        out_shape=[jax.ShapeDtypeStruct((n, width), F32),
                   jax.ShapeDtypeStruct((2, 1, width), F32)],
        compiler_params=pltpu.CompilerParams(
            dimension_semantics=("arbitrary",), vmem_limit_bytes=VMEM_LIMIT),
        name="hyena_filter",
    )(zz, w1, padv(p['filt_b1']), w2, padv(p['filt_b2']), w3, padv(p['filt_b3']),
      padv(p['filt_freq']), w4, deltas)
    n1, h1 = tb['n1'], tb['h1']
    return pl.pallas_call(
        functools.partial(_kspec_kernel, n1=n1),
        grid=(width // LANES,),
        in_specs=[pl.BlockSpec((n, LANES), lambda j: (0, j)),
                  pl.BlockSpec((2, 1, LANES), lambda j: (0, 0, j)),
                  _const_spec((n1, n1)), _const_spec((h1, 2 * FFT_N2, 2 * FFT_N2))],
        out_specs=pl.BlockSpec((n, LANES), lambda j: (0, j)),
        out_shape=jax.ShapeDtypeStruct((n, width), BF16),
        scratch_shapes=[pltpu.VMEM((h1 * A_PITCH, LANES), F32),
                        pltpu.VMEM((n1 * T_PITCH, LANES), F32)],
        compiler_params=pltpu.CompilerParams(
            dimension_semantics=("parallel",), vmem_limit_bytes=VMEM_LIMIT),
        name="hyena_filter_spectrum",
    )(filt, sums, _bf16_const(tb['m1_full']), _bf16_const(tb['m2']))


CHUNK_UNROLL = 8


def _for_chunks(count, body):
    body(0, True, False)

    def step(t1, carry):
        body(t1, False, False)
        return carry

    lax.fori_loop(1, count - CHUNK_UNROLL + 1, step, 0, unroll=CHUNK_UNROLL)
    for t1 in range(count - CHUNK_UNROLL + 1, count):
        body(t1, False, t1 == count - 1)


def _short_conv_rows(ref, w_ref, b_ref, r0, nrows, first, last):
    x = ref[pl.ds(r0, nrows), :]
    rid = lax.broadcasted_iota(jnp.int32, x.shape, 0)
    if first:
        prev = jnp.where(rid == 0, 0.0, pltpu.roll(x, 1, axis=0))
    else:
        prev = ref[pl.ds(r0 - 1, nrows), :]
    if last:
        nxt = jnp.where(rid == nrows - 1, 0.0, pltpu.roll(x, nrows - 1, axis=0))
    else:
        nxt = ref[pl.ds(r0 + 1, nrows), :]
    return prev * w_ref[0:1, :] + x * w_ref[1:2, :] + nxt * w_ref[2:3, :] + b_ref[...]


def _long_conv_kernel(*refs, n1, conv_z, conv_g):
    refs = list(refs)
    a_ref, tp_ref = refs[-2:]
    slabs = a_ref.shape[0]

    def take_operand(conv):
        return [tuple(refs.pop(0) for _ in range(3 if conv else 1)) for _ in range(slabs)]

    z_ops = take_operand(conv_z)
    g_ops = take_operand(conv_g)
    k_ref, bias_ref, m1_ref, m1i_ref, m2_ref, m2i_ref, o_ref = refs[:-2]
    n2 = FFT_N2
    h1 = n1 // 2

    def put(ref, idx, x):
        for s in range(slabs):
            ref[s, idx, :] = x[:, s * LANES:(s + 1) * LANES]

    def get(ref, idx):
        return jnp.concatenate([ref[s, idx, :] for s in range(slabs)], axis=1)

    def rows_of(ops, conv, t1, first, last):
        r0 = t1 * n2 if isinstance(t1, int) else pl.multiple_of(t1 * n2, n2)
        if conv:
            parts = [_short_conv_rows(ref, w_ref, b_ref, r0, n2, first, last) for ref, w_ref, b_ref in ops]
        else:
            parts = [ref[pl.ds(r0, n2), :] for ref, in ops]
        return jnp.concatenate(parts, axis=1)

    def z_rows(t1, first, last):
        return rows_of(z_ops, conv_z, t1, first, last)

    def g_rows(t1, first, last):
        return rows_of(g_ops, conv_g, t1, first, last)

    def pitched(t1):
        r0 = t1 * T_PITCH
        return pl.ds(r0 if isinstance(t1, int) else pl.multiple_of(r0, PITCH_ALIGN), n2)

    def pad(t1, first, last):
        put(tp_ref, pitched(t1), z_rows(t1, first, last))

    _for_chunks(h1, pad)

    def stage1(t2, carry):
        zt = get(tp_ref, pl.ds(t2, h1, stride=T_PITCH))
        a = _bdot(m1_ref[...], zt.astype(BF16))
        put(a_ref, pl.ds(t2, h1, stride=A_PITCH), a[:h1])
        put(a_ref, pl.ds(t2 + n2, h1, stride=A_PITCH), a[h1:])
        return carry

    lax.fori_loop(0, n2, stage1, 0, unroll=4)

    def stage2(f1, carry):
        ra = pl.ds(pl.multiple_of(f1 * A_PITCH, PITCH_ALIGN), 2 * n2)
        rk = pl.multiple_of(f1 * 2 * n2, 2 * n2)
        x = _bdot(m2_ref[f1], get(a_ref, ra).astype(BF16))
        kk = k_ref[pl.ds(rk, 2 * n2), :].astype(F32)
        xr, xi, kr, ki = x[:n2], x[n2:], kk[:n2], kk[n2:]
        y = jnp.concatenate([xr * kr - xi * ki, xr * ki + xi * kr], axis=0)
        put(a_ref, ra, _bdot(m2i_ref[f1], y.astype(BF16)))
        return carry

    lax.fori_loop(0, h1, stage2, 0, unroll=16)

    def stage3(t2, carry):
        br = get(a_ref, pl.ds(t2, h1, stride=A_PITCH))
        bi = get(a_ref, pl.ds(t2 + n2, h1, stride=A_PITCH))
        y = _bdot(m1i_ref[...], jnp.concatenate([br, bi], axis=0).astype(BF16))
        put(tp_ref, pl.ds(t2, h1, stride=T_PITCH), y)
        return carry

    lax.fori_loop(0, n2, stage3, 0, unroll=4)

    bias = bias_ref[...]

    def finish(t1, first, last):
        r0 = t1 * n2 if isinstance(t1, int) else pl.multiple_of(t1 * n2, n2)
        y = get(tp_ref, pitched(t1))
        o_ref[pl.ds(r0, n2), :] = g_rows(t1, first, last) * (y + bias * z_rows(t1, first, last))

    _for_chunks(h1, finish)


def _bf16_const(a):
    return jnp.asarray(a, F32).astype(BF16)


def _long_conv_slabs(seq):
    n1 = 2 * seq // FFT_N2
    h1 = n1 // 2
    tables = 2 * (2 * n1 * h1 + 2 * h1 * (2 * FFT_N2) ** 2)
    for slabs in (2, 1):
        io = 3 * 2 * seq * 4
        per_lane = io + 2 * 2 * seq * 2 + h1 * (A_PITCH + T_PITCH) * 4
        if per_lane * slabs * LANES + tables <= VMEM_LIMIT - VMEM_HEADROOM:
            return slabs
    raise ValueError(f"long conv of length {seq} does not fit VMEM")


def _long_conv(z, z_off, gate, g_off, kspec, k_off, bias, batch, seq, short=None, conv_z=False):
    n = 2 * seq
    tb = _fft_tables(seq)
    n1, h1 = tb['n1'], tb['h1']
    conv_g = short is not None
    slabs = _long_conv_slabs(seq)
    width = slabs * LANES
    assert z_off % slabs == 0 and g_off % slabs == 0 and k_off % slabs == 0

    def operand(arr, off, conv):
        specs, args = [], []
        for s in range(slabs):
            specs.append(pl.BlockSpec((seq, LANES), lambda j, b, s=s: (b, off + j * slabs + s)))
            args.append(arr)
            if conv:
                specs += [pl.BlockSpec((3, LANES), lambda j, b, s=s: (0, off + j * slabs + s)),
                          pl.BlockSpec((1, LANES), lambda j, b, s=s: (0, off + j * slabs + s))]
                args += list(short)
        return specs, args

    z_specs, z_args = operand(z, z_off, conv_z)
    g_specs, g_args = operand(gate, g_off, conv_g)
    k_blk = k_off // slabs
    return pl.pallas_call(
        functools.partial(_long_conv_kernel, n1=n1, conv_z=conv_z, conv_g=conv_g),
        grid=(D_HYENA // width, batch),
        in_specs=z_specs + g_specs + [
                  pl.BlockSpec((n, width), lambda j, b: (0, k_blk + j)),
                  pl.BlockSpec((1, width), lambda j, b: (0, j)),
                  _const_spec((n1, h1)), _const_spec((h1, n1)),
                  _const_spec((h1, 2 * FFT_N2, 2 * FFT_N2)),
                  _const_spec((h1, 2 * FFT_N2, 2 * FFT_N2))],
        out_specs=pl.BlockSpec((seq, width), lambda j, b: (b, j)),
        out_shape=jax.ShapeDtypeStruct((batch * seq, D_HYENA), F32),
        scratch_shapes=[pltpu.VMEM((slabs, h1 * A_PITCH, LANES), F32),
                        pltpu.VMEM((slabs, h1 * T_PITCH, LANES), F32)],
        compiler_params=pltpu.CompilerParams(
            dimension_semantics=("parallel", "parallel"), vmem_limit_bytes=VMEM_LIMIT),
        name="hyena_long_conv",
    )(*z_args, *g_args, kspec, bias, _bf16_const(tb['m1']), _bf16_const(tb['m1_inv']),
      _bf16_const(tb['m2']), _bf16_const(tb['m2_inv']))


def _layer(x3, p):
    batch, seq, _ = x3.shape
    x = x3.reshape(batch * seq, D_MODEL)
    x = _ffn(x, p['ffn1_pre_g'], p['ffn1_w1'], p['ffn1_w3'], p['ffn1_w2'], p['ffn1_post_g'])
    hy, qkv, gate = _inproj(x, p['mix_pre_g'], p['w_in'], seq)
    kspec = _filter_spectra(seq, p)
    tiles = D_HYENA // LANES
    short = (p['short_w'], p['short_b'])
    z1 = _long_conv(hy, 0, hy, tiles, kspec, 0, p['hyena_bias'][0:1], batch, seq, short=short, conv_z=True)
    yh = _long_conv(z1, 0, hy, 2 * tiles, kspec, tiles, p['hyena_bias'][1:2], batch, seq, short=short)
    yr = _retention(qkv, gate, p['ret_log_decay_f'], p['ret_log_decay_b'], batch, seq)
    mix = (yh, yr, p['w_out'], p['mix_post_g'])
    x = _ffn(x, p['ffn2_pre_g'], p['ffn2_w1'], p['ffn2_w3'], p['ffn2_w2'], p['ffn2_post_g'], mix=mix)
    return x.reshape(batch, seq, D_MODEL)


_MATRIX_PARAMS = ('ffn1_w1', 'ffn1_w3', 'ffn1_w2', 'w_in', 'w_out', 'ffn2_w1', 'ffn2_w3', 'ffn2_w2')
_GAIN_PARAMS = ('ffn1_pre_g', 'ffn1_post_g', 'mix_pre_g', 'mix_post_g', 'ffn2_pre_g', 'ffn2_post_g')


def kernel(x_prompt, x_sample, ffn1_pre_g, ffn1_w1, ffn1_w3, ffn1_w2, ffn1_post_g, mix_pre_g, w_in, short_w, short_b, filt_w1, filt_b1, filt_w2, filt_b2, filt_w3, filt_b3, filt_w4, filt_freq, hyena_bias, ret_log_decay_f, ret_log_decay_b, w_out, mix_post_g, ffn2_pre_g, ffn2_w1, ffn2_w3, ffn2_w2, ffn2_post_g):
    params = dict(ffn1_pre_g=ffn1_pre_g, ffn1_w1=ffn1_w1, ffn1_w3=ffn1_w3, ffn1_w2=ffn1_w2,
                  ffn1_post_g=ffn1_post_g, mix_pre_g=mix_pre_g, w_in=w_in, short_w=short_w,
                  short_b=short_b, filt_w1=filt_w1, filt_b1=filt_b1, filt_w2=filt_w2,
                  filt_b2=filt_b2, filt_w3=filt_w3, filt_b3=filt_b3, filt_w4=filt_w4,
                  filt_freq=filt_freq, hyena_bias=hyena_bias, ret_log_decay_f=ret_log_decay_f,
                  ret_log_decay_b=ret_log_decay_b, w_out=w_out, mix_post_g=mix_post_g,
                  ffn2_pre_g=ffn2_pre_g, ffn2_w1=ffn2_w1, ffn2_w3=ffn2_w3, ffn2_w2=ffn2_w2,
                  ffn2_post_g=ffn2_post_g)
    depth = ffn1_w1.shape[0]

    def run(x):
        for l in range(depth):
            p = {k: v[l] for k, v in params.items()}
            for k in _MATRIX_PARAMS:
                p[k] = p[k].astype(BF16)
            for k in _GAIN_PARAMS:
                p[k] = p[k][None, :]
            p['short_b'] = p['short_b'][None, :]
            x = _layer(x, p)
        return x

    return (run(x_prompt), run(x_sample))
```

```python
import functools
import math

import numpy as np
import jax
import jax.numpy as jnp
from jax import lax
from jax.experimental import pallas as pl
from jax.experimental.pallas import tpu as pltpu

F32 = jnp.float32
BF16 = jnp.bfloat16

D_MODEL = 1024
D_HYENA = 512
D_RET = 512
HYENA_ORDER = 2
N_RET_HEADS = 4
RET_HEAD_DIM = 128
D_FF = 2816
FILT_EMB = 33
FILT_BANDS = 16
FILT_HIDDEN = 64
ROPE_BASE = 10000.0
NORM_EPS = 1e-6
HYENA_TARGET = 1e-2
FAST_DECAY_PCT = 0.3
SLOW_DECAY_PCT = 1.5
N_HY_COLS = (HYENA_ORDER + 1) * D_HYENA
D_IN = N_HY_COLS + 4 * D_RET

LANES = 128
VMEM_LIMIT = 56 * 1024 * 1024
VMEM_HEADROOM = 4 * 1024 * 1024
FFT_N2 = 32
RET_CHUNK = 128
RET_GROUP = 32
TOKEN_TILE = 512


def _const_spec(shape):
    nd = len(shape)
    return pl.BlockSpec(shape, lambda *_: (0,) * nd, pipeline_mode=pl.Buffered(1))


def _rms(x, g):
    ms = jnp.mean(x * x, axis=-1, keepdims=True)
    return x * lax.rsqrt(ms + NORM_EPS) * g


def _bdot(a, b):
    return jnp.dot(a, b, preferred_element_type=F32)


def _ffn_core(x, pre_ref, w1_ref, w3_ref, w2_ref, post_ref, o_ref):
    h = _rms(x, pre_ref[...]).astype(BF16)
    a = _bdot(h, w1_ref[...])
    b = _bdot(h, w3_ref[...])
    g = (a * jax.nn.sigmoid(a) * b).astype(BF16)
    y = _bdot(g, w2_ref[...])
    o_ref[...] = x + 0.5 * _rms(y, post_ref[...])


def _ffn_kernel(x_ref, pre_ref, w1_ref, w3_ref, w2_ref, post_ref, o_ref):
    _ffn_core(x_ref[...], pre_ref, w1_ref, w3_ref, w2_ref, post_ref, o_ref)


def _mix_ffn_kernel(x_ref, yh_ref, yr_ref, woh_ref, wor_ref, mg_ref,
                    pre_ref, w1_ref, w3_ref, w2_ref, post_ref, o_ref):
    y = _bdot(yh_ref[...].astype(BF16), woh_ref[...])
    y = y + _bdot(yr_ref[...].astype(BF16), wor_ref[...])
    x = x_ref[...] + _rms(y, mg_ref[...])
    _ffn_core(x, pre_ref, w1_ref, w3_ref, w2_ref, post_ref, o_ref)


def _row_spec(tm, width):
    return pl.BlockSpec((tm, width), lambda i: (i, 0))


def _ffn(x, pre_g, w1, w3, w2, post_g, mix=None):
    t = x.shape[0]
    tm = TOKEN_TILE
    ffn_specs = [_const_spec((1, D_MODEL)), _const_spec((D_MODEL, D_FF)),
                 _const_spec((D_MODEL, D_FF)), _const_spec((D_FF, D_MODEL)),
                 _const_spec((1, D_MODEL))]
    ffn_args = (pre_g, w1, w3, w2, post_g)
    if mix is None:
        body, args = _ffn_kernel, (x,) + ffn_args
        specs = [_row_spec(tm, D_MODEL)] + ffn_specs
    else:
        yh, yr, w_out, mg = mix
        body, args = _mix_ffn_kernel, (x, yh, yr, w_out, w_out, mg) + ffn_args
        specs = [_row_spec(tm, D_MODEL), _row_spec(tm, D_HYENA), _row_spec(tm, D_RET),
                 pl.BlockSpec((D_HYENA, D_MODEL), lambda i: (0, 0), pipeline_mode=pl.Buffered(1)),
                 pl.BlockSpec((D_RET, D_MODEL), lambda i: (D_HYENA // D_RET, 0),
                              pipeline_mode=pl.Buffered(1)),
                 _const_spec((1, D_MODEL))] + ffn_specs
    return pl.pallas_call(
        body,
        grid=(t // tm,),
        in_specs=specs,
        out_specs=_row_spec(tm, D_MODEL),
        out_shape=jax.ShapeDtypeStruct((t, D_MODEL), F32),
        compiler_params=pltpu.CompilerParams(
            dimension_semantics=("parallel",), vmem_limit_bytes=VMEM_LIMIT),
        name="ffn_mix" if mix is not None else "ffn",
    )(*args)


def _inproj_kernel(x_ref, g_ref, w_ref, cc_ref, ss_ref, hy_ref, qkv_ref, gate_ref):
    h = _rms(x_ref[...], g_ref[...]).astype(BF16)
    qk0 = N_HY_COLS
    v0 = qk0 + 2 * D_RET
    g0 = v0 + D_RET
    hy_ref[...] = _bdot(h, w_ref[:, :qk0])
    gate_ref[...] = _bdot(h, w_ref[:, g0:])
    qkv_ref[:, 2 * D_RET:] = _bdot(h, w_ref[:, v0:g0]).astype(BF16)
    qk = _bdot(h, w_ref[:, qk0:v0])
    cc = cc_ref[...]
    ss = ss_ref[...]
    d = RET_HEAD_DIM
    for blk in range(2 * N_RET_HEADS):
        x = qk[:, blk * d:(blk + 1) * d]
        r = x * cc + pltpu.roll(x, d // 2, axis=1) * ss
        if blk >= N_RET_HEADS:
            r = r * (d ** -0.5)
        qkv_ref[:, blk * d:(blk + 1) * d] = r.astype(BF16)


def _inproj(x, g, w, seq):
    t = x.shape[0]
    tm = TOKEN_TILE
    cc, ss = _rope_tables(seq)
    pos_blocks = seq // tm
    rope_spec = pl.BlockSpec((tm, RET_HEAD_DIM), lambda i: (i % pos_blocks, 0))
    return pl.pallas_call(
        _inproj_kernel,
        grid=(t // tm,),
        in_specs=[_row_spec(tm, D_MODEL), _const_spec((1, D_MODEL)),
                  _const_spec((D_MODEL, D_IN)), rope_spec, rope_spec],
        out_specs=[_row_spec(tm, N_HY_COLS), _row_spec(tm, 3 * D_RET), _row_spec(tm, D_RET)],
        out_shape=[jax.ShapeDtypeStruct((t, N_HY_COLS), F32),
                   jax.ShapeDtypeStruct((t, 3 * D_RET), BF16),
                   jax.ShapeDtypeStruct((t, D_RET), F32)],
        compiler_params=pltpu.CompilerParams(
            dimension_semantics=("parallel",), vmem_limit_bytes=VMEM_LIMIT),
        name="inproj",
    )(x, g, w, cc, ss)


def _ret_kernel(lgf_ref, lgb_ref, q_ref, k_ref, v_ref, g_ref, o_ref, kt_ref, *, seq):
    c = RET_CHUNK
    d = RET_HEAD_DIM
    n_chunks = seq // c
    head = pl.program_id(1)
    lgf = jnp.full((c, d), lgf_ref[head], F32)
    lgb = jnp.full((c, d), lgb_ref[head], F32)
    row = lax.broadcasted_iota(jnp.int32, (c, d), 0).astype(F32)
    col = lax.broadcasted_iota(jnp.int32, (c, d), 1).astype(F32)
    diff = row - col
    dmat = jnp.where(diff >= 0.0, jnp.exp(jnp.maximum(diff, 0.0) * lgf),
                     jnp.exp(jnp.maximum(-diff, 0.0) * lgb))
    wq_f = jnp.exp((row + 1.0) * lgf)
    wk_f = jnp.exp((c - 1.0 - row) * lgf)
    wq_b = jnp.exp((c - row) * lgb)
    wk_b = jnp.exp(row * lgb)
    gc_f = jnp.exp(c * lgf)
    gc_b = jnp.exp(c * lgb)

    grp = RET_GROUP
    n_groups = n_chunks // grp

    def fwd(gi, state):
        rows = [pl.ds(pl.multiple_of((gi * grp + j) * c, c), c) for j in range(grp)]
        qbs, kts, vs = [], [], []
        for r in rows:
            ktb = k_ref[r, :].astype(F32).T.astype(BF16)
            kt_ref[r, :] = ktb
            qbs.append(q_ref[r, :])
            kts.append(ktb)
            vs.append(v_ref[r, :])
        scores = [_bdot(qb, ktb) for qb, ktb in zip(qbs, kts)]
        kvs = [_bdot(ktb, (v.astype(F32) * wk_f).astype(BF16)) for ktb, v in zip(kts, vs)]
        states = []
        for kv in kvs:
            states.append(state)
            state = state * gc_f + kv
        intras = [_bdot((s * dmat).astype(BF16), v) for s, v in zip(scores, vs)]
        crosses = [_bdot(qb, st.astype(BF16)) for qb, st in zip(qbs, states)]
        for r, intra, cross in zip(rows, intras, crosses):
            o_ref[r, :] = intra + wq_f * cross
        return state

    lax.fori_loop(0, n_groups, fwd, jnp.zeros((d, d), F32))

    def bwd(gi, state):
        rows = [pl.ds(pl.multiple_of((n_chunks - 1 - gi * grp - j) * c, c), c) for j in range(grp)]
        kvs = [_bdot(kt_ref[r, :], (v_ref[r, :].astype(F32) * wk_b).astype(BF16)) for r in rows]
        states = []
        for kv in kvs:
            states.append(state)
            state = state * gc_b + kv
        crosses = [_bdot(q_ref[r, :], st.astype(BF16)) for r, st in zip(rows, states)]
        outs = [o_ref[r, :] + wq_b * cross for r, cross in zip(rows, crosses)]
        means = [jnp.mean(o * o, axis=-1, keepdims=True) for o in outs]
        norms = [lax.rsqrt(m + NORM_EPS) for m in means]
        gates = [g_ref[r, :] for r in rows]
        gates = [g * jax.nn.sigmoid(g) for g in gates]
        for r, o, nrm, g in zip(rows, outs, norms, gates):
            o_ref[r, :] = g * (o * nrm)
        return state

    lax.fori_loop(0, n_groups, bwd, jnp.zeros((d, d), F32))


def _rope_tables(seq):
    d = RET_HEAD_DIM
    inv = 1.0 / (ROPE_BASE ** (jnp.arange(0, d, 2, dtype=F32) / d))
    ang = jnp.arange(seq, dtype=F32)[:, None] * inv[None, :]
    c, s = jnp.cos(ang), jnp.sin(ang)
    return jnp.concatenate([c, c], axis=-1), jnp.concatenate([-s, s], axis=-1)


def _retention(qkv, gate, lg_f, lg_b, batch, seq):
    assert RET_CHUNK == RET_HEAD_DIM
    heads = N_RET_HEADS

    def col(off):
        return pl.BlockSpec((seq, LANES), lambda b, h, *_: (b, off * heads + h))

    grid_spec = pltpu.PrefetchScalarGridSpec(
        num_scalar_prefetch=2,
        grid=(batch, heads),
        in_specs=[col(0), col(1), col(2), col(0)],
        out_specs=pl.BlockSpec((seq, LANES), lambda b, h, *_: (b, h)),
        scratch_shapes=[pltpu.VMEM((seq, RET_CHUNK), BF16)],
    )
    return pl.pallas_call(
        functools.partial(_ret_kernel, seq=seq),
        grid_spec=grid_spec,
        out_shape=jax.ShapeDtypeStruct((batch * seq, D_RET), F32),
        compiler_params=pltpu.CompilerParams(
            dimension_semantics=("parallel", "parallel"), vmem_limit_bytes=VMEM_LIMIT),
        name="retention",
    )(lg_f, lg_b, qkv, qkv, qkv, gate)


def _split(x):
    hi = x.astype(BF16)
    return hi, (x - hi.astype(F32)).astype(BF16)


def _dot3(a, b):
    a_hi, a_lo = a
    b_hi, b_lo = b
    return _bdot(a_hi, b_hi) + _bdot(a_hi, b_lo) + _bdot(a_lo, b_hi)


@functools.lru_cache(maxsize=None)
def _fft_tables(seq):
    n = 2 * seq
    n2 = FFT_N2
    n1 = n // n2
    h1 = n1 // 2
    f1 = np.arange(h1, dtype=np.float64) + 0.5
    th = 2.0 * np.pi * f1[:, None] * np.arange(n1, dtype=np.float64)[None, :] / n1
    m1_full = np.concatenate([np.cos(th), -np.sin(th)], axis=0)
    m1 = m1_full[:, :h1]
    m1_inv = (2.0 / n) * m1.T
    t2 = np.arange(n2, dtype=np.float64)
    phi = 2.0 * np.pi * (np.arange(n2, dtype=np.float64)[None, :, None] * t2[None, None, :] / n2
                         + f1[:, None, None] * t2[None, None, :] / n)
    gr, gi = np.cos(phi), -np.sin(phi)
    m2 = np.concatenate([np.concatenate([gr, -gi], axis=2),
                         np.concatenate([gi, gr], axis=2)], axis=1)
    m2_inv = np.transpose(m2, (0, 2, 1))
    return dict(n1=n1, h1=h1, m1_full=m1_full, m1=m1, m1_inv=m1_inv, m2=m2, m2_inv=m2_inv)


FILT_ROWS = 512
FILT_PACK = LANES // 2
PITCH_ALIGN = 4
A_PITCH = 2 * FFT_N2 + PITCH_ALIGN
T_PITCH = FFT_N2 + PITCH_ALIGN


def _filter_kernel(z_ref, w1_ref, b1_ref, w2_ref, b2_ref, w3_ref, b3_ref, fr_ref,
                   w4_ref, dl_ref, o_ref, sum_ref, *, nblk):
    i = pl.program_id(0)
    half = FILT_ROWS // 2
    fr = fr_ref[...]
    z = z_ref[...]
    h = jnp.sin(fr * (_dot3(_split(z), _split(w1_ref[...])) + b1_ref[...]))
    h = jnp.sin(fr * (_dot3(_split(h), _split(w2_ref[...])) + b2_ref[...]))
    h = jnp.sin(fr * (_dot3(_split(h), _split(w3_ref[...])) + b3_ref[...]))
    hs = _split(h)
    adl = jnp.abs(dl_ref[...])
    backward = i >= nblk
    for side in range(2):
        lane = side * FILT_PACK
        t = jnp.broadcast_to(z[:, lane:lane + 1], z.shape)
        out_rows = slice(side * half, (side + 1) * half)
        for c in range(HYENA_ORDER * D_HYENA // LANES):
            cols = slice(c * LANES, (c + 1) * LANES)
            hc = _dot3(hs, _split(w4_ref[0, side, :, cols])) * jnp.exp(-t * adl[:, cols])
            part = jnp.sum(jnp.abs(hc), axis=0, keepdims=True)

            if side == 0:
                @pl.when(i % nblk == 0)
                def _():
                    sum_ref[0, :, cols] = part

                @pl.when(i % nblk != 0)
                def _():
                    sum_ref[0, :, cols] = sum_ref[0, :, cols] + part

                rows = lax.broadcasted_iota(jnp.int32, hc.shape, 0)
                flipped = jnp.where(jnp.logical_and(rows == 0, i == nblk), 0.0, -hc)
            else:
                sum_ref[0, :, cols] = sum_ref[0, :, cols] + part
                flipped = -hc
            o_ref[out_rows, cols] = jnp.where(backward, flipped, hc)


def _kspec_kernel(k_ref, s_ref, m1_ref, m2_ref, o_ref, a_ref, kp_ref, *, n1):
    n2 = FFT_N2
    h1 = n1 // 2
    inv_f = 1.0 / s_ref[0]
    inv_b = 1.0 / s_ref[1]

    def pad(t1, carry):
        src = pl.multiple_of(t1 * n2, n2)
        dst = pl.multiple_of(t1 * T_PITCH, PITCH_ALIGN)
        kp_ref[pl.ds(dst, n2), :] = k_ref[pl.ds(src, n2), :] * jnp.where(t1 < h1, inv_f, inv_b)
        return carry

    lax.fori_loop(0, n1, pad, 0, unroll=8)

    def stage1(t2, carry):
        kt = kp_ref[pl.ds(t2, n1, stride=T_PITCH), :]
        a = _bdot(m1_ref[...], kt.astype(BF16))
        a_ref[pl.ds(t2, h1, stride=A_PITCH), :] = a[:h1]
        a_ref[pl.ds(t2 + n2, h1, stride=A_PITCH), :] = a[h1:]
        return carry

    lax.fori_loop(0, n2, stage1, 0, unroll=4)

    def stage2(f1, carry):
        src = pl.multiple_of(f1 * A_PITCH, PITCH_ALIGN)
        dst = pl.multiple_of(f1 * 2 * n2, 2 * n2)
        spec = _bdot(m2_ref[f1], a_ref[pl.ds(src, 2 * n2), :].astype(BF16))
        o_ref[pl.ds(dst, 2 * n2), :] = spec.astype(o_ref.dtype)
        return carry

    lax.fori_loop(0, h1, stage2, 0, unroll=16)


def _filter_spectra(seq, p):
    n = 2 * seq
    tb = _fft_tables(seq)
    nblk = seq // FILT_ROWS
    half = FILT_ROWS // 2
    pack = FILT_PACK

    def packed(v):
        return jnp.transpose(v.reshape(2 * nblk, 2, half), (0, 2, 1)).reshape(n // 2, 2, 1)

    fwd_pos = jnp.arange(seq, dtype=jnp.int32)
    pos = jnp.concatenate([fwd_pos, (seq - fwd_pos) % seq])
    t_fwd = jnp.linspace(0.0, 1.0, seq, dtype=F32)
    t = packed(jnp.concatenate([t_fwd, jnp.roll(t_fwd[::-1], 1)]))
    w = packed(2.0 * math.pi * pos.astype(F32) / seq)
    fb = jnp.linspace(1e-4, FILT_BANDS - 1, FILT_BANDS, dtype=F32)[None, None, :]
    zero = jnp.zeros((n // 2, 2, pack - FILT_EMB), F32)
    zz = jnp.concatenate([t, jnp.cos(fb * w), -jnp.sin(fb * w), zero], axis=-1).reshape(n // 2, LANES)

    def padw(a):
        a = jnp.pad(a, ((0, pack - a.shape[0]), (0, pack - a.shape[1])))
        zero = jnp.zeros_like(a)
        return jnp.concatenate([jnp.concatenate([a, zero], axis=1),
                                jnp.concatenate([zero, a], axis=1)], axis=0)

    def padv(a):
        return jnp.tile(jnp.pad(a, (0, pack - a.shape[0])), 2)[None, :]

    w1 = padw(p['filt_w1'])
    w2 = padw(p['filt_w2'])
    w3 = padw(p['filt_w3'])
    w4 = jnp.pad(p['filt_w4'], ((0, pack - FILT_HIDDEN), (0, 0))).reshape(pack, HYENA_ORDER, 2, D_HYENA)
    w4 = jnp.transpose(w4, (2, 0, 1, 3)).reshape(2, pack, HYENA_ORDER * D_HYENA)
    w4 = jnp.stack([jnp.pad(w4, ((0, 0), (0, pack), (0, 0))),
                    jnp.pad(w4, ((0, 0), (pack, 0), (0, 0)))], axis=1)
    min_decay = math.log(HYENA_TARGET) / SLOW_DECAY_PCT
    max_decay = math.log(HYENA_TARGET) / FAST_DECAY_PCT
    deltas = jnp.linspace(min_decay, max_decay, D_HYENA, dtype=F32)
    deltas = jnp.tile(deltas, HYENA_ORDER)[None, :]
    width = HYENA_ORDER * D_HYENA
    sq = _const_spec((LANES, LANES))
    vec = _const_spec((1, LANES))
    filt, sums = pl.pallas_call(
        functools.partial(_filter_kernel, nblk=nblk),
        grid=(2 * nblk,),
        in_specs=[pl.BlockSpec((half, LANES), lambda i: (i, 0)),
                  sq, vec, sq, vec, sq, vec, vec,
                  pl.BlockSpec((1, 2, LANES, width), lambda i: (i // nblk, 0, 0, 0)),
                  _const_spec((1, width))],
        out_specs=[pl.BlockSpec((FILT_ROWS, width), lambda i: (i, 0)),
                   pl.BlockSpec((1, 1, width), lambda i: (i // nblk, 0, 0))],
        out_shape=[jax.ShapeDtypeStruct((n, width), F32),
                   jax.ShapeDtypeStruct((2, 1, width), F32)],
        compiler_params=pltpu.CompilerParams(
            dimension_semantics=("arbitrary",), vmem_limit_bytes=VMEM_LIMIT),
        name="hyena_filter",
    )(zz, w1, padv(p['filt_b1']), w2, padv(p['filt_b2']), w3, padv(p['filt_b3']),
      padv(p['filt_freq']), w4, deltas)
    n1, h1 = tb['n1'], tb['h1']
    return pl.pallas_call(
        functools.partial(_kspec_kernel, n1=n1),
        grid=(width // LANES,),
        in_specs=[pl.BlockSpec((n, LANES), lambda j: (0, j)),
                  pl.BlockSpec((2, 1, LANES), lambda j: (0, 0, j)),
                  _const_spec((n1, n1)), _const_spec((h1, 2 * FFT_N2, 2 * FFT_N2))],
        out_specs=pl.BlockSpec((n, LANES), lambda j: (0, j)),
        out_shape=jax.ShapeDtypeStruct((n, width), BF16),
        scratch_shapes=[pltpu.VMEM((h1 * A_PITCH, LANES), F32),
                        pltpu.VMEM((n1 * T_PITCH, LANES), F32)],
        compiler_params=pltpu.CompilerParams(
            dimension_semantics=("parallel",), vmem_limit_bytes=VMEM_LIMIT),
        name="hyena_filter_spectrum",
    )(filt, sums, _bf16_const(tb['m1_full']), _bf16_const(tb['m2']))


CHUNK_UNROLL = 8


def _for_chunks(count, body):
    body(0, True, False)

    def step(t1, carry):
        body(t1, False, False)
        return carry

    lax.fori_loop(1, count - CHUNK_UNROLL + 1, step, 0, unroll=CHUNK_UNROLL)
    for t1 in range(count - CHUNK_UNROLL + 1, count):
        body(t1, False, t1 == count - 1)


def _short_conv_rows(ref, w_ref, b_ref, r0, nrows, first, last):
    x = ref[pl.ds(r0, nrows), :]
    rid = lax.broadcasted_iota(jnp.int32, x.shape, 0)
    if first:
        prev = jnp.where(rid == 0, 0.0, pltpu.roll(x, 1, axis=0))
    else:
        prev = ref[pl.ds(r0 - 1, nrows), :]
    if last:
        nxt = jnp.where(rid == nrows - 1, 0.0, pltpu.roll(x, nrows - 1, axis=0))
    else:
        nxt = ref[pl.ds(r0 + 1, nrows), :]
    return prev * w_ref[0:1, :] + x * w_ref[1:2, :] + nxt * w_ref[2:3, :] + b_ref[...]


def _long_conv_kernel(*refs, n1, conv_z, conv_g):
    refs = list(refs)
    a_ref, tp_ref = refs[-2:]
    slabs = a_ref.shape[0]

    def take_operand(conv):
        return [tuple(refs.pop(0) for _ in range(3 if conv else 1)) for _ in range(slabs)]

    z_ops = take_operand(conv_z)
    g_ops = take_operand(conv_g)
    k_ref, bias_ref, m1_ref, m1i_ref, m2_ref, m2i_ref, o_ref = refs[:-2]
    n2 = FFT_N2
    h1 = n1 // 2

    def put(ref, idx, x):
        for s in range(slabs):
            ref[s, idx, :] = x[:, s * LANES:(s + 1) * LANES]

    def get(ref, idx):
        return jnp.concatenate([ref[s, idx, :] for s in range(slabs)], axis=1)

    def rows_of(ops, conv, t1, first, last):
        r0 = t1 * n2 if isinstance(t1, int) else pl.multiple_of(t1 * n2, n2)
        if conv:
            parts = [_short_conv_rows(ref, w_ref, b_ref, r0, n2, first, last) for ref, w_ref, b_ref in ops]
        else:
            parts = [ref[pl.ds(r0, n2), :] for ref, in ops]
        return jnp.concatenate(parts, axis=1)

    def z_rows(t1, first, last):
        return rows_of(z_ops, conv_z, t1, first, last)

    def g_rows(t1, first, last):
        return rows_of(g_ops, conv_g, t1, first, last)

    def pitched(t1):
        r0 = t1 * T_PITCH
        return pl.ds(r0 if isinstance(t1, int) else pl.multiple_of(r0, PITCH_ALIGN), n2)

    def pad(t1, first, last):
        put(tp_ref, pitched(t1), z_rows(t1, first, last))

    _for_chunks(h1, pad)

    def stage1(t2, carry):
        zt = get(tp_ref, pl.ds(t2, h1, stride=T_PITCH))
        a = _bdot(m1_ref[...], zt.astype(BF16))
        put(a_ref, pl.ds(t2, h1, stride=A_PITCH), a[:h1])
        put(a_ref, pl.ds(t2 + n2, h1, stride=A_PITCH), a[h1:])
        return carry

    lax.fori_loop(0, n2, stage1, 0, unroll=4)

    def stage2(f1, carry):
        ra = pl.ds(pl.multiple_of(f1 * A_PITCH, PITCH_ALIGN), 2 * n2)
        rk = pl.multiple_of(f1 * 2 * n2, 2 * n2)
        x = _bdot(m2_ref[f1], get(a_ref, ra).astype(BF16))
        kk = k_ref[pl.ds(rk, 2 * n2), :].astype(F32)
        xr, xi, kr, ki = x[:n2], x[n2:], kk[:n2], kk[n2:]
        y = jnp.concatenate([xr * kr - xi * ki, xr * ki + xi * kr], axis=0)
        put(a_ref, ra, _bdot(m2i_ref[f1], y.astype(BF16)))
        return carry

    lax.fori_loop(0, h1, stage2, 0, unroll=16)

    def stage3(t2, carry):
        br = get(a_ref, pl.ds(t2, h1, stride=A_PITCH))
        bi = get(a_ref, pl.ds(t2 + n2, h1, stride=A_PITCH))
        y = _bdot(m1i_ref[...], jnp.concatenate([br, bi], axis=0).astype(BF16))
        put(tp_ref, pl.ds(t2, h1, stride=T_PITCH), y)
        return carry

    lax.fori_loop(0, n2, stage3, 0, unroll=4)

    bias = bias_ref[...]

    def finish(t1, first, last):
        r0 = t1 * n2 if isinstance(t1, int) else pl.multiple_of(t1 * n2, n2)
        y = get(tp_ref, pitched(t1))
        out = g_rows(t1, first, last) * (y + bias * z_rows(t1, first, last))
        o_ref[pl.ds(r0, n2), :] = out.astype(o_ref.dtype)

    _for_chunks(h1, finish)


def _bf16_const(a):
    return jnp.asarray(a, F32).astype(BF16)


def _long_conv_slabs(seq):
    n1 = 2 * seq // FFT_N2
    h1 = n1 // 2
    tables = 2 * (2 * n1 * h1 + 2 * h1 * (2 * FFT_N2) ** 2)
    for slabs in (2, 1):
        io = 3 * 2 * seq * 4
        per_lane = io + 2 * 2 * seq * 2 + h1 * (A_PITCH + T_PITCH) * 4
        if per_lane * slabs * LANES + tables <= VMEM_LIMIT - VMEM_HEADROOM:
            return slabs
    raise ValueError(f"long conv of length {seq} does not fit VMEM")


def _long_conv(z, z_off, gate, g_off, kspec, k_off, bias, batch, seq, short=None, conv_z=False,
               out_dtype=F32):
    n = 2 * seq
    tb = _fft_tables(seq)
    n1, h1 = tb['n1'], tb['h1']
    conv_g = short is not None
    slabs = _long_conv_slabs(seq)
    width = slabs * LANES
    assert z_off % slabs == 0 and g_off % slabs == 0 and k_off % slabs == 0

    def operand(arr, off, conv):
        specs, args = [], []
        for s in range(slabs):
            specs.append(pl.BlockSpec((seq, LANES), lambda j, b, s=s: (b, off + j * slabs + s)))
            args.append(arr)
            if conv:
                specs += [pl.BlockSpec((3, LANES), lambda j, b, s=s: (0, off + j * slabs + s)),
                          pl.BlockSpec((1, LANES), lambda j, b, s=s: (0, off + j * slabs + s))]
                args += list(short)
        return specs, args

    z_specs, z_args = operand(z, z_off, conv_z)
    g_specs, g_args = operand(gate, g_off, conv_g)
    k_blk = k_off // slabs
    return pl.pallas_call(
        functools.partial(_long_conv_kernel, n1=n1, conv_z=conv_z, conv_g=conv_g),
        grid=(D_HYENA // width, batch),
        in_specs=z_specs + g_specs + [
                  pl.BlockSpec((n, width), lambda j, b: (0, k_blk + j)),
                  pl.BlockSpec((1, width), lambda j, b: (0, j)),
                  _const_spec((n1, h1)), _const_spec((h1, n1)),
                  _const_spec((h1, 2 * FFT_N2, 2 * FFT_N2)),
                  _const_spec((h1, 2 * FFT_N2, 2 * FFT_N2))],
        out_specs=pl.BlockSpec((seq, width), lambda j, b: (b, j)),
        out_shape=jax.ShapeDtypeStruct((batch * seq, D_HYENA), out_dtype),
        scratch_shapes=[pltpu.VMEM((slabs, h1 * A_PITCH, LANES), F32),
                        pltpu.VMEM((slabs, h1 * T_PITCH, LANES), F32)],
        compiler_params=pltpu.CompilerParams(
            dimension_semantics=("parallel", "parallel"), vmem_limit_bytes=VMEM_LIMIT),
        name="hyena_long_conv",
    )(*z_args, *g_args, kspec, bias, _bf16_const(tb['m1']), _bf16_const(tb['m1_inv']),
      _bf16_const(tb['m2']), _bf16_const(tb['m2_inv']))


def _layer(x3, p):
    batch, seq, _ = x3.shape
    x = x3.reshape(batch * seq, D_MODEL)
    x = _ffn(x, p['ffn1_pre_g'], p['ffn1_w1'], p['ffn1_w3'], p['ffn1_w2'], p['ffn1_post_g'])
    hy, qkv, gate = _inproj(x, p['mix_pre_g'], p['w_in'], seq)
    kspec = _filter_spectra(seq, p)
    tiles = D_HYENA // LANES
    short = (p['short_w'], p['short_b'])
    z1 = _long_conv(hy, 0, hy, tiles, kspec, 0, p['hyena_bias'][0:1], batch, seq, short=short, conv_z=True)
    yh = _long_conv(z1, 0, hy, 2 * tiles, kspec, tiles, p['hyena_bias'][1:2], batch, seq, short=short,
                    out_dtype=BF16)
    yr = _retention(qkv, gate, p['ret_log_decay_f'], p['ret_log_decay_b'], batch, seq)
    mix = (yh, yr, p['w_out'], p['mix_post_g'])
    x = _ffn(x, p['ffn2_pre_g'], p['ffn2_w1'], p['ffn2_w3'], p['ffn2_w2'], p['ffn2_post_g'], mix=mix)
    return x.reshape(batch, seq, D_MODEL)


_MATRIX_PARAMS = ('ffn1_w1', 'ffn1_w3', 'ffn1_w2', 'w_in', 'w_out', 'ffn2_w1', 'ffn2_w3', 'ffn2_w2')
_GAIN_PARAMS = ('ffn1_pre_g', 'ffn1_post_g', 'mix_pre_g', 'mix_post_g', 'ffn2_pre_g', 'ffn2_post_g')


def kernel(x_prompt, x_sample, ffn1_pre_g, ffn1_w1, ffn1_w3, ffn1_w2, ffn1_post_g, mix_pre_g, w_in, short_w, short_b, filt_w1, filt_b1, filt_w2, filt_b2, filt_w3, filt_b3, filt_w4, filt_freq, hyena_bias, ret_log_decay_f, ret_log_decay_b, w_out, mix_post_g, ffn2_pre_g, ffn2_w1, ffn2_w3, ffn2_w2, ffn2_post_g):
    params = dict(ffn1_pre_g=ffn1_pre_g, ffn1_w1=ffn1_w1, ffn1_w3=ffn1_w3, ffn1_w2=ffn1_w2,
                  ffn1_post_g=ffn1_post_g, mix_pre_g=mix_pre_g, w_in=w_in, short_w=short_w,
                  short_b=short_b, filt_w1=filt_w1, filt_b1=filt_b1, filt_w2=filt_w2,
                  filt_b2=filt_b2, filt_w3=filt_w3, filt_b3=filt_b3, filt_w4=filt_w4,
                  filt_freq=filt_freq, hyena_bias=hyena_bias, ret_log_decay_f=ret_log_decay_f,
                  ret_log_decay_b=ret_log_decay_b, w_out=w_out, mix_post_g=mix_post_g,
                  ffn2_pre_g=ffn2_pre_g, ffn2_w1=ffn2_w1, ffn2_w3=ffn2_w3, ffn2_w2=ffn2_w2,
                  ffn2_post_g=ffn2_post_g)
    depth = ffn1_w1.shape[0]

    def run(x):
        for l in range(depth):
            p = {k: v[l] for k, v in params.items()}
            for k in _MATRIX_PARAMS:
                p[k] = p[k].astype(BF16)
            for k in _GAIN_PARAMS:
                p[k] = p[k][None, :]
            p['short_b'] = p['short_b'][None, :]
            x = _layer(x, p)
        return x

    return (run(x_prompt), run(x_sample))
```

```python
import functools
import math

import numpy as np
import jax
import jax.numpy as jnp
from jax import lax
from jax.experimental import pallas as pl
from jax.experimental.pallas import tpu as pltpu

F32 = jnp.float32
BF16 = jnp.bfloat16

D_MODEL = 1024
D_HYENA = 512
D_RET = 512
HYENA_ORDER = 2
N_RET_HEADS = 4
RET_HEAD_DIM = 128
D_FF = 2816
FILT_EMB = 33
FILT_BANDS = 16
FILT_HIDDEN = 64
ROPE_BASE = 10000.0
NORM_EPS = 1e-6
HYENA_TARGET = 1e-2
FAST_DECAY_PCT = 0.3
SLOW_DECAY_PCT = 1.5
N_HY_COLS = (HYENA_ORDER + 1) * D_HYENA
D_IN = N_HY_COLS + 4 * D_RET

LANES = 128
MXU_COLUMNS = 256
VMEM_LIMIT = 56 * 1024 * 1024
VMEM_HEADROOM = 4 * 1024 * 1024
FFT_N2 = 32
RET_CHUNK = 128
RET_GROUP = 32
TOKEN_TILE = 512


def _const_spec(shape):
    nd = len(shape)
    return pl.BlockSpec(shape, lambda *_: (0,) * nd, pipeline_mode=pl.Buffered(1))


def _rms(x, g):
    ms = jnp.mean(x * x, axis=-1, keepdims=True)
    return x * lax.rsqrt(ms + NORM_EPS) * g


def _bdot(a, b):
    return jnp.dot(a, b, preferred_element_type=F32)


def _ffn_core(x, pre_ref, w1_ref, w3_ref, w2_ref, post_ref, o_ref):
    h = _rms(x, pre_ref[...]).astype(BF16)
    a = _bdot(h, w1_ref[...])
    b = _bdot(h, w3_ref[...])
    g = (a * jax.nn.sigmoid(a) * b).astype(BF16)
    y = _bdot(g, w2_ref[...])
    o_ref[...] = x + 0.5 * _rms(y, post_ref[...])


def _ffn_kernel(x_ref, pre_ref, w1_ref, w3_ref, w2_ref, post_ref, o_ref):
    _ffn_core(x_ref[...], pre_ref, w1_ref, w3_ref, w2_ref, post_ref, o_ref)


def _mix_ffn_kernel(x_ref, yh_ref, yr_ref, woh_ref, wor_ref, mg_ref,
                    pre_ref, w1_ref, w3_ref, w2_ref, post_ref, o_ref):
    y = _bdot(yh_ref[...].astype(BF16), woh_ref[...])
    y = y + _bdot(yr_ref[...].astype(BF16), wor_ref[...])
    x = x_ref[...] + _rms(y, mg_ref[...])
    _ffn_core(x, pre_ref, w1_ref, w3_ref, w2_ref, post_ref, o_ref)


def _row_spec(tm, width):
    return pl.BlockSpec((tm, width), lambda i: (i, 0))


def _ffn(x, pre_g, w1, w3, w2, post_g, mix=None):
    t = x.shape[0]
    tm = TOKEN_TILE
    ffn_specs = [_const_spec((1, D_MODEL)), _const_spec((D_MODEL, D_FF)),
                 _const_spec((D_MODEL, D_FF)), _const_spec((D_FF, D_MODEL)),
                 _const_spec((1, D_MODEL))]
    ffn_args = (pre_g, w1, w3, w2, post_g)
    if mix is None:
        body, args = _ffn_kernel, (x,) + ffn_args
        specs = [_row_spec(tm, D_MODEL)] + ffn_specs
    else:
        yh, yr, w_out, mg = mix
        body, args = _mix_ffn_kernel, (x, yh, yr, w_out, w_out, mg) + ffn_args
        specs = [_row_spec(tm, D_MODEL), _row_spec(tm, D_HYENA), _row_spec(tm, D_RET),
                 pl.BlockSpec((D_HYENA, D_MODEL), lambda i: (0, 0), pipeline_mode=pl.Buffered(1)),
                 pl.BlockSpec((D_RET, D_MODEL), lambda i: (D_HYENA // D_RET, 0),
                              pipeline_mode=pl.Buffered(1)),
                 _const_spec((1, D_MODEL))] + ffn_specs
    return pl.pallas_call(
        body,
        grid=(t // tm,),
        in_specs=specs,
        out_specs=_row_spec(tm, D_MODEL),
        out_shape=jax.ShapeDtypeStruct((t, D_MODEL), F32),
        compiler_params=pltpu.CompilerParams(
            dimension_semantics=("parallel",), vmem_limit_bytes=VMEM_LIMIT),
        name="ffn_mix" if mix is not None else "ffn",
    )(*args)


def _inproj_kernel(x_ref, g_ref, w_ref, cc_ref, ss_ref, hy_ref, qkv_ref, gate_ref):
    h = _rms(x_ref[...], g_ref[...]).astype(BF16)
    qk0 = N_HY_COLS
    v0 = qk0 + 2 * D_RET
    g0 = v0 + D_RET
    hy_ref[...] = _bdot(h, w_ref[:, :qk0])
    gate_ref[...] = _bdot(h, w_ref[:, g0:])
    qkv_ref[:, 2 * D_RET:] = _bdot(h, w_ref[:, v0:g0]).astype(BF16)
    qk = _bdot(h, w_ref[:, qk0:v0])
    cc = cc_ref[...]
    ss = ss_ref[...]
    d = RET_HEAD_DIM
    for blk in range(2 * N_RET_HEADS):
        x = qk[:, blk * d:(blk + 1) * d]
        r = x * cc + pltpu.roll(x, d // 2, axis=1) * ss
        if blk >= N_RET_HEADS:
            r = r * (d ** -0.5)
        qkv_ref[:, blk * d:(blk + 1) * d] = r.astype(BF16)


def _inproj(x, g, w, seq):
    t = x.shape[0]
    tm = TOKEN_TILE
    cc, ss = _rope_tables(seq)
    pos_blocks = seq // tm
    rope_spec = pl.BlockSpec((tm, RET_HEAD_DIM), lambda i: (i % pos_blocks, 0))
    return pl.pallas_call(
        _inproj_kernel,
        grid=(t // tm,),
        in_specs=[_row_spec(tm, D_MODEL), _const_spec((1, D_MODEL)),
                  _const_spec((D_MODEL, D_IN)), rope_spec, rope_spec],
        out_specs=[_row_spec(tm, N_HY_COLS), _row_spec(tm, 3 * D_RET), _row_spec(tm, D_RET)],
        out_shape=[jax.ShapeDtypeStruct((t, N_HY_COLS), F32),
                   jax.ShapeDtypeStruct((t, 3 * D_RET), BF16),
                   jax.ShapeDtypeStruct((t, D_RET), F32)],
        compiler_params=pltpu.CompilerParams(
            dimension_semantics=("parallel",), vmem_limit_bytes=VMEM_LIMIT),
        name="inproj",
    )(x, g, w, cc, ss)


def _ret_kernel(lgf_ref, lgb_ref, q_ref, k_ref, v_ref, g_ref, o_ref, kt_ref, *, seq):
    c = RET_CHUNK
    d = RET_HEAD_DIM
    n_chunks = seq // c
    head = pl.program_id(1)
    lgf = jnp.full((c, d), lgf_ref[head], F32)
    lgb = jnp.full((c, d), lgb_ref[head], F32)
    row = lax.broadcasted_iota(jnp.int32, (c, d), 0).astype(F32)
    col = lax.broadcasted_iota(jnp.int32, (c, d), 1).astype(F32)
    diff = row - col
    dmat = jnp.where(diff >= 0.0, jnp.exp(jnp.maximum(diff, 0.0) * lgf),
                     jnp.exp(jnp.maximum(-diff, 0.0) * lgb))
    wq_f = jnp.exp((row + 1.0) * lgf)
    wk_f = jnp.exp((c - 1.0 - row) * lgf)
    wq_b = jnp.exp((c - row) * lgb)
    wk_b = jnp.exp(row * lgb)
    gc_f = jnp.exp(c * lgf)
    gc_b = jnp.exp(c * lgb)

    grp = RET_GROUP
    n_groups = n_chunks // grp

    def fwd(gi, state):
        rows = [pl.ds(pl.multiple_of((gi * grp + j) * c, c), c) for j in range(grp)]
        qbs, kts, vs = [], [], []
        for r in rows:
            ktb = k_ref[r, :].astype(F32).T.astype(BF16)
            kt_ref[r, :] = ktb
            qbs.append(q_ref[r, :])
            kts.append(ktb)
            vs.append(v_ref[r, :])
        scores = [_bdot(qb, ktb) for qb, ktb in zip(qbs, kts)]
        kvs = [_bdot(ktb, (v.astype(F32) * wk_f).astype(BF16)) for ktb, v in zip(kts, vs)]
        states = []
        for kv in kvs:
            states.append(state)
            state = state * gc_f + kv
        intras = [_bdot((s * dmat).astype(BF16), v) for s, v in zip(scores, vs)]
        crosses = [_bdot(qb, st.astype(BF16)) for qb, st in zip(qbs, states)]
        for r, intra, cross in zip(rows, intras, crosses):
            o_ref[r, :] = intra + wq_f * cross
        return state

    lax.fori_loop(0, n_groups, fwd, jnp.zeros((d, d), F32))

    def bwd(gi, state):
        rows = [pl.ds(pl.multiple_of((n_chunks - 1 - gi * grp - j) * c, c), c) for j in range(grp)]
        kvs = [_bdot(kt_ref[r, :], (v_ref[r, :].astype(F32) * wk_b).astype(BF16)) for r in rows]
        states = []
        for kv in kvs:
            states.append(state)
            state = state * gc_b + kv
        crosses = [_bdot(q_ref[r, :], st.astype(BF16)) for r, st in zip(rows, states)]
        outs = [o_ref[r, :] + wq_b * cross for r, cross in zip(rows, crosses)]
        means = [jnp.mean(o * o, axis=-1, keepdims=True) for o in outs]
        norms = [lax.rsqrt(m + NORM_EPS) for m in means]
        gates = [g_ref[r, :] for r in rows]
        gates = [g * jax.nn.sigmoid(g) for g in gates]
        for r, o, nrm, g in zip(rows, outs, norms, gates):
            o_ref[r, :] = g * (o * nrm)
        return state

    lax.fori_loop(0, n_groups, bwd, jnp.zeros((d, d), F32))


def _rope_tables(seq):
    d = RET_HEAD_DIM
    inv = 1.0 / (ROPE_BASE ** (jnp.arange(0, d, 2, dtype=F32) / d))
    ang = jnp.arange(seq, dtype=F32)[:, None] * inv[None, :]
    c, s = jnp.cos(ang), jnp.sin(ang)
    return jnp.concatenate([c, c], axis=-1), jnp.concatenate([-s, s], axis=-1)


def _retention(qkv, gate, lg_f, lg_b, batch, seq):
    assert RET_CHUNK == RET_HEAD_DIM
    heads = N_RET_HEADS

    def col(off):
        return pl.BlockSpec((seq, LANES), lambda b, h, *_: (b, off * heads + h))

    grid_spec = pltpu.PrefetchScalarGridSpec(
        num_scalar_prefetch=2,
        grid=(batch, heads),
        in_specs=[col(0), col(1), col(2), col(0)],
        out_specs=pl.BlockSpec((seq, LANES), lambda b, h, *_: (b, h)),
        scratch_shapes=[pltpu.VMEM((seq, RET_CHUNK), BF16)],
    )
    return pl.pallas_call(
        functools.partial(_ret_kernel, seq=seq),
        grid_spec=grid_spec,
        out_shape=jax.ShapeDtypeStruct((batch * seq, D_RET), F32),
        compiler_params=pltpu.CompilerParams(
            dimension_semantics=("parallel", "parallel"), vmem_limit_bytes=VMEM_LIMIT),
        name="retention",
    )(lg_f, lg_b, qkv, qkv, qkv, gate)


def _split(x):
    hi = x.astype(BF16)
    return hi, (x - hi.astype(F32)).astype(BF16)


def _dot3(a, b):
    a_hi, a_lo = a
    b_hi, b_lo = b
    return _bdot(a_hi, b_hi) + _bdot(a_hi, b_lo) + _bdot(a_lo, b_hi)


@functools.lru_cache(maxsize=None)
def _fft_tables(seq):
    n = 2 * seq
    n2 = FFT_N2
    n1 = n // n2
    h1 = n1 // 2
    f1 = np.arange(h1, dtype=np.float64) + 0.5
    th = 2.0 * np.pi * f1[:, None] * np.arange(n1, dtype=np.float64)[None, :] / n1
    m1_full = np.concatenate([np.cos(th), -np.sin(th)], axis=0)
    m1 = m1_full[:, :h1]
    m1_inv = (2.0 / n) * m1.T
    t2 = np.arange(n2, dtype=np.float64)
    phi = 2.0 * np.pi * (np.arange(n2, dtype=np.float64)[None, :, None] * t2[None, None, :] / n2
                         + f1[:, None, None] * t2[None, None, :] / n)
    gr, gi = np.cos(phi), -np.sin(phi)
    m2 = np.concatenate([np.concatenate([gr, -gi], axis=2),
                         np.concatenate([gi, gr], axis=2)], axis=1)
    m2_inv = np.transpose(m2, (0, 2, 1))
    return dict(n1=n1, h1=h1, m1_full=m1_full, m1=m1, m1_inv=m1_inv, m2=m2, m2_inv=m2_inv)


FILT_ROWS = 512
FILT_PACK = LANES // 2
PITCH_ALIGN = 4
A_PITCH = 2 * FFT_N2 + PITCH_ALIGN
T_PITCH = FFT_N2 + PITCH_ALIGN


def _filter_kernel(z_ref, w1_ref, b1_ref, w2_ref, b2_ref, w3_ref, b3_ref, fr_ref,
                   w4_ref, dl_ref, o_ref, sum_ref, *, nblk):
    i = pl.program_id(0)
    half = FILT_ROWS // 2
    fr = fr_ref[...]
    z = z_ref[...]
    h = jnp.sin(fr * (_dot3(_split(z), _split(w1_ref[...])) + b1_ref[...]))
    h = jnp.sin(fr * (_dot3(_split(h), _split(w2_ref[...])) + b2_ref[...]))
    h = jnp.sin(fr * (_dot3(_split(h), _split(w3_ref[...])) + b3_ref[...]))
    hs = _split(h)
    adl = jnp.abs(dl_ref[...])
    backward = i >= nblk
    for side in range(2):
        lane = side * FILT_PACK
        t = jnp.broadcast_to(z[:, lane:lane + 1], z.shape)
        out_rows = slice(side * half, (side + 1) * half)
        for c in range(HYENA_ORDER * D_HYENA // LANES):
            cols = slice(c * LANES, (c + 1) * LANES)
            hc = _dot3(hs, _split(w4_ref[0, side, :, cols])) * jnp.exp(-t * adl[:, cols])
            part = jnp.sum(jnp.abs(hc), axis=0, keepdims=True)

            if side == 0:
                @pl.when(i % nblk == 0)
                def _():
                    sum_ref[0, :, cols] = part

                @pl.when(i % nblk != 0)
                def _():
                    sum_ref[0, :, cols] = sum_ref[0, :, cols] + part

                rows = lax.broadcasted_iota(jnp.int32, hc.shape, 0)
                flipped = jnp.where(jnp.logical_and(rows == 0, i == nblk), 0.0, -hc)
            else:
                sum_ref[0, :, cols] = sum_ref[0, :, cols] + part
                flipped = -hc
            o_ref[out_rows, cols] = jnp.where(backward, flipped, hc)


def _kspec_kernel(k_ref, s_ref, m1_ref, m2_ref, o_ref, a_ref, kp_ref, *, n1):
    n2 = FFT_N2
    h1 = n1 // 2
    inv_f = 1.0 / s_ref[0]
    inv_b = 1.0 / s_ref[1]

    def pad(t1, carry):
        src = pl.multiple_of(t1 * n2, n2)
        dst = pl.multiple_of(t1 * T_PITCH, PITCH_ALIGN)
        kp_ref[pl.ds(dst, n2), :] = k_ref[pl.ds(src, n2), :] * jnp.where(t1 < h1, inv_f, inv_b)
        return carry

    lax.fori_loop(0, n1, pad, 0, unroll=8)

    wide = MXU_COLUMNS // LANES

    def stage1(i, carry):
        t2s = [i * wide + j for j in range(wide)]
        kt = jnp.concatenate([kp_ref[pl.ds(t2, n1, stride=T_PITCH), :] for t2 in t2s], axis=1)
        a = _bdot(m1_ref[...], kt.astype(BF16))
        for j, t2 in enumerate(t2s):
            cols = slice(j * LANES, (j + 1) * LANES)
            a_ref[pl.ds(t2, h1, stride=A_PITCH), :] = a[:h1, cols]
            a_ref[pl.ds(t2 + n2, h1, stride=A_PITCH), :] = a[h1:, cols]
        return carry

    lax.fori_loop(0, n2 // wide, stage1, 0, unroll=4)

    def stage2(f1, carry):
        src = pl.multiple_of(f1 * A_PITCH, PITCH_ALIGN)
        dst = pl.multiple_of(f1 * 2 * n2, 2 * n2)
        spec = _bdot(m2_ref[f1], a_ref[pl.ds(src, 2 * n2), :].astype(BF16))
        o_ref[pl.ds(dst, 2 * n2), :] = spec.astype(o_ref.dtype)
        return carry

    lax.fori_loop(0, h1, stage2, 0, unroll=16)


def _filter_spectra(seq, p):
    n = 2 * seq
    tb = _fft_tables(seq)
    nblk = seq // FILT_ROWS
    half = FILT_ROWS // 2
    pack = FILT_PACK

    def packed(v):
        return jnp.transpose(v.reshape(2 * nblk, 2, half), (0, 2, 1)).reshape(n // 2, 2, 1)

    fwd_pos = jnp.arange(seq, dtype=jnp.int32)
    pos = jnp.concatenate([fwd_pos, (seq - fwd_pos) % seq])
    t_fwd = jnp.linspace(0.0, 1.0, seq, dtype=F32)
    t = packed(jnp.concatenate([t_fwd, jnp.roll(t_fwd[::-1], 1)]))
    w = packed(2.0 * math.pi * pos.astype(F32) / seq)
    fb = jnp.linspace(1e-4, FILT_BANDS - 1, FILT_BANDS, dtype=F32)[None, None, :]
    zero = jnp.zeros((n // 2, 2, pack - FILT_EMB), F32)
    zz = jnp.concatenate([t, jnp.cos(fb * w), -jnp.sin(fb * w), zero], axis=-1).reshape(n // 2, LANES)

    def padw(a):
        a = jnp.pad(a, ((0, pack - a.shape[0]), (0, pack - a.shape[1])))
        zero = jnp.zeros_like(a)
        return jnp.concatenate([jnp.concatenate([a, zero], axis=1),
                                jnp.concatenate([zero, a], axis=1)], axis=0)

    def padv(a):
        return jnp.tile(jnp.pad(a, (0, pack - a.shape[0])), 2)[None, :]

    w1 = padw(p['filt_w1'])
    w2 = padw(p['filt_w2'])
    w3 = padw(p['filt_w3'])
    w4 = jnp.pad(p['filt_w4'], ((0, pack - FILT_HIDDEN), (0, 0))).reshape(pack, HYENA_ORDER, 2, D_HYENA)
    w4 = jnp.transpose(w4, (2, 0, 1, 3)).reshape(2, pack, HYENA_ORDER * D_HYENA)
    w4 = jnp.stack([jnp.pad(w4, ((0, 0), (0, pack), (0, 0))),
                    jnp.pad(w4, ((0, 0), (pack, 0), (0, 0)))], axis=1)
    min_decay = math.log(HYENA_TARGET) / SLOW_DECAY_PCT
    max_decay = math.log(HYENA_TARGET) / FAST_DECAY_PCT
    deltas = jnp.linspace(min_decay, max_decay, D_HYENA, dtype=F32)
    deltas = jnp.tile(deltas, HYENA_ORDER)[None, :]
    width = HYENA_ORDER * D_HYENA
    sq = _const_spec((LANES, LANES))
    vec = _const_spec((1, LANES))
    filt, sums = pl.pallas_call(
        functools.partial(_filter_kernel, nblk=nblk),
        grid=(2 * nblk,),
        in_specs=[pl.BlockSpec((half, LANES), lambda i: (i, 0)),
                  sq, vec, sq, vec, sq, vec, vec,
                  pl.BlockSpec((1, 2, LANES, width), lambda i: (i // nblk, 0, 0, 0)),
                  _const_spec((1, width))],
        out_specs=[pl.BlockSpec((FILT_ROWS, width), lambda i: (i, 0)),
                   pl.BlockSpec((1, 1, width), lambda i: (i // nblk, 0, 0))],
        out_shape=[jax.ShapeDtypeStruct((n, width), F32),
                   jax.ShapeDtypeStruct((2, 1, width), F32)],
        compiler_params=pltpu.CompilerParams(
            dimension_semantics=("arbitrary",), vmem_limit_bytes=VMEM_LIMIT),
        name="hyena_filter",
    )(zz, w1, padv(p['filt_b1']), w2, padv(p['filt_b2']), w3, padv(p['filt_b3']),
      padv(p['filt_freq']), w4, deltas)
    n1, h1 = tb['n1'], tb['h1']
    return pl.pallas_call(
        functools.partial(_kspec_kernel, n1=n1),
        grid=(width // LANES,),
        in_specs=[pl.BlockSpec((n, LANES), lambda j: (0, j)),
                  pl.BlockSpec((2, 1, LANES), lambda j: (0, 0, j)),
                  _const_spec((n1, n1)), _const_spec((h1, 2 * FFT_N2, 2 * FFT_N2))],
        out_specs=pl.BlockSpec((n, LANES), lambda j: (0, j)),
        out_shape=jax.ShapeDtypeStruct((n, width), BF16),
        scratch_shapes=[pltpu.VMEM((h1 * A_PITCH, LANES), F32),
                        pltpu.VMEM((n1 * T_PITCH, LANES), F32)],
        compiler_params=pltpu.CompilerParams(
            dimension_semantics=("parallel",), vmem_limit_bytes=VMEM_LIMIT),
        name="hyena_filter_spectrum",
    )(filt, sums, _bf16_const(tb['m1_full']), _bf16_const(tb['m2']))


CHUNK_UNROLL = 8


def _for_chunks(count, body):
    body(0, True, False)

    def step(t1, carry):
        body(t1, False, False)
        return carry

    lax.fori_loop(1, count - CHUNK_UNROLL + 1, step, 0, unroll=CHUNK_UNROLL)
    for t1 in range(count - CHUNK_UNROLL + 1, count):
        body(t1, False, t1 == count - 1)


def _short_conv_rows(ref, w_ref, b_ref, r0, nrows, first, last):
    x = ref[pl.ds(r0, nrows), :]
    rid = lax.broadcasted_iota(jnp.int32, x.shape, 0)
    if first:
        prev = jnp.where(rid == 0, 0.0, pltpu.roll(x, 1, axis=0))
    else:
        prev = ref[pl.ds(r0 - 1, nrows), :]
    if last:
        nxt = jnp.where(rid == nrows - 1, 0.0, pltpu.roll(x, nrows - 1, axis=0))
    else:
        nxt = ref[pl.ds(r0 + 1, nrows), :]
    return prev * w_ref[0:1, :] + x * w_ref[1:2, :] + nxt * w_ref[2:3, :] + b_ref[...]


def _long_conv_kernel(*refs, n1, conv_z, conv_g):
    refs = list(refs)
    a_ref, tp_ref = refs[-2:]
    slabs = a_ref.shape[0]

    def take_operand(conv):
        return [tuple(refs.pop(0) for _ in range(3 if conv else 1)) for _ in range(slabs)]

    z_ops = take_operand(conv_z)
    g_ops = take_operand(conv_g)
    k_ref, bias_ref, m1_ref, m1i_ref, m2_ref, m2i_ref, o_ref = refs[:-2]
    n2 = FFT_N2
    h1 = n1 // 2

    def put(ref, idx, x):
        for s in range(slabs):
            ref[s, idx, :] = x[:, s * LANES:(s + 1) * LANES]

    def get(ref, idx):
        return jnp.concatenate([ref[s, idx, :] for s in range(slabs)], axis=1)

    def rows_of(ops, conv, t1, first, last):
        r0 = t1 * n2 if isinstance(t1, int) else pl.multiple_of(t1 * n2, n2)
        if conv:
            parts = [_short_conv_rows(ref, w_ref, b_ref, r0, n2, first, last) for ref, w_ref, b_ref in ops]
        else:
            parts = [ref[pl.ds(r0, n2), :] for ref, in ops]
        return jnp.concatenate(parts, axis=1)

    def z_rows(t1, first, last):
        return rows_of(z_ops, conv_z, t1, first, last)

    def g_rows(t1, first, last):
        return rows_of(g_ops, conv_g, t1, first, last)

    def pitched(t1):
        r0 = t1 * T_PITCH
        return pl.ds(r0 if isinstance(t1, int) else pl.multiple_of(r0, PITCH_ALIGN), n2)

    def pad(t1, first, last):
        put(tp_ref, pitched(t1), z_rows(t1, first, last))

    _for_chunks(h1, pad)

    wide = max(1, MXU_COLUMNS // (slabs * LANES))
    width = slabs * LANES

    def stage1(i, carry):
        t2s = [i * wide + j for j in range(wide)]
        zt = jnp.concatenate([get(tp_ref, pl.ds(t2, h1, stride=T_PITCH)) for t2 in t2s], axis=1)
        a = _bdot(m1_ref[...], zt.astype(BF16))
        for j, t2 in enumerate(t2s):
            cols = slice(j * width, (j + 1) * width)
            put(a_ref, pl.ds(t2, h1, stride=A_PITCH), a[:h1, cols])
            put(a_ref, pl.ds(t2 + n2, h1, stride=A_PITCH), a[h1:, cols])
        return carry

    lax.fori_loop(0, n2 // wide, stage1, 0, unroll=4)

    def stage2(f1, carry):
        ra = pl.ds(pl.multiple_of(f1 * A_PITCH, PITCH_ALIGN), 2 * n2)
        rk = pl.multiple_of(f1 * 2 * n2, 2 * n2)
        x = _bdot(m2_ref[f1], get(a_ref, ra).astype(BF16))
        kk = k_ref[pl.ds(rk, 2 * n2), :].astype(F32)
        xr, xi, kr, ki = x[:n2], x[n2:], kk[:n2], kk[n2:]
        y = jnp.concatenate([xr * kr - xi * ki, xr * ki + xi * kr], axis=0)
        put(a_ref, ra, _bdot(m2i_ref[f1], y.astype(BF16)))
        return carry

    lax.fori_loop(0, h1, stage2, 0, unroll=16)

    def stage3(i, carry):
        t2s = [i * wide + j for j in range(wide)]
        br = jnp.concatenate([get(a_ref, pl.ds(t2, h1, stride=A_PITCH)) for t2 in t2s], axis=1)
        bi = jnp.concatenate([get(a_ref, pl.ds(t2 + n2, h1, stride=A_PITCH)) for t2 in t2s], axis=1)
        y = _bdot(m1i_ref[...], jnp.concatenate([br, bi], axis=0).astype(BF16))
        for j, t2 in enumerate(t2s):
            put(tp_ref, pl.ds(t2, h1, stride=T_PITCH), y[:, j * width:(j + 1) * width])
        return carry

    lax.fori_loop(0, n2 // wide, stage3, 0, unroll=4)

    bias = bias_ref[...]

    def finish(t1, first, last):
        r0 = t1 * n2 if isinstance(t1, int) else pl.multiple_of(t1 * n2, n2)
        y = get(tp_ref, pitched(t1))
        out = g_rows(t1, first, last) * (y + bias * z_rows(t1, first, last))
        o_ref[pl.ds(r0, n2), :] = out.astype(o_ref.dtype)

    _for_chunks(h1, finish)


def _bf16_const(a):
    return jnp.asarray(a, F32).astype(BF16)


def _long_conv_slabs(seq):
    n1 = 2 * seq // FFT_N2
    h1 = n1 // 2
    tables = 2 * (2 * n1 * h1 + 2 * h1 * (2 * FFT_N2) ** 2)
    for slabs in (2, 1):
        io = 3 * 2 * seq * 4
        per_lane = io + 2 * 2 * seq * 2 + h1 * (A_PITCH + T_PITCH) * 4
        if per_lane * slabs * LANES + tables <= VMEM_LIMIT - VMEM_HEADROOM:
            return slabs
    raise ValueError(f"long conv of length {seq} does not fit VMEM")


def _long_conv(z, z_off, gate, g_off, kspec, k_off, bias, batch, seq, short=None, conv_z=False,
               out_dtype=F32):
    n = 2 * seq
    tb = _fft_tables(seq)
    n1, h1 = tb['n1'], tb['h1']
    conv_g = short is not None
    slabs = _long_conv_slabs(seq)
    width = slabs * LANES
    assert z_off % slabs == 0 and g_off % slabs == 0 and k_off % slabs == 0

    def operand(arr, off, conv):
        specs, args = [], []
        for s in range(slabs):
            specs.append(pl.BlockSpec((seq, LANES), lambda j, b, s=s: (b, off + j * slabs + s)))
            args.append(arr)
            if conv:
                specs += [pl.BlockSpec((3, LANES), lambda j, b, s=s: (0, off + j * slabs + s)),
                          pl.BlockSpec((1, LANES), lambda j, b, s=s: (0, off + j * slabs + s))]
                args += list(short)
        return specs, args

    z_specs, z_args = operand(z, z_off, conv_z)
    g_specs, g_args = operand(gate, g_off, conv_g)
    k_blk = k_off // slabs
    return pl.pallas_call(
        functools.partial(_long_conv_kernel, n1=n1, conv_z=conv_z, conv_g=conv_g),
        grid=(D_HYENA // width, batch),
        in_specs=z_specs + g_specs + [
                  pl.BlockSpec((n, width), lambda j, b: (0, k_blk + j)),
                  pl.BlockSpec((1, width), lambda j, b: (0, j)),
                  _const_spec((n1, h1)), _const_spec((h1, n1)),
                  _const_spec((h1, 2 * FFT_N2, 2 * FFT_N2)),
                  _const_spec((h1, 2 * FFT_N2, 2 * FFT_N2))],
        out_specs=pl.BlockSpec((seq, width), lambda j, b: (b, j)),
        out_shape=jax.ShapeDtypeStruct((batch * seq, D_HYENA), out_dtype),
        scratch_shapes=[pltpu.VMEM((slabs, h1 * A_PITCH, LANES), F32),
                        pltpu.VMEM((slabs, h1 * T_PITCH, LANES), F32)],
        compiler_params=pltpu.CompilerParams(
            dimension_semantics=("parallel", "parallel"), vmem_limit_bytes=VMEM_LIMIT),
        name="hyena_long_conv",
    )(*z_args, *g_args, kspec, bias, _bf16_const(tb['m1']), _bf16_const(tb['m1_inv']),
      _bf16_const(tb['m2']), _bf16_const(tb['m2_inv']))


def _layer(x3, p):
    batch, seq, _ = x3.shape
    x = x3.reshape(batch * seq, D_MODEL)
    x = _ffn(x, p['ffn1_pre_g'], p['ffn1_w1'], p['ffn1_w3'], p['ffn1_w2'], p['ffn1_post_g'])
    hy, qkv, gate = _inproj(x, p['mix_pre_g'], p['w_in'], seq)
    kspec = _filter_spectra(seq, p)
    tiles = D_HYENA // LANES
    short = (p['short_w'], p['short_b'])
    z1 = _long_conv(hy, 0, hy, tiles, kspec, 0, p['hyena_bias'][0:1], batch, seq, short=short, conv_z=True)
    yh = _long_conv(z1, 0, hy, 2 * tiles, kspec, tiles, p['hyena_bias'][1:2], batch, seq, short=short,
                    out_dtype=BF16)
    yr = _retention(qkv, gate, p['ret_log_decay_f'], p['ret_log_decay_b'], batch, seq)
    mix = (yh, yr, p['w_out'], p['mix_post_g'])
    x = _ffn(x, p['ffn2_pre_g'], p['ffn2_w1'], p['ffn2_w3'], p['ffn2_w2'], p['ffn2_post_g'], mix=mix)
    return x.reshape(batch, seq, D_MODEL)


_MATRIX_PARAMS = ('ffn1_w1', 'ffn1_w3', 'ffn1_w2', 'w_in', 'w_out', 'ffn2_w1', 'ffn2_w3', 'ffn2_w2')
_GAIN_PARAMS = ('ffn1_pre_g', 'ffn1_post_g', 'mix_pre_g', 'mix_post_g', 'ffn2_pre_g', 'ffn2_post_g')


def kernel(x_prompt, x_sample, ffn1_pre_g, ffn1_w1, ffn1_w3, ffn1_w2, ffn1_post_g, mix_pre_g, w_in, short_w, short_b, filt_w1, filt_b1, filt_w2, filt_b2, filt_w3, filt_b3, filt_w4, filt_freq, hyena_bias, ret_log_decay_f, ret_log_decay_b, w_out, mix_post_g, ffn2_pre_g, ffn2_w1, ffn2_w3, ffn2_w2, ffn2_post_g):
    params = dict(ffn1_pre_g=ffn1_pre_g, ffn1_w1=ffn1_w1, ffn1_w3=ffn1_w3, ffn1_w2=ffn1_w2,
                  ffn1_post_g=ffn1_post_g, mix_pre_g=mix_pre_g, w_in=w_in, short_w=short_w,
                  short_b=short_b, filt_w1=filt_w1, filt_b1=filt_b1, filt_w2=filt_w2,
                  filt_b2=filt_b2, filt_w3=filt_w3, filt_b3=filt_b3, filt_w4=filt_w4,
                  filt_freq=filt_freq, hyena_bias=hyena_bias, ret_log_decay_f=ret_log_decay_f,
                  ret_log_decay_b=ret_log_decay_b, w_out=w_out, mix_post_g=mix_post_g,
                  ffn2_pre_g=ffn2_pre_g, ffn2_w1=ffn2_w1, ffn2_w3=ffn2_w3, ffn2_w2=ffn2_w2,
                  ffn2_post_g=ffn2_post_g)
    depth = ffn1_w1.shape[0]

    def run(x):
        for l in range(depth):
            p = {k: v[l] for k, v in params.items()}
            for k in _MATRIX_PARAMS:
                p[k] = p[k].astype(BF16)
            for k in _GAIN_PARAMS:
                p[k] = p[k][None, :]
            p['short_b'] = p['short_b'][None, :]
            x = _layer(x, p)
        return x

    return (run(x_prompt), run(x_sample))
```

```python
import functools
import math

import numpy as np
import jax
import jax.numpy as jnp
from jax import lax
from jax.experimental import pallas as pl
from jax.experimental.pallas import tpu as pltpu

F32 = jnp.float32
BF16 = jnp.bfloat16

D_MODEL = 1024
D_HYENA = 512
D_RET = 512
HYENA_ORDER = 2
N_RET_HEADS = 4
RET_HEAD_DIM = 128
D_FF = 2816
FILT_EMB = 33
FILT_BANDS = 16
FILT_HIDDEN = 64
ROPE_BASE = 10000.0
NORM_EPS = 1e-6
HYENA_TARGET = 1e-2
FAST_DECAY_PCT = 0.3
SLOW_DECAY_PCT = 1.5
N_HY_COLS = (HYENA_ORDER + 1) * D_HYENA
D_IN = N_HY_COLS + 4 * D_RET

LANES = 128
MXU_COLUMNS = 256
VMEM_LIMIT = 56 * 1024 * 1024
VMEM_HEADROOM = 4 * 1024 * 1024
FFT_N2 = 32
RET_CHUNK = 128
RET_GROUP = 32
TOKEN_TILE = 512


def _const_spec(shape):
    nd = len(shape)
    return pl.BlockSpec(shape, lambda *_: (0,) * nd, pipeline_mode=pl.Buffered(1))


def _rms(x, g):
    ms = jnp.mean(x * x, axis=-1, keepdims=True)
    return x * lax.rsqrt(ms + NORM_EPS) * g


def _bdot(a, b):
    return jnp.dot(a, b, preferred_element_type=F32)


def _ffn_core(x, pre_ref, w1_ref, w3_ref, w2_ref, post_ref, o_ref):
    h = _rms(x, pre_ref[...]).astype(BF16)
    a = _bdot(h, w1_ref[...])
    b = _bdot(h, w3_ref[...])
    g = (a * jax.nn.sigmoid(a) * b).astype(BF16)
    y = _bdot(g, w2_ref[...])
    o_ref[...] = x + 0.5 * _rms(y, post_ref[...])


def _ffn_kernel(x_ref, pre_ref, w1_ref, w3_ref, w2_ref, post_ref, o_ref):
    _ffn_core(x_ref[...], pre_ref, w1_ref, w3_ref, w2_ref, post_ref, o_ref)


def _mix_ffn_kernel(x_ref, yh_ref, yr_ref, woh_ref, wor_ref, mg_ref,
                    pre_ref, w1_ref, w3_ref, w2_ref, post_ref, o_ref):
    y = _bdot(yh_ref[...].astype(BF16), woh_ref[...])
    y = y + _bdot(yr_ref[...].astype(BF16), wor_ref[...])
    x = x_ref[...] + _rms(y, mg_ref[...])
    _ffn_core(x, pre_ref, w1_ref, w3_ref, w2_ref, post_ref, o_ref)


def _row_spec(tm, width):
    return pl.BlockSpec((tm, width), lambda i: (i, 0))


def _ffn(x, pre_g, w1, w3, w2, post_g, mix=None):
    t = x.shape[0]
    tm = TOKEN_TILE
    ffn_specs = [_const_spec((1, D_MODEL)), _const_spec((D_MODEL, D_FF)),
                 _const_spec((D_MODEL, D_FF)), _const_spec((D_FF, D_MODEL)),
                 _const_spec((1, D_MODEL))]
    ffn_args = (pre_g, w1, w3, w2, post_g)
    if mix is None:
        body, args = _ffn_kernel, (x,) + ffn_args
        specs = [_row_spec(tm, D_MODEL)] + ffn_specs
    else:
        yh, yr, w_out, mg = mix
        body, args = _mix_ffn_kernel, (x, yh, yr, w_out, w_out, mg) + ffn_args
        specs = [_row_spec(tm, D_MODEL), _row_spec(tm, D_HYENA), _row_spec(tm, D_RET),
                 pl.BlockSpec((D_HYENA, D_MODEL), lambda i: (0, 0), pipeline_mode=pl.Buffered(1)),
                 pl.BlockSpec((D_RET, D_MODEL), lambda i: (D_HYENA // D_RET, 0),
                              pipeline_mode=pl.Buffered(1)),
                 _const_spec((1, D_MODEL))] + ffn_specs
    return pl.pallas_call(
        body,
        grid=(t // tm,),
        in_specs=specs,
        out_specs=_row_spec(tm, D_MODEL),
        out_shape=jax.ShapeDtypeStruct((t, D_MODEL), F32),
        compiler_params=pltpu.CompilerParams(
            dimension_semantics=("parallel",), vmem_limit_bytes=VMEM_LIMIT),
        name="ffn_mix" if mix is not None else "ffn",
    )(*args)


def _inproj_kernel(x_ref, g_ref, w_ref, cc_ref, ss_ref, hy_ref, qkv_ref, gate_ref):
    h = _rms(x_ref[...], g_ref[...]).astype(BF16)
    qk0 = N_HY_COLS
    v0 = qk0 + 2 * D_RET
    g0 = v0 + D_RET
    hy_ref[...] = _bdot(h, w_ref[:, :qk0])
    gate_ref[...] = _bdot(h, w_ref[:, g0:])
    qkv_ref[:, 2 * D_RET:] = _bdot(h, w_ref[:, v0:g0]).astype(BF16)
    qk = _bdot(h, w_ref[:, qk0:v0])
    cc = cc_ref[...]
    ss = ss_ref[...]
    d = RET_HEAD_DIM
    for blk in range(2 * N_RET_HEADS):
        x = qk[:, blk * d:(blk + 1) * d]
        r = x * cc + pltpu.roll(x, d // 2, axis=1) * ss
        if blk >= N_RET_HEADS:
            r = r * (d ** -0.5)
        qkv_ref[:, blk * d:(blk + 1) * d] = r.astype(BF16)


def _inproj(x, g, w, seq):
    t = x.shape[0]
    tm = TOKEN_TILE
    cc, ss = _rope_tables(seq)
    pos_blocks = seq // tm
    rope_spec = pl.BlockSpec((tm, RET_HEAD_DIM), lambda i: (i % pos_blocks, 0))
    return pl.pallas_call(
        _inproj_kernel,
        grid=(t // tm,),
        in_specs=[_row_spec(tm, D_MODEL), _const_spec((1, D_MODEL)),
                  _const_spec((D_MODEL, D_IN)), rope_spec, rope_spec],
        out_specs=[_row_spec(tm, N_HY_COLS), _row_spec(tm, 3 * D_RET), _row_spec(tm, D_RET)],
        out_shape=[jax.ShapeDtypeStruct((t, N_HY_COLS), F32),
                   jax.ShapeDtypeStruct((t, 3 * D_RET), BF16),
                   jax.ShapeDtypeStruct((t, D_RET), F32)],
        compiler_params=pltpu.CompilerParams(
            dimension_semantics=("parallel",), vmem_limit_bytes=VMEM_LIMIT),
        name="inproj",
    )(x, g, w, cc, ss)


def _ret_kernel(lgf_ref, lgb_ref, q_ref, k_ref, v_ref, g_ref, o_ref, kt_ref, *, seq):
    c = RET_CHUNK
    d = RET_HEAD_DIM
    n_chunks = seq // c
    head = pl.program_id(1)
    lgf = jnp.full((c, d), lgf_ref[head], F32)
    lgb = jnp.full((c, d), lgb_ref[head], F32)
    row = lax.broadcasted_iota(jnp.int32, (c, d), 0).astype(F32)
    col = lax.broadcasted_iota(jnp.int32, (c, d), 1).astype(F32)
    diff = row - col
    dmat = jnp.where(diff >= 0.0, jnp.exp(jnp.maximum(diff, 0.0) * lgf),
                     jnp.exp(jnp.maximum(-diff, 0.0) * lgb))
    wq_f = jnp.exp((row + 1.0) * lgf)
    wk_f = jnp.exp((c - 1.0 - row) * lgf)
    wq_b = jnp.exp((c - row) * lgb)
    wk_b = jnp.exp(row * lgb)
    gc_f = jnp.exp(c * lgf)
    gc_b = jnp.exp(c * lgb)

    grp = RET_GROUP
    n_groups = n_chunks // grp

    def fwd(gi, state):
        rows = [pl.ds(pl.multiple_of((gi * grp + j) * c, c), c) for j in range(grp)]
        qbs, kts, vs = [], [], []
        for r in rows:
            ktb = k_ref[r, :].astype(F32).T.astype(BF16)
            kt_ref[r, :] = ktb
            qbs.append(q_ref[r, :])
            kts.append(ktb)
            vs.append(v_ref[r, :])
        scores = [_bdot(qb, ktb) for qb, ktb in zip(qbs, kts)]
        kvs = [_bdot(ktb, (v.astype(F32) * wk_f).astype(BF16)) for ktb, v in zip(kts, vs)]
        states = []
        for kv in kvs:
            states.append(state)
            state = state * gc_f + kv
        intras = [_bdot((s * dmat).astype(BF16), v) for s, v in zip(scores, vs)]
        crosses = [_bdot(qb, st.astype(BF16)) for qb, st in zip(qbs, states)]
        for r, intra, cross in zip(rows, intras, crosses):
            o_ref[r, :] = intra + wq_f * cross
        return state

    lax.fori_loop(0, n_groups, fwd, jnp.zeros((d, d), F32))

    def bwd(gi, state):
        rows = [pl.ds(pl.multiple_of((n_chunks - 1 - gi * grp - j) * c, c), c) for j in range(grp)]
        kvs = [_bdot(kt_ref[r, :], (v_ref[r, :].astype(F32) * wk_b).astype(BF16)) for r in rows]
        states = []
        for kv in kvs:
            states.append(state)
            state = state * gc_b + kv
        crosses = [_bdot(q_ref[r, :], st.astype(BF16)) for r, st in zip(rows, states)]
        outs = [o_ref[r, :] + wq_b * cross for r, cross in zip(rows, crosses)]
        means = [jnp.mean(o * o, axis=-1, keepdims=True) for o in outs]
        norms = [lax.rsqrt(m + NORM_EPS) for m in means]
        gates = [g_ref[r, :] for r in rows]
        gates = [g * jax.nn.sigmoid(g) for g in gates]
        for r, o, nrm, g in zip(rows, outs, norms, gates):
            o_ref[r, :] = g * (o * nrm)
        return state

    lax.fori_loop(0, n_groups, bwd, jnp.zeros((d, d), F32))


def _rope_tables(seq):
    d = RET_HEAD_DIM
    inv = 1.0 / (ROPE_BASE ** (jnp.arange(0, d, 2, dtype=F32) / d))
    ang = jnp.arange(seq, dtype=F32)[:, None] * inv[None, :]
    c, s = jnp.cos(ang), jnp.sin(ang)
    return jnp.concatenate([c, c], axis=-1), jnp.concatenate([-s, s], axis=-1)


def _retention(qkv, gate, lg_f, lg_b, batch, seq):
    assert RET_CHUNK == RET_HEAD_DIM
    heads = N_RET_HEADS

    def col(off):
        return pl.BlockSpec((seq, LANES), lambda b, h, *_: (b, off * heads + h))

    grid_spec = pltpu.PrefetchScalarGridSpec(
        num_scalar_prefetch=2,
        grid=(batch, heads),
        in_specs=[col(0), col(1), col(2), col(0)],
        out_specs=pl.BlockSpec((seq, LANES), lambda b, h, *_: (b, h)),
        scratch_shapes=[pltpu.VMEM((seq, RET_CHUNK), BF16)],
    )
    return pl.pallas_call(
        functools.partial(_ret_kernel, seq=seq),
        grid_spec=grid_spec,
        out_shape=jax.ShapeDtypeStruct((batch * seq, D_RET), F32),
        compiler_params=pltpu.CompilerParams(
            dimension_semantics=("parallel", "parallel"), vmem_limit_bytes=VMEM_LIMIT),
        name="retention",
    )(lg_f, lg_b, qkv, qkv, qkv, gate)


def _split(x):
    hi = x.astype(BF16)
    return hi, (x - hi.astype(F32)).astype(BF16)


def _dot3(a, b):
    a_hi, a_lo = a
    b_hi, b_lo = b
    return _bdot(a_hi, b_hi) + _bdot(a_hi, b_lo) + _bdot(a_lo, b_hi)


@functools.lru_cache(maxsize=None)
def _fft_tables(seq):
    n = 2 * seq
    n2 = FFT_N2
    n1 = n // n2
    h1 = n1 // 2
    f1 = np.arange(h1, dtype=np.float64) + 0.5
    th = 2.0 * np.pi * f1[:, None] * np.arange(n1, dtype=np.float64)[None, :] / n1
    m1_full = np.concatenate([np.cos(th), -np.sin(th)], axis=0)
    m1 = m1_full[:, :h1]
    m1_inv = (2.0 / n) * m1.T
    t2 = np.arange(n2, dtype=np.float64)
    phi = 2.0 * np.pi * (np.arange(n2, dtype=np.float64)[None, :, None] * t2[None, None, :] / n2
                         + f1[:, None, None] * t2[None, None, :] / n)
    gr, gi = np.cos(phi), -np.sin(phi)
    m2 = np.concatenate([np.concatenate([gr, -gi], axis=2),
                         np.concatenate([gi, gr], axis=2)], axis=1)
    m2_inv = np.transpose(m2, (0, 2, 1))
    return dict(n1=n1, h1=h1, m1_full=m1_full, m1=m1, m1_inv=m1_inv, m2=m2, m2_inv=m2_inv)


FILT_ROWS = 512
FILT_BLOCKS = 4
FILT_PACK = LANES // 2
PITCH_ALIGN = 4
A_PITCH = 2 * FFT_N2 + PITCH_ALIGN
T_PITCH = FFT_N2 + PITCH_ALIGN


def _filter_kernel(z_ref, w1_ref, b1_ref, w2_ref, b2_ref, w3_ref, b3_ref, fr_ref,
                   w4_ref, dl_ref, o_ref, sum_ref, *, nsteps):
    i = pl.program_id(0)
    half = FILT_ROWS // 2
    fr = fr_ref[...]
    zs = [z_ref[b * half:(b + 1) * half, :] for b in range(FILT_BLOCKS)]

    def layer(xs, w_ref, b_ref):
        w = _split(w_ref[...])
        pre = [_dot3(_split(x), w) for x in xs]
        return [jnp.sin(fr * (v + b_ref[...])) for v in pre]

    hs = layer(layer(layer(zs, w1_ref, b1_ref), w2_ref, b2_ref), w3_ref, b3_ref)
    hs = [_split(h) for h in hs]
    adl = jnp.abs(dl_ref[...])
    backward = i >= nsteps
    for side in range(2):
        lane = side * FILT_PACK
        ts = [jnp.broadcast_to(z[:, lane:lane + 1], z.shape) for z in zs]
        for c in range(HYENA_ORDER * D_HYENA // LANES):
            cols = slice(c * LANES, (c + 1) * LANES)
            w4 = _split(w4_ref[0, side, :, cols])
            hcs = [_dot3(h, w4) * jnp.exp(-t * adl[:, cols]) for h, t in zip(hs, ts)]
            part = sum(jnp.sum(jnp.abs(hc), axis=0, keepdims=True) for hc in hcs)

            if side == 0:
                @pl.when(i % nsteps == 0)
                def _():
                    sum_ref[0, :, cols] = part

                @pl.when(i % nsteps != 0)
                def _():
                    sum_ref[0, :, cols] = sum_ref[0, :, cols] + part
            else:
                sum_ref[0, :, cols] = sum_ref[0, :, cols] + part

            for b, hc in enumerate(hcs):
                if side == 0 and b == 0:
                    rows = lax.broadcasted_iota(jnp.int32, hc.shape, 0)
                    flipped = jnp.where(jnp.logical_and(rows == 0, i == nsteps), 0.0, -hc)
                else:
                    flipped = -hc
                r0 = b * FILT_ROWS + side * half
                o_ref[r0:r0 + half, cols] = jnp.where(backward, flipped, hc)


def _kspec_kernel(k_ref, s_ref, m1_ref, m2_ref, o_ref, a_ref, kp_ref, *, n1):
    n2 = FFT_N2
    h1 = n1 // 2
    inv_f = 1.0 / s_ref[0]
    inv_b = 1.0 / s_ref[1]

    def pad(t1, carry):
        src = pl.multiple_of(t1 * n2, n2)
        dst = pl.multiple_of(t1 * T_PITCH, PITCH_ALIGN)
        kp_ref[pl.ds(dst, n2), :] = k_ref[pl.ds(src, n2), :] * jnp.where(t1 < h1, inv_f, inv_b)
        return carry

    lax.fori_loop(0, n1, pad, 0, unroll=8)

    wide = MXU_COLUMNS // LANES

    def stage1(i, carry):
        t2s = [i * wide + j for j in range(wide)]
        kt = jnp.concatenate([kp_ref[pl.ds(t2, n1, stride=T_PITCH), :] for t2 in t2s], axis=1)
        a = _bdot(m1_ref[...], kt.astype(BF16))
        for j, t2 in enumerate(t2s):
            cols = slice(j * LANES, (j + 1) * LANES)
            a_ref[pl.ds(t2, h1, stride=A_PITCH), :] = a[:h1, cols]
            a_ref[pl.ds(t2 + n2, h1, stride=A_PITCH), :] = a[h1:, cols]
        return carry

    lax.fori_loop(0, n2 // wide, stage1, 0, unroll=4)

    def stage2(f1, carry):
        src = pl.multiple_of(f1 * A_PITCH, PITCH_ALIGN)
        dst = pl.multiple_of(f1 * 2 * n2, 2 * n2)
        spec = _bdot(m2_ref[f1], a_ref[pl.ds(src, 2 * n2), :].astype(BF16))
        o_ref[pl.ds(dst, 2 * n2), :] = spec.astype(o_ref.dtype)
        return carry

    lax.fori_loop(0, h1, stage2, 0, unroll=16)


def _filter_spectra(seq, p):
    n = 2 * seq
    tb = _fft_tables(seq)
    nblk = seq // FILT_ROWS
    half = FILT_ROWS // 2
    pack = FILT_PACK

    def packed(v):
        return jnp.transpose(v.reshape(2 * nblk, 2, half), (0, 2, 1)).reshape(n // 2, 2, 1)

    fwd_pos = jnp.arange(seq, dtype=jnp.int32)
    pos = jnp.concatenate([fwd_pos, (seq - fwd_pos) % seq])
    t_fwd = jnp.linspace(0.0, 1.0, seq, dtype=F32)
    t = packed(jnp.concatenate([t_fwd, jnp.roll(t_fwd[::-1], 1)]))
    w = packed(2.0 * math.pi * pos.astype(F32) / seq)
    fb = jnp.linspace(1e-4, FILT_BANDS - 1, FILT_BANDS, dtype=F32)[None, None, :]
    zero = jnp.zeros((n // 2, 2, pack - FILT_EMB), F32)
    zz = jnp.concatenate([t, jnp.cos(fb * w), -jnp.sin(fb * w), zero], axis=-1).reshape(n // 2, LANES)

    def padw(a):
        a = jnp.pad(a, ((0, pack - a.shape[0]), (0, pack - a.shape[1])))
        zero = jnp.zeros_like(a)
        return jnp.concatenate([jnp.concatenate([a, zero], axis=1),
                                jnp.concatenate([zero, a], axis=1)], axis=0)

    def padv(a):
        return jnp.tile(jnp.pad(a, (0, pack - a.shape[0])), 2)[None, :]

    w1 = padw(p['filt_w1'])
    w2 = padw(p['filt_w2'])
    w3 = padw(p['filt_w3'])
    w4 = jnp.pad(p['filt_w4'], ((0, pack - FILT_HIDDEN), (0, 0))).reshape(pack, HYENA_ORDER, 2, D_HYENA)
    w4 = jnp.transpose(w4, (2, 0, 1, 3)).reshape(2, pack, HYENA_ORDER * D_HYENA)
    w4 = jnp.stack([jnp.pad(w4, ((0, 0), (0, pack), (0, 0))),
                    jnp.pad(w4, ((0, 0), (pack, 0), (0, 0)))], axis=1)
    min_decay = math.log(HYENA_TARGET) / SLOW_DECAY_PCT
    max_decay = math.log(HYENA_TARGET) / FAST_DECAY_PCT
    deltas = jnp.linspace(min_decay, max_decay, D_HYENA, dtype=F32)
    deltas = jnp.tile(deltas, HYENA_ORDER)[None, :]
    width = HYENA_ORDER * D_HYENA
    sq = _const_spec((LANES, LANES))
    vec = _const_spec((1, LANES))
    nsteps = nblk // FILT_BLOCKS
    filt, sums = pl.pallas_call(
        functools.partial(_filter_kernel, nsteps=nsteps),
        grid=(2 * nsteps,),
        in_specs=[pl.BlockSpec((FILT_BLOCKS * half, LANES), lambda i: (i, 0)),
                  sq, vec, sq, vec, sq, vec, vec,
                  pl.BlockSpec((1, 2, LANES, width), lambda i: (i // nsteps, 0, 0, 0)),
                  _const_spec((1, width))],
        out_specs=[pl.BlockSpec((FILT_BLOCKS * FILT_ROWS, width), lambda i: (i, 0)),
                   pl.BlockSpec((1, 1, width), lambda i: (i // nsteps, 0, 0))],
        out_shape=[jax.ShapeDtypeStruct((n, width), F32),
                   jax.ShapeDtypeStruct((2, 1, width), F32)],
        compiler_params=pltpu.CompilerParams(
            dimension_semantics=("arbitrary",), vmem_limit_bytes=VMEM_LIMIT),
        name="hyena_filter",
    )(zz, w1, padv(p['filt_b1']), w2, padv(p['filt_b2']), w3, padv(p['filt_b3']),
      padv(p['filt_freq']), w4, deltas)
    n1, h1 = tb['n1'], tb['h1']
    return pl.pallas_call(
        functools.partial(_kspec_kernel, n1=n1),
        grid=(width // LANES,),
        in_specs=[pl.BlockSpec((n, LANES), lambda j: (0, j)),
                  pl.BlockSpec((2, 1, LANES), lambda j: (0, 0, j)),
                  _const_spec((n1, n1)), _const_spec((h1, 2 * FFT_N2, 2 * FFT_N2))],
        out_specs=pl.BlockSpec((n, LANES), lambda j: (0, j)),
        out_shape=jax.ShapeDtypeStruct((n, width), BF16),
        scratch_shapes=[pltpu.VMEM((h1 * A_PITCH, LANES), F32),
                        pltpu.VMEM((n1 * T_PITCH, LANES), F32)],
        compiler_params=pltpu.CompilerParams(
            dimension_semantics=("parallel",), vmem_limit_bytes=VMEM_LIMIT),
        name="hyena_filter_spectrum",
    )(filt, sums, _bf16_const(tb['m1_full']), _bf16_const(tb['m2']))


CHUNK_UNROLL = 8


def _for_chunks(count, body):
    body(0, True, False)

    def step(t1, carry):
        body(t1, False, False)
        return carry

    lax.fori_loop(1, count - CHUNK_UNROLL + 1, step, 0, unroll=CHUNK_UNROLL)
    for t1 in range(count - CHUNK_UNROLL + 1, count):
        body(t1, False, t1 == count - 1)


def _short_conv_rows(ref, w_ref, b_ref, r0, nrows, first, last):
    x = ref[pl.ds(r0, nrows), :]
    rid = lax.broadcasted_iota(jnp.int32, x.shape, 0)
    if first:
        prev = jnp.where(rid == 0, 0.0, pltpu.roll(x, 1, axis=0))
    else:
        prev = ref[pl.ds(r0 - 1, nrows), :]
    if last:
        nxt = jnp.where(rid == nrows - 1, 0.0, pltpu.roll(x, nrows - 1, axis=0))
    else:
        nxt = ref[pl.ds(r0 + 1, nrows), :]
    return prev * w_ref[0:1, :] + x * w_ref[1:2, :] + nxt * w_ref[2:3, :] + b_ref[...]


def _long_conv_kernel(*refs, n1, conv_z, conv_g):
    refs = list(refs)
    a_ref, tp_ref = refs[-2:]
    slabs = a_ref.shape[0]

    def take_operand(conv):
        return [tuple(refs.pop(0) for _ in range(3 if conv else 1)) for _ in range(slabs)]

    z_ops = take_operand(conv_z)
    g_ops = take_operand(conv_g)
    k_ref, bias_ref, m1_ref, m1i_ref, m2_ref, m2i_ref, o_ref = refs[:-2]
    n2 = FFT_N2
    h1 = n1 // 2

    def put(ref, idx, x):
        for s in range(slabs):
            ref[s, idx, :] = x[:, s * LANES:(s + 1) * LANES]

    def get(ref, idx):
        return jnp.concatenate([ref[s, idx, :] for s in range(slabs)], axis=1)

    def rows_of(ops, conv, t1, first, last):
        r0 = t1 * n2 if isinstance(t1, int) else pl.multiple_of(t1 * n2, n2)
        if conv:
            parts = [_short_conv_rows(ref, w_ref, b_ref, r0, n2, first, last) for ref, w_ref, b_ref in ops]
        else:
            parts = [ref[pl.ds(r0, n2), :] for ref, in ops]
        return jnp.concatenate(parts, axis=1)

    def z_rows(t1, first, last):
        return rows_of(z_ops, conv_z, t1, first, last)

    def g_rows(t1, first, last):
        return rows_of(g_ops, conv_g, t1, first, last)

    def pitched(t1):
        r0 = t1 * T_PITCH
        return pl.ds(r0 if isinstance(t1, int) else pl.multiple_of(r0, PITCH_ALIGN), n2)

    def pad(t1, first, last):
        put(tp_ref, pitched(t1), z_rows(t1, first, last))

    _for_chunks(h1, pad)

    wide = max(1, MXU_COLUMNS // (slabs * LANES))
    width = slabs * LANES

    def stage1(i, carry):
        t2s = [i * wide + j for j in range(wide)]
        zt = jnp.concatenate([get(tp_ref, pl.ds(t2, h1, stride=T_PITCH)) for t2 in t2s], axis=1)
        a = _bdot(m1_ref[...], zt.astype(BF16))
        for j, t2 in enumerate(t2s):
            cols = slice(j * width, (j + 1) * width)
            put(a_ref, pl.ds(t2, h1, stride=A_PITCH), a[:h1, cols])
            put(a_ref, pl.ds(t2 + n2, h1, stride=A_PITCH), a[h1:, cols])
        return carry

    lax.fori_loop(0, n2 // wide, stage1, 0, unroll=4)

    def stage2(f1, carry):
        ra = pl.ds(pl.multiple_of(f1 * A_PITCH, PITCH_ALIGN), 2 * n2)
        rk = pl.multiple_of(f1 * 2 * n2, 2 * n2)
        x = _bdot(m2_ref[f1], get(a_ref, ra).astype(BF16))
        kk = k_ref[pl.ds(rk, 2 * n2), :].astype(F32)
        xr, xi, kr, ki = x[:n2], x[n2:], kk[:n2], kk[n2:]
        y = jnp.concatenate([xr * kr - xi * ki, xr * ki + xi * kr], axis=0)
        put(a_ref, ra, _bdot(m2i_ref[f1], y.astype(BF16)))
        return carry

    lax.fori_loop(0, h1, stage2, 0, unroll=16)

    def stage3(i, carry):
        t2s = [i * wide + j for j in range(wide)]
        br = jnp.concatenate([get(a_ref, pl.ds(t2, h1, stride=A_PITCH)) for t2 in t2s], axis=1)
        bi = jnp.concatenate([get(a_ref, pl.ds(t2 + n2, h1, stride=A_PITCH)) for t2 in t2s], axis=1)
        y = _bdot(m1i_ref[...], jnp.concatenate([br, bi], axis=0).astype(BF16))
        for j, t2 in enumerate(t2s):
            put(tp_ref, pl.ds(t2, h1, stride=T_PITCH), y[:, j * width:(j + 1) * width])
        return carry

    lax.fori_loop(0, n2 // wide, stage3, 0, unroll=4)

    bias = bias_ref[...]

    def finish(t1, first, last):
        r0 = t1 * n2 if isinstance(t1, int) else pl.multiple_of(t1 * n2, n2)
        y = get(tp_ref, pitched(t1))
        out = g_rows(t1, first, last) * (y + bias * z_rows(t1, first, last))
        o_ref[pl.ds(r0, n2), :] = out.astype(o_ref.dtype)

    _for_chunks(h1, finish)


def _bf16_const(a):
    return jnp.asarray(a, F32).astype(BF16)


def _long_conv_slabs(seq):
    n1 = 2 * seq // FFT_N2
    h1 = n1 // 2
    tables = 2 * (2 * n1 * h1 + 2 * h1 * (2 * FFT_N2) ** 2)
    for slabs in (2, 1):
        io = 3 * 2 * seq * 4
        per_lane = io + 2 * 2 * seq * 2 + h1 * (A_PITCH + T_PITCH) * 4
        if per_lane * slabs * LANES + tables <= VMEM_LIMIT - VMEM_HEADROOM:
            return slabs
    raise ValueError(f"long conv of length {seq} does not fit VMEM")


def _long_conv(z, z_off, gate, g_off, kspec, k_off, bias, batch, seq, short=None, conv_z=False,
               out_dtype=F32):
    n = 2 * seq
    tb = _fft_tables(seq)
    n1, h1 = tb['n1'], tb['h1']
    conv_g = short is not None
    slabs = _long_conv_slabs(seq)
    width = slabs * LANES
    assert z_off % slabs == 0 and g_off % slabs == 0 and k_off % slabs == 0

    def operand(arr, off, conv):
        specs, args = [], []
        for s in range(slabs):
            specs.append(pl.BlockSpec((seq, LANES), lambda j, b, s=s: (b, off + j * slabs + s)))
            args.append(arr)
            if conv:
                specs += [pl.BlockSpec((3, LANES), lambda j, b, s=s: (0, off + j * slabs + s)),
                          pl.BlockSpec((1, LANES), lambda j, b, s=s: (0, off + j * slabs + s))]
                args += list(short)
        return specs, args

    z_specs, z_args = operand(z, z_off, conv_z)
    g_specs, g_args = operand(gate, g_off, conv_g)
    k_blk = k_off // slabs
    return pl.pallas_call(
        functools.partial(_long_conv_kernel, n1=n1, conv_z=conv_z, conv_g=conv_g),
        grid=(D_HYENA // width, batch),
        in_specs=z_specs + g_specs + [
                  pl.BlockSpec((n, width), lambda j, b: (0, k_blk + j)),
                  pl.BlockSpec((1, width), lambda j, b: (0, j)),
                  _const_spec((n1, h1)), _const_spec((h1, n1)),
                  _const_spec((h1, 2 * FFT_N2, 2 * FFT_N2)),
                  _const_spec((h1, 2 * FFT_N2, 2 * FFT_N2))],
        out_specs=pl.BlockSpec((seq, width), lambda j, b: (b, j)),
        out_shape=jax.ShapeDtypeStruct((batch * seq, D_HYENA), out_dtype),
        scratch_shapes=[pltpu.VMEM((slabs, h1 * A_PITCH, LANES), F32),
                        pltpu.VMEM((slabs, h1 * T_PITCH, LANES), F32)],
        compiler_params=pltpu.CompilerParams(
            dimension_semantics=("parallel", "parallel"), vmem_limit_bytes=VMEM_LIMIT),
        name="hyena_long_conv",
    )(*z_args, *g_args, kspec, bias, _bf16_const(tb['m1']), _bf16_const(tb['m1_inv']),
      _bf16_const(tb['m2']), _bf16_const(tb['m2_inv']))


def _layer(x3, p):
    batch, seq, _ = x3.shape
    x = x3.reshape(batch * seq, D_MODEL)
    x = _ffn(x, p['ffn1_pre_g'], p['ffn1_w1'], p['ffn1_w3'], p['ffn1_w2'], p['ffn1_post_g'])
    hy, qkv, gate = _inproj(x, p['mix_pre_g'], p['w_in'], seq)
    kspec = _filter_spectra(seq, p)
    tiles = D_HYENA // LANES
    short = (p['short_w'], p['short_b'])
    z1 = _long_conv(hy, 0, hy, tiles, kspec, 0, p['hyena_bias'][0:1], batch, seq, short=short, conv_z=True)
    yh = _long_conv(z1, 0, hy, 2 * tiles, kspec, tiles, p['hyena_bias'][1:2], batch, seq, short=short,
                    out_dtype=BF16)
    yr = _retention(qkv, gate, p['ret_log_decay_f'], p['ret_log_decay_b'], batch, seq)
    mix = (yh, yr, p['w_out'], p['mix_post_g'])
    x = _ffn(x, p['ffn2_pre_g'], p['ffn2_w1'], p['ffn2_w3'], p['ffn2_w2'], p['ffn2_post_g'], mix=mix)
    return x.reshape(batch, seq, D_MODEL)


_MATRIX_PARAMS = ('ffn1_w1', 'ffn1_w3', 'ffn1_w2', 'w_in', 'w_out', 'ffn2_w1', 'ffn2_w3', 'ffn2_w2')
_GAIN_PARAMS = ('ffn1_pre_g', 'ffn1_post_g', 'mix_pre_g', 'mix_post_g', 'ffn2_pre_g', 'ffn2_post_g')


def kernel(x_prompt, x_sample, ffn1_pre_g, ffn1_w1, ffn1_w3, ffn1_w2, ffn1_post_g, mix_pre_g, w_in, short_w, short_b, filt_w1, filt_b1, filt_w2, filt_b2, filt_w3, filt_b3, filt_w4, filt_freq, hyena_bias, ret_log_decay_f, ret_log_decay_b, w_out, mix_post_g, ffn2_pre_g, ffn2_w1, ffn2_w3, ffn2_w2, ffn2_post_g):
    params = dict(ffn1_pre_g=ffn1_pre_g, ffn1_w1=ffn1_w1, ffn1_w3=ffn1_w3, ffn1_w2=ffn1_w2,
                  ffn1_post_g=ffn1_post_g, mix_pre_g=mix_pre_g, w_in=w_in, short_w=short_w,
                  short_b=short_b, filt_w1=filt_w1, filt_b1=filt_b1, filt_w2=filt_w2,
                  filt_b2=filt_b2, filt_w3=filt_w3, filt_b3=filt_b3, filt_w4=filt_w4,
                  filt_freq=filt_freq, hyena_bias=hyena_bias, ret_log_decay_f=ret_log_decay_f,
                  ret_log_decay_b=ret_log_decay_b, w_out=w_out, mix_post_g=mix_post_g,
                  ffn2_pre_g=ffn2_pre_g, ffn2_w1=ffn2_w1, ffn2_w3=ffn2_w3, ffn2_w2=ffn2_w2,
                  ffn2_post_g=ffn2_post_g)
    depth = ffn1_w1.shape[0]

    def run(x):
        for l in range(depth):
            p = {k: v[l] for k, v in params.items()}
            for k in _MATRIX_PARAMS:
                p[k] = p[k].astype(BF16)
            for k in _GAIN_PARAMS:
                p[k] = p[k][None, :]
            p['short_b'] = p['short_b'][None, :]
            x = _layer(x, p)
        return x

    return (run(x_prompt), run(x_sample))
```

```python
import functools
import math

import numpy as np
import jax
import jax.numpy as jnp
from jax import lax
from jax.experimental import pallas as pl
from jax.experimental.pallas import tpu as pltpu

F32 = jnp.float32
BF16 = jnp.bfloat16

D_MODEL = 1024
D_HYENA = 512
D_RET = 512
HYENA_ORDER = 2
N_RET_HEADS = 4
RET_HEAD_DIM = 128
D_FF = 2816
FILT_EMB = 33
FILT_BANDS = 16
FILT_HIDDEN = 64
ROPE_BASE = 10000.0
NORM_EPS = 1e-6
HYENA_TARGET = 1e-2
FAST_DECAY_PCT = 0.3
SLOW_DECAY_PCT = 1.5
N_HY_COLS = (HYENA_ORDER + 1) * D_HYENA
D_IN = N_HY_COLS + 4 * D_RET

LANES = 128
MXU_COLUMNS = 256
VMEM_LIMIT = 56 * 1024 * 1024
VMEM_HEADROOM = 4 * 1024 * 1024
FFT_N2 = 32
RET_CHUNK = 128
RET_GROUP = 32
TOKEN_TILE = 512


def _const_spec(shape):
    nd = len(shape)
    return pl.BlockSpec(shape, lambda *_: (0,) * nd, pipeline_mode=pl.Buffered(1))


def _rms(x, g):
    ms = jnp.mean(x * x, axis=-1, keepdims=True)
    return x * lax.rsqrt(ms + NORM_EPS) * g


def _bdot(a, b):
    return jnp.dot(a, b, preferred_element_type=F32)


def _ffn_core(x, pre_ref, w1_ref, w3_ref, w2_ref, post_ref, o_ref):
    h = _rms(x, pre_ref[...]).astype(BF16)
    a = _bdot(h, w1_ref[...])
    b = _bdot(h, w3_ref[...])
    g = (a * jax.nn.sigmoid(a) * b).astype(BF16)
    y = _bdot(g, w2_ref[...])
    o_ref[...] = x + 0.5 * _rms(y, post_ref[...])


def _ffn_kernel(x_ref, pre_ref, w1_ref, w3_ref, w2_ref, post_ref, o_ref):
    _ffn_core(x_ref[...], pre_ref, w1_ref, w3_ref, w2_ref, post_ref, o_ref)


def _mix_ffn_kernel(x_ref, yh_ref, yr_ref, woh_ref, wor_ref, mg_ref,
                    pre_ref, w1_ref, w3_ref, w2_ref, post_ref, o_ref):
    y = _bdot(yh_ref[...].astype(BF16), woh_ref[...])
    y = y + _bdot(yr_ref[...].astype(BF16), wor_ref[...])
    x = x_ref[...] + _rms(y, mg_ref[...])
    _ffn_core(x, pre_ref, w1_ref, w3_ref, w2_ref, post_ref, o_ref)


def _row_spec(tm, width):
    return pl.BlockSpec((tm, width), lambda i: (i, 0))


def _ffn(x, pre_g, w1, w3, w2, post_g, mix=None):
    t = x.shape[0]
    tm = TOKEN_TILE
    ffn_specs = [_const_spec((1, D_MODEL)), _const_spec((D_MODEL, D_FF)),
                 _const_spec((D_MODEL, D_FF)), _const_spec((D_FF, D_MODEL)),
                 _const_spec((1, D_MODEL))]
    ffn_args = (pre_g, w1, w3, w2, post_g)
    if mix is None:
        body, args = _ffn_kernel, (x,) + ffn_args
        specs = [_row_spec(tm, D_MODEL)] + ffn_specs
    else:
        yh, yr, w_out, mg = mix
        body, args = _mix_ffn_kernel, (x, yh, yr, w_out, w_out, mg) + ffn_args
        specs = [_row_spec(tm, D_MODEL), _row_spec(tm, D_HYENA), _row_spec(tm, D_RET),
                 pl.BlockSpec((D_HYENA, D_MODEL), lambda i: (0, 0), pipeline_mode=pl.Buffered(1)),
                 pl.BlockSpec((D_RET, D_MODEL), lambda i: (D_HYENA // D_RET, 0),
                              pipeline_mode=pl.Buffered(1)),
                 _const_spec((1, D_MODEL))] + ffn_specs
    return pl.pallas_call(
        body,
        grid=(t // tm,),
        in_specs=specs,
        out_specs=_row_spec(tm, D_MODEL),
        out_shape=jax.ShapeDtypeStruct((t, D_MODEL), F32),
        compiler_params=pltpu.CompilerParams(
            dimension_semantics=("parallel",), vmem_limit_bytes=VMEM_LIMIT),
        name="ffn_mix" if mix is not None else "ffn",
    )(*args)


def _inproj_kernel(x_ref, g_ref, w_ref, cc_ref, ss_ref, hy_ref, qkv_ref, gate_ref):
    h = _rms(x_ref[...], g_ref[...]).astype(BF16)
    qk0 = N_HY_COLS
    v0 = qk0 + 2 * D_RET
    g0 = v0 + D_RET
    hy_ref[...] = _bdot(h, w_ref[:, :qk0])
    gate_ref[...] = _bdot(h, w_ref[:, g0:])
    qkv_ref[:, 2 * D_RET:] = _bdot(h, w_ref[:, v0:g0]).astype(BF16)
    qk = _bdot(h, w_ref[:, qk0:v0])
    cc = cc_ref[...]
    ss = ss_ref[...]
    d = RET_HEAD_DIM
    for blk in range(2 * N_RET_HEADS):
        x = qk[:, blk * d:(blk + 1) * d]
        r = x * cc + pltpu.roll(x, d // 2, axis=1) * ss
        if blk >= N_RET_HEADS:
            r = r * (d ** -0.5)
        qkv_ref[:, blk * d:(blk + 1) * d] = r.astype(BF16)


def _inproj(x, g, w, seq):
    t = x.shape[0]
    tm = TOKEN_TILE
    cc, ss = _rope_tables(seq)
    pos_blocks = seq // tm
    rope_spec = pl.BlockSpec((tm, RET_HEAD_DIM), lambda i: (i % pos_blocks, 0))
    return pl.pallas_call(
        _inproj_kernel,
        grid=(t // tm,),
        in_specs=[_row_spec(tm, D_MODEL), _const_spec((1, D_MODEL)),
                  _const_spec((D_MODEL, D_IN)), rope_spec, rope_spec],
        out_specs=[_row_spec(tm, N_HY_COLS), _row_spec(tm, 3 * D_RET), _row_spec(tm, D_RET)],
        out_shape=[jax.ShapeDtypeStruct((t, N_HY_COLS), F32),
                   jax.ShapeDtypeStruct((t, 3 * D_RET), BF16),
                   jax.ShapeDtypeStruct((t, D_RET), F32)],
        compiler_params=pltpu.CompilerParams(
            dimension_semantics=("parallel",), vmem_limit_bytes=VMEM_LIMIT),
        name="inproj",
    )(x, g, w, cc, ss)


def _ret_kernel(lgf_ref, lgb_ref, q_ref, k_ref, v_ref, g_ref, o_ref, kt_ref, *, seq):
    c = RET_CHUNK
    d = RET_HEAD_DIM
    n_chunks = seq // c
    head = pl.program_id(1)
    lgf = jnp.full((c, d), lgf_ref[head], F32)
    lgb = jnp.full((c, d), lgb_ref[head], F32)
    row = lax.broadcasted_iota(jnp.int32, (c, d), 0).astype(F32)
    col = lax.broadcasted_iota(jnp.int32, (c, d), 1).astype(F32)
    diff = row - col
    dmat = jnp.where(diff >= 0.0, jnp.exp(jnp.maximum(diff, 0.0) * lgf),
                     jnp.exp(jnp.maximum(-diff, 0.0) * lgb))
    wq_f = jnp.exp((row + 1.0) * lgf)
    wk_f = jnp.exp((c - 1.0 - row) * lgf)
    wq_b = jnp.exp((c - row) * lgb)
    wk_b = jnp.exp(row * lgb)
    gc_f = jnp.exp(c * lgf)
    gc_b = jnp.exp(c * lgb)

    grp = RET_GROUP
    n_groups = n_chunks // grp

    def fwd(gi, state):
        rows = [pl.ds(pl.multiple_of((gi * grp + j) * c, c), c) for j in range(grp)]
        qbs, kts, vs = [], [], []
        for r in rows:
            ktb = k_ref[r, :].astype(F32).T.astype(BF16)
            kt_ref[r, :] = ktb
            qbs.append(q_ref[r, :])
            kts.append(ktb)
            vs.append(v_ref[r, :])
        scores = [_bdot(qb, ktb) for qb, ktb in zip(qbs, kts)]
        kvs = [_bdot(ktb, (v.astype(F32) * wk_f).astype(BF16)) for ktb, v in zip(kts, vs)]
        states = []
        for kv in kvs:
            states.append(state)
            state = state * gc_f + kv
        intras = [_bdot((s * dmat).astype(BF16), v) for s, v in zip(scores, vs)]
        crosses = [_bdot(qb, st.astype(BF16)) for qb, st in zip(qbs, states)]
        for r, intra, cross in zip(rows, intras, crosses):
            o_ref[r, :] = intra + wq_f * cross
        return state

    lax.fori_loop(0, n_groups, fwd, jnp.zeros((d, d), F32))

    def bwd(gi, state):
        rows = [pl.ds(pl.multiple_of((n_chunks - 1 - gi * grp - j) * c, c), c) for j in range(grp)]
        kvs = [_bdot(kt_ref[r, :], (v_ref[r, :].astype(F32) * wk_b).astype(BF16)) for r in rows]
        states = []
        for kv in kvs:
            states.append(state)
            state = state * gc_b + kv
        crosses = [_bdot(q_ref[r, :], st.astype(BF16)) for r, st in zip(rows, states)]
        outs = [o_ref[r, :] + wq_b * cross for r, cross in zip(rows, crosses)]
        means = [jnp.mean(o * o, axis=-1, keepdims=True) for o in outs]
        norms = [lax.rsqrt(m + NORM_EPS) for m in means]
        gates = [g_ref[r, :] for r in rows]
        gates = [g * jax.nn.sigmoid(g) for g in gates]
        for r, o, nrm, g in zip(rows, outs, norms, gates):
            o_ref[r, :] = g * (o * nrm)
        return state

    lax.fori_loop(0, n_groups, bwd, jnp.zeros((d, d), F32))


def _rope_tables(seq):
    d = RET_HEAD_DIM
    inv = 1.0 / (ROPE_BASE ** (jnp.arange(0, d, 2, dtype=F32) / d))
    ang = jnp.arange(seq, dtype=F32)[:, None] * inv[None, :]
    c, s = jnp.cos(ang), jnp.sin(ang)
    return jnp.concatenate([c, c], axis=-1), jnp.concatenate([-s, s], axis=-1)


def _retention(qkv, gate, lg_f, lg_b, batch, seq):
    assert RET_CHUNK == RET_HEAD_DIM
    heads = N_RET_HEADS

    def col(off):
        return pl.BlockSpec((seq, LANES), lambda b, h, *_: (b, off * heads + h))

    grid_spec = pltpu.PrefetchScalarGridSpec(
        num_scalar_prefetch=2,
        grid=(batch, heads),
        in_specs=[col(0), col(1), col(2), col(0)],
        out_specs=pl.BlockSpec((seq, LANES), lambda b, h, *_: (b, h)),
        scratch_shapes=[pltpu.VMEM((seq, RET_CHUNK), BF16)],
    )
    return pl.pallas_call(
        functools.partial(_ret_kernel, seq=seq),
        grid_spec=grid_spec,
        out_shape=jax.ShapeDtypeStruct((batch * seq, D_RET), F32),
        compiler_params=pltpu.CompilerParams(
            dimension_semantics=("parallel", "parallel"), vmem_limit_bytes=VMEM_LIMIT),
        name="retention",
    )(lg_f, lg_b, qkv, qkv, qkv, gate)


def _split(x):
    hi = x.astype(BF16)
    return hi, (x - hi.astype(F32)).astype(BF16)


def _dot3(a, b):
    a_hi, a_lo = a
    b_hi, b_lo = b
    return _bdot(a_hi, b_hi) + _bdot(a_hi, b_lo) + _bdot(a_lo, b_hi)


@functools.lru_cache(maxsize=None)
def _fft_tables(seq):
    n = 2 * seq
    n2 = FFT_N2
    n1 = n // n2
    h1 = n1 // 2
    f1 = np.arange(h1, dtype=np.float64) + 0.5
    th = 2.0 * np.pi * f1[:, None] * np.arange(n1, dtype=np.float64)[None, :] / n1
    m1_full = np.concatenate([np.cos(th), -np.sin(th)], axis=0)
    m1 = m1_full[:, :h1]
    m1_inv = (2.0 / n) * m1.T
    t2 = np.arange(n2, dtype=np.float64)
    phi = 2.0 * np.pi * (np.arange(n2, dtype=np.float64)[None, :, None] * t2[None, None, :] / n2
                         + f1[:, None, None] * t2[None, None, :] / n)
    gr, gi = np.cos(phi), -np.sin(phi)
    m2 = np.concatenate([np.concatenate([gr, -gi], axis=2),
                         np.concatenate([gi, gr], axis=2)], axis=1)
    m2_inv = np.transpose(m2, (0, 2, 1))
    return dict(n1=n1, h1=h1, m1_full=m1_full, m1=m1, m1_inv=m1_inv, m2=m2, m2_inv=m2_inv)


OUTER_GROUP = 8
STAGE2_GROUP = 32
FILT_ROWS = 512
FILT_BLOCKS = 4
FILT_PACK = LANES // 2
PITCH_ALIGN = 4
A_PITCH = 2 * FFT_N2 + PITCH_ALIGN
T_PITCH = FFT_N2 + PITCH_ALIGN


def _filter_kernel(z_ref, w1_ref, b1_ref, w2_ref, b2_ref, w3_ref, b3_ref, fr_ref,
                   w4_ref, dl_ref, o_ref, sum_ref, *, nsteps):
    i = pl.program_id(0)
    half = FILT_ROWS // 2
    fr = fr_ref[...]
    zs = [z_ref[b * half:(b + 1) * half, :] for b in range(FILT_BLOCKS)]

    def layer(xs, w_ref, b_ref):
        w = _split(w_ref[...])
        pre = [_dot3(_split(x), w) for x in xs]
        return [jnp.sin(fr * (v + b_ref[...])) for v in pre]

    hs = layer(layer(layer(zs, w1_ref, b1_ref), w2_ref, b2_ref), w3_ref, b3_ref)
    hs = [_split(h) for h in hs]
    adl = jnp.abs(dl_ref[...])
    backward = i >= nsteps
    for side in range(2):
        lane = side * FILT_PACK
        ts = [jnp.broadcast_to(z[:, lane:lane + 1], z.shape) for z in zs]
        for c in range(HYENA_ORDER * D_HYENA // LANES):
            cols = slice(c * LANES, (c + 1) * LANES)
            w4 = _split(w4_ref[0, side, :, cols])
            hcs = [_dot3(h, w4) * jnp.exp(-t * adl[:, cols]) for h, t in zip(hs, ts)]
            part = sum(jnp.sum(jnp.abs(hc), axis=0, keepdims=True) for hc in hcs)

            if side == 0:
                @pl.when(i % nsteps == 0)
                def _():
                    sum_ref[0, :, cols] = part

                @pl.when(i % nsteps != 0)
                def _():
                    sum_ref[0, :, cols] = sum_ref[0, :, cols] + part
            else:
                sum_ref[0, :, cols] = sum_ref[0, :, cols] + part

            for b, hc in enumerate(hcs):
                if side == 0 and b == 0:
                    rows = lax.broadcasted_iota(jnp.int32, hc.shape, 0)
                    flipped = jnp.where(jnp.logical_and(rows == 0, i == nsteps), 0.0, -hc)
                else:
                    flipped = -hc
                r0 = b * FILT_ROWS + side * half
                o_ref[r0:r0 + half, cols] = jnp.where(backward, flipped, hc)


def _kspec_kernel(k_ref, s_ref, m1_ref, m2_ref, o_ref, a_ref, kp_ref, *, n1):
    n2 = FFT_N2
    h1 = n1 // 2
    inv_f = 1.0 / s_ref[0]
    inv_b = 1.0 / s_ref[1]

    def pad(t1, carry):
        src = pl.multiple_of(t1 * n2, n2)
        dst = pl.multiple_of(t1 * T_PITCH, PITCH_ALIGN)
        kp_ref[pl.ds(dst, n2), :] = k_ref[pl.ds(src, n2), :] * jnp.where(t1 < h1, inv_f, inv_b)
        return carry

    lax.fori_loop(0, n1, pad, 0, unroll=8)

    wide = MXU_COLUMNS // LANES

    def stage1(gi, carry):
        sets = [[(gi * OUTER_GROUP + u) * wide + j for j in range(wide)] for u in range(OUTER_GROUP)]
        kts = [jnp.concatenate([kp_ref[pl.ds(t2, n1, stride=T_PITCH), :] for t2 in t2s], axis=1)
               for t2s in sets]
        prods = [_bdot(m1_ref[...], kt.astype(BF16)) for kt in kts]
        for t2s, a in zip(sets, prods):
            for j, t2 in enumerate(t2s):
                cols = slice(j * LANES, (j + 1) * LANES)
                a_ref[pl.ds(t2, h1, stride=A_PITCH), :] = a[:h1, cols]
                a_ref[pl.ds(t2 + n2, h1, stride=A_PITCH), :] = a[h1:, cols]
        return carry

    lax.fori_loop(0, n2 // (wide * OUTER_GROUP), stage1, 0)

    def stage2(gi, carry):
        f1s = [gi * STAGE2_GROUP + j for j in range(STAGE2_GROUP)]
        srcs = [pl.ds(pl.multiple_of(f1 * A_PITCH, PITCH_ALIGN), 2 * n2) for f1 in f1s]
        specs = [_bdot(m2_ref[f1], a_ref[src, :].astype(BF16)) for f1, src in zip(f1s, srcs)]
        for f1, spec in zip(f1s, specs):
            dst = pl.multiple_of(f1 * 2 * n2, 2 * n2)
            o_ref[pl.ds(dst, 2 * n2), :] = spec.astype(o_ref.dtype)
        return carry

    lax.fori_loop(0, h1 // STAGE2_GROUP, stage2, 0)


def _filter_spectra(seq, p):
    n = 2 * seq
    tb = _fft_tables(seq)
    nblk = seq // FILT_ROWS
    half = FILT_ROWS // 2
    pack = FILT_PACK

    def packed(v):
        return jnp.transpose(v.reshape(2 * nblk, 2, half), (0, 2, 1)).reshape(n // 2, 2, 1)

    fwd_pos = jnp.arange(seq, dtype=jnp.int32)
    pos = jnp.concatenate([fwd_pos, (seq - fwd_pos) % seq])
    t_fwd = jnp.linspace(0.0, 1.0, seq, dtype=F32)
    t = packed(jnp.concatenate([t_fwd, jnp.roll(t_fwd[::-1], 1)]))
    w = packed(2.0 * math.pi * pos.astype(F32) / seq)
    fb = jnp.linspace(1e-4, FILT_BANDS - 1, FILT_BANDS, dtype=F32)[None, None, :]
    zero = jnp.zeros((n // 2, 2, pack - FILT_EMB), F32)
    zz = jnp.concatenate([t, jnp.cos(fb * w), -jnp.sin(fb * w), zero], axis=-1).reshape(n // 2, LANES)

    def padw(a):
        a = jnp.pad(a, ((0, pack - a.shape[0]), (0, pack - a.shape[1])))
        zero = jnp.zeros_like(a)
        return jnp.concatenate([jnp.concatenate([a, zero], axis=1),
                                jnp.concatenate([zero, a], axis=1)], axis=0)

    def padv(a):
        return jnp.tile(jnp.pad(a, (0, pack - a.shape[0])), 2)[None, :]

    w1 = padw(p['filt_w1'])
    w2 = padw(p['filt_w2'])
    w3 = padw(p['filt_w3'])
    w4 = jnp.pad(p['filt_w4'], ((0, pack - FILT_HIDDEN), (0, 0))).reshape(pack, HYENA_ORDER, 2, D_HYENA)
    w4 = jnp.transpose(w4, (2, 0, 1, 3)).reshape(2, pack, HYENA_ORDER * D_HYENA)
    w4 = jnp.stack([jnp.pad(w4, ((0, 0), (0, pack), (0, 0))),
                    jnp.pad(w4, ((0, 0), (pack, 0), (0, 0)))], axis=1)
    min_decay = math.log(HYENA_TARGET) / SLOW_DECAY_PCT
    max_decay = math.log(HYENA_TARGET) / FAST_DECAY_PCT
    deltas = jnp.linspace(min_decay, max_decay, D_HYENA, dtype=F32)
    deltas = jnp.tile(deltas, HYENA_ORDER)[None, :]
    width = HYENA_ORDER * D_HYENA
    sq = _const_spec((LANES, LANES))
    vec = _const_spec((1, LANES))
    nsteps = nblk // FILT_BLOCKS
    filt, sums = pl.pallas_call(
        functools.partial(_filter_kernel, nsteps=nsteps),
        grid=(2 * nsteps,),
        in_specs=[pl.BlockSpec((FILT_BLOCKS * half, LANES), lambda i: (i, 0)),
                  sq, vec, sq, vec, sq, vec, vec,
                  pl.BlockSpec((1, 2, LANES, width), lambda i: (i // nsteps, 0, 0, 0)),
                  _const_spec((1, width))],
        out_specs=[pl.BlockSpec((FILT_BLOCKS * FILT_ROWS, width), lambda i: (i, 0)),
                   pl.BlockSpec((1, 1, width), lambda i: (i // nsteps, 0, 0))],
        out_shape=[jax.ShapeDtypeStruct((n, width), F32),
                   jax.ShapeDtypeStruct((2, 1, width), F32)],
        compiler_params=pltpu.CompilerParams(
            dimension_semantics=("arbitrary",), vmem_limit_bytes=VMEM_LIMIT),
        name="hyena_filter",
    )(zz, w1, padv(p['filt_b1']), w2, padv(p['filt_b2']), w3, padv(p['filt_b3']),
      padv(p['filt_freq']), w4, deltas)
    n1, h1 = tb['n1'], tb['h1']
    return pl.pallas_call(
        functools.partial(_kspec_kernel, n1=n1),
        grid=(width // LANES,),
        in_specs=[pl.BlockSpec((n, LANES), lambda j: (0, j)),
                  pl.BlockSpec((2, 1, LANES), lambda j: (0, 0, j)),
                  _const_spec((n1, n1)), _const_spec((h1, 2 * FFT_N2, 2 * FFT_N2))],
        out_specs=pl.BlockSpec((n, LANES), lambda j: (0, j)),
        out_shape=jax.ShapeDtypeStruct((n, width), BF16),
        scratch_shapes=[pltpu.VMEM((h1 * A_PITCH, LANES), F32),
                        pltpu.VMEM((n1 * T_PITCH, LANES), F32)],
        compiler_params=pltpu.CompilerParams(
            dimension_semantics=("parallel",), vmem_limit_bytes=VMEM_LIMIT),
        name="hyena_filter_spectrum",
    )(filt, sums, _bf16_const(tb['m1_full']), _bf16_const(tb['m2']))


CHUNK_UNROLL = 8


def _for_chunks(count, body):
    body(0, True, False)

    def step(t1, carry):
        body(t1, False, False)
        return carry

    lax.fori_loop(1, count - CHUNK_UNROLL + 1, step, 0, unroll=CHUNK_UNROLL)
    for t1 in range(count - CHUNK_UNROLL + 1, count):
        body(t1, False, t1 == count - 1)


def _short_conv_rows(ref, w_ref, b_ref, r0, nrows, first, last):
    x = ref[pl.ds(r0, nrows), :]
    rid = lax.broadcasted_iota(jnp.int32, x.shape, 0)
    if first:
        prev = jnp.where(rid == 0, 0.0, pltpu.roll(x, 1, axis=0))
    else:
        prev = ref[pl.ds(r0 - 1, nrows), :]
    if last:
        nxt = jnp.where(rid == nrows - 1, 0.0, pltpu.roll(x, nrows - 1, axis=0))
    else:
        nxt = ref[pl.ds(r0 + 1, nrows), :]
    return prev * w_ref[0:1, :] + x * w_ref[1:2, :] + nxt * w_ref[2:3, :] + b_ref[...]


def _long_conv_kernel(*refs, n1, conv_z, conv_g):
    refs = list(refs)
    a_ref, tp_ref = refs[-2:]
    slabs = a_ref.shape[0]

    def take_operand(conv):
        return [tuple(refs.pop(0) for _ in range(3 if conv else 1)) for _ in range(slabs)]

    z_ops = take_operand(conv_z)
    g_ops = take_operand(conv_g)
    k_ref, bias_ref, m1_ref, m1i_ref, m2_ref, m2i_ref, o_ref = refs[:-2]
    n2 = FFT_N2
    h1 = n1 // 2

    def put(ref, idx, x):
        for s in range(slabs):
            ref[s, idx, :] = x[:, s * LANES:(s + 1) * LANES]

    def get(ref, idx):
        return jnp.concatenate([ref[s, idx, :] for s in range(slabs)], axis=1)

    def rows_of(ops, conv, t1, first, last):
        r0 = t1 * n2 if isinstance(t1, int) else pl.multiple_of(t1 * n2, n2)
        if conv:
            parts = [_short_conv_rows(ref, w_ref, b_ref, r0, n2, first, last) for ref, w_ref, b_ref in ops]
        else:
            parts = [ref[pl.ds(r0, n2), :] for ref, in ops]
        return jnp.concatenate(parts, axis=1)

    def z_rows(t1, first, last):
        return rows_of(z_ops, conv_z, t1, first, last)

    def g_rows(t1, first, last):
        return rows_of(g_ops, conv_g, t1, first, last)

    def pitched(t1):
        r0 = t1 * T_PITCH
        return pl.ds(r0 if isinstance(t1, int) else pl.multiple_of(r0, PITCH_ALIGN), n2)

    def pad(t1, first, last):
        put(tp_ref, pitched(t1), z_rows(t1, first, last))

    _for_chunks(h1, pad)

    wide = max(1, MXU_COLUMNS // (slabs * LANES))
    width = slabs * LANES

    def t2_sets(gi):
        return [[(gi * OUTER_GROUP + u) * wide + j for j in range(wide)] for u in range(OUTER_GROUP)]

    def stage1(gi, carry):
        sets = t2_sets(gi)
        zts = [jnp.concatenate([get(tp_ref, pl.ds(t2, h1, stride=T_PITCH)) for t2 in t2s], axis=1)
               for t2s in sets]
        prods = [_bdot(m1_ref[...], zt.astype(BF16)) for zt in zts]
        for t2s, a in zip(sets, prods):
            for j, t2 in enumerate(t2s):
                cols = slice(j * width, (j + 1) * width)
                put(a_ref, pl.ds(t2, h1, stride=A_PITCH), a[:h1, cols])
                put(a_ref, pl.ds(t2 + n2, h1, stride=A_PITCH), a[h1:, cols])
        return carry

    lax.fori_loop(0, n2 // (wide * OUTER_GROUP), stage1, 0)

    group = STAGE2_GROUP // slabs

    def stage2(gi, carry):
        f1s = [gi * group + j for j in range(group)]
        ras = [pl.ds(pl.multiple_of(f1 * A_PITCH, PITCH_ALIGN), 2 * n2) for f1 in f1s]
        xs = [_bdot(m2_ref[f1], get(a_ref, ra).astype(BF16)) for f1, ra in zip(f1s, ras)]
        ys = []
        for f1, x in zip(f1s, xs):
            kk = k_ref[pl.ds(pl.multiple_of(f1 * 2 * n2, 2 * n2), 2 * n2), :].astype(F32)
            xr, xi, kr, ki = x[:n2], x[n2:], kk[:n2], kk[n2:]
            ys.append(jnp.concatenate([xr * kr - xi * ki, xr * ki + xi * kr], axis=0).astype(BF16))
        outs = [_bdot(m2i_ref[f1], y) for f1, y in zip(f1s, ys)]
        for ra, out in zip(ras, outs):
            put(a_ref, ra, out)
        return carry

    lax.fori_loop(0, h1 // group, stage2, 0)

    def stage3(gi, carry):
        sets = t2_sets(gi)
        specs = []
        for t2s in sets:
            br = jnp.concatenate([get(a_ref, pl.ds(t2, h1, stride=A_PITCH)) for t2 in t2s], axis=1)
            bi = jnp.concatenate([get(a_ref, pl.ds(t2 + n2, h1, stride=A_PITCH)) for t2 in t2s], axis=1)
            specs.append(jnp.concatenate([br, bi], axis=0).astype(BF16))
        prods = [_bdot(m1i_ref[...], sp) for sp in specs]
        for t2s, y in zip(sets, prods):
            for j, t2 in enumerate(t2s):
                put(tp_ref, pl.ds(t2, h1, stride=T_PITCH), y[:, j * width:(j + 1) * width])
        return carry

    lax.fori_loop(0, n2 // (wide * OUTER_GROUP), stage3, 0)

    bias = bias_ref[...]

    def finish(t1, first, last):
        r0 = t1 * n2 if isinstance(t1, int) else pl.multiple_of(t1 * n2, n2)
        y = get(tp_ref, pitched(t1))
        out = g_rows(t1, first, last) * (y + bias * z_rows(t1, first, last))
        o_ref[pl.ds(r0, n2), :] = out.astype(o_ref.dtype)

    _for_chunks(h1, finish)


def _bf16_const(a):
    return jnp.asarray(a, F32).astype(BF16)


def _long_conv_slabs(seq):
    n1 = 2 * seq // FFT_N2
    h1 = n1 // 2
    tables = 2 * (2 * n1 * h1 + 2 * h1 * (2 * FFT_N2) ** 2)
    for slabs in (2, 1):
        io = 3 * 2 * seq * 4
        per_lane = io + 2 * 2 * seq * 2 + h1 * (A_PITCH + T_PITCH) * 4
        if per_lane * slabs * LANES + tables <= VMEM_LIMIT - VMEM_HEADROOM:
            return slabs
    raise ValueError(f"long conv of length {seq} does not fit VMEM")


def _long_conv(z, z_off, gate, g_off, kspec, k_off, bias, batch, seq, short=None, conv_z=False,
               out_dtype=F32):
    n = 2 * seq
    tb = _fft_tables(seq)
    n1, h1 = tb['n1'], tb['h1']
    conv_g = short is not None
    slabs = _long_conv_slabs(seq)
    width = slabs * LANES
    assert z_off % slabs == 0 and g_off % slabs == 0 and k_off % slabs == 0

    def operand(arr, off, conv):
        specs, args = [], []
        for s in range(slabs):
            specs.append(pl.BlockSpec((seq, LANES), lambda j, b, s=s: (b, off + j * slabs + s)))
            args.append(arr)
            if conv:
                specs += [pl.BlockSpec((3, LANES), lambda j, b, s=s: (0, off + j * slabs + s)),
                          pl.BlockSpec((1, LANES), lambda j, b, s=s: (0, off + j * slabs + s))]
                args += list(short)
        return specs, args

    z_specs, z_args = operand(z, z_off, conv_z)
    g_specs, g_args = operand(gate, g_off, conv_g)
    k_blk = k_off // slabs
    return pl.pallas_call(
        functools.partial(_long_conv_kernel, n1=n1, conv_z=conv_z, conv_g=conv_g),
        grid=(D_HYENA // width, batch),
        in_specs=z_specs + g_specs + [
                  pl.BlockSpec((n, width), lambda j, b: (0, k_blk + j)),
                  pl.BlockSpec((1, width), lambda j, b: (0, j)),
                  _const_spec((n1, h1)), _const_spec((h1, n1)),
                  _const_spec((h1, 2 * FFT_N2, 2 * FFT_N2)),
                  _const_spec((h1, 2 * FFT_N2, 2 * FFT_N2))],
        out_specs=pl.BlockSpec((seq, width), lambda j, b: (b, j)),
        out_shape=jax.ShapeDtypeStruct((batch * seq, D_HYENA), out_dtype),
        scratch_shapes=[pltpu.VMEM((slabs, h1 * A_PITCH, LANES), F32),
                        pltpu.VMEM((slabs, h1 * T_PITCH, LANES), F32)],
        compiler_params=pltpu.CompilerParams(
            dimension_semantics=("parallel", "parallel"), vmem_limit_bytes=VMEM_LIMIT),
        name="hyena_long_conv",
    )(*z_args, *g_args, kspec, bias, _bf16_const(tb['m1']), _bf16_const(tb['m1_inv']),
      _bf16_const(tb['m2']), _bf16_const(tb['m2_inv']))


def _layer(x3, p):
    batch, seq, _ = x3.shape
    x = x3.reshape(batch * seq, D_MODEL)
    x = _ffn(x, p['ffn1_pre_g'], p['ffn1_w1'], p['ffn1_w3'], p['ffn1_w2'], p['ffn1_post_g'])
    hy, qkv, gate = _inproj(x, p['mix_pre_g'], p['w_in'], seq)
    kspec = _filter_spectra(seq, p)
    tiles = D_HYENA // LANES
    short = (p['short_w'], p['short_b'])
    z1 = _long_conv(hy, 0, hy, tiles, kspec, 0, p['hyena_bias'][0:1], batch, seq, short=short, conv_z=True)
    yh = _long_conv(z1, 0, hy, 2 * tiles, kspec, tiles, p['hyena_bias'][1:2], batch, seq, short=short,
                    out_dtype=BF16)
    yr = _retention(qkv, gate, p['ret_log_decay_f'], p['ret_log_decay_b'], batch, seq)
    mix = (yh, yr, p['w_out'], p['mix_post_g'])
    x = _ffn(x, p['ffn2_pre_g'], p['ffn2_w1'], p['ffn2_w3'], p['ffn2_w2'], p['ffn2_post_g'], mix=mix)
    return x.reshape(batch, seq, D_MODEL)


_MATRIX_PARAMS = ('ffn1_w1', 'ffn1_w3', 'ffn1_w2', 'w_in', 'w_out', 'ffn2_w1', 'ffn2_w3', 'ffn2_w2')
_GAIN_PARAMS = ('ffn1_pre_g', 'ffn1_post_g', 'mix_pre_g', 'mix_post_g', 'ffn2_pre_g', 'ffn2_post_g')


def kernel(x_prompt, x_sample, ffn1_pre_g, ffn1_w1, ffn1_w3, ffn1_w2, ffn1_post_g, mix_pre_g, w_in, short_w, short_b, filt_w1, filt_b1, filt_w2, filt_b2, filt_w3, filt_b3, filt_w4, filt_freq, hyena_bias, ret_log_decay_f, ret_log_decay_b, w_out, mix_post_g, ffn2_pre_g, ffn2_w1, ffn2_w3, ffn2_w2, ffn2_post_g):
    params = dict(ffn1_pre_g=ffn1_pre_g, ffn1_w1=ffn1_w1, ffn1_w3=ffn1_w3, ffn1_w2=ffn1_w2,
                  ffn1_post_g=ffn1_post_g, mix_pre_g=mix_pre_g, w_in=w_in, short_w=short_w,
                  short_b=short_b, filt_w1=filt_w1, filt_b1=filt_b1, filt_w2=filt_w2,
                  filt_b2=filt_b2, filt_w3=filt_w3, filt_b3=filt_b3, filt_w4=filt_w4,
                  filt_freq=filt_freq, hyena_bias=hyena_bias, ret_log_decay_f=ret_log_decay_f,
                  ret_log_decay_b=ret_log_decay_b, w_out=w_out, mix_post_g=mix_post_g,
                  ffn2_pre_g=ffn2_pre_g, ffn2_w1=ffn2_w1, ffn2_w3=ffn2_w3, ffn2_w2=ffn2_w2,
                  ffn2_post_g=ffn2_post_g)
    depth = ffn1_w1.shape[0]

    def run(x):
        for l in range(depth):
            p = {k: v[l] for k, v in params.items()}
            for k in _MATRIX_PARAMS:
                p[k] = p[k].astype(BF16)
            for k in _GAIN_PARAMS:
                p[k] = p[k][None, :]
            p['short_b'] = p['short_b'][None, :]
            x = _layer(x, p)
        return x

    return (run(x_prompt), run(x_sample))
```

```python
import functools
import math

import numpy as np
import jax
import jax.numpy as jnp
from jax import lax
from jax.experimental import pallas as pl
from jax.experimental.pallas import tpu as pltpu

F32 = jnp.float32
BF16 = jnp.bfloat16

D_MODEL = 1024
D_HYENA = 512
D_RET = 512
HYENA_ORDER = 2
N_RET_HEADS = 4
RET_HEAD_DIM = 128
D_FF = 2816
FILT_EMB = 33
FILT_BANDS = 16
FILT_HIDDEN = 64
ROPE_BASE = 10000.0
NORM_EPS = 1e-6
HYENA_TARGET = 1e-2
FAST_DECAY_PCT = 0.3
SLOW_DECAY_PCT = 1.5
N_HY_COLS = (HYENA_ORDER + 1) * D_HYENA
D_IN = N_HY_COLS + 4 * D_RET

LANES = 128
MXU_COLUMNS = 256
VMEM_LIMIT = 56 * 1024 * 1024
VMEM_HEADROOM = 4 * 1024 * 1024
FFT_N2 = 32
RET_CHUNK = 128
RET_GROUP = 32
TOKEN_TILE = 512


def _const_spec(shape):
    nd = len(shape)
    return pl.BlockSpec(shape, lambda *_: (0,) * nd, pipeline_mode=pl.Buffered(1))


def _rms(x, g):
    ms = jnp.mean(x * x, axis=-1, keepdims=True)
    return x * lax.rsqrt(ms + NORM_EPS) * g


def _bdot(a, b):
    return jnp.dot(a, b, preferred_element_type=F32)


def _ffn_core(x, pre_ref, w1_ref, w3_ref, w2_ref, post_ref, o_ref):
    h = _rms(x, pre_ref[...]).astype(BF16)
    a = _bdot(h, w1_ref[...])
    b = _bdot(h, w3_ref[...])
    g = (a * jax.nn.sigmoid(a) * b).astype(BF16)
    y = _bdot(g, w2_ref[...])
    o_ref[...] = x + 0.5 * _rms(y, post_ref[...])


def _ffn_kernel(x_ref, pre_ref, w1_ref, w3_ref, w2_ref, post_ref, o_ref):
    _ffn_core(x_ref[...], pre_ref, w1_ref, w3_ref, w2_ref, post_ref, o_ref)


def _mix_ffn_kernel(x_ref, yh_ref, yr_ref, woh_ref, wor_ref, mg_ref,
                    pre_ref, w1_ref, w3_ref, w2_ref, post_ref, o_ref):
    y = _bdot(yh_ref[...].astype(BF16), woh_ref[...])
    y = y + _bdot(yr_ref[...].astype(BF16), wor_ref[...])
    x = x_ref[...] + _rms(y, mg_ref[...])
    _ffn_core(x, pre_ref, w1_ref, w3_ref, w2_ref, post_ref, o_ref)


def _row_spec(tm, width):
    return pl.BlockSpec((tm, width), lambda i: (i, 0))


def _ffn(x, pre_g, w1, w3, w2, post_g, mix=None):
    t = x.shape[0]
    tm = TOKEN_TILE
    ffn_specs = [_const_spec((1, D_MODEL)), _const_spec((D_MODEL, D_FF)),
                 _const_spec((D_MODEL, D_FF)), _const_spec((D_FF, D_MODEL)),
                 _const_spec((1, D_MODEL))]
    ffn_args = (pre_g, w1, w3, w2, post_g)
    if mix is None:
        body, args = _ffn_kernel, (x,) + ffn_args
        specs = [_row_spec(tm, D_MODEL)] + ffn_specs
    else:
        yh, yr, w_out, mg = mix
        body, args = _mix_ffn_kernel, (x, yh, yr, w_out, w_out, mg) + ffn_args
        specs = [_row_spec(tm, D_MODEL), _row_spec(tm, D_HYENA), _row_spec(tm, D_RET),
                 pl.BlockSpec((D_HYENA, D_MODEL), lambda i: (0, 0), pipeline_mode=pl.Buffered(1)),
                 pl.BlockSpec((D_RET, D_MODEL), lambda i: (D_HYENA // D_RET, 0),
                              pipeline_mode=pl.Buffered(1)),
                 _const_spec((1, D_MODEL))] + ffn_specs
    return pl.pallas_call(
        body,
        grid=(t // tm,),
        in_specs=specs,
        out_specs=_row_spec(tm, D_MODEL),
        out_shape=jax.ShapeDtypeStruct((t, D_MODEL), F32),
        compiler_params=pltpu.CompilerParams(
            dimension_semantics=("parallel",), vmem_limit_bytes=VMEM_LIMIT),
        name="ffn_mix" if mix is not None else "ffn",
    )(*args)


def _inproj_kernel(x_ref, g_ref, w_ref, cc_ref, ss_ref, hy_ref, qkv_ref, gate_ref):
    h = _rms(x_ref[...], g_ref[...]).astype(BF16)
    qk0 = N_HY_COLS
    v0 = qk0 + 2 * D_RET
    g0 = v0 + D_RET
    qk = _bdot(h, w_ref[:, qk0:v0])
    hy_ref[...] = _bdot(h, w_ref[:, :qk0])
    gate_ref[...] = _bdot(h, w_ref[:, g0:])
    qkv_ref[:, 2 * D_RET:] = _bdot(h, w_ref[:, v0:g0]).astype(BF16)
    cc = cc_ref[...]
    ss = ss_ref[...]
    d = RET_HEAD_DIM
    for blk in range(2 * N_RET_HEADS):
        x = qk[:, blk * d:(blk + 1) * d]
        r = x * cc + pltpu.roll(x, d // 2, axis=1) * ss
        if blk >= N_RET_HEADS:
            r = r * (d ** -0.5)
        qkv_ref[:, blk * d:(blk + 1) * d] = r.astype(BF16)


def _inproj(x, g, w, seq):
    t = x.shape[0]
    tm = TOKEN_TILE
    cc, ss = _rope_tables(seq)
    pos_blocks = seq // tm
    rope_spec = pl.BlockSpec((tm, RET_HEAD_DIM), lambda i: (i % pos_blocks, 0))
    return pl.pallas_call(
        _inproj_kernel,
        grid=(t // tm,),
        in_specs=[_row_spec(tm, D_MODEL), _const_spec((1, D_MODEL)),
                  _const_spec((D_MODEL, D_IN)), rope_spec, rope_spec],
        out_specs=[_row_spec(tm, N_HY_COLS), _row_spec(tm, 3 * D_RET), _row_spec(tm, D_RET)],
        out_shape=[jax.ShapeDtypeStruct((t, N_HY_COLS), F32),
                   jax.ShapeDtypeStruct((t, 3 * D_RET), BF16),
                   jax.ShapeDtypeStruct((t, D_RET), F32)],
        compiler_params=pltpu.CompilerParams(
            dimension_semantics=("parallel",), vmem_limit_bytes=VMEM_LIMIT),
        name="inproj",
    )(x, g, w, cc, ss)


def _ret_kernel(lgf_ref, lgb_ref, q_ref, k_ref, v_ref, g_ref, o_ref, kt_ref, *, seq):
    c = RET_CHUNK
    d = RET_HEAD_DIM
    n_chunks = seq // c
    head = pl.program_id(1)
    lgf = jnp.full((c, d), lgf_ref[head], F32)
    lgb = jnp.full((c, d), lgb_ref[head], F32)
    row = lax.broadcasted_iota(jnp.int32, (c, d), 0).astype(F32)
    col = lax.broadcasted_iota(jnp.int32, (c, d), 1).astype(F32)
    diff = row - col
    dmat = jnp.where(diff >= 0.0, jnp.exp(jnp.maximum(diff, 0.0) * lgf),
                     jnp.exp(jnp.maximum(-diff, 0.0) * lgb))
    wq_f = jnp.exp((row + 1.0) * lgf)
    wk_f = jnp.exp((c - 1.0 - row) * lgf)
    wq_b = jnp.exp((c - row) * lgb)
    wk_b = jnp.exp(row * lgb)
    gc_f = jnp.exp(c * lgf)
    gc_b = jnp.exp(c * lgb)

    grp = RET_GROUP
    n_groups = n_chunks // grp

    def fwd(gi, state):
        rows = [pl.ds(pl.multiple_of((gi * grp + j) * c, c), c) for j in range(grp)]
        qbs, kts, vs = [], [], []
        for r in rows:
            ktb = k_ref[r, :].astype(F32).T.astype(BF16)
            kt_ref[r, :] = ktb
            qbs.append(q_ref[r, :])
            kts.append(ktb)
            vs.append(v_ref[r, :])
        scores = [_bdot(qb, ktb) for qb, ktb in zip(qbs, kts)]
        kvs = [_bdot(ktb, (v.astype(F32) * wk_f).astype(BF16)) for ktb, v in zip(kts, vs)]
        states = []
        for kv in kvs:
            states.append(state)
            state = state * gc_f + kv
        intras = [_bdot((s * dmat).astype(BF16), v) for s, v in zip(scores, vs)]
        crosses = [_bdot(qb, st.astype(BF16)) for qb, st in zip(qbs, states)]
        for r, intra, cross in zip(rows, intras, crosses):
            o_ref[r, :] = intra + wq_f * cross
        return state

    lax.fori_loop(0, n_groups, fwd, jnp.zeros((d, d), F32))

    def bwd(gi, state):
        rows = [pl.ds(pl.multiple_of((n_chunks - 1 - gi * grp - j) * c, c), c) for j in range(grp)]
        kvs = [_bdot(kt_ref[r, :], (v_ref[r, :].astype(F32) * wk_b).astype(BF16)) for r in rows]
        states = []
        for kv in kvs:
            states.append(state)
            state = state * gc_b + kv
        crosses = [_bdot(q_ref[r, :], st.astype(BF16)) for r, st in zip(rows, states)]
        outs = [o_ref[r, :] + wq_b * cross for r, cross in zip(rows, crosses)]
        means = [jnp.mean(o * o, axis=-1, keepdims=True) for o in outs]
        norms = [lax.rsqrt(m + NORM_EPS) for m in means]
        gates = [g_ref[r, :] for r in rows]
        gates = [g * jax.nn.sigmoid(g) for g in gates]
        for r, o, nrm, g in zip(rows, outs, norms, gates):
            o_ref[r, :] = g * (o * nrm)
        return state

    lax.fori_loop(0, n_groups, bwd, jnp.zeros((d, d), F32))


def _rope_tables(seq):
    d = RET_HEAD_DIM
    inv = 1.0 / (ROPE_BASE ** (jnp.arange(0, d, 2, dtype=F32) / d))
    ang = jnp.arange(seq, dtype=F32)[:, None] * inv[None, :]
    c, s = jnp.cos(ang), jnp.sin(ang)
    return jnp.concatenate([c, c], axis=-1), jnp.concatenate([-s, s], axis=-1)


def _retention(qkv, gate, lg_f, lg_b, batch, seq):
    assert RET_CHUNK == RET_HEAD_DIM
    heads = N_RET_HEADS

    def col(off):
        return pl.BlockSpec((seq, LANES), lambda b, h, *_: (b, off * heads + h))

    grid_spec = pltpu.PrefetchScalarGridSpec(
        num_scalar_prefetch=2,
        grid=(batch, heads),
        in_specs=[col(0), col(1), col(2), col(0)],
        out_specs=pl.BlockSpec((seq, LANES), lambda b, h, *_: (b, h)),
        scratch_shapes=[pltpu.VMEM((seq, RET_CHUNK), BF16)],
    )
    return pl.pallas_call(
        functools.partial(_ret_kernel, seq=seq),
        grid_spec=grid_spec,
        out_shape=jax.ShapeDtypeStruct((batch * seq, D_RET), F32),
        compiler_params=pltpu.CompilerParams(
            dimension_semantics=("parallel", "parallel"), vmem_limit_bytes=VMEM_LIMIT),
        name="retention",
    )(lg_f, lg_b, qkv, qkv, qkv, gate)


def _split(x):
    hi = x.astype(BF16)
    return hi, (x - hi.astype(F32)).astype(BF16)


def _dot3(a, b):
    a_hi, a_lo = a
    b_hi, b_lo = b
    return _bdot(a_hi, b_hi) + _bdot(a_hi, b_lo) + _bdot(a_lo, b_hi)


@functools.lru_cache(maxsize=None)
def _fft_tables(seq):
    n = 2 * seq
    n2 = FFT_N2
    n1 = n // n2
    h1 = n1 // 2
    f1 = np.arange(h1, dtype=np.float64) + 0.5
    th = 2.0 * np.pi * f1[:, None] * np.arange(n1, dtype=np.float64)[None, :] / n1
    m1_full = np.concatenate([np.cos(th), -np.sin(th)], axis=0)
    m1 = m1_full[:, :h1]
    m1_inv = (2.0 / n) * m1.T
    t2 = np.arange(n2, dtype=np.float64)
    phi = 2.0 * np.pi * (np.arange(n2, dtype=np.float64)[None, :, None] * t2[None, None, :] / n2
                         + f1[:, None, None] * t2[None, None, :] / n)
    gr, gi = np.cos(phi), -np.sin(phi)
    m2 = np.concatenate([np.concatenate([gr, -gi], axis=2),
                         np.concatenate([gi, gr], axis=2)], axis=1)
    m2_inv = np.transpose(m2, (0, 2, 1))
    return dict(n1=n1, h1=h1, m1_full=m1_full, m1=m1, m1_inv=m1_inv, m2=m2, m2_inv=m2_inv)


OUTER_GROUP = 8
STAGE2_GROUP = 32
FILT_ROWS = 512
FILT_BLOCKS = 4
FILT_PACK = LANES // 2
PITCH_ALIGN = 4
A_PITCH = 2 * FFT_N2 + PITCH_ALIGN
T_PITCH = FFT_N2 + PITCH_ALIGN


def _filter_kernel(z_ref, w1_ref, b1_ref, w2_ref, b2_ref, w3_ref, b3_ref, fr_ref,
                   w4_ref, dl_ref, o_ref, sum_ref, *, nsteps):
    i = pl.program_id(0)
    half = FILT_ROWS // 2
    fr = fr_ref[...]
    zs = [z_ref[b * half:(b + 1) * half, :] for b in range(FILT_BLOCKS)]

    def layer(xs, w_ref, b_ref):
        w = _split(w_ref[...])
        pre = [_dot3(_split(x), w) for x in xs]
        return [jnp.sin(fr * (v + b_ref[...])) for v in pre]

    hs = layer(layer(layer(zs, w1_ref, b1_ref), w2_ref, b2_ref), w3_ref, b3_ref)
    hs = [_split(h) for h in hs]
    adl = jnp.abs(dl_ref[...])
    backward = i >= nsteps
    for side in range(2):
        lane = side * FILT_PACK
        ts = [jnp.broadcast_to(z[:, lane:lane + 1], z.shape) for z in zs]
        for c in range(HYENA_ORDER * D_HYENA // LANES):
            cols = slice(c * LANES, (c + 1) * LANES)
            w4 = _split(w4_ref[0, side, :, cols])
            hcs = [_dot3(h, w4) * jnp.exp(-t * adl[:, cols]) for h, t in zip(hs, ts)]
            part = sum(jnp.sum(jnp.abs(hc), axis=0, keepdims=True) for hc in hcs)

            if side == 0:
                @pl.when(i % nsteps == 0)
                def _():
                    sum_ref[0, :, cols] = part

                @pl.when(i % nsteps != 0)
                def _():
                    sum_ref[0, :, cols] = sum_ref[0, :, cols] + part
            else:
                sum_ref[0, :, cols] = sum_ref[0, :, cols] + part

            for b, hc in enumerate(hcs):
                if side == 0 and b == 0:
                    rows = lax.broadcasted_iota(jnp.int32, hc.shape, 0)
                    flipped = jnp.where(jnp.logical_and(rows == 0, i == nsteps), 0.0, -hc)
                else:
                    flipped = -hc
                r0 = b * FILT_ROWS + side * half
                o_ref[r0:r0 + half, cols] = jnp.where(backward, flipped, hc)


def _kspec_kernel(k_ref, s_ref, m1_ref, m2_ref, o_ref, a_ref, kp_ref, *, n1):
    n2 = FFT_N2
    h1 = n1 // 2
    inv_f = 1.0 / s_ref[0]
    inv_b = 1.0 / s_ref[1]

    def pad(t1, carry):
        src = pl.multiple_of(t1 * n2, n2)
        dst = pl.multiple_of(t1 * T_PITCH, PITCH_ALIGN)
        kp_ref[pl.ds(dst, n2), :] = k_ref[pl.ds(src, n2), :] * jnp.where(t1 < h1, inv_f, inv_b)
        return carry

    lax.fori_loop(0, n1, pad, 0, unroll=8)

    wide = MXU_COLUMNS // LANES

    def stage1(gi, carry):
        sets = [[(gi * OUTER_GROUP + u) * wide + j for j in range(wide)] for u in range(OUTER_GROUP)]
        kts = [jnp.concatenate([kp_ref[pl.ds(t2, n1, stride=T_PITCH), :] for t2 in t2s], axis=1)
               for t2s in sets]
        prods = [_bdot(m1_ref[...], kt.astype(BF16)) for kt in kts]
        for t2s, a in zip(sets, prods):
            for j, t2 in enumerate(t2s):
                cols = slice(j * LANES, (j + 1) * LANES)
                a_ref[pl.ds(t2, h1, stride=A_PITCH), :] = a[:h1, cols]
                a_ref[pl.ds(t2 + n2, h1, stride=A_PITCH), :] = a[h1:, cols]
        return carry

    lax.fori_loop(0, n2 // (wide * OUTER_GROUP), stage1, 0)

    def stage2(gi, carry):
        f1s = [gi * STAGE2_GROUP + j for j in range(STAGE2_GROUP)]
        srcs = [pl.ds(pl.multiple_of(f1 * A_PITCH, PITCH_ALIGN), 2 * n2) for f1 in f1s]
        specs = [_bdot(m2_ref[f1], a_ref[src, :].astype(BF16)) for f1, src in zip(f1s, srcs)]
        for f1, spec in zip(f1s, specs):
            dst = pl.multiple_of(f1 * 2 * n2, 2 * n2)
            o_ref[pl.ds(dst, 2 * n2), :] = spec.astype(o_ref.dtype)
        return carry

    lax.fori_loop(0, h1 // STAGE2_GROUP, stage2, 0)


def _filter_spectra(seq, p):
    n = 2 * seq
    tb = _fft_tables(seq)
    nblk = seq // FILT_ROWS
    half = FILT_ROWS // 2
    pack = FILT_PACK

    def packed(v):
        return jnp.transpose(v.reshape(2 * nblk, 2, half), (0, 2, 1)).reshape(n // 2, 2, 1)

    fwd_pos = jnp.arange(seq, dtype=jnp.int32)
    pos = jnp.concatenate([fwd_pos, (seq - fwd_pos) % seq])
    t_fwd = jnp.linspace(0.0, 1.0, seq, dtype=F32)
    t = packed(jnp.concatenate([t_fwd, jnp.roll(t_fwd[::-1], 1)]))
    w = packed(2.0 * math.pi * pos.astype(F32) / seq)
    fb = jnp.linspace(1e-4, FILT_BANDS - 1, FILT_BANDS, dtype=F32)[None, None, :]
    zero = jnp.zeros((n // 2, 2, pack - FILT_EMB), F32)
    zz = jnp.concatenate([t, jnp.cos(fb * w), -jnp.sin(fb * w), zero], axis=-1).reshape(n // 2, LANES)

    def padw(a):
        a = jnp.pad(a, ((0, pack - a.shape[0]), (0, pack - a.shape[1])))
        zero = jnp.zeros_like(a)
        return jnp.concatenate([jnp.concatenate([a, zero], axis=1),
                                jnp.concatenate([zero, a], axis=1)], axis=0)

    def padv(a):
        return jnp.tile(jnp.pad(a, (0, pack - a.shape[0])), 2)[None, :]

    w1 = padw(p['filt_w1'])
    w2 = padw(p['filt_w2'])
    w3 = padw(p['filt_w3'])
    w4 = jnp.pad(p['filt_w4'], ((0, pack - FILT_HIDDEN), (0, 0))).reshape(pack, HYENA_ORDER, 2, D_HYENA)
    w4 = jnp.transpose(w4, (2, 0, 1, 3)).reshape(2, pack, HYENA_ORDER * D_HYENA)
    w4 = jnp.stack([jnp.pad(w4, ((0, 0), (0, pack), (0, 0))),
                    jnp.pad(w4, ((0, 0), (pack, 0), (0, 0)))], axis=1)
    min_decay = math.log(HYENA_TARGET) / SLOW_DECAY_PCT
    max_decay = math.log(HYENA_TARGET) / FAST_DECAY_PCT
    deltas = jnp.linspace(min_decay, max_decay, D_HYENA, dtype=F32)
    deltas = jnp.tile(deltas, HYENA_ORDER)[None, :]
    width = HYENA_ORDER * D_HYENA
    sq = _const_spec((LANES, LANES))
    vec = _const_spec((1, LANES))
    nsteps = nblk // FILT_BLOCKS
    filt, sums = pl.pallas_call(
        functools.partial(_filter_kernel, nsteps=nsteps),
        grid=(2 * nsteps,),
        in_specs=[pl.BlockSpec((FILT_BLOCKS * half, LANES), lambda i: (i, 0)),
                  sq, vec, sq, vec, sq, vec, vec,
                  pl.BlockSpec((1, 2, LANES, width), lambda i: (i // nsteps, 0, 0, 0)),
                  _const_spec((1, width))],
        out_specs=[pl.BlockSpec((FILT_BLOCKS * FILT_ROWS, width), lambda i: (i, 0)),
                   pl.BlockSpec((1, 1, width), lambda i: (i // nsteps, 0, 0))],
        out_shape=[jax.ShapeDtypeStruct((n, width), F32),
                   jax.ShapeDtypeStruct((2, 1, width), F32)],
        compiler_params=pltpu.CompilerParams(
            dimension_semantics=("arbitrary",), vmem_limit_bytes=VMEM_LIMIT),
        name="hyena_filter",
    )(zz, w1, padv(p['filt_b1']), w2, padv(p['filt_b2']), w3, padv(p['filt_b3']),
      padv(p['filt_freq']), w4, deltas)
    n1, h1 = tb['n1'], tb['h1']
    return pl.pallas_call(
        functools.partial(_kspec_kernel, n1=n1),
        grid=(width // LANES,),
        in_specs=[pl.BlockSpec((n, LANES), lambda j: (0, j)),
                  pl.BlockSpec((2, 1, LANES), lambda j: (0, 0, j)),
                  _const_spec((n1, n1)), _const_spec((h1, 2 * FFT_N2, 2 * FFT_N2))],
        out_specs=pl.BlockSpec((n, LANES), lambda j: (0, j)),
        out_shape=jax.ShapeDtypeStruct((n, width), BF16),
        scratch_shapes=[pltpu.VMEM((h1 * A_PITCH, LANES), F32),
                        pltpu.VMEM((n1 * T_PITCH, LANES), F32)],
        compiler_params=pltpu.CompilerParams(
            dimension_semantics=("parallel",), vmem_limit_bytes=VMEM_LIMIT),
        name="hyena_filter_spectrum",
    )(filt, sums, _bf16_const(tb['m1_full']), _bf16_const(tb['m2']))


CHUNK_UNROLL = 8


def _for_chunks(count, body):
    body(0, True, False)

    def step(t1, carry):
        body(t1, False, False)
        return carry

    lax.fori_loop(1, count - CHUNK_UNROLL + 1, step, 0, unroll=CHUNK_UNROLL)
    for t1 in range(count - CHUNK_UNROLL + 1, count):
        body(t1, False, t1 == count - 1)


def _short_conv_rows(ref, w_ref, b_ref, r0, nrows, first, last):
    x = ref[pl.ds(r0, nrows), :]
    rid = lax.broadcasted_iota(jnp.int32, x.shape, 0)
    if first:
        prev = jnp.where(rid == 0, 0.0, pltpu.roll(x, 1, axis=0))
    else:
        prev = ref[pl.ds(r0 - 1, nrows), :]
    if last:
        nxt = jnp.where(rid == nrows - 1, 0.0, pltpu.roll(x, nrows - 1, axis=0))
    else:
        nxt = ref[pl.ds(r0 + 1, nrows), :]
    return prev * w_ref[0:1, :] + x * w_ref[1:2, :] + nxt * w_ref[2:3, :] + b_ref[...]


def _long_conv_kernel(*refs, n1, conv_z, conv_g):
    refs = list(refs)
    a_ref, tp_ref = refs[-2:]
    slabs = a_ref.shape[0]

    def take_operand(conv):
        return [tuple(refs.pop(0) for _ in range(3 if conv else 1)) for _ in range(slabs)]

    z_ops = take_operand(conv_z)
    g_ops = take_operand(conv_g)
    k_ref, bias_ref, m1_ref, m1i_ref, m2_ref, m2i_ref, o_ref = refs[:-2]
    n2 = FFT_N2
    h1 = n1 // 2

    def put(ref, idx, x):
        for s in range(slabs):
            ref[s, idx, :] = x[:, s * LANES:(s + 1) * LANES]

    def get(ref, idx):
        return jnp.concatenate([ref[s, idx, :] for s in range(slabs)], axis=1)

    def rows_of(ops, conv, t1, first, last):
        r0 = t1 * n2 if isinstance(t1, int) else pl.multiple_of(t1 * n2, n2)
        if conv:
            parts = [_short_conv_rows(ref, w_ref, b_ref, r0, n2, first, last) for ref, w_ref, b_ref in ops]
        else:
            parts = [ref[pl.ds(r0, n2), :] for ref, in ops]
        return jnp.concatenate(parts, axis=1)

    def z_rows(t1, first, last):
        return rows_of(z_ops, conv_z, t1, first, last)

    def g_rows(t1, first, last):
        return rows_of(g_ops, conv_g, t1, first, last)

    def pitched(t1):
        r0 = t1 * T_PITCH
        return pl.ds(r0 if isinstance(t1, int) else pl.multiple_of(r0, PITCH_ALIGN), n2)

    def pad(t1, first, last):
        put(tp_ref, pitched(t1), z_rows(t1, first, last))

    _for_chunks(h1, pad)

    wide = max(1, MXU_COLUMNS // (slabs * LANES))
    width = slabs * LANES

    def t2_sets(gi):
        return [[(gi * OUTER_GROUP + u) * wide + j for j in range(wide)] for u in range(OUTER_GROUP)]

    def stage1(gi, carry):
        sets = t2_sets(gi)
        zts = [jnp.concatenate([get(tp_ref, pl.ds(t2, h1, stride=T_PITCH)) for t2 in t2s], axis=1)
               for t2s in sets]
        prods = [_bdot(m1_ref[...], zt.astype(BF16)) for zt in zts]
        for t2s, a in zip(sets, prods):
            for j, t2 in enumerate(t2s):
                cols = slice(j * width, (j + 1) * width)
                put(a_ref, pl.ds(t2, h1, stride=A_PITCH), a[:h1, cols])
                put(a_ref, pl.ds(t2 + n2, h1, stride=A_PITCH), a[h1:, cols])
        return carry

    lax.fori_loop(0, n2 // (wide * OUTER_GROUP), stage1, 0)

    group = STAGE2_GROUP // slabs

    def stage2(gi, carry):
        f1s = [gi * group + j for j in range(group)]
        ras = [pl.ds(pl.multiple_of(f1 * A_PITCH, PITCH_ALIGN), 2 * n2) for f1 in f1s]
        xs = [_bdot(m2_ref[f1], get(a_ref, ra).astype(BF16)) for f1, ra in zip(f1s, ras)]
        ys = []
        for f1, x in zip(f1s, xs):
            kk = k_ref[pl.ds(pl.multiple_of(f1 * 2 * n2, 2 * n2), 2 * n2), :].astype(F32)
            xr, xi, kr, ki = x[:n2], x[n2:], kk[:n2], kk[n2:]
            ys.append(jnp.concatenate([xr * kr - xi * ki, xr * ki + xi * kr], axis=0).astype(BF16))
        outs = [_bdot(m2i_ref[f1], y) for f1, y in zip(f1s, ys)]
        for ra, out in zip(ras, outs):
            put(a_ref, ra, out)
        return carry

    lax.fori_loop(0, h1 // group, stage2, 0)

    def stage3(gi, carry):
        sets = t2_sets(gi)
        specs = []
        for t2s in sets:
            br = jnp.concatenate([get(a_ref, pl.ds(t2, h1, stride=A_PITCH)) for t2 in t2s], axis=1)
            bi = jnp.concatenate([get(a_ref, pl.ds(t2 + n2, h1, stride=A_PITCH)) for t2 in t2s], axis=1)
            specs.append(jnp.concatenate([br, bi], axis=0).astype(BF16))
        prods = [_bdot(m1i_ref[...], sp) for sp in specs]
        for t2s, y in zip(sets, prods):
            for j, t2 in enumerate(t2s):
                put(tp_ref, pl.ds(t2, h1, stride=T_PITCH), y[:, j * width:(j + 1) * width])
        return carry

    lax.fori_loop(0, n2 // (wide * OUTER_GROUP), stage3, 0)

    bias = bias_ref[...]

    def finish(t1, first, last):
        r0 = t1 * n2 if isinstance(t1, int) else pl.multiple_of(t1 * n2, n2)
        y = get(tp_ref, pitched(t1))
        out = g_rows(t1, first, last) * (y + bias * z_rows(t1, first, last))
        o_ref[pl.ds(r0, n2), :] = out.astype(o_ref.dtype)

    _for_chunks(h1, finish)


def _bf16_const(a):
    return jnp.asarray(a, F32).astype(BF16)


def _long_conv_slabs(seq):
    n1 = 2 * seq // FFT_N2
    h1 = n1 // 2
    tables = 2 * (2 * n1 * h1 + 2 * h1 * (2 * FFT_N2) ** 2)
    for slabs in (2, 1):
        io = 3 * 2 * seq * 4
        per_lane = io + 2 * 2 * seq * 2 + h1 * (A_PITCH + T_PITCH) * 4
        if per_lane * slabs * LANES + tables <= VMEM_LIMIT - VMEM_HEADROOM:
            return slabs
    raise ValueError(f"long conv of length {seq} does not fit VMEM")


def _long_conv(z, z_off, gate, g_off, kspec, k_off, bias, batch, seq, short=None, conv_z=False,
               out_dtype=F32):
    n = 2 * seq
    tb = _fft_tables(seq)
    n1, h1 = tb['n1'], tb['h1']
    conv_g = short is not None
    slabs = _long_conv_slabs(seq)
    width = slabs * LANES
    assert z_off % slabs == 0 and g_off % slabs == 0 and k_off % slabs == 0

    def operand(arr, off, conv):
        specs, args = [], []
        for s in range(slabs):
            specs.append(pl.BlockSpec((seq, LANES), lambda j, b, s=s: (b, off + j * slabs + s)))
            args.append(arr)
            if conv:
                specs += [pl.BlockSpec((3, LANES), lambda j, b, s=s: (0, off + j * slabs + s)),
                          pl.BlockSpec((1, LANES), lambda j, b, s=s: (0, off + j * slabs + s))]
                args += list(short)
        return specs, args

    z_specs, z_args = operand(z, z_off, conv_z)
    g_specs, g_args = operand(gate, g_off, conv_g)
    k_blk = k_off // slabs
    return pl.pallas_call(
        functools.partial(_long_conv_kernel, n1=n1, conv_z=conv_z, conv_g=conv_g),
        grid=(D_HYENA // width, batch),
        in_specs=z_specs + g_specs + [
                  pl.BlockSpec((n, width), lambda j, b: (0, k_blk + j)),
                  pl.BlockSpec((1, width), lambda j, b: (0, j)),
                  _const_spec((n1, h1)), _const_spec((h1, n1)),
                  _const_spec((h1, 2 * FFT_N2, 2 * FFT_N2)),
                  _const_spec((h1, 2 * FFT_N2, 2 * FFT_N2))],
        out_specs=pl.BlockSpec((seq, width), lambda j, b: (b, j)),
        out_shape=jax.ShapeDtypeStruct((batch * seq, D_HYENA), out_dtype),
        scratch_shapes=[pltpu.VMEM((slabs, h1 * A_PITCH, LANES), F32),
                        pltpu.VMEM((slabs, h1 * T_PITCH, LANES), F32)],
        compiler_params=pltpu.CompilerParams(
            dimension_semantics=("parallel", "parallel"), vmem_limit_bytes=VMEM_LIMIT),
        name="hyena_long_conv",
    )(*z_args, *g_args, kspec, bias, _bf16_const(tb['m1']), _bf16_const(tb['m1_inv']),
      _bf16_const(tb['m2']), _bf16_const(tb['m2_inv']))


def _layer(x3, p):
    batch, seq, _ = x3.shape
    x = x3.reshape(batch * seq, D_MODEL)
    x = _ffn(x, p['ffn1_pre_g'], p['ffn1_w1'], p['ffn1_w3'], p['ffn1_w2'], p['ffn1_post_g'])
    hy, qkv, gate = _inproj(x, p['mix_pre_g'], p['w_in'], seq)
    kspec = _filter_spectra(seq, p)
    tiles = D_HYENA // LANES
    short = (p['short_w'], p['short_b'])
    z1 = _long_conv(hy, 0, hy, tiles, kspec, 0, p['hyena_bias'][0:1], batch, seq, short=short, conv_z=True)
    yh = _long_conv(z1, 0, hy, 2 * tiles, kspec, tiles, p['hyena_bias'][1:2], batch, seq, short=short,
                    out_dtype=BF16)
    yr = _retention(qkv, gate, p['ret_log_decay_f'], p['ret_log_decay_b'], batch, seq)
    mix = (yh, yr, p['w_out'], p['mix_post_g'])
    x = _ffn(x, p['ffn2_pre_g'], p['ffn2_w1'], p['ffn2_w3'], p['ffn2_w2'], p['ffn2_post_g'], mix=mix)
    return x.reshape(batch, seq, D_MODEL)


_MATRIX_PARAMS = ('ffn1_w1', 'ffn1_w3', 'ffn1_w2', 'w_in', 'w_out', 'ffn2_w1', 'ffn2_w3', 'ffn2_w2')
_GAIN_PARAMS = ('ffn1_pre_g', 'ffn1_post_g', 'mix_pre_g', 'mix_post_g', 'ffn2_pre_g', 'ffn2_post_g')


def kernel(x_prompt, x_sample, ffn1_pre_g, ffn1_w1, ffn1_w3, ffn1_w2, ffn1_post_g, mix_pre_g, w_in, short_w, short_b, filt_w1, filt_b1, filt_w2, filt_b2, filt_w3, filt_b3, filt_w4, filt_freq, hyena_bias, ret_log_decay_f, ret_log_decay_b, w_out, mix_post_g, ffn2_pre_g, ffn2_w1, ffn2_w3, ffn2_w2, ffn2_post_g):
    params = dict(ffn1_pre_g=ffn1_pre_g, ffn1_w1=ffn1_w1, ffn1_w3=ffn1_w3, ffn1_w2=ffn1_w2,
                  ffn1_post_g=ffn1_post_g, mix_pre_g=mix_pre_g, w_in=w_in, short_w=short_w,
                  short_b=short_b, filt_w1=filt_w1, filt_b1=filt_b1, filt_w2=filt_w2,
                  filt_b2=filt_b2, filt_w3=filt_w3, filt_b3=filt_b3, filt_w4=filt_w4,
                  filt_freq=filt_freq, hyena_bias=hyena_bias, ret_log_decay_f=ret_log_decay_f,
                  ret_log_decay_b=ret_log_decay_b, w_out=w_out, mix_post_g=mix_post_g,
                  ffn2_pre_g=ffn2_pre_g, ffn2_w1=ffn2_w1, ffn2_w3=ffn2_w3, ffn2_w2=ffn2_w2,
                  ffn2_post_g=ffn2_post_g)
    depth = ffn1_w1.shape[0]

    def run(x):
        for l in range(depth):
            p = {k: v[l] for k, v in params.items()}
            for k in _MATRIX_PARAMS:
                p[k] = p[k].astype(BF16)
            for k in _GAIN_PARAMS:
                p[k] = p[k][None, :]
            p['short_b'] = p['short_b'][None, :]
            x = _layer(x, p)
        return x

    return (run(x_prompt), run(x_sample))
```

```python
import functools
import math

import numpy as np
import jax
import jax.numpy as jnp
from jax import lax
from jax.experimental import pallas as pl
from jax.experimental.pallas import tpu as pltpu

F32 = jnp.float32
BF16 = jnp.bfloat16

D_MODEL = 1024
D_HYENA = 512
D_RET = 512
HYENA_ORDER = 2
N_RET_HEADS = 4
RET_HEAD_DIM = 128
D_FF = 2816
FILT_EMB = 33
FILT_BANDS = 16
FILT_HIDDEN = 64
ROPE_BASE = 10000.0
NORM_EPS = 1e-6
HYENA_TARGET = 1e-2
FAST_DECAY_PCT = 0.3
SLOW_DECAY_PCT = 1.5
N_HY_COLS = (HYENA_ORDER + 1) * D_HYENA
D_IN = N_HY_COLS + 4 * D_RET

LANES = 128
MXU_COLUMNS = 256
VMEM_LIMIT = 56 * 1024 * 1024
VMEM_HEADROOM = 4 * 1024 * 1024
FFT_N2 = 32
RET_CHUNK = 128
RET_GROUP = 32
TOKEN_TILE = 512


def _const_spec(shape):
    nd = len(shape)
    return pl.BlockSpec(shape, lambda *_: (0,) * nd, pipeline_mode=pl.Buffered(1))


def _rms(x, g):
    ms = jnp.mean(x * x, axis=-1, keepdims=True)
    return x * lax.rsqrt(ms + NORM_EPS) * g


def _bdot(a, b):
    return jnp.dot(a, b, preferred_element_type=F32)


def _ffn_core(x, pre_ref, w1_ref, w3_ref, w2_ref, post_ref, o_ref):
    h = _rms(x, pre_ref[...]).astype(BF16)
    a = _bdot(h, w1_ref[...])
    b = _bdot(h, w3_ref[...])
    g = (a * jax.nn.sigmoid(a) * b).astype(BF16)
    y = _bdot(g, w2_ref[...])
    o_ref[...] = x + 0.5 * _rms(y, post_ref[...])


def _ffn_kernel(x_ref, pre_ref, w1_ref, w3_ref, w2_ref, post_ref, o_ref):
    _ffn_core(x_ref[...], pre_ref, w1_ref, w3_ref, w2_ref, post_ref, o_ref)


def _mix_ffn_kernel(x_ref, yh_ref, yr_ref, wo_ref, mg_ref,
                    pre_ref, w1_ref, w3_ref, w2_ref, post_ref, o_ref):
    mixed = jnp.concatenate([yh_ref[...].astype(BF16), yr_ref[...].astype(BF16)], axis=1)
    y = _bdot(mixed, wo_ref[...])
    x = x_ref[...] + _rms(y, mg_ref[...])
    _ffn_core(x, pre_ref, w1_ref, w3_ref, w2_ref, post_ref, o_ref)


def _row_spec(tm, width):
    return pl.BlockSpec((tm, width), lambda i: (i, 0))


def _ffn(x, pre_g, w1, w3, w2, post_g, mix=None):
    t = x.shape[0]
    tm = TOKEN_TILE
    ffn_specs = [_const_spec((1, D_MODEL)), _const_spec((D_MODEL, D_FF)),
                 _const_spec((D_MODEL, D_FF)), _const_spec((D_FF, D_MODEL)),
                 _const_spec((1, D_MODEL))]
    ffn_args = (pre_g, w1, w3, w2, post_g)
    if mix is None:
        body, args = _ffn_kernel, (x,) + ffn_args
        specs = [_row_spec(tm, D_MODEL)] + ffn_specs
    else:
        yh, yr, w_out, mg = mix
        body, args = _mix_ffn_kernel, (x, yh, yr, w_out, mg) + ffn_args
        specs = [_row_spec(tm, D_MODEL), _row_spec(tm, D_HYENA), _row_spec(tm, D_RET),
                 _const_spec((D_HYENA + D_RET, D_MODEL)), _const_spec((1, D_MODEL))] + ffn_specs
    return pl.pallas_call(
        body,
        grid=(t // tm,),
        in_specs=specs,
        out_specs=_row_spec(tm, D_MODEL),
        out_shape=jax.ShapeDtypeStruct((t, D_MODEL), F32),
        compiler_params=pltpu.CompilerParams(
            dimension_semantics=("parallel",), vmem_limit_bytes=VMEM_LIMIT),
        name="ffn_mix" if mix is not None else "ffn",
    )(*args)


def _inproj_kernel(x_ref, g_ref, w_ref, cc_ref, ss_ref, hy_ref, qkv_ref, gate_ref):
    h = _rms(x_ref[...], g_ref[...]).astype(BF16)
    qk0 = N_HY_COLS
    v0 = qk0 + 2 * D_RET
    g0 = v0 + D_RET
    qk = _bdot(h, w_ref[:, qk0:v0])
    hy_ref[...] = _bdot(h, w_ref[:, :qk0])
    gate_ref[...] = _bdot(h, w_ref[:, g0:])
    qkv_ref[:, 2 * D_RET:] = _bdot(h, w_ref[:, v0:g0]).astype(BF16)
    cc = cc_ref[...]
    ss = ss_ref[...]
    d = RET_HEAD_DIM
    for blk in range(2 * N_RET_HEADS):
        x = qk[:, blk * d:(blk + 1) * d]
        r = x * cc + pltpu.roll(x, d // 2, axis=1) * ss
        if blk >= N_RET_HEADS:
            r = r * (d ** -0.5)
        qkv_ref[:, blk * d:(blk + 1) * d] = r.astype(BF16)


def _inproj(x, g, w, seq):
    t = x.shape[0]
    tm = TOKEN_TILE
    cc, ss = _rope_tables(seq)
    pos_blocks = seq // tm
    rope_spec = pl.BlockSpec((tm, RET_HEAD_DIM), lambda i: (i % pos_blocks, 0))
    return pl.pallas_call(
        _inproj_kernel,
        grid=(t // tm,),
        in_specs=[_row_spec(tm, D_MODEL), _const_spec((1, D_MODEL)),
                  _const_spec((D_MODEL, D_IN)), rope_spec, rope_spec],
        out_specs=[_row_spec(tm, N_HY_COLS), _row_spec(tm, 3 * D_RET), _row_spec(tm, D_RET)],
        out_shape=[jax.ShapeDtypeStruct((t, N_HY_COLS), F32),
                   jax.ShapeDtypeStruct((t, 3 * D_RET), BF16),
                   jax.ShapeDtypeStruct((t, D_RET), F32)],
        compiler_params=pltpu.CompilerParams(
            dimension_semantics=("parallel",), vmem_limit_bytes=VMEM_LIMIT),
        name="inproj",
    )(x, g, w, cc, ss)


def _ret_kernel(lgf_ref, lgb_ref, q_ref, k_ref, v_ref, g_ref, o_ref, kt_ref, *, seq):
    c = RET_CHUNK
    d = RET_HEAD_DIM
    n_chunks = seq // c
    head = pl.program_id(1)
    lgf = jnp.full((c, d), lgf_ref[head], F32)
    lgb = jnp.full((c, d), lgb_ref[head], F32)
    row = lax.broadcasted_iota(jnp.int32, (c, d), 0).astype(F32)
    col = lax.broadcasted_iota(jnp.int32, (c, d), 1).astype(F32)
    diff = row - col
    dmat = jnp.where(diff >= 0.0, jnp.exp(jnp.maximum(diff, 0.0) * lgf),
                     jnp.exp(jnp.maximum(-diff, 0.0) * lgb))
    wq_f = jnp.exp((row + 1.0) * lgf)
    wk_f = jnp.exp((c - 1.0 - row) * lgf)
    wq_b = jnp.exp((c - row) * lgb)
    wk_b = jnp.exp(row * lgb)
    gc_f = jnp.exp(c * lgf)
    gc_b = jnp.exp(c * lgb)

    grp = RET_GROUP
    n_groups = n_chunks // grp

    def fwd(gi, state):
        rows = [pl.ds(pl.multiple_of((gi * grp + j) * c, c), c) for j in range(grp)]
        qbs, kts, vs = [], [], []
        for r in rows:
            ktb = k_ref[r, :].astype(F32).T.astype(BF16)
            kt_ref[r, :] = ktb
            qbs.append(q_ref[r, :])
            kts.append(ktb)
            vs.append(v_ref[r, :])
        scores = [_bdot(qb, ktb) for qb, ktb in zip(qbs, kts)]
        kvs = [_bdot(ktb, (v.astype(F32) * wk_f).astype(BF16)) for ktb, v in zip(kts, vs)]
        states = []
        for kv in kvs:
            states.append(state)
            state = state * gc_f + kv
        intras = [_bdot((s * dmat).astype(BF16), v) for s, v in zip(scores, vs)]
        crosses = [_bdot(qb, st.astype(BF16)) for qb, st in zip(qbs, states)]
        for r, intra, cross in zip(rows, intras, crosses):
            o_ref[r, :] = intra + wq_f * cross
        return state

    lax.fori_loop(0, n_groups, fwd, jnp.zeros((d, d), F32))

    def bwd(gi, state):
        rows = [pl.ds(pl.multiple_of((n_chunks - 1 - gi * grp - j) * c, c), c) for j in range(grp)]
        kvs = [_bdot(kt_ref[r, :], (v_ref[r, :].astype(F32) * wk_b).astype(BF16)) for r in rows]
        states = []
        for kv in kvs:
            states.append(state)
            state = state * gc_b + kv
        crosses = [_bdot(q_ref[r, :], st.astype(BF16)) for r, st in zip(rows, states)]
        outs = [o_ref[r, :] + wq_b * cross for r, cross in zip(rows, crosses)]
        means = [jnp.mean(o * o, axis=-1, keepdims=True) for o in outs]
        norms = [lax.rsqrt(m + NORM_EPS) for m in means]
        gates = [g_ref[r, :] for r in rows]
        gates = [g * jax.nn.sigmoid(g) for g in gates]
        for r, o, nrm, g in zip(rows, outs, norms, gates):
            o_ref[r, :] = g * (o * nrm)
        return state

    lax.fori_loop(0, n_groups, bwd, jnp.zeros((d, d), F32))


def _rope_tables(seq):
    d = RET_HEAD_DIM
    inv = 1.0 / (ROPE_BASE ** (jnp.arange(0, d, 2, dtype=F32) / d))
    ang = jnp.arange(seq, dtype=F32)[:, None] * inv[None, :]
    c, s = jnp.cos(ang), jnp.sin(ang)
    return jnp.concatenate([c, c], axis=-1), jnp.concatenate([-s, s], axis=-1)


def _retention(qkv, gate, lg_f, lg_b, batch, seq):
    assert RET_CHUNK == RET_HEAD_DIM
    heads = N_RET_HEADS

    def col(off):
        return pl.BlockSpec((seq, LANES), lambda b, h, *_: (b, off * heads + h))

    grid_spec = pltpu.PrefetchScalarGridSpec(
        num_scalar_prefetch=2,
        grid=(batch, heads),
        in_specs=[col(0), col(1), col(2), col(0)],
        out_specs=pl.BlockSpec((seq, LANES), lambda b, h, *_: (b, h)),
        scratch_shapes=[pltpu.VMEM((seq, RET_CHUNK), BF16)],
    )
    return pl.pallas_call(
        functools.partial(_ret_kernel, seq=seq),
        grid_spec=grid_spec,
        out_shape=jax.ShapeDtypeStruct((batch * seq, D_RET), F32),
        compiler_params=pltpu.CompilerParams(
            dimension_semantics=("parallel", "parallel"), vmem_limit_bytes=VMEM_LIMIT),
        name="retention",
    )(lg_f, lg_b, qkv, qkv, qkv, gate)


def _split(x):
    hi = x.astype(BF16)
    return hi, (x - hi.astype(F32)).astype(BF16)


def _dot3(a, b):
    a_hi, a_lo = a
    b_hi, b_lo = b
    return _bdot(a_hi, b_hi) + _bdot(a_hi, b_lo) + _bdot(a_lo, b_hi)


@functools.lru_cache(maxsize=None)
def _fft_tables(seq):
    n = 2 * seq
    n2 = FFT_N2
    n1 = n // n2
    h1 = n1 // 2
    f1 = np.arange(h1, dtype=np.float64) + 0.5
    th = 2.0 * np.pi * f1[:, None] * np.arange(n1, dtype=np.float64)[None, :] / n1
    m1_full = np.concatenate([np.cos(th), -np.sin(th)], axis=0)
    m1 = m1_full[:, :h1]
    m1_inv = (2.0 / n) * m1.T
    t2 = np.arange(n2, dtype=np.float64)
    phi = 2.0 * np.pi * (np.arange(n2, dtype=np.float64)[None, :, None] * t2[None, None, :] / n2
                         + f1[:, None, None] * t2[None, None, :] / n)
    gr, gi = np.cos(phi), -np.sin(phi)
    m2 = np.concatenate([np.concatenate([gr, -gi], axis=2),
                         np.concatenate([gi, gr], axis=2)], axis=1)
    m2_inv = np.transpose(m2, (0, 2, 1))
    return dict(n1=n1, h1=h1, m1_full=m1_full, m1=m1, m1_inv=m1_inv, m2=m2, m2_inv=m2_inv)


OUTER_GROUP = 16
STAGE2_GROUP = 32
FILT_ROWS = 512
FILT_BLOCKS = 4
FILT_PACK = LANES // 2
PITCH_ALIGN = 4
A_PITCH = 2 * FFT_N2 + PITCH_ALIGN
T_PITCH = FFT_N2 + PITCH_ALIGN


def _filter_kernel(z_ref, w1_ref, b1_ref, w2_ref, b2_ref, w3_ref, b3_ref, fr_ref,
                   w4_ref, dl_ref, o_ref, sum_ref, *, nsteps):
    i = pl.program_id(0)
    half = FILT_ROWS // 2
    fr = fr_ref[...]
    zs = [z_ref[b * half:(b + 1) * half, :] for b in range(FILT_BLOCKS)]

    def layer(xs, w_ref, b_ref):
        w = _split(w_ref[...])
        pre = [_dot3(_split(x), w) for x in xs]
        return [jnp.sin(fr * (v + b_ref[...])) for v in pre]

    hs = layer(layer(layer(zs, w1_ref, b1_ref), w2_ref, b2_ref), w3_ref, b3_ref)
    hs = [_split(h) for h in hs]
    adl = jnp.abs(dl_ref[...])
    backward = i >= nsteps
    for side in range(2):
        lane = side * FILT_PACK
        ts = [jnp.broadcast_to(z[:, lane:lane + 1], z.shape) for z in zs]
        for c in range(HYENA_ORDER * D_HYENA // LANES):
            cols = slice(c * LANES, (c + 1) * LANES)
            w4 = _split(w4_ref[0, side, :, cols])
            hcs = [_dot3(h, w4) * jnp.exp(-t * adl[:, cols]) for h, t in zip(hs, ts)]
            part = sum(jnp.sum(jnp.abs(hc), axis=0, keepdims=True) for hc in hcs)

            if side == 0:
                @pl.when(i % nsteps == 0)
                def _():
                    sum_ref[0, :, cols] = part

                @pl.when(i % nsteps != 0)
                def _():
                    sum_ref[0, :, cols] = sum_ref[0, :, cols] + part
            else:
                sum_ref[0, :, cols] = sum_ref[0, :, cols] + part

            for b, hc in enumerate(hcs):
                if side == 0 and b == 0:
                    rows = lax.broadcasted_iota(jnp.int32, hc.shape, 0)
                    flipped = jnp.where(jnp.logical_and(rows == 0, i == nsteps), 0.0, -hc)
                else:
                    flipped = -hc
                r0 = b * FILT_ROWS + side * half
                o_ref[r0:r0 + half, cols] = jnp.where(backward, flipped, hc)


def _kspec_kernel(k_ref, s_ref, m1_ref, m2_ref, o_ref, a_ref, kp_ref, *, n1):
    n2 = FFT_N2
    h1 = n1 // 2
    inv_f = 1.0 / s_ref[0]
    inv_b = 1.0 / s_ref[1]

    def pad(t1, carry):
        src = pl.multiple_of(t1 * n2, n2)
        dst = pl.multiple_of(t1 * T_PITCH, PITCH_ALIGN)
        kp_ref[pl.ds(dst, n2), :] = k_ref[pl.ds(src, n2), :] * jnp.where(t1 < h1, inv_f, inv_b)
        return carry

    lax.fori_loop(0, n1, pad, 0, unroll=8)

    wide = MXU_COLUMNS // LANES

    def stage1(gi, carry):
        sets = [[(gi * OUTER_GROUP + u) * wide + j for j in range(wide)] for u in range(OUTER_GROUP)]
        kts = [jnp.concatenate([kp_ref[pl.ds(t2, n1, stride=T_PITCH), :] for t2 in t2s], axis=1)
               for t2s in sets]
        prods = [_bdot(m1_ref[...], kt.astype(BF16)) for kt in kts]
        for t2s, a in zip(sets, prods):
            for j, t2 in enumerate(t2s):
                cols = slice(j * LANES, (j + 1) * LANES)
                a_ref[pl.ds(t2, h1, stride=A_PITCH), :] = a[:h1, cols]
                a_ref[pl.ds(t2 + n2, h1, stride=A_PITCH), :] = a[h1:, cols]
        return carry

    lax.fori_loop(0, n2 // (wide * OUTER_GROUP), stage1, 0)

    def stage2(gi, carry):
        f1s = [gi * STAGE2_GROUP + j for j in range(STAGE2_GROUP)]
        srcs = [pl.ds(pl.multiple_of(f1 * A_PITCH, PITCH_ALIGN), 2 * n2) for f1 in f1s]
        specs = [_bdot(m2_ref[f1], a_ref[src, :].astype(BF16)) for f1, src in zip(f1s, srcs)]
        for f1, spec in zip(f1s, specs):
            dst = pl.multiple_of(f1 * 2 * n2, 2 * n2)
            o_ref[pl.ds(dst, 2 * n2), :] = spec.astype(o_ref.dtype)
        return carry

    lax.fori_loop(0, h1 // STAGE2_GROUP, stage2, 0)


def _filter_spectra(seq, p):
    n = 2 * seq
    tb = _fft_tables(seq)
    nblk = seq // FILT_ROWS
    half = FILT_ROWS // 2
    pack = FILT_PACK

    def packed(v):
        return jnp.transpose(v.reshape(2 * nblk, 2, half), (0, 2, 1)).reshape(n // 2, 2, 1)

    fwd_pos = jnp.arange(seq, dtype=jnp.int32)
    pos = jnp.concatenate([fwd_pos, (seq - fwd_pos) % seq])
    t_fwd = jnp.linspace(0.0, 1.0, seq, dtype=F32)
    t = packed(jnp.concatenate([t_fwd, jnp.roll(t_fwd[::-1], 1)]))
    w = packed(2.0 * math.pi * pos.astype(F32) / seq)
    fb = jnp.linspace(1e-4, FILT_BANDS - 1, FILT_BANDS, dtype=F32)[None, None, :]
    zero = jnp.zeros((n // 2, 2, pack - FILT_EMB), F32)
    zz = jnp.concatenate([t, jnp.cos(fb * w), -jnp.sin(fb * w), zero], axis=-1).reshape(n // 2, LANES)

    def padw(a):
        a = jnp.pad(a, ((0, pack - a.shape[0]), (0, pack - a.shape[1])))
        zero = jnp.zeros_like(a)
        return jnp.concatenate([jnp.concatenate([a, zero], axis=1),
                                jnp.concatenate([zero, a], axis=1)], axis=0)

    def padv(a):
        return jnp.tile(jnp.pad(a, (0, pack - a.shape[0])), 2)[None, :]

    w1 = padw(p['filt_w1'])
    w2 = padw(p['filt_w2'])
    w3 = padw(p['filt_w3'])
    w4 = jnp.pad(p['filt_w4'], ((0, pack - FILT_HIDDEN), (0, 0))).reshape(pack, HYENA_ORDER, 2, D_HYENA)
    w4 = jnp.transpose(w4, (2, 0, 1, 3)).reshape(2, pack, HYENA_ORDER * D_HYENA)
    w4 = jnp.stack([jnp.pad(w4, ((0, 0), (0, pack), (0, 0))),
                    jnp.pad(w4, ((0, 0), (pack, 0), (0, 0)))], axis=1)
    min_decay = math.log(HYENA_TARGET) / SLOW_DECAY_PCT
    max_decay = math.log(HYENA_TARGET) / FAST_DECAY_PCT
    deltas = jnp.linspace(min_decay, max_decay, D_HYENA, dtype=F32)
    deltas = jnp.tile(deltas, HYENA_ORDER)[None, :]
    width = HYENA_ORDER * D_HYENA
    sq = _const_spec((LANES, LANES))
    vec = _const_spec((1, LANES))
    nsteps = nblk // FILT_BLOCKS
    filt, sums = pl.pallas_call(
        functools.partial(_filter_kernel, nsteps=nsteps),
        grid=(2 * nsteps,),
        in_specs=[pl.BlockSpec((FILT_BLOCKS * half, LANES), lambda i: (i, 0)),
                  sq, vec, sq, vec, sq, vec, vec,
                  pl.BlockSpec((1, 2, LANES, width), lambda i: (i // nsteps, 0, 0, 0)),
                  _const_spec((1, width))],
        out_specs=[pl.BlockSpec((FILT_BLOCKS * FILT_ROWS, width), lambda i: (i, 0)),
                   pl.BlockSpec((1, 1, width), lambda i: (i // nsteps, 0, 0))],
        out_shape=[jax.ShapeDtypeStruct((n, width), F32),
                   jax.ShapeDtypeStruct((2, 1, width), F32)],
        compiler_params=pltpu.CompilerParams(
            dimension_semantics=("arbitrary",), vmem_limit_bytes=VMEM_LIMIT),
        name="hyena_filter",
    )(zz, w1, padv(p['filt_b1']), w2, padv(p['filt_b2']), w3, padv(p['filt_b3']),
      padv(p['filt_freq']), w4, deltas)
    n1, h1 = tb['n1'], tb['h1']
    return pl.pallas_call(
        functools.partial(_kspec_kernel, n1=n1),
        grid=(width // LANES,),
        in_specs=[pl.BlockSpec((n, LANES), lambda j: (0, j)),
                  pl.BlockSpec((2, 1, LANES), lambda j: (0, 0, j)),
                  _const_spec((n1, n1)), _const_spec((h1, 2 * FFT_N2, 2 * FFT_N2))],
        out_specs=pl.BlockSpec((n, LANES), lambda j: (0, j)),
        out_shape=jax.ShapeDtypeStruct((n, width), BF16),
        scratch_shapes=[pltpu.VMEM((h1 * A_PITCH, LANES), F32),
                        pltpu.VMEM((n1 * T_PITCH, LANES), F32)],
        compiler_params=pltpu.CompilerParams(
            dimension_semantics=("parallel",), vmem_limit_bytes=VMEM_LIMIT),
        name="hyena_filter_spectrum",
    )(filt, sums, _bf16_const(tb['m1_full']), _bf16_const(tb['m2']))


CHUNK_UNROLL = 8


def _for_chunks(count, body):
    body(0, True, False)

    def step(t1, carry):
        body(t1, False, False)
        return carry

    lax.fori_loop(1, count - CHUNK_UNROLL + 1, step, 0, unroll=CHUNK_UNROLL)
    for t1 in range(count - CHUNK_UNROLL + 1, count):
        body(t1, False, t1 == count - 1)


def _short_conv_rows(ref, w_ref, b_ref, r0, nrows, first, last):
    x = ref[pl.ds(r0, nrows), :]
    rid = lax.broadcasted_iota(jnp.int32, x.shape, 0)
    if first:
        prev = jnp.where(rid == 0, 0.0, pltpu.roll(x, 1, axis=0))
    else:
        prev = ref[pl.ds(r0 - 1, nrows), :]
    if last:
        nxt = jnp.where(rid == nrows - 1, 0.0, pltpu.roll(x, nrows - 1, axis=0))
    else:
        nxt = ref[pl.ds(r0 + 1, nrows), :]
    return prev * w_ref[0:1, :] + x * w_ref[1:2, :] + nxt * w_ref[2:3, :] + b_ref[...]


def _long_conv_kernel(*refs, n1, conv_z, conv_g):
    refs = list(refs)
    a_ref, tp_ref = refs[-2:]
    slabs = a_ref.shape[0]

    def take_operand(conv):
        return [tuple(refs.pop(0) for _ in range(3 if conv else 1)) for _ in range(slabs)]

    z_ops = take_operand(conv_z)
    g_ops = take_operand(conv_g)
    k_ref, bias_ref, m1_ref, m1i_ref, m2_ref, m2i_ref, o_ref = refs[:-2]
    n2 = FFT_N2
    h1 = n1 // 2

    def put(ref, idx, x):
        for s in range(slabs):
            ref[s, idx, :] = x[:, s * LANES:(s + 1) * LANES]

    def get(ref, idx):
        return jnp.concatenate([ref[s, idx, :] for s in range(slabs)], axis=1)

    def rows_of(ops, conv, t1, first, last):
        r0 = t1 * n2 if isinstance(t1, int) else pl.multiple_of(t1 * n2, n2)
        if conv:
            parts = [_short_conv_rows(ref, w_ref, b_ref, r0, n2, first, last) for ref, w_ref, b_ref in ops]
        else:
            parts = [ref[pl.ds(r0, n2), :] for ref, in ops]
        return jnp.concatenate(parts, axis=1)

    def z_rows(t1, first, last):
        return rows_of(z_ops, conv_z, t1, first, last)

    def g_rows(t1, first, last):
        return rows_of(g_ops, conv_g, t1, first, last)

    def pitched(t1):
        r0 = t1 * T_PITCH
        return pl.ds(r0 if isinstance(t1, int) else pl.multiple_of(r0, PITCH_ALIGN), n2)

    def pad(t1, first, last):
        put(tp_ref, pitched(t1), z_rows(t1, first, last))

    _for_chunks(h1, pad)

    wide = max(1, MXU_COLUMNS // (slabs * LANES))
    width = slabs * LANES

    def t2_sets(gi):
        return [[(gi * OUTER_GROUP + u) * wide + j for j in range(wide)] for u in range(OUTER_GROUP)]

    def stage1(gi, carry):
        sets = t2_sets(gi)
        zts = [jnp.concatenate([get(tp_ref, pl.ds(t2, h1, stride=T_PITCH)) for t2 in t2s], axis=1)
               for t2s in sets]
        prods = [_bdot(m1_ref[...], zt.astype(BF16)) for zt in zts]
        for t2s, a in zip(sets, prods):
            for j, t2 in enumerate(t2s):
                cols = slice(j * width, (j + 1) * width)
                put(a_ref, pl.ds(t2, h1, stride=A_PITCH), a[:h1, cols])
                put(a_ref, pl.ds(t2 + n2, h1, stride=A_PITCH), a[h1:, cols])
        return carry

    lax.fori_loop(0, n2 // (wide * OUTER_GROUP), stage1, 0)

    group = STAGE2_GROUP // slabs

    def stage2(gi, carry):
        f1s = [gi * group + j for j in range(group)]
        ras = [pl.ds(pl.multiple_of(f1 * A_PITCH, PITCH_ALIGN), 2 * n2) for f1 in f1s]
        xs = [_bdot(m2_ref[f1], get(a_ref, ra).astype(BF16)) for f1, ra in zip(f1s, ras)]
        ys = []
        for f1, x in zip(f1s, xs):
            kk = k_ref[pl.ds(pl.multiple_of(f1 * 2 * n2, 2 * n2), 2 * n2), :].astype(F32)
            xr, xi, kr, ki = x[:n2], x[n2:], kk[:n2], kk[n2:]
            ys.append(jnp.concatenate([xr * kr - xi * ki, xr * ki + xi * kr], axis=0).astype(BF16))
        outs = [_bdot(m2i_ref[f1], y) for f1, y in zip(f1s, ys)]
        for ra, out in zip(ras, outs):
            put(a_ref, ra, out)
        return carry

    lax.fori_loop(0, h1 // group, stage2, 0)

    def stage3(gi, carry):
        sets = t2_sets(gi)
        specs = []
        for t2s in sets:
            br = jnp.concatenate([get(a_ref, pl.ds(t2, h1, stride=A_PITCH)) for t2 in t2s], axis=1)
            bi = jnp.concatenate([get(a_ref, pl.ds(t2 + n2, h1, stride=A_PITCH)) for t2 in t2s], axis=1)
            specs.append(jnp.concatenate([br, bi], axis=0).astype(BF16))
        prods = [_bdot(m1i_ref[...], sp) for sp in specs]
        for t2s, y in zip(sets, prods):
            for j, t2 in enumerate(t2s):
                put(tp_ref, pl.ds(t2, h1, stride=T_PITCH), y[:, j * width:(j + 1) * width])
        return carry

    lax.fori_loop(0, n2 // (wide * OUTER_GROUP), stage3, 0)

    bias = bias_ref[...]

    def finish(t1, first, last):
        r0 = t1 * n2 if isinstance(t1, int) else pl.multiple_of(t1 * n2, n2)
        y = get(tp_ref, pitched(t1))
        out = g_rows(t1, first, last) * (y + bias * z_rows(t1, first, last))
        o_ref[pl.ds(r0, n2), :] = out.astype(o_ref.dtype)

    _for_chunks(h1, finish)


def _bf16_const(a):
    return jnp.asarray(a, F32).astype(BF16)


def _long_conv_slabs(seq):
    n1 = 2 * seq // FFT_N2
    h1 = n1 // 2
    tables = 2 * (2 * n1 * h1 + 2 * h1 * (2 * FFT_N2) ** 2)
    for slabs in (2, 1):
        io = 3 * 2 * seq * 4
        per_lane = io + 2 * 2 * seq * 2 + h1 * (A_PITCH + T_PITCH) * 4
        if per_lane * slabs * LANES + tables <= VMEM_LIMIT - VMEM_HEADROOM:
            return slabs
    raise ValueError(f"long conv of length {seq} does not fit VMEM")


def _long_conv(z, z_off, gate, g_off, kspec, k_off, bias, batch, seq, short=None, conv_z=False,
               out_dtype=F32):
    n = 2 * seq
    tb = _fft_tables(seq)
    n1, h1 = tb['n1'], tb['h1']
    conv_g = short is not None
    slabs = _long_conv_slabs(seq)
    width = slabs * LANES
    assert z_off % slabs == 0 and g_off % slabs == 0 and k_off % slabs == 0

    def operand(arr, off, conv):
        specs, args = [], []
        for s in range(slabs):
            specs.append(pl.BlockSpec((seq, LANES), lambda j, b, s=s: (b, off + j * slabs + s)))
            args.append(arr)
            if conv:
                specs += [pl.BlockSpec((3, LANES), lambda j, b, s=s: (0, off + j * slabs + s)),
                          pl.BlockSpec((1, LANES), lambda j, b, s=s: (0, off + j * slabs + s))]
                args += list(short)
        return specs, args

    z_specs, z_args = operand(z, z_off, conv_z)
    g_specs, g_args = operand(gate, g_off, conv_g)
    k_blk = k_off // slabs
    return pl.pallas_call(
        functools.partial(_long_conv_kernel, n1=n1, conv_z=conv_z, conv_g=conv_g),
        grid=(D_HYENA // width, batch),
        in_specs=z_specs + g_specs + [
                  pl.BlockSpec((n, width), lambda j, b: (0, k_blk + j)),
                  pl.BlockSpec((1, width), lambda j, b: (0, j)),
                  _const_spec((n1, h1)), _const_spec((h1, n1)),
                  _const_spec((h1, 2 * FFT_N2, 2 * FFT_N2)),
                  _const_spec((h1, 2 * FFT_N2, 2 * FFT_N2))],
        out_specs=pl.BlockSpec((seq, width), lambda j, b: (b, j)),
        out_shape=jax.ShapeDtypeStruct((batch * seq, D_HYENA), out_dtype),
        scratch_shapes=[pltpu.VMEM((slabs, h1 * A_PITCH, LANES), F32),
                        pltpu.VMEM((slabs, h1 * T_PITCH, LANES), F32)],
        compiler_params=pltpu.CompilerParams(
            dimension_semantics=("parallel", "parallel"), vmem_limit_bytes=VMEM_LIMIT),
        name="hyena_long_conv",
    )(*z_args, *g_args, kspec, bias, _bf16_const(tb['m1']), _bf16_const(tb['m1_inv']),
      _bf16_const(tb['m2']), _bf16_const(tb['m2_inv']))


def _layer(x3, p):
    batch, seq, _ = x3.shape
    x = x3.reshape(batch * seq, D_MODEL)
    x = _ffn(x, p['ffn1_pre_g'], p['ffn1_w1'], p['ffn1_w3'], p['ffn1_w2'], p['ffn1_post_g'])
    hy, qkv, gate = _inproj(x, p['mix_pre_g'], p['w_in'], seq)
    kspec = _filter_spectra(seq, p)
    tiles = D_HYENA // LANES
    short = (p['short_w'], p['short_b'])
    z1 = _long_conv(hy, 0, hy, tiles, kspec, 0, p['hyena_bias'][0:1], batch, seq, short=short, conv_z=True)
    yh = _long_conv(z1, 0, hy, 2 * tiles, kspec, tiles, p['hyena_bias'][1:2], batch, seq, short=short,
                    out_dtype=BF16)
    yr = _retention(qkv, gate, p['ret_log_decay_f'], p['ret_log_decay_b'], batch, seq)
    mix = (yh, yr, p['w_out'], p['mix_post_g'])
    x = _ffn(x, p['ffn2_pre_g'], p['ffn2_w1'], p['ffn2_w3'], p['ffn2_w2'], p['ffn2_post_g'], mix=mix)
    return x.reshape(batch, seq, D_MODEL)


_MATRIX_PARAMS = ('ffn1_w1', 'ffn1_w3', 'ffn1_w2', 'w_in', 'w_out', 'ffn2_w1', 'ffn2_w3', 'ffn2_w2')
_GAIN_PARAMS = ('ffn1_pre_g', 'ffn1_post_g', 'mix_pre_g', 'mix_post_g', 'ffn2_pre_g', 'ffn2_post_g')


def kernel(x_prompt, x_sample, ffn1_pre_g, ffn1_w1, ffn1_w3, ffn1_w2, ffn1_post_g, mix_pre_g, w_in, short_w, short_b, filt_w1, filt_b1, filt_w2, filt_b2, filt_w3, filt_b3, filt_w4, filt_freq, hyena_bias, ret_log_decay_f, ret_log_decay_b, w_out, mix_post_g, ffn2_pre_g, ffn2_w1, ffn2_w3, ffn2_w2, ffn2_post_g):
    params = dict(ffn1_pre_g=ffn1_pre_g, ffn1_w1=ffn1_w1, ffn1_w3=ffn1_w3, ffn1_w2=ffn1_w2,
                  ffn1_post_g=ffn1_post_g, mix_pre_g=mix_pre_g, w_in=w_in, short_w=short_w,
                  short_b=short_b, filt_w1=filt_w1, filt_b1=filt_b1, filt_w2=filt_w2,
                  filt_b2=filt_b2, filt_w3=filt_w3, filt_b3=filt_b3, filt_w4=filt_w4,
                  filt_freq=filt_freq, hyena_bias=hyena_bias, ret_log_decay_f=ret_log_decay_f,
                  ret_log_decay_b=ret_log_decay_b, w_out=w_out, mix_post_g=mix_post_g,
                  ffn2_pre_g=ffn2_pre_g, ffn2_w1=ffn2_w1, ffn2_w3=ffn2_w3, ffn2_w2=ffn2_w2,
                  ffn2_post_g=ffn2_post_g)
    depth = ffn1_w1.shape[0]

    def run(x):
        for l in range(depth):
            p = {k: v[l] for k, v in params.items()}
            for k in _MATRIX_PARAMS:
                p[k] = p[k].astype(BF16)
            for k in _GAIN_PARAMS:
                p[k] = p[k][None, :]
            p['short_b'] = p['short_b'][None, :]
            x = _layer(x, p)
        return x

    return (run(x_prompt), run(x_sample))
```

```python
import functools
import math

import numpy as np
import jax
import jax.numpy as jnp
from jax import lax
from jax.experimental import pallas as pl
from jax.experimental.pallas import tpu as pltpu

F32 = jnp.float32
BF16 = jnp.bfloat16

D_MODEL = 1024
D_HYENA = 512
D_RET = 512
HYENA_ORDER = 2
N_RET_HEADS = 4
RET_HEAD_DIM = 128
D_FF = 2816
FILT_EMB = 33
FILT_BANDS = 16
FILT_HIDDEN = 64
ROPE_BASE = 10000.0
NORM_EPS = 1e-6
HYENA_TARGET = 1e-2
FAST_DECAY_PCT = 0.3
SLOW_DECAY_PCT = 1.5
N_HY_COLS = (HYENA_ORDER + 1) * D_HYENA
D_IN = N_HY_COLS + 4 * D_RET

LANES = 128
MXU_COLUMNS = 256
VMEM_LIMIT = 56 * 1024 * 1024
VMEM_HEADROOM = 4 * 1024 * 1024
FFT_N2 = 32
RET_CHUNK = 128
RET_GROUP = 32
TOKEN_TILE = 512


def _const_spec(shape):
    nd = len(shape)
    return pl.BlockSpec(shape, lambda *_: (0,) * nd, pipeline_mode=pl.Buffered(1))


def _rms(x, g):
    ms = jnp.mean(x * x, axis=-1, keepdims=True)
    return x * lax.rsqrt(ms + NORM_EPS) * g


def _bdot(a, b):
    return jnp.dot(a, b, preferred_element_type=F32)


def _ffn_core(x, pre_ref, w1_ref, w3_ref, w2_ref, post_ref, o_ref):
    h = _rms(x, pre_ref[...]).astype(BF16)
    a = _bdot(h, w1_ref[...])
    b = _bdot(h, w3_ref[...])
    g = (a * jax.nn.sigmoid(a) * b).astype(BF16)
    y = _bdot(g, w2_ref[...])
    o_ref[...] = x + 0.5 * _rms(y, post_ref[...])


def _ffn_kernel(x_ref, pre_ref, w1_ref, w3_ref, w2_ref, post_ref, o_ref):
    _ffn_core(x_ref[...], pre_ref, w1_ref, w3_ref, w2_ref, post_ref, o_ref)


def _mix_ffn_kernel(x_ref, yh_ref, yr_ref, wo_ref, mg_ref,
                    pre_ref, w1_ref, w3_ref, w2_ref, post_ref, o_ref):
    mixed = jnp.concatenate([yh_ref[...].astype(BF16), yr_ref[...].astype(BF16)], axis=1)
    y = _bdot(mixed, wo_ref[...])
    x = x_ref[...] + _rms(y, mg_ref[...])
    _ffn_core(x, pre_ref, w1_ref, w3_ref, w2_ref, post_ref, o_ref)


def _row_spec(tm, width):
    return pl.BlockSpec((tm, width), lambda i: (i, 0))


def _ffn(x, pre_g, w1, w3, w2, post_g, mix=None):
    t = x.shape[0]
    tm = TOKEN_TILE
    ffn_specs = [_const_spec((1, D_MODEL)), _const_spec((D_MODEL, D_FF)),
                 _const_spec((D_MODEL, D_FF)), _const_spec((D_FF, D_MODEL)),
                 _const_spec((1, D_MODEL))]
    ffn_args = (pre_g, w1, w3, w2, post_g)
    if mix is None:
        body, args = _ffn_kernel, (x,) + ffn_args
        specs = [_row_spec(tm, D_MODEL)] + ffn_specs
    else:
        yh, yr, w_out, mg = mix
        body, args = _mix_ffn_kernel, (x, yh, yr, w_out, mg) + ffn_args
        specs = [_row_spec(tm, D_MODEL), _row_spec(tm, D_HYENA), _row_spec(tm, D_RET),
                 _const_spec((D_HYENA + D_RET, D_MODEL)), _const_spec((1, D_MODEL))] + ffn_specs
    return pl.pallas_call(
        body,
        grid=(t // tm,),
        in_specs=specs,
        out_specs=_row_spec(tm, D_MODEL),
        out_shape=jax.ShapeDtypeStruct((t, D_MODEL), F32),
        compiler_params=pltpu.CompilerParams(
            dimension_semantics=("parallel",), vmem_limit_bytes=VMEM_LIMIT),
        name="ffn_mix" if mix is not None else "ffn",
    )(*args)


def _inproj_kernel(x_ref, g_ref, w_ref, cc_ref, ss_ref, hy_ref, qkv_ref, gate_ref):
    h = _rms(x_ref[...], g_ref[...]).astype(BF16)
    qk0 = N_HY_COLS
    v0 = qk0 + 2 * D_RET
    g0 = v0 + D_RET
    qk = _bdot(h, w_ref[:, qk0:v0])
    hy_ref[...] = _bdot(h, w_ref[:, :qk0])
    gate_ref[...] = _bdot(h, w_ref[:, g0:])
    qkv_ref[:, 2 * D_RET:] = _bdot(h, w_ref[:, v0:g0]).astype(BF16)
    cc = cc_ref[...]
    ss = ss_ref[...]
    d = RET_HEAD_DIM
    for blk in range(2 * N_RET_HEADS):
        x = qk[:, blk * d:(blk + 1) * d]
        r = x * cc + pltpu.roll(x, d // 2, axis=1) * ss
        if blk >= N_RET_HEADS:
            r = r * (d ** -0.5)
        qkv_ref[:, blk * d:(blk + 1) * d] = r.astype(BF16)


def _inproj(x, g, w, seq, rope):
    t = x.shape[0]
    tm = TOKEN_TILE
    cc, ss = rope
    assert cc.shape[0] >= seq
    pos_blocks = seq // tm
    rope_spec = pl.BlockSpec((tm, RET_HEAD_DIM), lambda i: (i % pos_blocks, 0))
    return pl.pallas_call(
        _inproj_kernel,
        grid=(t // tm,),
        in_specs=[_row_spec(tm, D_MODEL), _const_spec((1, D_MODEL)),
                  _const_spec((D_MODEL, D_IN)), rope_spec, rope_spec],
        out_specs=[_row_spec(tm, N_HY_COLS), _row_spec(tm, 3 * D_RET), _row_spec(tm, D_RET)],
        out_shape=[jax.ShapeDtypeStruct((t, N_HY_COLS), F32),
                   jax.ShapeDtypeStruct((t, 3 * D_RET), BF16),
                   jax.ShapeDtypeStruct((t, D_RET), F32)],
        compiler_params=pltpu.CompilerParams(
            dimension_semantics=("parallel",), vmem_limit_bytes=VMEM_LIMIT),
        name="inproj",
    )(x, g, w, cc, ss)


def _ret_kernel(lgf_ref, lgb_ref, q_ref, k_ref, v_ref, g_ref, o_ref, kt_ref, *, seq):
    c = RET_CHUNK
    d = RET_HEAD_DIM
    n_chunks = seq // c
    head = pl.program_id(1)
    lgf = jnp.full((c, d), lgf_ref[head], F32)
    lgb = jnp.full((c, d), lgb_ref[head], F32)
    row = lax.broadcasted_iota(jnp.int32, (c, d), 0).astype(F32)
    col = lax.broadcasted_iota(jnp.int32, (c, d), 1).astype(F32)
    diff = row - col
    dmat = jnp.where(diff >= 0.0, jnp.exp(jnp.maximum(diff, 0.0) * lgf),
                     jnp.exp(jnp.maximum(-diff, 0.0) * lgb))
    wq_f = jnp.exp((row + 1.0) * lgf)
    wk_f = jnp.exp((c - 1.0 - row) * lgf)
    wq_b = jnp.exp((c - row) * lgb)
    wk_b = jnp.exp(row * lgb)
    gc_f = jnp.exp(c * lgf)
    gc_b = jnp.exp(c * lgb)

    grp = RET_GROUP
    n_groups = n_chunks // grp

    def fwd(gi, state):
        rows = [pl.ds(pl.multiple_of((gi * grp + j) * c, c), c) for j in range(grp)]
        qbs, kts, vs = [], [], []
        for r in rows:
            ktb = k_ref[r, :].astype(F32).T.astype(BF16)
            kt_ref[r, :] = ktb
            qbs.append(q_ref[r, :])
            kts.append(ktb)
            vs.append(v_ref[r, :])
        scores = [_bdot(qb, ktb) for qb, ktb in zip(qbs, kts)]
        kvs = [_bdot(ktb, (v.astype(F32) * wk_f).astype(BF16)) for ktb, v in zip(kts, vs)]
        states = []
        for kv in kvs:
            states.append(state)
            state = state * gc_f + kv
        intras = [_bdot((s * dmat).astype(BF16), v) for s, v in zip(scores, vs)]
        crosses = [_bdot(qb, st.astype(BF16)) for qb, st in zip(qbs, states)]
        for r, intra, cross in zip(rows, intras, crosses):
            o_ref[r, :] = intra + wq_f * cross
        return state

    lax.fori_loop(0, n_groups, fwd, jnp.zeros((d, d), F32))

    def bwd(gi, state):
        rows = [pl.ds(pl.multiple_of((n_chunks - 1 - gi * grp - j) * c, c), c) for j in range(grp)]
        kvs = [_bdot(kt_ref[r, :], (v_ref[r, :].astype(F32) * wk_b).astype(BF16)) for r in rows]
        states = []
        for kv in kvs:
            states.append(state)
            state = state * gc_b + kv
        crosses = [_bdot(q_ref[r, :], st.astype(BF16)) for r, st in zip(rows, states)]
        outs = [o_ref[r, :] + wq_b * cross for r, cross in zip(rows, crosses)]
        means = [jnp.mean(o * o, axis=-1, keepdims=True) for o in outs]
        norms = [lax.rsqrt(m + NORM_EPS) for m in means]
        gates = [g_ref[r, :] for r in rows]
        gates = [g * jax.nn.sigmoid(g) for g in gates]
        for r, o, nrm, g in zip(rows, outs, norms, gates):
            o_ref[r, :] = g * (o * nrm)
        return state

    lax.fori_loop(0, n_groups, bwd, jnp.zeros((d, d), F32))


def _rope_tables(seq):
    d = RET_HEAD_DIM
    inv = 1.0 / (ROPE_BASE ** (jnp.arange(0, d, 2, dtype=F32) / d))
    ang = jnp.arange(seq, dtype=F32)[:, None] * inv[None, :]
    c, s = jnp.cos(ang), jnp.sin(ang)
    return jnp.concatenate([c, c], axis=-1), jnp.concatenate([-s, s], axis=-1)


def _retention(qkv, gate, lg_f, lg_b, batch, seq):
    assert RET_CHUNK == RET_HEAD_DIM
    heads = N_RET_HEADS

    def col(off):
        return pl.BlockSpec((seq, LANES), lambda b, h, *_: (b, off * heads + h))

    grid_spec = pltpu.PrefetchScalarGridSpec(
        num_scalar_prefetch=2,
        grid=(batch, heads),
        in_specs=[col(0), col(1), col(2), col(0)],
        out_specs=pl.BlockSpec((seq, LANES), lambda b, h, *_: (b, h)),
        scratch_shapes=[pltpu.VMEM((seq, RET_CHUNK), BF16)],
    )
    return pl.pallas_call(
        functools.partial(_ret_kernel, seq=seq),
        grid_spec=grid_spec,
        out_shape=jax.ShapeDtypeStruct((batch * seq, D_RET), F32),
        compiler_params=pltpu.CompilerParams(
            dimension_semantics=("parallel", "parallel"), vmem_limit_bytes=VMEM_LIMIT),
        name="retention",
    )(lg_f, lg_b, qkv, qkv, qkv, gate)


def _split(x):
    hi = x.astype(BF16)
    return hi, (x - hi.astype(F32)).astype(BF16)


def _dot3(a, b):
    a_hi, a_lo = a
    b_hi, b_lo = b
    return _bdot(a_hi, b_hi) + _bdot(a_hi, b_lo) + _bdot(a_lo, b_hi)


@functools.lru_cache(maxsize=None)
def _fft_tables(seq):
    n = 2 * seq
    n2 = FFT_N2
    n1 = n // n2
    h1 = n1 // 2
    f1 = np.arange(h1, dtype=np.float64) + 0.5
    th = 2.0 * np.pi * f1[:, None] * np.arange(n1, dtype=np.float64)[None, :] / n1
    m1_full = np.concatenate([np.cos(th), -np.sin(th)], axis=0)
    m1 = m1_full[:, :h1]
    m1_inv = (2.0 / n) * m1.T
    t2 = np.arange(n2, dtype=np.float64)
    phi = 2.0 * np.pi * (np.arange(n2, dtype=np.float64)[None, :, None] * t2[None, None, :] / n2
                         + f1[:, None, None] * t2[None, None, :] / n)
    gr, gi = np.cos(phi), -np.sin(phi)
    m2 = np.concatenate([np.concatenate([gr, -gi], axis=2),
                         np.concatenate([gi, gr], axis=2)], axis=1)
    m2_inv = np.transpose(m2, (0, 2, 1))
    return dict(n1=n1, h1=h1, m1_full=m1_full, m1=m1, m1_inv=m1_inv, m2=m2, m2_inv=m2_inv)


OUTER_GROUP = 16
STAGE2_GROUP = 32
FILT_ROWS = 512
FILT_BLOCKS = 4
FILT_PACK = LANES // 2
PITCH_ALIGN = 4
A_PITCH = 2 * FFT_N2 + PITCH_ALIGN
T_PITCH = FFT_N2 + PITCH_ALIGN


def _filter_kernel(z_ref, w1_ref, b1_ref, w2_ref, b2_ref, w3_ref, b3_ref, fr_ref,
                   w4_ref, dl_ref, o_ref, sum_ref, *, nsteps):
    i = pl.program_id(0)
    half = FILT_ROWS // 2
    fr = fr_ref[...]
    zs = [z_ref[b * half:(b + 1) * half, :] for b in range(FILT_BLOCKS)]

    def layer(xs, w_ref, b_ref):
        w = _split(w_ref[...])
        pre = [_dot3(_split(x), w) for x in xs]
        return [jnp.sin(fr * (v + b_ref[...])) for v in pre]

    hs = layer(layer(layer(zs, w1_ref, b1_ref), w2_ref, b2_ref), w3_ref, b3_ref)
    hs = [_split(h) for h in hs]
    adl = jnp.abs(dl_ref[...])
    backward = i >= nsteps
    for side in range(2):
        lane = side * FILT_PACK
        ts = [jnp.broadcast_to(z[:, lane:lane + 1], z.shape) for z in zs]
        for c in range(HYENA_ORDER * D_HYENA // LANES):
            cols = slice(c * LANES, (c + 1) * LANES)
            w4 = _split(w4_ref[0, side, :, cols])
            hcs = [_dot3(h, w4) * jnp.exp(-t * adl[:, cols]) for h, t in zip(hs, ts)]
            part = sum(jnp.sum(jnp.abs(hc), axis=0, keepdims=True) for hc in hcs)

            if side == 0:
                @pl.when(i % nsteps == 0)
                def _():
                    sum_ref[0, :, cols] = part

                @pl.when(i % nsteps != 0)
                def _():
                    sum_ref[0, :, cols] = sum_ref[0, :, cols] + part
            else:
                sum_ref[0, :, cols] = sum_ref[0, :, cols] + part

            for b, hc in enumerate(hcs):
                if side == 0 and b == 0:
                    rows = lax.broadcasted_iota(jnp.int32, hc.shape, 0)
                    flipped = jnp.where(jnp.logical_and(rows == 0, i == nsteps), 0.0, -hc)
                else:
                    flipped = -hc
                r0 = b * FILT_ROWS + side * half
                o_ref[r0:r0 + half, cols] = jnp.where(backward, flipped, hc)


def _kspec_kernel(k_ref, s_ref, m1_ref, m2_ref, o_ref, a_ref, kp_ref, *, n1):
    n2 = FFT_N2
    h1 = n1 // 2
    inv_f = 1.0 / s_ref[0]
    inv_b = 1.0 / s_ref[1]

    def pad(t1, carry):
        src = pl.multiple_of(t1 * n2, n2)
        dst = pl.multiple_of(t1 * T_PITCH, PITCH_ALIGN)
        kp_ref[pl.ds(dst, n2), :] = k_ref[pl.ds(src, n2), :] * jnp.where(t1 < h1, inv_f, inv_b)
        return carry

    lax.fori_loop(0, n1, pad, 0, unroll=8)

    wide = MXU_COLUMNS // LANES

    def stage1(gi, carry):
        sets = [[(gi * OUTER_GROUP + u) * wide + j for j in range(wide)] for u in range(OUTER_GROUP)]
        kts = [jnp.concatenate([kp_ref[pl.ds(t2, n1, stride=T_PITCH), :] for t2 in t2s], axis=1)
               for t2s in sets]
        prods = [_bdot(m1_ref[...], kt.astype(BF16)) for kt in kts]
        for t2s, a in zip(sets, prods):
            for j, t2 in enumerate(t2s):
                cols = slice(j * LANES, (j + 1) * LANES)
                a_ref[pl.ds(t2, h1, stride=A_PITCH), :] = a[:h1, cols]
                a_ref[pl.ds(t2 + n2, h1, stride=A_PITCH), :] = a[h1:, cols]
        return carry

    lax.fori_loop(0, n2 // (wide * OUTER_GROUP), stage1, 0)

    def stage2(gi, carry):
        f1s = [gi * STAGE2_GROUP + j for j in range(STAGE2_GROUP)]
        srcs = [pl.ds(pl.multiple_of(f1 * A_PITCH, PITCH_ALIGN), 2 * n2) for f1 in f1s]
        specs = [_bdot(m2_ref[f1], a_ref[src, :].astype(BF16)) for f1, src in zip(f1s, srcs)]
        for f1, spec in zip(f1s, specs):
            dst = pl.multiple_of(f1 * 2 * n2, 2 * n2)
            o_ref[pl.ds(dst, 2 * n2), :] = spec.astype(o_ref.dtype)
        return carry

    lax.fori_loop(0, h1 // STAGE2_GROUP, stage2, 0)


def _filter_spectra(seq, p):
    n = 2 * seq
    tb = _fft_tables(seq)
    nblk = seq // FILT_ROWS
    half = FILT_ROWS // 2
    pack = FILT_PACK

    def packed(v):
        return jnp.transpose(v.reshape(2 * nblk, 2, half), (0, 2, 1)).reshape(n // 2, 2, 1)

    fwd_pos = jnp.arange(seq, dtype=jnp.int32)
    pos = jnp.concatenate([fwd_pos, (seq - fwd_pos) % seq])
    t_fwd = jnp.linspace(0.0, 1.0, seq, dtype=F32)
    t = packed(jnp.concatenate([t_fwd, jnp.roll(t_fwd[::-1], 1)]))
    w = packed(2.0 * math.pi * pos.astype(F32) / seq)
    fb = jnp.linspace(1e-4, FILT_BANDS - 1, FILT_BANDS, dtype=F32)[None, None, :]
    zero = jnp.zeros((n // 2, 2, pack - FILT_EMB), F32)
    zz = jnp.concatenate([t, jnp.cos(fb * w), -jnp.sin(fb * w), zero], axis=-1).reshape(n // 2, LANES)

    def padw(a):
        a = jnp.pad(a, ((0, pack - a.shape[0]), (0, pack - a.shape[1])))
        zero = jnp.zeros_like(a)
        return jnp.concatenate([jnp.concatenate([a, zero], axis=1),
                                jnp.concatenate([zero, a], axis=1)], axis=0)

    def padv(a):
        return jnp.tile(jnp.pad(a, (0, pack - a.shape[0])), 2)[None, :]

    w1 = padw(p['filt_w1'])
    w2 = padw(p['filt_w2'])
    w3 = padw(p['filt_w3'])
    w4 = jnp.pad(p['filt_w4'], ((0, pack - FILT_HIDDEN), (0, 0))).reshape(pack, HYENA_ORDER, 2, D_HYENA)
    w4 = jnp.transpose(w4, (2, 0, 1, 3)).reshape(2, pack, HYENA_ORDER * D_HYENA)
    w4 = jnp.stack([jnp.pad(w4, ((0, 0), (0, pack), (0, 0))),
                    jnp.pad(w4, ((0, 0), (pack, 0), (0, 0)))], axis=1)
    min_decay = math.log(HYENA_TARGET) / SLOW_DECAY_PCT
    max_decay = math.log(HYENA_TARGET) / FAST_DECAY_PCT
    deltas = jnp.linspace(min_decay, max_decay, D_HYENA, dtype=F32)
    deltas = jnp.tile(deltas, HYENA_ORDER)[None, :]
    width = HYENA_ORDER * D_HYENA
    sq = _const_spec((LANES, LANES))
    vec = _const_spec((1, LANES))
    nsteps = nblk // FILT_BLOCKS
    filt, sums = pl.pallas_call(
        functools.partial(_filter_kernel, nsteps=nsteps),
        grid=(2 * nsteps,),
        in_specs=[pl.BlockSpec((FILT_BLOCKS * half, LANES), lambda i: (i, 0)),
                  sq, vec, sq, vec, sq, vec, vec,
                  pl.BlockSpec((1, 2, LANES, width), lambda i: (i // nsteps, 0, 0, 0)),
                  _const_spec((1, width))],
        out_specs=[pl.BlockSpec((FILT_BLOCKS * FILT_ROWS, width), lambda i: (i, 0)),
                   pl.BlockSpec((1, 1, width), lambda i: (i // nsteps, 0, 0))],
        out_shape=[jax.ShapeDtypeStruct((n, width), F32),
                   jax.ShapeDtypeStruct((2, 1, width), F32)],
        compiler_params=pltpu.CompilerParams(
            dimension_semantics=("arbitrary",), vmem_limit_bytes=VMEM_LIMIT),
        name="hyena_filter",
    )(zz, w1, padv(p['filt_b1']), w2, padv(p['filt_b2']), w3, padv(p['filt_b3']),
      padv(p['filt_freq']), w4, deltas)
    n1, h1 = tb['n1'], tb['h1']
    return pl.pallas_call(
        functools.partial(_kspec_kernel, n1=n1),
        grid=(width // LANES,),
        in_specs=[pl.BlockSpec((n, LANES), lambda j: (0, j)),
                  pl.BlockSpec((2, 1, LANES), lambda j: (0, 0, j)),
                  _const_spec((n1, n1)), _const_spec((h1, 2 * FFT_N2, 2 * FFT_N2))],
        out_specs=pl.BlockSpec((n, LANES), lambda j: (0, j)),
        out_shape=jax.ShapeDtypeStruct((n, width), BF16),
        scratch_shapes=[pltpu.VMEM((h1 * A_PITCH, LANES), F32),
                        pltpu.VMEM((n1 * T_PITCH, LANES), F32)],
        compiler_params=pltpu.CompilerParams(
            dimension_semantics=("parallel",), vmem_limit_bytes=VMEM_LIMIT),
        name="hyena_filter_spectrum",
    )(filt, sums, _bf16_const(tb['m1_full']), _bf16_const(tb['m2']))


CHUNK_UNROLL = 8


def _for_chunks(count, body):
    body(0, True, False)

    def step(t1, carry):
        body(t1, False, False)
        return carry

    lax.fori_loop(1, count - CHUNK_UNROLL + 1, step, 0, unroll=CHUNK_UNROLL)
    for t1 in range(count - CHUNK_UNROLL + 1, count):
        body(t1, False, t1 == count - 1)


def _short_conv_rows(ref, w_ref, b_ref, r0, nrows, first, last):
    x = ref[pl.ds(r0, nrows), :]
    rid = lax.broadcasted_iota(jnp.int32, x.shape, 0)
    if first:
        prev = jnp.where(rid == 0, 0.0, pltpu.roll(x, 1, axis=0))
    else:
        prev = ref[pl.ds(r0 - 1, nrows), :]
    if last:
        nxt = jnp.where(rid == nrows - 1, 0.0, pltpu.roll(x, nrows - 1, axis=0))
    else:
        nxt = ref[pl.ds(r0 + 1, nrows), :]
    return prev * w_ref[0:1, :] + x * w_ref[1:2, :] + nxt * w_ref[2:3, :] + b_ref[...]


def _long_conv_kernel(*refs, n1, conv_z, conv_g):
    refs = list(refs)
    a_ref, tp_ref = refs[-2:]
    slabs = a_ref.shape[0]

    def take_operand(conv):
        return [tuple(refs.pop(0) for _ in range(3 if conv else 1)) for _ in range(slabs)]

    z_ops = take_operand(conv_z)
    g_ops = take_operand(conv_g)
    k_ref, bias_ref, m1_ref, m1i_ref, m2_ref, m2i_ref, o_ref = refs[:-2]
    n2 = FFT_N2
    h1 = n1 // 2

    def put(ref, idx, x):
        for s in range(slabs):
            ref[s, idx, :] = x[:, s * LANES:(s + 1) * LANES]

    def get(ref, idx):
        return jnp.concatenate([ref[s, idx, :] for s in range(slabs)], axis=1)

    def rows_of(ops, conv, t1, first, last):
        r0 = t1 * n2 if isinstance(t1, int) else pl.multiple_of(t1 * n2, n2)
        if conv:
            parts = [_short_conv_rows(ref, w_ref, b_ref, r0, n2, first, last) for ref, w_ref, b_ref in ops]
        else:
            parts = [ref[pl.ds(r0, n2), :] for ref, in ops]
        return jnp.concatenate(parts, axis=1)

    def z_rows(t1, first, last):
        return rows_of(z_ops, conv_z, t1, first, last)

    def g_rows(t1, first, last):
        return rows_of(g_ops, conv_g, t1, first, last)

    def pitched(t1):
        r0 = t1 * T_PITCH
        return pl.ds(r0 if isinstance(t1, int) else pl.multiple_of(r0, PITCH_ALIGN), n2)

    park_z = conv_z and o_ref.dtype == F32

    def natural(t1):
        r0 = t1 * n2 if isinstance(t1, int) else pl.multiple_of(t1 * n2, n2)
        return pl.ds(r0, n2)

    def pad(t1, first, last):
        z = z_rows(t1, first, last)
        put(tp_ref, pitched(t1), z)
        if park_z:
            o_ref[natural(t1), :] = z

    _for_chunks(h1, pad)

    wide = max(1, MXU_COLUMNS // (slabs * LANES))
    width = slabs * LANES

    def t2_sets(gi):
        return [[(gi * OUTER_GROUP + u) * wide + j for j in range(wide)] for u in range(OUTER_GROUP)]

    def stage1(gi, carry):
        sets = t2_sets(gi)
        zts = [jnp.concatenate([get(tp_ref, pl.ds(t2, h1, stride=T_PITCH)) for t2 in t2s], axis=1)
               for t2s in sets]
        prods = [_bdot(m1_ref[...], zt.astype(BF16)) for zt in zts]
        for t2s, a in zip(sets, prods):
            for j, t2 in enumerate(t2s):
                cols = slice(j * width, (j + 1) * width)
                put(a_ref, pl.ds(t2, h1, stride=A_PITCH), a[:h1, cols])
                put(a_ref, pl.ds(t2 + n2, h1, stride=A_PITCH), a[h1:, cols])
        return carry

    lax.fori_loop(0, n2 // (wide * OUTER_GROUP), stage1, 0)

    group = STAGE2_GROUP // slabs

    def stage2(gi, carry):
        f1s = [gi * group + j for j in range(group)]
        ras = [pl.ds(pl.multiple_of(f1 * A_PITCH, PITCH_ALIGN), 2 * n2) for f1 in f1s]
        xs = [_bdot(m2_ref[f1], get(a_ref, ra).astype(BF16)) for f1, ra in zip(f1s, ras)]
        ys = []
        for f1, x in zip(f1s, xs):
            kk = k_ref[pl.ds(pl.multiple_of(f1 * 2 * n2, 2 * n2), 2 * n2), :].astype(F32)
            xr, xi, kr, ki = x[:n2], x[n2:], kk[:n2], kk[n2:]
            ys.append(jnp.concatenate([xr * kr - xi * ki, xr * ki + xi * kr], axis=0).astype(BF16))
        outs = [_bdot(m2i_ref[f1], y) for f1, y in zip(f1s, ys)]
        for ra, out in zip(ras, outs):
            put(a_ref, ra, out)
        return carry

    lax.fori_loop(0, h1 // group, stage2, 0)

    def stage3(gi, carry):
        sets = t2_sets(gi)
        specs = []
        for t2s in sets:
            br = jnp.concatenate([get(a_ref, pl.ds(t2, h1, stride=A_PITCH)) for t2 in t2s], axis=1)
            bi = jnp.concatenate([get(a_ref, pl.ds(t2 + n2, h1, stride=A_PITCH)) for t2 in t2s], axis=1)
            specs.append(jnp.concatenate([br, bi], axis=0).astype(BF16))
        prods = [_bdot(m1i_ref[...], sp) for sp in specs]
        for t2s, y in zip(sets, prods):
            for j, t2 in enumerate(t2s):
                put(tp_ref, pl.ds(t2, h1, stride=T_PITCH), y[:, j * width:(j + 1) * width])
        return carry

    lax.fori_loop(0, n2 // (wide * OUTER_GROUP), stage3, 0)

    bias = bias_ref[...]

    def finish(t1, first, last):
        y = get(tp_ref, pitched(t1))
        z = o_ref[natural(t1), :] if park_z else z_rows(t1, first, last)
        out = g_rows(t1, first, last) * (y + bias * z)
        o_ref[natural(t1), :] = out.astype(o_ref.dtype)

    _for_chunks(h1, finish)


def _bf16_const(a):
    return jnp.asarray(a, F32).astype(BF16)


def _long_conv_slabs(seq):
    n1 = 2 * seq // FFT_N2
    h1 = n1 // 2
    tables = 2 * (2 * n1 * h1 + 2 * h1 * (2 * FFT_N2) ** 2)
    for slabs in (2, 1):
        io = 3 * 2 * seq * 4
        per_lane = io + 2 * 2 * seq * 2 + h1 * (A_PITCH + T_PITCH) * 4
        if per_lane * slabs * LANES + tables <= VMEM_LIMIT - VMEM_HEADROOM:
            return slabs
    raise ValueError(f"long conv of length {seq} does not fit VMEM")


def _long_conv(z, z_off, gate, g_off, kspec, k_off, bias, batch, seq, short=None, conv_z=False,
               out_dtype=F32):
    n = 2 * seq
    tb = _fft_tables(seq)
    n1, h1 = tb['n1'], tb['h1']
    conv_g = short is not None
    slabs = _long_conv_slabs(seq)
    width = slabs * LANES
    assert z_off % slabs == 0 and g_off % slabs == 0 and k_off % slabs == 0

    def operand(arr, off, conv):
        specs, args = [], []
        for s in range(slabs):
            specs.append(pl.BlockSpec((seq, LANES), lambda j, b, s=s: (b, off + j * slabs + s)))
            args.append(arr)
            if conv:
                specs += [pl.BlockSpec((3, LANES), lambda j, b, s=s: (0, off + j * slabs + s)),
                          pl.BlockSpec((1, LANES), lambda j, b, s=s: (0, off + j * slabs + s))]
                args += list(short)
        return specs, args

    z_specs, z_args = operand(z, z_off, conv_z)
    g_specs, g_args = operand(gate, g_off, conv_g)
    k_blk = k_off // slabs
    return pl.pallas_call(
        functools.partial(_long_conv_kernel, n1=n1, conv_z=conv_z, conv_g=conv_g),
        grid=(D_HYENA // width, batch),
        in_specs=z_specs + g_specs + [
                  pl.BlockSpec((n, width), lambda j, b: (0, k_blk + j)),
                  pl.BlockSpec((1, width), lambda j, b: (0, j)),
                  _const_spec((n1, h1)), _const_spec((h1, n1)),
                  _const_spec((h1, 2 * FFT_N2, 2 * FFT_N2)),
                  _const_spec((h1, 2 * FFT_N2, 2 * FFT_N2))],
        out_specs=pl.BlockSpec((seq, width), lambda j, b: (b, j)),
        out_shape=jax.ShapeDtypeStruct((batch * seq, D_HYENA), out_dtype),
        scratch_shapes=[pltpu.VMEM((slabs, h1 * A_PITCH, LANES), F32),
                        pltpu.VMEM((slabs, h1 * T_PITCH, LANES), F32)],
        compiler_params=pltpu.CompilerParams(
            dimension_semantics=("parallel", "parallel"), vmem_limit_bytes=VMEM_LIMIT),
        name="hyena_long_conv",
    )(*z_args, *g_args, kspec, bias, _bf16_const(tb['m1']), _bf16_const(tb['m1_inv']),
      _bf16_const(tb['m2']), _bf16_const(tb['m2_inv']))


def _layer(x3, p, rope):
    batch, seq, _ = x3.shape
    x = x3.reshape(batch * seq, D_MODEL)
    x = _ffn(x, p['ffn1_pre_g'], p['ffn1_w1'], p['ffn1_w3'], p['ffn1_w2'], p['ffn1_post_g'])
    hy, qkv, gate = _inproj(x, p['mix_pre_g'], p['w_in'], seq, rope)
    kspec = _filter_spectra(seq, p)
    tiles = D_HYENA // LANES
    short = (p['short_w'], p['short_b'])
    z1 = _long_conv(hy, 0, hy, tiles, kspec, 0, p['hyena_bias'][0:1], batch, seq, short=short, conv_z=True)
    yh = _long_conv(z1, 0, hy, 2 * tiles, kspec, tiles, p['hyena_bias'][1:2], batch, seq, short=short,
                    out_dtype=BF16)
    yr = _retention(qkv, gate, p['ret_log_decay_f'], p['ret_log_decay_b'], batch, seq)
    mix = (yh, yr, p['w_out'], p['mix_post_g'])
    x = _ffn(x, p['ffn2_pre_g'], p['ffn2_w1'], p['ffn2_w3'], p['ffn2_w2'], p['ffn2_post_g'], mix=mix)
    return x.reshape(batch, seq, D_MODEL)


_MATRIX_PARAMS = ('ffn1_w1', 'ffn1_w3', 'ffn1_w2', 'w_in', 'w_out', 'ffn2_w1', 'ffn2_w3', 'ffn2_w2')
_GAIN_PARAMS = ('ffn1_pre_g', 'ffn1_post_g', 'mix_pre_g', 'mix_post_g', 'ffn2_pre_g', 'ffn2_post_g')


def kernel(x_prompt, x_sample, ffn1_pre_g, ffn1_w1, ffn1_w3, ffn1_w2, ffn1_post_g, mix_pre_g, w_in, short_w, short_b, filt_w1, filt_b1, filt_w2, filt_b2, filt_w3, filt_b3, filt_w4, filt_freq, hyena_bias, ret_log_decay_f, ret_log_decay_b, w_out, mix_post_g, ffn2_pre_g, ffn2_w1, ffn2_w3, ffn2_w2, ffn2_post_g):
    params = dict(ffn1_pre_g=ffn1_pre_g, ffn1_w1=ffn1_w1, ffn1_w3=ffn1_w3, ffn1_w2=ffn1_w2,
                  ffn1_post_g=ffn1_post_g, mix_pre_g=mix_pre_g, w_in=w_in, short_w=short_w,
                  short_b=short_b, filt_w1=filt_w1, filt_b1=filt_b1, filt_w2=filt_w2,
                  filt_b2=filt_b2, filt_w3=filt_w3, filt_b3=filt_b3, filt_w4=filt_w4,
                  filt_freq=filt_freq, hyena_bias=hyena_bias, ret_log_decay_f=ret_log_decay_f,
                  ret_log_decay_b=ret_log_decay_b, w_out=w_out, mix_post_g=mix_post_g,
                  ffn2_pre_g=ffn2_pre_g, ffn2_w1=ffn2_w1, ffn2_w3=ffn2_w3, ffn2_w2=ffn2_w2,
                  ffn2_post_g=ffn2_post_g)
    depth = ffn1_w1.shape[0]
    rope = _rope_tables(max(x_prompt.shape[1], x_sample.shape[1]))

    def run(x):
        for l in range(depth):
            p = {k: v[l] for k, v in params.items()}
            for k in _MATRIX_PARAMS:
                p[k] = p[k].astype(BF16)
            for k in _GAIN_PARAMS:
                p[k] = p[k][None, :]
            p['short_b'] = p['short_b'][None, :]
            x = _layer(x, p, rope)
        return x

    return (run(x_prompt), run(x_sample))
```

```python
import functools
import math

import numpy as np
import jax
import jax.numpy as jnp
from jax import lax
from jax.experimental import pallas as pl
from jax.experimental.pallas import tpu as pltpu

F32 = jnp.float32
BF16 = jnp.bfloat16

D_MODEL = 1024
D_HYENA = 512
D_RET = 512
HYENA_ORDER = 2
N_RET_HEADS = 4
RET_HEAD_DIM = 128
D_FF = 2816
FILT_EMB = 33
FILT_BANDS = 16
FILT_HIDDEN = 64
ROPE_BASE = 10000.0
NORM_EPS = 1e-6
HYENA_TARGET = 1e-2
FAST_DECAY_PCT = 0.3
SLOW_DECAY_PCT = 1.5
N_HY_COLS = (HYENA_ORDER + 1) * D_HYENA
D_IN = N_HY_COLS + 4 * D_RET

LANES = 128
MXU_COLUMNS = 256
VMEM_LIMIT = 56 * 1024 * 1024
VMEM_HEADROOM = 4 * 1024 * 1024
FFT_N2 = 32
RET_CHUNK = 128
RET_GROUP = 32
TOKEN_TILE = 512
INPROJ_TILE = 1024


def _const_spec(shape):
    nd = len(shape)
    return pl.BlockSpec(shape, lambda *_: (0,) * nd, pipeline_mode=pl.Buffered(1))


def _rms(x, g):
    ms = jnp.mean(x * x, axis=-1, keepdims=True)
    return x * lax.rsqrt(ms + NORM_EPS) * g


def _bdot(a, b):
    return jnp.dot(a, b, preferred_element_type=F32)


def _ffn_core(x, pre_ref, w1_ref, w3_ref, w2_ref, post_ref, o_ref):
    h = _rms(x, pre_ref[...]).astype(BF16)
    a = _bdot(h, w1_ref[...])
    b = _bdot(h, w3_ref[...])
    g = (a * jax.nn.sigmoid(a) * b).astype(BF16)
    y = _bdot(g, w2_ref[...])
    o_ref[...] = x + 0.5 * _rms(y, post_ref[...])


def _ffn_kernel(x_ref, pre_ref, w1_ref, w3_ref, w2_ref, post_ref, o_ref):
    _ffn_core(x_ref[...], pre_ref, w1_ref, w3_ref, w2_ref, post_ref, o_ref)


def _mix_ffn_kernel(x_ref, yh_ref, yr_ref, wo_ref, mg_ref,
                    pre_ref, w1_ref, w3_ref, w2_ref, post_ref, o_ref):
    mixed = jnp.concatenate([yh_ref[...].astype(BF16), yr_ref[...].astype(BF16)], axis=1)
    y = _bdot(mixed, wo_ref[...])
    x = x_ref[...] + _rms(y, mg_ref[...])
    _ffn_core(x, pre_ref, w1_ref, w3_ref, w2_ref, post_ref, o_ref)


def _row_spec(tm, width):
    return pl.BlockSpec((tm, width), lambda i: (i, 0))


def _ffn(x, pre_g, w1, w3, w2, post_g, mix=None):
    t = x.shape[0]
    tm = TOKEN_TILE
    ffn_specs = [_const_spec((1, D_MODEL)), _const_spec((D_MODEL, D_FF)),
                 _const_spec((D_MODEL, D_FF)), _const_spec((D_FF, D_MODEL)),
                 _const_spec((1, D_MODEL))]
    ffn_args = (pre_g, w1, w3, w2, post_g)
    if mix is None:
        body, args = _ffn_kernel, (x,) + ffn_args
        specs = [_row_spec(tm, D_MODEL)] + ffn_specs
    else:
        yh, yr, w_out, mg = mix
        body, args = _mix_ffn_kernel, (x, yh, yr, w_out, mg) + ffn_args
        specs = [_row_spec(tm, D_MODEL), _row_spec(tm, D_HYENA), _row_spec(tm, D_RET),
                 _const_spec((D_HYENA + D_RET, D_MODEL)), _const_spec((1, D_MODEL))] + ffn_specs
    return pl.pallas_call(
        body,
        grid=(t // tm,),
        in_specs=specs,
        out_specs=_row_spec(tm, D_MODEL),
        out_shape=jax.ShapeDtypeStruct((t, D_MODEL), F32),
        compiler_params=pltpu.CompilerParams(
            dimension_semantics=("parallel",), vmem_limit_bytes=VMEM_LIMIT),
        name="ffn_mix" if mix is not None else "ffn",
    )(*args)


def _inproj_kernel(x_ref, g_ref, w_ref, cc_ref, ss_ref, hy_ref, qkv_ref, gate_ref):
    h = _rms(x_ref[...], g_ref[...]).astype(BF16)
    qk0 = N_HY_COLS
    v0 = qk0 + 2 * D_RET
    g0 = v0 + D_RET
    qk = _bdot(h, w_ref[:, qk0:v0])
    hy_ref[...] = _bdot(h, w_ref[:, :qk0])
    gate_ref[...] = _bdot(h, w_ref[:, g0:])
    qkv_ref[:, 2 * D_RET:] = _bdot(h, w_ref[:, v0:g0]).astype(BF16)
    cc = cc_ref[...]
    ss = ss_ref[...]
    d = RET_HEAD_DIM
    for blk in range(2 * N_RET_HEADS):
        x = qk[:, blk * d:(blk + 1) * d]
        r = x * cc + pltpu.roll(x, d // 2, axis=1) * ss
        if blk >= N_RET_HEADS:
            r = r * (d ** -0.5)
        qkv_ref[:, blk * d:(blk + 1) * d] = r.astype(BF16)


def _inproj(x, g, w, seq, rope):
    t = x.shape[0]
    tm = INPROJ_TILE
    cc, ss = rope
    assert cc.shape[0] >= seq
    pos_blocks = seq // tm
    rope_spec = pl.BlockSpec((tm, RET_HEAD_DIM), lambda i: (i % pos_blocks, 0))
    return pl.pallas_call(
        _inproj_kernel,
        grid=(t // tm,),
        in_specs=[_row_spec(tm, D_MODEL), _const_spec((1, D_MODEL)),
                  _const_spec((D_MODEL, D_IN)), rope_spec, rope_spec],
        out_specs=[_row_spec(tm, N_HY_COLS), _row_spec(tm, 3 * D_RET), _row_spec(tm, D_RET)],
        out_shape=[jax.ShapeDtypeStruct((t, N_HY_COLS), F32),
                   jax.ShapeDtypeStruct((t, 3 * D_RET), BF16),
                   jax.ShapeDtypeStruct((t, D_RET), F32)],
        compiler_params=pltpu.CompilerParams(
            dimension_semantics=("parallel",), vmem_limit_bytes=VMEM_LIMIT),
        name="inproj",
    )(x, g, w, cc, ss)


def _ret_kernel(lgf_ref, lgb_ref, q_ref, k_ref, v_ref, g_ref, o_ref, kt_ref, *, seq):
    c = RET_CHUNK
    d = RET_HEAD_DIM
    n_chunks = seq // c
    head = pl.program_id(1)
    lgf = jnp.full((c, d), lgf_ref[head], F32)
    lgb = jnp.full((c, d), lgb_ref[head], F32)
    row = lax.broadcasted_iota(jnp.int32, (c, d), 0).astype(F32)
    col = lax.broadcasted_iota(jnp.int32, (c, d), 1).astype(F32)
    diff = row - col
    dmat = jnp.where(diff >= 0.0, jnp.exp(jnp.maximum(diff, 0.0) * lgf),
                     jnp.exp(jnp.maximum(-diff, 0.0) * lgb))
    wq_f = jnp.exp((row + 1.0) * lgf)
    wk_f = jnp.exp((c - 1.0 - row) * lgf)
    wq_b = jnp.exp((c - row) * lgb)
    wk_b = jnp.exp(row * lgb)
    gc_f = jnp.exp(c * lgf)
    gc_b = jnp.exp(c * lgb)

    grp = RET_GROUP
    n_groups = n_chunks // grp

    def fwd(gi, state):
        rows = [pl.ds(pl.multiple_of((gi * grp + j) * c, c), c) for j in range(grp)]
        qbs, kts, vs = [], [], []
        for r in rows:
            ktb = k_ref[r, :].astype(F32).T.astype(BF16)
            kt_ref[r, :] = ktb
            qbs.append(q_ref[r, :])
            kts.append(ktb)
            vs.append(v_ref[r, :])
        scores = [_bdot(qb, ktb) for qb, ktb in zip(qbs, kts)]
        kvs = [_bdot(ktb, (v.astype(F32) * wk_f).astype(BF16)) for ktb, v in zip(kts, vs)]
        states = []
        for kv in kvs:
            states.append(state)
            state = state * gc_f + kv
        intras = [_bdot((s * dmat).astype(BF16), v) for s, v in zip(scores, vs)]
        crosses = [_bdot(qb, st.astype(BF16)) for qb, st in zip(qbs, states)]
        for r, intra, cross in zip(rows, intras, crosses):
            o_ref[r, :] = intra + wq_f * cross
        return state

    lax.fori_loop(0, n_groups, fwd, jnp.zeros((d, d), F32))

    def bwd(gi, state):
        rows = [pl.ds(pl.multiple_of((n_chunks - 1 - gi * grp - j) * c, c), c) for j in range(grp)]
        kvs = [_bdot(kt_ref[r, :], (v_ref[r, :].astype(F32) * wk_b).astype(BF16)) for r in rows]
        states = []
        for kv in kvs:
            states.append(state)
            state = state * gc_b + kv
        crosses = [_bdot(q_ref[r, :], st.astype(BF16)) for r, st in zip(rows, states)]
        outs = [o_ref[r, :] + wq_b * cross for r, cross in zip(rows, crosses)]
        means = [jnp.mean(o * o, axis=-1, keepdims=True) for o in outs]
        norms = [lax.rsqrt(m + NORM_EPS) for m in means]
        gates = [g_ref[r, :] for r in rows]
        gates = [g * jax.nn.sigmoid(g) for g in gates]
        for r, o, nrm, g in zip(rows, outs, norms, gates):
            o_ref[r, :] = g * (o * nrm)
        return state

    lax.fori_loop(0, n_groups, bwd, jnp.zeros((d, d), F32))


def _rope_tables(seq):
    d = RET_HEAD_DIM
    inv = 1.0 / (ROPE_BASE ** (jnp.arange(0, d, 2, dtype=F32) / d))
    ang = jnp.arange(seq, dtype=F32)[:, None] * inv[None, :]
    c, s = jnp.cos(ang), jnp.sin(ang)
    return jnp.concatenate([c, c], axis=-1), jnp.concatenate([-s, s], axis=-1)


def _retention(qkv, gate, lg_f, lg_b, batch, seq):
    assert RET_CHUNK == RET_HEAD_DIM
    heads = N_RET_HEADS

    def col(off):
        return pl.BlockSpec((seq, LANES), lambda b, h, *_: (b, off * heads + h))

    grid_spec = pltpu.PrefetchScalarGridSpec(
        num_scalar_prefetch=2,
        grid=(batch, heads),
        in_specs=[col(0), col(1), col(2), col(0)],
        out_specs=pl.BlockSpec((seq, LANES), lambda b, h, *_: (b, h)),
        scratch_shapes=[pltpu.VMEM((seq, RET_CHUNK), BF16)],
    )
    return pl.pallas_call(
        functools.partial(_ret_kernel, seq=seq),
        grid_spec=grid_spec,
        out_shape=jax.ShapeDtypeStruct((batch * seq, D_RET), F32),
        compiler_params=pltpu.CompilerParams(
            dimension_semantics=("parallel", "parallel"), vmem_limit_bytes=VMEM_LIMIT),
        name="retention",
    )(lg_f, lg_b, qkv, qkv, qkv, gate)


def _split(x):
    hi = x.astype(BF16)
    return hi, (x - hi.astype(F32)).astype(BF16)


def _dot3(a, b):
    a_hi, a_lo = a
    b_hi, b_lo = b
    return _bdot(a_hi, b_hi) + _bdot(a_hi, b_lo) + _bdot(a_lo, b_hi)


@functools.lru_cache(maxsize=None)
def _fft_tables(seq):
    n = 2 * seq
    n2 = FFT_N2
    n1 = n // n2
    h1 = n1 // 2
    f1 = np.arange(h1, dtype=np.float64) + 0.5
    th = 2.0 * np.pi * f1[:, None] * np.arange(n1, dtype=np.float64)[None, :] / n1
    m1_full = np.concatenate([np.cos(th), -np.sin(th)], axis=0)
    m1 = m1_full[:, :h1]
    m1_inv = (2.0 / n) * m1.T
    t2 = np.arange(n2, dtype=np.float64)
    phi = 2.0 * np.pi * (np.arange(n2, dtype=np.float64)[None, :, None] * t2[None, None, :] / n2
                         + f1[:, None, None] * t2[None, None, :] / n)
    gr, gi = np.cos(phi), -np.sin(phi)
    m2 = np.concatenate([np.concatenate([gr, -gi], axis=2),
                         np.concatenate([gi, gr], axis=2)], axis=1)
    m2_inv = np.transpose(m2, (0, 2, 1))
    return dict(n1=n1, h1=h1, m1_full=m1_full, m1=m1, m1_inv=m1_inv, m2=m2, m2_inv=m2_inv)


OUTER_GROUP = 16
STAGE2_GROUP = 32
SPECTRUM_GROUP = 64
FILT_ROWS = 512
FILT_BLOCKS = 4
FILT_PACK = LANES // 2
PITCH_ALIGN = 4
A_PITCH = 2 * FFT_N2 + PITCH_ALIGN
T_PITCH = FFT_N2 + PITCH_ALIGN


def _filter_kernel(z_ref, w1_ref, b1_ref, w2_ref, b2_ref, w3_ref, b3_ref, fr_ref,
                   w4_ref, dl_ref, o_ref, sum_ref, *, nsteps):
    i = pl.program_id(0)
    half = FILT_ROWS // 2
    fr = fr_ref[...]
    zs = [z_ref[b * half:(b + 1) * half, :] for b in range(FILT_BLOCKS)]

    def layer(xs, w_ref, b_ref):
        w = _split(w_ref[...])
        pre = [_dot3(_split(x), w) for x in xs]
        return [jnp.sin(fr * (v + b_ref[...])) for v in pre]

    hs = layer(layer(layer(zs, w1_ref, b1_ref), w2_ref, b2_ref), w3_ref, b3_ref)
    hs = [_split(h) for h in hs]
    adl = jnp.abs(dl_ref[...])
    backward = i >= nsteps
    for side in range(2):
        lane = side * FILT_PACK
        ts = [jnp.broadcast_to(z[:, lane:lane + 1], z.shape) for z in zs]
        for c in range(HYENA_ORDER * D_HYENA // LANES):
            cols = slice(c * LANES, (c + 1) * LANES)
            w4 = _split(w4_ref[0, side, :, cols])
            hcs = [_dot3(h, w4) * jnp.exp(-t * adl[:, cols]) for h, t in zip(hs, ts)]
            part = sum(jnp.sum(jnp.abs(hc), axis=0, keepdims=True) for hc in hcs)

            if side == 0:
                @pl.when(i % nsteps == 0)
                def _():
                    sum_ref[0, :, cols] = part

                @pl.when(i % nsteps != 0)
                def _():
                    sum_ref[0, :, cols] = sum_ref[0, :, cols] + part
            else:
                sum_ref[0, :, cols] = sum_ref[0, :, cols] + part

            for b, hc in enumerate(hcs):
                if side == 0 and b == 0:
                    rows = lax.broadcasted_iota(jnp.int32, hc.shape, 0)
                    flipped = jnp.where(jnp.logical_and(rows == 0, i == nsteps), 0.0, -hc)
                else:
                    flipped = -hc
                r0 = b * FILT_ROWS + side * half
                o_ref[r0:r0 + half, cols] = jnp.where(backward, flipped, hc)


def _kspec_kernel(k_ref, s_ref, m1_ref, m2_ref, o_ref, a_ref, kp_ref, *, n1):
    n2 = FFT_N2
    h1 = n1 // 2
    inv_f = 1.0 / s_ref[0]
    inv_b = 1.0 / s_ref[1]

    def pad(t1, carry):
        src = pl.multiple_of(t1 * n2, n2)
        dst = pl.multiple_of(t1 * T_PITCH, PITCH_ALIGN)
        kp_ref[pl.ds(dst, n2), :] = k_ref[pl.ds(src, n2), :] * jnp.where(t1 < h1, inv_f, inv_b)
        return carry

    lax.fori_loop(0, n1, pad, 0, unroll=8)

    wide = MXU_COLUMNS // LANES

    def stage1(gi, carry):
        sets = [[(gi * OUTER_GROUP + u) * wide + j for j in range(wide)] for u in range(OUTER_GROUP)]
        kts = [jnp.concatenate([kp_ref[pl.ds(t2, n1, stride=T_PITCH), :] for t2 in t2s], axis=1)
               for t2s in sets]
        prods = [_bdot(m1_ref[...], kt.astype(BF16)) for kt in kts]
        for t2s, a in zip(sets, prods):
            for j, t2 in enumerate(t2s):
                cols = slice(j * LANES, (j + 1) * LANES)
                a_ref[pl.ds(t2, h1, stride=A_PITCH), :] = a[:h1, cols]
                a_ref[pl.ds(t2 + n2, h1, stride=A_PITCH), :] = a[h1:, cols]
        return carry

    lax.fori_loop(0, n2 // (wide * OUTER_GROUP), stage1, 0)

    def stage2(gi, carry):
        f1s = [gi * SPECTRUM_GROUP + j for j in range(SPECTRUM_GROUP)]
        srcs = [pl.ds(pl.multiple_of(f1 * A_PITCH, PITCH_ALIGN), 2 * n2) for f1 in f1s]
        specs = [_bdot(m2_ref[f1], a_ref[src, :].astype(BF16)) for f1, src in zip(f1s, srcs)]
        for f1, spec in zip(f1s, specs):
            dst = pl.multiple_of(f1 * 2 * n2, 2 * n2)
            o_ref[pl.ds(dst, 2 * n2), :] = spec.astype(o_ref.dtype)
        return carry

    lax.fori_loop(0, h1 // SPECTRUM_GROUP, stage2, 0)


def _filter_spectra(seq, p):
    n = 2 * seq
    tb = _fft_tables(seq)
    nblk = seq // FILT_ROWS
    half = FILT_ROWS // 2
    pack = FILT_PACK

    def packed(v):
        return jnp.transpose(v.reshape(2 * nblk, 2, half), (0, 2, 1)).reshape(n // 2, 2, 1)

    fwd_pos = jnp.arange(seq, dtype=jnp.int32)
    pos = jnp.concatenate([fwd_pos, (seq - fwd_pos) % seq])
    t_fwd = jnp.linspace(0.0, 1.0, seq, dtype=F32)
    t = packed(jnp.concatenate([t_fwd, jnp.roll(t_fwd[::-1], 1)]))
    w = packed(2.0 * math.pi * pos.astype(F32) / seq)
    fb = jnp.linspace(1e-4, FILT_BANDS - 1, FILT_BANDS, dtype=F32)[None, None, :]
    zero = jnp.zeros((n // 2, 2, pack - FILT_EMB), F32)
    zz = jnp.concatenate([t, jnp.cos(fb * w), -jnp.sin(fb * w), zero], axis=-1).reshape(n // 2, LANES)

    def padw(a):
        a = jnp.pad(a, ((0, pack - a.shape[0]), (0, pack - a.shape[1])))
        zero = jnp.zeros_like(a)
        return jnp.concatenate([jnp.concatenate([a, zero], axis=1),
                                jnp.concatenate([zero, a], axis=1)], axis=0)

    def padv(a):
        return jnp.tile(jnp.pad(a, (0, pack - a.shape[0])), 2)[None, :]

    w1 = padw(p['filt_w1'])
    w2 = padw(p['filt_w2'])
    w3 = padw(p['filt_w3'])
    w4 = jnp.pad(p['filt_w4'], ((0, pack - FILT_HIDDEN), (0, 0))).reshape(pack, HYENA_ORDER, 2, D_HYENA)
    w4 = jnp.transpose(w4, (2, 0, 1, 3)).reshape(2, pack, HYENA_ORDER * D_HYENA)
    w4 = jnp.stack([jnp.pad(w4, ((0, 0), (0, pack), (0, 0))),
                    jnp.pad(w4, ((0, 0), (pack, 0), (0, 0)))], axis=1)
    min_decay = math.log(HYENA_TARGET) / SLOW_DECAY_PCT
    max_decay = math.log(HYENA_TARGET) / FAST_DECAY_PCT
    deltas = jnp.linspace(min_decay, max_decay, D_HYENA, dtype=F32)
    deltas = jnp.tile(deltas, HYENA_ORDER)[None, :]
    width = HYENA_ORDER * D_HYENA
    sq = _const_spec((LANES, LANES))
    vec = _const_spec((1, LANES))
    nsteps = nblk // FILT_BLOCKS
    filt, sums = pl.pallas_call(
        functools.partial(_filter_kernel, nsteps=nsteps),
        grid=(2 * nsteps,),
        in_specs=[pl.BlockSpec((FILT_BLOCKS * half, LANES), lambda i: (i, 0)),
                  sq, vec, sq, vec, sq, vec, vec,
                  pl.BlockSpec((1, 2, LANES, width), lambda i: (i // nsteps, 0, 0, 0)),
                  _const_spec((1, width))],
        out_specs=[pl.BlockSpec((FILT_BLOCKS * FILT_ROWS, width), lambda i: (i, 0)),
                   pl.BlockSpec((1, 1, width), lambda i: (i // nsteps, 0, 0))],
        out_shape=[jax.ShapeDtypeStruct((n, width), F32),
                   jax.ShapeDtypeStruct((2, 1, width), F32)],
        compiler_params=pltpu.CompilerParams(
            dimension_semantics=("arbitrary",), vmem_limit_bytes=VMEM_LIMIT),
        name="hyena_filter",
    )(zz, w1, padv(p['filt_b1']), w2, padv(p['filt_b2']), w3, padv(p['filt_b3']),
      padv(p['filt_freq']), w4, deltas)
    n1, h1 = tb['n1'], tb['h1']
    return pl.pallas_call(
        functools.partial(_kspec_kernel, n1=n1),
        grid=(width // LANES,),
        in_specs=[pl.BlockSpec((n, LANES), lambda j: (0, j)),
                  pl.BlockSpec((2, 1, LANES), lambda j: (0, 0, j)),
                  _const_spec((n1, n1)), _const_spec((h1, 2 * FFT_N2, 2 * FFT_N2))],
        out_specs=pl.BlockSpec((n, LANES), lambda j: (0, j)),
        out_shape=jax.ShapeDtypeStruct((n, width), BF16),
        scratch_shapes=[pltpu.VMEM((h1 * A_PITCH, LANES), F32),
                        pltpu.VMEM((n1 * T_PITCH, LANES), F32)],
        compiler_params=pltpu.CompilerParams(
            dimension_semantics=("parallel",), vmem_limit_bytes=VMEM_LIMIT),
        name="hyena_filter_spectrum",
    )(filt, sums, _bf16_const(tb['m1_full']), _bf16_const(tb['m2']))


CHUNK_UNROLL = 8


def _for_chunks(count, body):
    body(0, True, False)

    def step(t1, carry):
        body(t1, False, False)
        return carry

    lax.fori_loop(1, count - CHUNK_UNROLL + 1, step, 0, unroll=CHUNK_UNROLL)
    for t1 in range(count - CHUNK_UNROLL + 1, count):
        body(t1, False, t1 == count - 1)


def _short_conv_rows(ref, w_ref, b_ref, r0, nrows, first, last):
    x = ref[pl.ds(r0, nrows), :]
    rid = lax.broadcasted_iota(jnp.int32, x.shape, 0)
    if first:
        prev = jnp.where(rid == 0, 0.0, pltpu.roll(x, 1, axis=0))
    else:
        prev = ref[pl.ds(r0 - 1, nrows), :]
    if last:
        nxt = jnp.where(rid == nrows - 1, 0.0, pltpu.roll(x, nrows - 1, axis=0))
    else:
        nxt = ref[pl.ds(r0 + 1, nrows), :]
    return prev * w_ref[0:1, :] + x * w_ref[1:2, :] + nxt * w_ref[2:3, :] + b_ref[...]


def _long_conv_kernel(*refs, n1, conv_z, conv_g):
    refs = list(refs)
    a_ref, tp_ref = refs[-2:]
    slabs = a_ref.shape[0]

    def take_operand(conv):
        return [tuple(refs.pop(0) for _ in range(3 if conv else 1)) for _ in range(slabs)]

    z_ops = take_operand(conv_z)
    g_ops = take_operand(conv_g)
    k_ref, bias_ref, m1_ref, m1i_ref, m2_ref, m2i_ref, o_ref = refs[:-2]
    n2 = FFT_N2
    h1 = n1 // 2

    def put(ref, idx, x):
        for s in range(slabs):
            ref[s, idx, :] = x[:, s * LANES:(s + 1) * LANES]

    def get(ref, idx):
        return jnp.concatenate([ref[s, idx, :] for s in range(slabs)], axis=1)

    def rows_of(ops, conv, t1, first, last):
        r0 = t1 * n2 if isinstance(t1, int) else pl.multiple_of(t1 * n2, n2)
        if conv:
            parts = [_short_conv_rows(ref, w_ref, b_ref, r0, n2, first, last) for ref, w_ref, b_ref in ops]
        else:
            parts = [ref[pl.ds(r0, n2), :] for ref, in ops]
        return jnp.concatenate(parts, axis=1)

    def z_rows(t1, first, last):
        return rows_of(z_ops, conv_z, t1, first, last)

    def g_rows(t1, first, last):
        return rows_of(g_ops, conv_g, t1, first, last)

    def pitched(t1):
        r0 = t1 * T_PITCH
        return pl.ds(r0 if isinstance(t1, int) else pl.multiple_of(r0, PITCH_ALIGN), n2)

    park_z = conv_z and o_ref.dtype == F32

    def natural(t1):
        r0 = t1 * n2 if isinstance(t1, int) else pl.multiple_of(t1 * n2, n2)
        return pl.ds(r0, n2)

    def pad(t1, first, last):
        z = z_rows(t1, first, last)
        put(tp_ref, pitched(t1), z)
        if park_z:
            o_ref[natural(t1), :] = z

    _for_chunks(h1, pad)

    wide = max(1, MXU_COLUMNS // (slabs * LANES))
    width = slabs * LANES

    def t2_sets(gi):
        return [[(gi * OUTER_GROUP + u) * wide + j for j in range(wide)] for u in range(OUTER_GROUP)]

    def stage1(gi, carry):
        sets = t2_sets(gi)
        zts = [jnp.concatenate([get(tp_ref, pl.ds(t2, h1, stride=T_PITCH)) for t2 in t2s], axis=1)
               for t2s in sets]
        prods = [_bdot(m1_ref[...], zt.astype(BF16)) for zt in zts]
        for t2s, a in zip(sets, prods):
            for j, t2 in enumerate(t2s):
                cols = slice(j * width, (j + 1) * width)
                put(a_ref, pl.ds(t2, h1, stride=A_PITCH), a[:h1, cols])
                put(a_ref, pl.ds(t2 + n2, h1, stride=A_PITCH), a[h1:, cols])
        return carry

    lax.fori_loop(0, n2 // (wide * OUTER_GROUP), stage1, 0)

    group = STAGE2_GROUP // slabs

    def stage2(gi, carry):
        f1s = [gi * group + j for j in range(group)]
        ras = [pl.ds(pl.multiple_of(f1 * A_PITCH, PITCH_ALIGN), 2 * n2) for f1 in f1s]
        xs = [_bdot(m2_ref[f1], get(a_ref, ra).astype(BF16)) for f1, ra in zip(f1s, ras)]
        ys = []
        for f1, x in zip(f1s, xs):
            kk = k_ref[pl.ds(pl.multiple_of(f1 * 2 * n2, 2 * n2), 2 * n2), :].astype(F32)
            xr, xi, kr, ki = x[:n2], x[n2:], kk[:n2], kk[n2:]
            ys.append(jnp.concatenate([xr * kr - xi * ki, xr * ki + xi * kr], axis=0).astype(BF16))
        outs = [_bdot(m2i_ref[f1], y) for f1, y in zip(f1s, ys)]
        for ra, out in zip(ras, outs):
            put(a_ref, ra, out)
        return carry

    lax.fori_loop(0, h1 // group, stage2, 0)

    def stage3(gi, carry):
        sets = t2_sets(gi)
        specs = []
        for t2s in sets:
            br = jnp.concatenate([get(a_ref, pl.ds(t2, h1, stride=A_PITCH)) for t2 in t2s], axis=1)
            bi = jnp.concatenate([get(a_ref, pl.ds(t2 + n2, h1, stride=A_PITCH)) for t2 in t2s], axis=1)
            specs.append(jnp.concatenate([br, bi], axis=0).astype(BF16))
        prods = [_bdot(m1i_ref[...], sp) for sp in specs]
        for t2s, y in zip(sets, prods):
            for j, t2 in enumerate(t2s):
                put(tp_ref, pl.ds(t2, h1, stride=T_PITCH), y[:, j * width:(j + 1) * width])
        return carry

    lax.fori_loop(0, n2 // (wide * OUTER_GROUP), stage3, 0)

    bias = bias_ref[...]

    def finish(t1, first, last):
        y = get(tp_ref, pitched(t1))
        z = o_ref[natural(t1), :] if park_z else z_rows(t1, first, last)
        out = g_rows(t1, first, last) * (y + bias * z)
        o_ref[natural(t1), :] = out.astype(o_ref.dtype)

    _for_chunks(h1, finish)


def _bf16_const(a):
    return jnp.asarray(a, F32).astype(BF16)


def _long_conv_slabs(seq):
    n1 = 2 * seq // FFT_N2
    h1 = n1 // 2
    tables = 2 * (2 * n1 * h1 + 2 * h1 * (2 * FFT_N2) ** 2)
    for slabs in (2, 1):
        io = 3 * 2 * seq * 4
        per_lane = io + 2 * 2 * seq * 2 + h1 * (A_PITCH + T_PITCH) * 4
        if per_lane * slabs * LANES + tables <= VMEM_LIMIT - VMEM_HEADROOM:
            return slabs
    raise ValueError(f"long conv of length {seq} does not fit VMEM")


def _long_conv(z, z_off, gate, g_off, kspec, k_off, bias, batch, seq, short=None, conv_z=False,
               out_dtype=F32):
    n = 2 * seq
    tb = _fft_tables(seq)
    n1, h1 = tb['n1'], tb['h1']
    conv_g = short is not None
    slabs = _long_conv_slabs(seq)
    width = slabs * LANES
    assert z_off % slabs == 0 and g_off % slabs == 0 and k_off % slabs == 0

    def operand(arr, off, conv):
        specs, args = [], []
        for s in range(slabs):
            specs.append(pl.BlockSpec((seq, LANES), lambda j, b, s=s: (b, off + j * slabs + s)))
            args.append(arr)
            if conv:
                specs += [pl.BlockSpec((3, LANES), lambda j, b, s=s: (0, off + j * slabs + s)),
                          pl.BlockSpec((1, LANES), lambda j, b, s=s: (0, off + j * slabs + s))]
                args += list(short)
        return specs, args

    z_specs, z_args = operand(z, z_off, conv_z)
    g_specs, g_args = operand(gate, g_off, conv_g)
    k_blk = k_off // slabs
    return pl.pallas_call(
        functools.partial(_long_conv_kernel, n1=n1, conv_z=conv_z, conv_g=conv_g),
        grid=(D_HYENA // width, batch),
        in_specs=z_specs + g_specs + [
                  pl.BlockSpec((n, width), lambda j, b: (0, k_blk + j)),
                  pl.BlockSpec((1, width), lambda j, b: (0, j)),
                  _const_spec((n1, h1)), _const_spec((h1, n1)),
                  _const_spec((h1, 2 * FFT_N2, 2 * FFT_N2)),
                  _const_spec((h1, 2 * FFT_N2, 2 * FFT_N2))],
        out_specs=pl.BlockSpec((seq, width), lambda j, b: (b, j)),
        out_shape=jax.ShapeDtypeStruct((batch * seq, D_HYENA), out_dtype),
        scratch_shapes=[pltpu.VMEM((slabs, h1 * A_PITCH, LANES), F32),
                        pltpu.VMEM((slabs, h1 * T_PITCH, LANES), F32)],
        compiler_params=pltpu.CompilerParams(
            dimension_semantics=("parallel", "parallel"), vmem_limit_bytes=VMEM_LIMIT),
        name="hyena_long_conv",
    )(*z_args, *g_args, kspec, bias, _bf16_const(tb['m1']), _bf16_const(tb['m1_inv']),
      _bf16_const(tb['m2']), _bf16_const(tb['m2_inv']))


def _layer(x3, p, rope):
    batch, seq, _ = x3.shape
    x = x3.reshape(batch * seq, D_MODEL)
    x = _ffn(x, p['ffn1_pre_g'], p['ffn1_w1'], p['ffn1_w3'], p['ffn1_w2'], p['ffn1_post_g'])
    hy, qkv, gate = _inproj(x, p['mix_pre_g'], p['w_in'], seq, rope)
    kspec = _filter_spectra(seq, p)
    tiles = D_HYENA // LANES
    short = (p['short_w'], p['short_b'])
    z1 = _long_conv(hy, 0, hy, tiles, kspec, 0, p['hyena_bias'][0:1], batch, seq, short=short, conv_z=True)
    yh = _long_conv(z1, 0, hy, 2 * tiles, kspec, tiles, p['hyena_bias'][1:2], batch, seq, short=short,
                    out_dtype=BF16)
    yr = _retention(qkv, gate, p['ret_log_decay_f'], p['ret_log_decay_b'], batch, seq)
    mix = (yh, yr, p['w_out'], p['mix_post_g'])
    x = _ffn(x, p['ffn2_pre_g'], p['ffn2_w1'], p['ffn2_w3'], p['ffn2_w2'], p['ffn2_post_g'], mix=mix)
    return x.reshape(batch, seq, D_MODEL)


_MATRIX_PARAMS = ('ffn1_w1', 'ffn1_w3', 'ffn1_w2', 'w_in', 'w_out', 'ffn2_w1', 'ffn2_w3', 'ffn2_w2')
_GAIN_PARAMS = ('ffn1_pre_g', 'ffn1_post_g', 'mix_pre_g', 'mix_post_g', 'ffn2_pre_g', 'ffn2_post_g')


def kernel(x_prompt, x_sample, ffn1_pre_g, ffn1_w1, ffn1_w3, ffn1_w2, ffn1_post_g, mix_pre_g, w_in, short_w, short_b, filt_w1, filt_b1, filt_w2, filt_b2, filt_w3, filt_b3, filt_w4, filt_freq, hyena_bias, ret_log_decay_f, ret_log_decay_b, w_out, mix_post_g, ffn2_pre_g, ffn2_w1, ffn2_w3, ffn2_w2, ffn2_post_g):
    params = dict(ffn1_pre_g=ffn1_pre_g, ffn1_w1=ffn1_w1, ffn1_w3=ffn1_w3, ffn1_w2=ffn1_w2,
                  ffn1_post_g=ffn1_post_g, mix_pre_g=mix_pre_g, w_in=w_in, short_w=short_w,
                  short_b=short_b, filt_w1=filt_w1, filt_b1=filt_b1, filt_w2=filt_w2,
                  filt_b2=filt_b2, filt_w3=filt_w3, filt_b3=filt_b3, filt_w4=filt_w4,
                  filt_freq=filt_freq, hyena_bias=hyena_bias, ret_log_decay_f=ret_log_decay_f,
                  ret_log_decay_b=ret_log_decay_b, w_out=w_out, mix_post_g=mix_post_g,
                  ffn2_pre_g=ffn2_pre_g, ffn2_w1=ffn2_w1, ffn2_w3=ffn2_w3, ffn2_w2=ffn2_w2,
                  ffn2_post_g=ffn2_post_g)
    depth = ffn1_w1.shape[0]
    rope = _rope_tables(max(x_prompt.shape[1], x_sample.shape[1]))

    def run(x):
        for l in range(depth):
            p = {k: v[l] for k, v in params.items()}
            for k in _MATRIX_PARAMS:
                p[k] = p[k].astype(BF16)
            for k in _GAIN_PARAMS:
                p[k] = p[k][None, :]
            p['short_b'] = p['short_b'][None, :]
            x = _layer(x, p, rope)
        return x

    return (run(x_prompt), run(x_sample))
```

```python
import functools
import math

import numpy as np
import jax
import jax.numpy as jnp
from jax import lax
from jax.experimental import pallas as pl
from jax.experimental.pallas import tpu as pltpu

F32 = jnp.float32
BF16 = jnp.bfloat16

D_MODEL = 1024
D_HYENA = 512
D_RET = 512
HYENA_ORDER = 2
N_RET_HEADS = 4
RET_HEAD_DIM = 128
D_FF = 2816
FILT_EMB = 33
FILT_BANDS = 16
FILT_HIDDEN = 64
ROPE_BASE = 10000.0
NORM_EPS = 1e-6
HYENA_TARGET = 1e-2
FAST_DECAY_PCT = 0.3
SLOW_DECAY_PCT = 1.5
N_HY_COLS = (HYENA_ORDER + 1) * D_HYENA
D_IN = N_HY_COLS + 4 * D_RET

LANES = 128
MXU_COLUMNS = 256
VMEM_LIMIT = 56 * 1024 * 1024
VMEM_HEADROOM = 4 * 1024 * 1024
FFT_N2 = 32
RET_CHUNK = 128
RET_GROUP = 32
TOKEN_TILE = 512
INPROJ_TILE = 1024


def _const_spec(shape):
    nd = len(shape)
    return pl.BlockSpec(shape, lambda *_: (0,) * nd, pipeline_mode=pl.Buffered(1))


def _rms(x, g):
    ms = jnp.mean(x * x, axis=-1, keepdims=True)
    return x * lax.rsqrt(ms + NORM_EPS) * g


def _bdot(a, b):
    return jnp.dot(a, b, preferred_element_type=F32)


def _ffn_core(x, pre_ref, w1_ref, w3_ref, w2_ref, post_ref, o_ref):
    h = _rms(x, pre_ref[...]).astype(BF16)
    a = _bdot(h, w1_ref[...])
    b = _bdot(h, w3_ref[...])
    g = (a * jax.nn.sigmoid(a) * b).astype(BF16)
    y = _bdot(g, w2_ref[...])
    o_ref[...] = x + 0.5 * _rms(y, post_ref[...])


def _ffn_kernel(x_ref, pre_ref, w1_ref, w3_ref, w2_ref, post_ref, o_ref):
    _ffn_core(x_ref[...], pre_ref, w1_ref, w3_ref, w2_ref, post_ref, o_ref)


def _mix_ffn_kernel(x_ref, yh_ref, yr_ref, wo_ref, mg_ref,
                    pre_ref, w1_ref, w3_ref, w2_ref, post_ref, o_ref):
    mixed = jnp.concatenate([yh_ref[...].astype(BF16), yr_ref[...].astype(BF16)], axis=1)
    y = _bdot(mixed, wo_ref[...])
    x = x_ref[...] + _rms(y, mg_ref[...])
    _ffn_core(x, pre_ref, w1_ref, w3_ref, w2_ref, post_ref, o_ref)


def _row_spec(tm, width):
    return pl.BlockSpec((tm, width), lambda i: (i, 0))


def _ffn(x, pre_g, w1, w3, w2, post_g, mix=None):
    t = x.shape[0]
    tm = TOKEN_TILE
    ffn_specs = [_const_spec((1, D_MODEL)), _const_spec((D_MODEL, D_FF)),
                 _const_spec((D_MODEL, D_FF)), _const_spec((D_FF, D_MODEL)),
                 _const_spec((1, D_MODEL))]
    ffn_args = (pre_g, w1, w3, w2, post_g)
    if mix is None:
        body, args = _ffn_kernel, (x,) + ffn_args
        specs = [_row_spec(tm, D_MODEL)] + ffn_specs
    else:
        yh, yr, w_out, mg = mix
        body, args = _mix_ffn_kernel, (x, yh, yr, w_out, mg) + ffn_args
        specs = [_row_spec(tm, D_MODEL), _row_spec(tm, D_HYENA), _row_spec(tm, D_RET),
                 _const_spec((D_HYENA + D_RET, D_MODEL)), _const_spec((1, D_MODEL))] + ffn_specs
    return pl.pallas_call(
        body,
        grid=(t // tm,),
        in_specs=specs,
        out_specs=_row_spec(tm, D_MODEL),
        out_shape=jax.ShapeDtypeStruct((t, D_MODEL), F32),
        compiler_params=pltpu.CompilerParams(
            dimension_semantics=("parallel",), vmem_limit_bytes=VMEM_LIMIT),
        name="ffn_mix" if mix is not None else "ffn",
    )(*args)


def _inproj_kernel(x_ref, g_ref, w_ref, cc_ref, ss_ref, hy_ref, qkv_ref, gate_ref):
    h = _rms(x_ref[...], g_ref[...]).astype(BF16)
    qk0 = N_HY_COLS
    v0 = qk0 + 2 * D_RET
    g0 = v0 + D_RET
    qk = _bdot(h, w_ref[:, qk0:v0])
    hy_ref[...] = _bdot(h, w_ref[:, :qk0])
    gate_ref[...] = _bdot(h, w_ref[:, g0:])
    qkv_ref[:, 2 * D_RET:] = _bdot(h, w_ref[:, v0:g0]).astype(BF16)
    cc = cc_ref[...]
    ss = ss_ref[...]
    d = RET_HEAD_DIM
    for blk in range(2 * N_RET_HEADS):
        x = qk[:, blk * d:(blk + 1) * d]
        r = x * cc + pltpu.roll(x, d // 2, axis=1) * ss
        if blk >= N_RET_HEADS:
            r = r * (d ** -0.5)
        qkv_ref[:, blk * d:(blk + 1) * d] = r.astype(BF16)


def _inproj(x, g, w, seq, rope):
    t = x.shape[0]
    tm = INPROJ_TILE
    cc, ss = rope
    assert cc.shape[0] >= seq
    pos_blocks = seq // tm
    rope_spec = pl.BlockSpec((tm, RET_HEAD_DIM), lambda i: (i % pos_blocks, 0))
    return pl.pallas_call(
        _inproj_kernel,
        grid=(t // tm,),
        in_specs=[_row_spec(tm, D_MODEL), _const_spec((1, D_MODEL)),
                  _const_spec((D_MODEL, D_IN)), rope_spec, rope_spec],
        out_specs=[_row_spec(tm, N_HY_COLS), _row_spec(tm, 3 * D_RET), _row_spec(tm, D_RET)],
        out_shape=[jax.ShapeDtypeStruct((t, N_HY_COLS), F32),
                   jax.ShapeDtypeStruct((t, 3 * D_RET), BF16),
                   jax.ShapeDtypeStruct((t, D_RET), F32)],
        compiler_params=pltpu.CompilerParams(
            dimension_semantics=("parallel",), vmem_limit_bytes=VMEM_LIMIT),
        name="inproj",
    )(x, g, w, cc, ss)


def _ret_kernel(lgf_ref, lgb_ref, q_ref, k_ref, v_ref, g_ref, o_ref, kt_ref, *, seq):
    c = RET_CHUNK
    d = RET_HEAD_DIM
    n_chunks = seq // c
    head = pl.program_id(1)
    lgf = jnp.full((c, d), lgf_ref[head], F32)
    lgb = jnp.full((c, d), lgb_ref[head], F32)
    row = lax.broadcasted_iota(jnp.int32, (c, d), 0).astype(F32)
    col = lax.broadcasted_iota(jnp.int32, (c, d), 1).astype(F32)
    diff = row - col
    dmat = jnp.where(diff >= 0.0, jnp.exp(jnp.maximum(diff, 0.0) * lgf),
                     jnp.exp(jnp.maximum(-diff, 0.0) * lgb))
    wq_f = jnp.exp((row + 1.0) * lgf)
    wk_f = jnp.exp((c - 1.0 - row) * lgf)
    wq_b = jnp.exp((c - row) * lgb)
    wk_b = jnp.exp(row * lgb)
    gc_f = jnp.exp(c * lgf)
    gc_b = jnp.exp(c * lgb)

    grp = RET_GROUP
    n_groups = n_chunks // grp

    def fwd(gi, state):
        rows = [pl.ds(pl.multiple_of((gi * grp + j) * c, c), c) for j in range(grp)]
        qbs, kts, vs = [], [], []
        for r in rows:
            ktb = k_ref[r, :].astype(F32).T.astype(BF16)
            kt_ref[r, :] = ktb
            qbs.append(q_ref[r, :])
            kts.append(ktb)
            vs.append(v_ref[r, :])
        scores = [_bdot(qb, ktb) for qb, ktb in zip(qbs, kts)]
        kvs = [_bdot(ktb, (v.astype(F32) * wk_f).astype(BF16)) for ktb, v in zip(kts, vs)]
        states = []
        for kv in kvs:
            states.append(state)
            state = state * gc_f + kv
        intras = [_bdot((s * dmat).astype(BF16), v) for s, v in zip(scores, vs)]
        crosses = [_bdot(qb, st.astype(BF16)) for qb, st in zip(qbs, states)]
        for r, intra, cross in zip(rows, intras, crosses):
            o_ref[r, :] = intra + wq_f * cross
        return state

    lax.fori_loop(0, n_groups, fwd, jnp.zeros((d, d), F32))

    def bwd(gi, state):
        rows = [pl.ds(pl.multiple_of((n_chunks - 1 - gi * grp - j) * c, c), c) for j in range(grp)]
        kvs = [_bdot(kt_ref[r, :], (v_ref[r, :].astype(F32) * wk_b).astype(BF16)) for r in rows]
        states = []
        for kv in kvs:
            states.append(state)
            state = state * gc_b + kv
        crosses = [_bdot(q_ref[r, :], st.astype(BF16)) for r, st in zip(rows, states)]
        outs = [o_ref[r, :] + wq_b * cross for r, cross in zip(rows, crosses)]
        means = [jnp.mean(o * o, axis=-1, keepdims=True) for o in outs]
        norms = [lax.rsqrt(m + NORM_EPS) for m in means]
        gates = [g_ref[r, :] for r in rows]
        gates = [g * jax.nn.sigmoid(g) for g in gates]
        for r, o, nrm, g in zip(rows, outs, norms, gates):
            o_ref[r, :] = g * (o * nrm)
        return state

    lax.fori_loop(0, n_groups, bwd, jnp.zeros((d, d), F32))


def _rope_tables(seq):
    d = RET_HEAD_DIM
    inv = 1.0 / (ROPE_BASE ** (jnp.arange(0, d, 2, dtype=F32) / d))
    ang = jnp.arange(seq, dtype=F32)[:, None] * inv[None, :]
    c, s = jnp.cos(ang), jnp.sin(ang)
    return jnp.concatenate([c, c], axis=-1), jnp.concatenate([-s, s], axis=-1)


def _retention(qkv, gate, lg_f, lg_b, batch, seq):
    assert RET_CHUNK == RET_HEAD_DIM
    heads = N_RET_HEADS

    def col(off):
        return pl.BlockSpec((seq, LANES), lambda b, h, *_: (b, off * heads + h))

    grid_spec = pltpu.PrefetchScalarGridSpec(
        num_scalar_prefetch=2,
        grid=(batch, heads),
        in_specs=[col(0), col(1), col(2), col(0)],
        out_specs=pl.BlockSpec((seq, LANES), lambda b, h, *_: (b, h)),
        scratch_shapes=[pltpu.VMEM((seq, RET_CHUNK), BF16)],
    )
    return pl.pallas_call(
        functools.partial(_ret_kernel, seq=seq),
        grid_spec=grid_spec,
        out_shape=jax.ShapeDtypeStruct((batch * seq, D_RET), F32),
        compiler_params=pltpu.CompilerParams(
            dimension_semantics=("parallel", "parallel"), vmem_limit_bytes=VMEM_LIMIT),
        name="retention",
    )(lg_f, lg_b, qkv, qkv, qkv, gate)


def _split(x):
    hi = x.astype(BF16)
    return hi, (x - hi.astype(F32)).astype(BF16)


def _dot3(a, b):
    a_hi, a_lo = a
    b_hi, b_lo = b
    return _bdot(a_hi, b_hi) + _bdot(a_hi, b_lo) + _bdot(a_lo, b_hi)


@functools.lru_cache(maxsize=None)
def _fft_tables(seq):
    n = 2 * seq
    n2 = FFT_N2
    n1 = n // n2
    h1 = n1 // 2
    f1 = np.arange(h1, dtype=np.float64) + 0.5
    th = 2.0 * np.pi * f1[:, None] * np.arange(n1, dtype=np.float64)[None, :] / n1
    m1_full = np.concatenate([np.cos(th), -np.sin(th)], axis=0)
    m1 = m1_full[:, :h1]
    m1_inv = (2.0 / n) * m1.T
    t2 = np.arange(n2, dtype=np.float64)
    phi = 2.0 * np.pi * (np.arange(n2, dtype=np.float64)[None, :, None] * t2[None, None, :] / n2
                         + f1[:, None, None] * t2[None, None, :] / n)
    gr, gi = np.cos(phi), -np.sin(phi)
    m2 = np.concatenate([np.concatenate([gr, -gi], axis=2),
                         np.concatenate([gi, gr], axis=2)], axis=1)
    m2_inv = np.transpose(m2, (0, 2, 1))
    return dict(n1=n1, h1=h1, m1_full=m1_full, m1=m1, m1_inv=m1_inv, m2=m2, m2_inv=m2_inv)


OUTER_GROUP = 16
STAGE2_GROUP = 32
SPECTRUM_GROUP = 64
FILT_ROWS = 512
FILT_BLOCKS = 4
FILT_PACK = LANES // 2
PITCH_ALIGN = 4
A_PITCH = 2 * FFT_N2 + PITCH_ALIGN
T_PITCH = FFT_N2 + PITCH_ALIGN


def _filter_kernel(z_ref, w1_ref, b1_ref, w2_ref, b2_ref, w3_ref, b3_ref, fr_ref,
                   w4_ref, dl_ref, o_ref, sum_ref, *, nsteps):
    i = pl.program_id(0)
    half = FILT_ROWS // 2
    fr = fr_ref[...]
    zs = [z_ref[b * half:(b + 1) * half, :] for b in range(FILT_BLOCKS)]

    def layer(xs, w_ref, b_ref):
        w = _split(w_ref[...])
        pre = [_dot3(_split(x), w) for x in xs]
        return [jnp.sin(fr * (v + b_ref[...])) for v in pre]

    hs = layer(layer(layer(zs, w1_ref, b1_ref), w2_ref, b2_ref), w3_ref, b3_ref)
    hs = [_split(h) for h in hs]
    adl = jnp.abs(dl_ref[...])
    backward = i >= nsteps
    for side in range(2):
        lane = side * FILT_PACK
        ts = [jnp.broadcast_to(z[:, lane:lane + 1], z.shape) for z in zs]
        for c in range(HYENA_ORDER * D_HYENA // LANES):
            cols = slice(c * LANES, (c + 1) * LANES)
            w4 = _split(w4_ref[0, side, :, cols])
            hcs = [_dot3(h, w4) * jnp.exp(-t * adl[:, cols]) for h, t in zip(hs, ts)]
            part = sum(jnp.sum(jnp.abs(hc), axis=0, keepdims=True) for hc in hcs)

            if side == 0:
                @pl.when(i % nsteps == 0)
                def _():
                    sum_ref[0, :, cols] = part

                @pl.when(i % nsteps != 0)
                def _():
                    sum_ref[0, :, cols] = sum_ref[0, :, cols] + part
            else:
                sum_ref[0, :, cols] = sum_ref[0, :, cols] + part

            for b, hc in enumerate(hcs):
                if side == 0 and b == 0:
                    rows = lax.broadcasted_iota(jnp.int32, hc.shape, 0)
                    flipped = jnp.where(jnp.logical_and(rows == 0, i == nsteps), 0.0, -hc)
                else:
                    flipped = -hc
                r0 = b * FILT_ROWS + side * half
                o_ref[r0:r0 + half, cols] = jnp.where(backward, flipped, hc)


def _kspec_kernel(k_ref, s_ref, m1_ref, m2_ref, o_ref, a_ref, kp_ref, *, n1):
    n2 = FFT_N2
    h1 = n1 // 2
    inv_f = 1.0 / s_ref[0]
    inv_b = 1.0 / s_ref[1]

    def pad(t1, carry):
        src = pl.multiple_of(t1 * n2, n2)
        dst = pl.multiple_of(t1 * T_PITCH, PITCH_ALIGN)
        kp_ref[pl.ds(dst, n2), :] = k_ref[pl.ds(src, n2), :] * jnp.where(t1 < h1, inv_f, inv_b)
        return carry

    lax.fori_loop(0, n1, pad, 0, unroll=8)

    wide = MXU_COLUMNS // LANES

    def stage1(gi, carry):
        sets = [[(gi * OUTER_GROUP + u) * wide + j for j in range(wide)] for u in range(OUTER_GROUP)]
        kts = [jnp.concatenate([kp_ref[pl.ds(t2, n1, stride=T_PITCH), :] for t2 in t2s], axis=1)
               for t2s in sets]
        prods = [_bdot(m1_ref[...], kt.astype(BF16)) for kt in kts]
        for t2s, a in zip(sets, prods):
            for j, t2 in enumerate(t2s):
                cols = slice(j * LANES, (j + 1) * LANES)
                a_ref[pl.ds(t2, h1, stride=A_PITCH), :] = a[:h1, cols]
                a_ref[pl.ds(t2 + n2, h1, stride=A_PITCH), :] = a[h1:, cols]
        return carry

    lax.fori_loop(0, n2 // (wide * OUTER_GROUP), stage1, 0)

    def stage2(gi, carry):
        f1s = [gi * SPECTRUM_GROUP + j for j in range(SPECTRUM_GROUP)]
        srcs = [pl.ds(pl.multiple_of(f1 * A_PITCH, PITCH_ALIGN), 2 * n2) for f1 in f1s]
        specs = [_bdot(m2_ref[f1], a_ref[src, :].astype(BF16)) for f1, src in zip(f1s, srcs)]
        for f1, spec in zip(f1s, specs):
            dst = pl.multiple_of(f1 * 2 * n2, 2 * n2)
            o_ref[pl.ds(dst, 2 * n2), :] = spec.astype(o_ref.dtype)
        return carry

    lax.fori_loop(0, h1 // SPECTRUM_GROUP, stage2, 0)


def _filter_spectra(seq, p):
    n = 2 * seq
    tb = _fft_tables(seq)
    nblk = seq // FILT_ROWS
    half = FILT_ROWS // 2
    pack = FILT_PACK

    lane = jnp.arange(LANES, dtype=jnp.int32)
    feat = lane % pack

    def packed(v):
        v = jnp.transpose(v.reshape(2 * nblk, 2, half), (0, 2, 1)).reshape(n // 2, 2)
        return jnp.where(lane[None, :] < pack, v[:, 0:1], v[:, 1:2])

    fwd_pos = jnp.arange(seq, dtype=jnp.int32)
    pos = jnp.concatenate([fwd_pos, (seq - fwd_pos) % seq])
    t_fwd = jnp.linspace(0.0, 1.0, seq, dtype=F32)
    t = packed(jnp.concatenate([t_fwd, jnp.roll(t_fwd[::-1], 1)]))
    w = packed(2.0 * math.pi * pos.astype(F32) / seq)
    fb = jnp.linspace(1e-4, FILT_BANDS - 1, FILT_BANDS, dtype=F32)
    fb = jnp.tile(jnp.concatenate([jnp.zeros((1,), F32), fb, fb, jnp.zeros((pack - FILT_EMB,), F32)]), 2)
    ang = fb[None, :] * w
    zz = jnp.where(feat == 0, t,
                   jnp.where(feat <= FILT_BANDS, jnp.cos(ang),
                             jnp.where(feat <= 2 * FILT_BANDS, -jnp.sin(ang), 0.0)))

    def padw(a):
        a = jnp.pad(a, ((0, pack - a.shape[0]), (0, pack - a.shape[1])))
        zero = jnp.zeros_like(a)
        return jnp.concatenate([jnp.concatenate([a, zero], axis=1),
                                jnp.concatenate([zero, a], axis=1)], axis=0)

    def padv(a):
        return jnp.tile(jnp.pad(a, (0, pack - a.shape[0])), 2)[None, :]

    w1 = padw(p['filt_w1'])
    w2 = padw(p['filt_w2'])
    w3 = padw(p['filt_w3'])
    w4 = jnp.pad(p['filt_w4'], ((0, pack - FILT_HIDDEN), (0, 0))).reshape(pack, HYENA_ORDER, 2, D_HYENA)
    w4 = jnp.transpose(w4, (2, 0, 1, 3)).reshape(2, pack, HYENA_ORDER * D_HYENA)
    w4 = jnp.stack([jnp.pad(w4, ((0, 0), (0, pack), (0, 0))),
                    jnp.pad(w4, ((0, 0), (pack, 0), (0, 0)))], axis=1)
    min_decay = math.log(HYENA_TARGET) / SLOW_DECAY_PCT
    max_decay = math.log(HYENA_TARGET) / FAST_DECAY_PCT
    deltas = jnp.linspace(min_decay, max_decay, D_HYENA, dtype=F32)
    deltas = jnp.tile(deltas, HYENA_ORDER)[None, :]
    width = HYENA_ORDER * D_HYENA
    sq = _const_spec((LANES, LANES))
    vec = _const_spec((1, LANES))
    nsteps = nblk // FILT_BLOCKS
    filt, sums = pl.pallas_call(
        functools.partial(_filter_kernel, nsteps=nsteps),
        grid=(2 * nsteps,),
        in_specs=[pl.BlockSpec((FILT_BLOCKS * half, LANES), lambda i: (i, 0)),
                  sq, vec, sq, vec, sq, vec, vec,
                  pl.BlockSpec((1, 2, LANES, width), lambda i: (i // nsteps, 0, 0, 0)),
                  _const_spec((1, width))],
        out_specs=[pl.BlockSpec((FILT_BLOCKS * FILT_ROWS, width), lambda i: (i, 0)),
                   pl.BlockSpec((1, 1, width), lambda i: (i // nsteps, 0, 0))],
        out_shape=[jax.ShapeDtypeStruct((n, width), F32),
                   jax.ShapeDtypeStruct((2, 1, width), F32)],
        compiler_params=pltpu.CompilerParams(
            dimension_semantics=("arbitrary",), vmem_limit_bytes=VMEM_LIMIT),
        name="hyena_filter",
    )(zz, w1, padv(p['filt_b1']), w2, padv(p['filt_b2']), w3, padv(p['filt_b3']),
      padv(p['filt_freq']), w4, deltas)
    n1, h1 = tb['n1'], tb['h1']
    return pl.pallas_call(
        functools.partial(_kspec_kernel, n1=n1),
        grid=(width // LANES,),
        in_specs=[pl.BlockSpec((n, LANES), lambda j: (0, j)),
                  pl.BlockSpec((2, 1, LANES), lambda j: (0, 0, j)),
                  _const_spec((n1, n1)), _const_spec((h1, 2 * FFT_N2, 2 * FFT_N2))],
        out_specs=pl.BlockSpec((n, LANES), lambda j: (0, j)),
        out_shape=jax.ShapeDtypeStruct((n, width), BF16),
        scratch_shapes=[pltpu.VMEM((h1 * A_PITCH, LANES), F32),
                        pltpu.VMEM((n1 * T_PITCH, LANES), F32)],
        compiler_params=pltpu.CompilerParams(
            dimension_semantics=("parallel",), vmem_limit_bytes=VMEM_LIMIT),
        name="hyena_filter_spectrum",
    )(filt, sums, _bf16_const(tb['m1_full']), _bf16_const(tb['m2']))


CHUNK_UNROLL = 8


def _for_chunks(count, body):
    body(0, True, False)

    def step(t1, carry):
        body(t1, False, False)
        return carry

    lax.fori_loop(1, count - CHUNK_UNROLL + 1, step, 0, unroll=CHUNK_UNROLL)
    for t1 in range(count - CHUNK_UNROLL + 1, count):
        body(t1, False, t1 == count - 1)


def _short_conv_rows(ref, w_ref, b_ref, r0, nrows, first, last):
    x = ref[pl.ds(r0, nrows), :]
    rid = lax.broadcasted_iota(jnp.int32, x.shape, 0)
    if first:
        prev = jnp.where(rid == 0, 0.0, pltpu.roll(x, 1, axis=0))
    else:
        prev = ref[pl.ds(r0 - 1, nrows), :]
    if last:
        nxt = jnp.where(rid == nrows - 1, 0.0, pltpu.roll(x, nrows - 1, axis=0))
    else:
        nxt = ref[pl.ds(r0 + 1, nrows), :]
    return prev * w_ref[0:1, :] + x * w_ref[1:2, :] + nxt * w_ref[2:3, :] + b_ref[...]


def _long_conv_kernel(*refs, n1, conv_z, conv_g):
    refs = list(refs)
    a_ref, tp_ref = refs[-2:]
    slabs = a_ref.shape[0]

    def take_operand(conv):
        return [tuple(refs.pop(0) for _ in range(3 if conv else 1)) for _ in range(slabs)]

    z_ops = take_operand(conv_z)
    g_ops = take_operand(conv_g)
    k_ref, bias_ref, m1_ref, m1i_ref, m2_ref, m2i_ref, o_ref = refs[:-2]
    n2 = FFT_N2
    h1 = n1 // 2

    def put(ref, idx, x):
        for s in range(slabs):
            ref[s, idx, :] = x[:, s * LANES:(s + 1) * LANES]

    def get(ref, idx):
        return jnp.concatenate([ref[s, idx, :] for s in range(slabs)], axis=1)

    def rows_of(ops, conv, t1, first, last):
        r0 = t1 * n2 if isinstance(t1, int) else pl.multiple_of(t1 * n2, n2)
        if conv:
            parts = [_short_conv_rows(ref, w_ref, b_ref, r0, n2, first, last) for ref, w_ref, b_ref in ops]
        else:
            parts = [ref[pl.ds(r0, n2), :] for ref, in ops]
        return jnp.concatenate(parts, axis=1)

    def z_rows(t1, first, last):
        return rows_of(z_ops, conv_z, t1, first, last)

    def g_rows(t1, first, last):
        return rows_of(g_ops, conv_g, t1, first, last)

    def pitched(t1):
        r0 = t1 * T_PITCH
        return pl.ds(r0 if isinstance(t1, int) else pl.multiple_of(r0, PITCH_ALIGN), n2)

    park_z = conv_z and o_ref.dtype == F32

    def natural(t1):
        r0 = t1 * n2 if isinstance(t1, int) else pl.multiple_of(t1 * n2, n2)
        return pl.ds(r0, n2)

    def pad(t1, first, last):
        z = z_rows(t1, first, last)
        put(tp_ref, pitched(t1), z)
        if park_z:
            o_ref[natural(t1), :] = z

    _for_chunks(h1, pad)

    wide = max(1, MXU_COLUMNS // (slabs * LANES))
    width = slabs * LANES

    def t2_sets(gi):
        return [[(gi * OUTER_GROUP + u) * wide + j for j in range(wide)] for u in range(OUTER_GROUP)]

    def stage1(gi, carry):
        sets = t2_sets(gi)
        zts = [jnp.concatenate([get(tp_ref, pl.ds(t2, h1, stride=T_PITCH)) for t2 in t2s], axis=1)
               for t2s in sets]
        prods = [_bdot(m1_ref[...], zt.astype(BF16)) for zt in zts]
        for t2s, a in zip(sets, prods):
            for j, t2 in enumerate(t2s):
                cols = slice(j * width, (j + 1) * width)
                put(a_ref, pl.ds(t2, h1, stride=A_PITCH), a[:h1, cols])
                put(a_ref, pl.ds(t2 + n2, h1, stride=A_PITCH), a[h1:, cols])
        return carry

    lax.fori_loop(0, n2 // (wide * OUTER_GROUP), stage1, 0)

    group = STAGE2_GROUP // slabs

    def stage2(gi, carry):
        f1s = [gi * group + j for j in range(group)]
        ras = [pl.ds(pl.multiple_of(f1 * A_PITCH, PITCH_ALIGN), 2 * n2) for f1 in f1s]
        xs = [_bdot(m2_ref[f1], get(a_ref, ra).astype(BF16)) for f1, ra in zip(f1s, ras)]
        ys = []
        for f1, x in zip(f1s, xs):
            kk = k_ref[pl.ds(pl.multiple_of(f1 * 2 * n2, 2 * n2), 2 * n2), :].astype(F32)
            xr, xi, kr, ki = x[:n2], x[n2:], kk[:n2], kk[n2:]
            ys.append(jnp.concatenate([xr * kr - xi * ki, xr * ki + xi * kr], axis=0).astype(BF16))
        outs = [_bdot(m2i_ref[f1], y) for f1, y in zip(f1s, ys)]
        for ra, out in zip(ras, outs):
            put(a_ref, ra, out)
        return carry

    lax.fori_loop(0, h1 // group, stage2, 0)

    def stage3(gi, carry):
        sets = t2_sets(gi)
        specs = []
        for t2s in sets:
            br = jnp.concatenate([get(a_ref, pl.ds(t2, h1, stride=A_PITCH)) for t2 in t2s], axis=1)
            bi = jnp.concatenate([get(a_ref, pl.ds(t2 + n2, h1, stride=A_PITCH)) for t2 in t2s], axis=1)
            specs.append(jnp.concatenate([br, bi], axis=0).astype(BF16))
        prods = [_bdot(m1i_ref[...], sp) for sp in specs]
        for t2s, y in zip(sets, prods):
            for j, t2 in enumerate(t2s):
                put(tp_ref, pl.ds(t2, h1, stride=T_PITCH), y[:, j * width:(j + 1) * width])
        return carry

    lax.fori_loop(0, n2 // (wide * OUTER_GROUP), stage3, 0)

    bias = bias_ref[...]

    def finish(t1, first, last):
        y = get(tp_ref, pitched(t1))
        z = o_ref[natural(t1), :] if park_z else z_rows(t1, first, last)
        out = g_rows(t1, first, last) * (y + bias * z)
        o_ref[natural(t1), :] = out.astype(o_ref.dtype)

    _for_chunks(h1, finish)


def _bf16_const(a):
    return jnp.asarray(a, F32).astype(BF16)


def _long_conv_slabs(seq):
    n1 = 2 * seq // FFT_N2
    h1 = n1 // 2
    tables = 2 * (2 * n1 * h1 + 2 * h1 * (2 * FFT_N2) ** 2)
    for slabs in (2, 1):
        io = 3 * 2 * seq * 4
        per_lane = io + 2 * 2 * seq * 2 + h1 * (A_PITCH + T_PITCH) * 4
        if per_lane * slabs * LANES + tables <= VMEM_LIMIT - VMEM_HEADROOM:
            return slabs
    raise ValueError(f"long conv of length {seq} does not fit VMEM")


def _long_conv(z, z_off, gate, g_off, kspec, k_off, bias, batch, seq, short=None, conv_z=False,
               out_dtype=F32):
    n = 2 * seq
    tb = _fft_tables(seq)
    n1, h1 = tb['n1'], tb['h1']
    conv_g = short is not None
    slabs = _long_conv_slabs(seq)
    width = slabs * LANES
    assert z_off % slabs == 0 and g_off % slabs == 0 and k_off % slabs == 0

    def operand(arr, off, conv):
        specs, args = [], []
        for s in range(slabs):
            specs.append(pl.BlockSpec((seq, LANES), lambda j, b, s=s: (b, off + j * slabs + s)))
            args.append(arr)
            if conv:
                specs += [pl.BlockSpec((3, LANES), lambda j, b, s=s: (0, off + j * slabs + s)),
                          pl.BlockSpec((1, LANES), lambda j, b, s=s: (0, off + j * slabs + s))]
                args += list(short)
        return specs, args

    z_specs, z_args = operand(z, z_off, conv_z)
    g_specs, g_args = operand(gate, g_off, conv_g)
    k_blk = k_off // slabs
    return pl.pallas_call(
        functools.partial(_long_conv_kernel, n1=n1, conv_z=conv_z, conv_g=conv_g),
        grid=(D_HYENA // width, batch),
        in_specs=z_specs + g_specs + [
                  pl.BlockSpec((n, width), lambda j, b: (0, k_blk + j)),
                  pl.BlockSpec((1, width), lambda j, b: (0, j)),
                  _const_spec((n1, h1)), _const_spec((h1, n1)),
                  _const_spec((h1, 2 * FFT_N2, 2 * FFT_N2)),
                  _const_spec((h1, 2 * FFT_N2, 2 * FFT_N2))],
        out_specs=pl.BlockSpec((seq, width), lambda j, b: (b, j)),
        out_shape=jax.ShapeDtypeStruct((batch * seq, D_HYENA), out_dtype),
        scratch_shapes=[pltpu.VMEM((slabs, h1 * A_PITCH, LANES), F32),
                        pltpu.VMEM((slabs, h1 * T_PITCH, LANES), F32)],
        compiler_params=pltpu.CompilerParams(
            dimension_semantics=("parallel", "parallel"), vmem_limit_bytes=VMEM_LIMIT),
        name="hyena_long_conv",
    )(*z_args, *g_args, kspec, bias, _bf16_const(tb['m1']), _bf16_const(tb['m1_inv']),
      _bf16_const(tb['m2']), _bf16_const(tb['m2_inv']))


def _layer(x3, p, rope):
    batch, seq, _ = x3.shape
    x = x3.reshape(batch * seq, D_MODEL)
    x = _ffn(x, p['ffn1_pre_g'], p['ffn1_w1'], p['ffn1_w3'], p['ffn1_w2'], p['ffn1_post_g'])
    hy, qkv, gate = _inproj(x, p['mix_pre_g'], p['w_in'], seq, rope)
    kspec = _filter_spectra(seq, p)
    tiles = D_HYENA // LANES
    short = (p['short_w'], p['short_b'])
    z1 = _long_conv(hy, 0, hy, tiles, kspec, 0, p['hyena_bias'][0:1], batch, seq, short=short, conv_z=True)
    yh = _long_conv(z1, 0, hy, 2 * tiles, kspec, tiles, p['hyena_bias'][1:2], batch, seq, short=short,
                    out_dtype=BF16)
    yr = _retention(qkv, gate, p['ret_log_decay_f'], p['ret_log_decay_b'], batch, seq)
    mix = (yh, yr, p['w_out'], p['mix_post_g'])
    x = _ffn(x, p['ffn2_pre_g'], p['ffn2_w1'], p['ffn2_w3'], p['ffn2_w2'], p['ffn2_post_g'], mix=mix)
    return x.reshape(batch, seq, D_MODEL)


_MATRIX_PARAMS = ('ffn1_w1', 'ffn1_w3', 'ffn1_w2', 'w_in', 'w_out', 'ffn2_w1', 'ffn2_w3', 'ffn2_w2')
_GAIN_PARAMS = ('ffn1_pre_g', 'ffn1_post_g', 'mix_pre_g', 'mix_post_g', 'ffn2_pre_g', 'ffn2_post_g')


def kernel(x_prompt, x_sample, ffn1_pre_g, ffn1_w1, ffn1_w3, ffn1_w2, ffn1_post_g, mix_pre_g, w_in, short_w, short_b, filt_w1, filt_b1, filt_w2, filt_b2, filt_w3, filt_b3, filt_w4, filt_freq, hyena_bias, ret_log_decay_f, ret_log_decay_b, w_out, mix_post_g, ffn2_pre_g, ffn2_w1, ffn2_w3, ffn2_w2, ffn2_post_g):
    params = dict(ffn1_pre_g=ffn1_pre_g, ffn1_w1=ffn1_w1, ffn1_w3=ffn1_w3, ffn1_w2=ffn1_w2,
                  ffn1_post_g=ffn1_post_g, mix_pre_g=mix_pre_g, w_in=w_in, short_w=short_w,
                  short_b=short_b, filt_w1=filt_w1, filt_b1=filt_b1, filt_w2=filt_w2,
                  filt_b2=filt_b2, filt_w3=filt_w3, filt_b3=filt_b3, filt_w4=filt_w4,
                  filt_freq=filt_freq, hyena_bias=hyena_bias, ret_log_decay_f=ret_log_decay_f,
                  ret_log_decay_b=ret_log_decay_b, w_out=w_out, mix_post_g=mix_post_g,
                  ffn2_pre_g=ffn2_pre_g, ffn2_w1=ffn2_w1, ffn2_w3=ffn2_w3, ffn2_w2=ffn2_w2,
                  ffn2_post_g=ffn2_post_g)
    depth = ffn1_w1.shape[0]
    rope = _rope_tables(max(x_prompt.shape[1], x_sample.shape[1]))

    def run(x):
        for l in range(depth):
            p = {k: v[l] for k, v in params.items()}
            for k in _MATRIX_PARAMS:
                p[k] = p[k].astype(BF16)
            for k in _GAIN_PARAMS:
                p[k] = p[k][None, :]
            p['short_b'] = p['short_b'][None, :]
            x = _layer(x, p, rope)
        return x

    return (run(x_prompt), run(x_sample))
```

```python
import functools
import math

import numpy as np
import jax
import jax.numpy as jnp
from jax import lax
from jax.experimental import pallas as pl
from jax.experimental.pallas import tpu as pltpu

F32 = jnp.float32
BF16 = jnp.bfloat16

D_MODEL = 1024
D_HYENA = 512
D_RET = 512
HYENA_ORDER = 2
N_RET_HEADS = 4
RET_HEAD_DIM = 128
D_FF = 2816
FILT_EMB = 33
FILT_BANDS = 16
FILT_HIDDEN = 64
ROPE_BASE = 10000.0
NORM_EPS = 1e-6
HYENA_TARGET = 1e-2
FAST_DECAY_PCT = 0.3
SLOW_DECAY_PCT = 1.5
N_HY_COLS = (HYENA_ORDER + 1) * D_HYENA
D_IN = N_HY_COLS + 4 * D_RET

LANES = 128
MXU_COLUMNS = 256
VMEM_LIMIT = 56 * 1024 * 1024
VMEM_HEADROOM = 4 * 1024 * 1024
FFT_N2 = 32
RET_CHUNK = 128
RET_GROUP = 32
TOKEN_TILE = 512
INPROJ_TILE = 1024


def _const_spec(shape):
    nd = len(shape)
    return pl.BlockSpec(shape, lambda *_: (0,) * nd, pipeline_mode=pl.Buffered(1))


def _rms(x, g):
    ms = jnp.mean(x * x, axis=-1, keepdims=True)
    return x * lax.rsqrt(ms + NORM_EPS) * g


def _bdot(a, b):
    return jnp.dot(a, b, preferred_element_type=F32)


def _ffn_core(x, pre_ref, w1_ref, w3_ref, w2_ref, post_ref, o_ref):
    h = _rms(x, pre_ref[...]).astype(BF16)
    a = _bdot(h, w1_ref[...])
    b = _bdot(h, w3_ref[...])
    g = (a * jax.nn.sigmoid(a) * b).astype(BF16)
    y = _bdot(g, w2_ref[...])
    o_ref[...] = x + 0.5 * _rms(y, post_ref[...])


def _ffn_kernel(x_ref, pre_ref, w1_ref, w3_ref, w2_ref, post_ref, o_ref):
    _ffn_core(x_ref[...], pre_ref, w1_ref, w3_ref, w2_ref, post_ref, o_ref)


def _mix_ffn_kernel(x_ref, yh_ref, yr_ref, wo_ref, mg_ref,
                    pre_ref, w1_ref, w3_ref, w2_ref, post_ref, o_ref):
    mixed = jnp.concatenate([yh_ref[...].astype(BF16), yr_ref[...].astype(BF16)], axis=1)
    y = _bdot(mixed, wo_ref[...])
    x = x_ref[...] + _rms(y, mg_ref[...])
    _ffn_core(x, pre_ref, w1_ref, w3_ref, w2_ref, post_ref, o_ref)


def _row_spec(tm, width):
    return pl.BlockSpec((tm, width), lambda i: (i, 0))


def _ffn(x, pre_g, w1, w3, w2, post_g, mix=None):
    t = x.shape[0]
    tm = TOKEN_TILE
    ffn_specs = [_const_spec((1, D_MODEL)), _const_spec((D_MODEL, D_FF)),
                 _const_spec((D_MODEL, D_FF)), _const_spec((D_FF, D_MODEL)),
                 _const_spec((1, D_MODEL))]
    ffn_args = (pre_g, w1, w3, w2, post_g)
    if mix is None:
        body, args = _ffn_kernel, (x,) + ffn_args
        specs = [_row_spec(tm, D_MODEL)] + ffn_specs
    else:
        yh, yr, w_out, mg = mix
        body, args = _mix_ffn_kernel, (x, yh, yr, w_out, mg) + ffn_args
        specs = [_row_spec(tm, D_MODEL), _row_spec(tm, D_HYENA), _row_spec(tm, D_RET),
                 _const_spec((D_HYENA + D_RET, D_MODEL)), _const_spec((1, D_MODEL))] + ffn_specs
    return pl.pallas_call(
        body,
        grid=(t // tm,),
        in_specs=specs,
        out_specs=_row_spec(tm, D_MODEL),
        out_shape=jax.ShapeDtypeStruct((t, D_MODEL), F32),
        compiler_params=pltpu.CompilerParams(
            dimension_semantics=("parallel",), vmem_limit_bytes=VMEM_LIMIT),
        name="ffn_mix" if mix is not None else "ffn",
    )(*args)


def _inproj_kernel(x_ref, g_ref, w_ref, cc_ref, ss_ref, hy_ref, qkv_ref, gate_ref):
    h = _rms(x_ref[...], g_ref[...]).astype(BF16)
    qk0 = N_HY_COLS
    v0 = qk0 + 2 * D_RET
    g0 = v0 + D_RET
    qk = _bdot(h, w_ref[:, qk0:v0])
    hy_ref[...] = _bdot(h, w_ref[:, :qk0])
    gate_ref[...] = _bdot(h, w_ref[:, g0:])
    qkv_ref[:, 2 * D_RET:] = _bdot(h, w_ref[:, v0:g0]).astype(BF16)
    cc = cc_ref[...]
    ss = ss_ref[...]
    d = RET_HEAD_DIM
    for blk in range(2 * N_RET_HEADS):
        x = qk[:, blk * d:(blk + 1) * d]
        r = x * cc + pltpu.roll(x, d // 2, axis=1) * ss
        if blk >= N_RET_HEADS:
            r = r * (d ** -0.5)
        qkv_ref[:, blk * d:(blk + 1) * d] = r.astype(BF16)


def _inproj(x, g, w, seq, rope):
    t = x.shape[0]
    tm = INPROJ_TILE
    cc, ss = rope
    assert cc.shape[0] >= seq
    pos_blocks = seq // tm
    rope_spec = pl.BlockSpec((tm, RET_HEAD_DIM), lambda i: (i % pos_blocks, 0))
    return pl.pallas_call(
        _inproj_kernel,
        grid=(t // tm,),
        in_specs=[_row_spec(tm, D_MODEL), _const_spec((1, D_MODEL)),
                  _const_spec((D_MODEL, D_IN)), rope_spec, rope_spec],
        out_specs=[_row_spec(tm, N_HY_COLS), _row_spec(tm, 3 * D_RET), _row_spec(tm, D_RET)],
        out_shape=[jax.ShapeDtypeStruct((t, N_HY_COLS), F32),
                   jax.ShapeDtypeStruct((t, 3 * D_RET), BF16),
                   jax.ShapeDtypeStruct((t, D_RET), F32)],
        compiler_params=pltpu.CompilerParams(
            dimension_semantics=("parallel",), vmem_limit_bytes=VMEM_LIMIT),
        name="inproj",
    )(x, g, w, cc, ss)


def _ret_kernel(lgf_ref, lgb_ref, q_ref, k_ref, v_ref, g_ref, o_ref, kt_ref, *, seq):
    c = RET_CHUNK
    d = RET_HEAD_DIM
    n_chunks = seq // c
    head = pl.program_id(1)
    lgf = jnp.full((c, d), lgf_ref[head], F32)
    lgb = jnp.full((c, d), lgb_ref[head], F32)
    row = lax.broadcasted_iota(jnp.int32, (c, d), 0).astype(F32)
    col = lax.broadcasted_iota(jnp.int32, (c, d), 1).astype(F32)
    diff = row - col
    dmat = jnp.where(diff >= 0.0, jnp.exp(jnp.maximum(diff, 0.0) * lgf),
                     jnp.exp(jnp.maximum(-diff, 0.0) * lgb))
    wq_f = jnp.exp((row + 1.0) * lgf)
    wk_f = jnp.exp((c - 1.0 - row) * lgf)
    wq_b = jnp.exp((c - row) * lgb)
    wk_b = jnp.exp(row * lgb)
    gc_f = jnp.exp(c * lgf)
    gc_b = jnp.exp(c * lgb)

    grp = RET_GROUP
    n_groups = n_chunks // grp

    def fwd(gi, state):
        rows = [pl.ds(pl.multiple_of((gi * grp + j) * c, c), c) for j in range(grp)]
        qbs, kts, vs = [], [], []
        for r in rows:
            ktb = k_ref[r, :].astype(F32).T.astype(BF16)
            kt_ref[r, :] = ktb
            qbs.append(q_ref[r, :])
            kts.append(ktb)
            vs.append(v_ref[r, :])
        scores = [_bdot(qb, ktb) for qb, ktb in zip(qbs, kts)]
        kvs = [_bdot(ktb, (v.astype(F32) * wk_f).astype(BF16)) for ktb, v in zip(kts, vs)]
        states = []
        for kv in kvs:
            states.append(state)
            state = state * gc_f + kv
        intras = [_bdot((s * dmat).astype(BF16), v) for s, v in zip(scores, vs)]
        crosses = [_bdot(qb, st.astype(BF16)) for qb, st in zip(qbs, states)]
        for r, intra, cross in zip(rows, intras, crosses):
            o_ref[r, :] = intra + wq_f * cross
        return state

    lax.fori_loop(0, n_groups, fwd, jnp.zeros((d, d), F32))

    def bwd(gi, state):
        rows = [pl.ds(pl.multiple_of((n_chunks - 1 - gi * grp - j) * c, c), c) for j in range(grp)]
        kvs = [_bdot(kt_ref[r, :], (v_ref[r, :].astype(F32) * wk_b).astype(BF16)) for r in rows]
        states = []
        for kv in kvs:
            states.append(state)
            state = state * gc_b + kv
        crosses = [_bdot(q_ref[r, :], st.astype(BF16)) for r, st in zip(rows, states)]
        outs = [o_ref[r, :] + wq_b * cross for r, cross in zip(rows, crosses)]
        means = [jnp.mean(o * o, axis=-1, keepdims=True) for o in outs]
        norms = [lax.rsqrt(m + NORM_EPS) for m in means]
        gates = [g_ref[r, :] for r in rows]
        gates = [g * jax.nn.sigmoid(g) for g in gates]
        for r, o, nrm, g in zip(rows, outs, norms, gates):
            o_ref[r, :] = g * (o * nrm)
        return state

    lax.fori_loop(0, n_groups, bwd, jnp.zeros((d, d), F32))


def _rope_tables(seq):
    d = RET_HEAD_DIM
    inv = 1.0 / (ROPE_BASE ** (jnp.arange(0, d, 2, dtype=F32) / d))
    ang = jnp.arange(seq, dtype=F32)[:, None] * inv[None, :]
    c, s = jnp.cos(ang), jnp.sin(ang)
    return jnp.concatenate([c, c], axis=-1), jnp.concatenate([-s, s], axis=-1)


def _retention(qkv, gate, lg_f, lg_b, batch, seq):
    assert RET_CHUNK == RET_HEAD_DIM
    heads = N_RET_HEADS

    def col(off):
        return pl.BlockSpec((seq, LANES), lambda b, h, *_: (b, off * heads + h))

    grid_spec = pltpu.PrefetchScalarGridSpec(
        num_scalar_prefetch=2,
        grid=(batch, heads),
        in_specs=[col(0), col(1), col(2), col(0)],
        out_specs=pl.BlockSpec((seq, LANES), lambda b, h, *_: (b, h)),
        scratch_shapes=[pltpu.VMEM((seq, RET_CHUNK), BF16)],
    )
    return pl.pallas_call(
        functools.partial(_ret_kernel, seq=seq),
        grid_spec=grid_spec,
        out_shape=jax.ShapeDtypeStruct((batch * seq, D_RET), F32),
        compiler_params=pltpu.CompilerParams(
            dimension_semantics=("parallel", "parallel"), vmem_limit_bytes=VMEM_LIMIT),
        name="retention",
    )(lg_f, lg_b, qkv, qkv, qkv, gate)


def _split(x):
    hi = x.astype(BF16)
    return hi, (x - hi.astype(F32)).astype(BF16)


def _dot3(a, b):
    a_hi, a_lo = a
    b_hi, b_lo = b
    return _bdot(a_hi, b_hi) + _bdot(a_hi, b_lo) + _bdot(a_lo, b_hi)


@functools.lru_cache(maxsize=None)
def _fft_tables(seq):
    n = 2 * seq
    n2 = FFT_N2
    n1 = n // n2
    h1 = n1 // 2
    f1 = np.arange(h1, dtype=np.float64) + 0.5
    th = 2.0 * np.pi * f1[:, None] * np.arange(n1, dtype=np.float64)[None, :] / n1
    m1_full = np.concatenate([np.cos(th), -np.sin(th)], axis=0)
    m1 = m1_full[:, :h1]
    m1_inv = (2.0 / n) * m1.T
    t2 = np.arange(n2, dtype=np.float64)
    phi = 2.0 * np.pi * (np.arange(n2, dtype=np.float64)[None, :, None] * t2[None, None, :] / n2
                         + f1[:, None, None] * t2[None, None, :] / n)
    gr, gi = np.cos(phi), -np.sin(phi)
    m2 = np.concatenate([np.concatenate([gr, -gi], axis=2),
                         np.concatenate([gi, gr], axis=2)], axis=1)
    m2_inv = np.transpose(m2, (0, 2, 1))
    return dict(n1=n1, h1=h1, m1_full=m1_full, m1=m1, m1_inv=m1_inv, m2=m2, m2_inv=m2_inv)


OUTER_GROUP = 16
STAGE2_GROUP = 32
SPECTRUM_GROUP = 64
FILT_ROWS = 512
FILT_BLOCKS = 4
FILT_PACK = LANES // 2
PITCH_ALIGN = 4
A_PITCH = 2 * FFT_N2 + PITCH_ALIGN
T_PITCH = FFT_N2 + PITCH_ALIGN


def _filter_kernel(z_ref, w1_ref, b1_ref, w2_ref, b2_ref, w3_ref, b3_ref, fr_ref,
                   w4_ref, dl_ref, o_ref, sum_ref, *, nsteps):
    i = pl.program_id(0)
    half = FILT_ROWS // 2
    fr = fr_ref[...]
    zs = [z_ref[b * half:(b + 1) * half, :] for b in range(FILT_BLOCKS)]

    def layer(xs, w_ref, b_ref):
        w = _split(w_ref[...])
        pre = [_dot3(_split(x), w) for x in xs]
        return [jnp.sin(fr * (v + b_ref[...])) for v in pre]

    hs = layer(layer(layer(zs, w1_ref, b1_ref), w2_ref, b2_ref), w3_ref, b3_ref)
    hs = [_split(h) for h in hs]
    adl = jnp.abs(dl_ref[...])
    backward = i >= nsteps
    for side in range(2):
        lane = side * FILT_PACK
        ts = [jnp.broadcast_to(z[:, lane:lane + 1], z.shape) for z in zs]
        for c in range(HYENA_ORDER * D_HYENA // LANES):
            cols = slice(c * LANES, (c + 1) * LANES)
            w4 = _split(w4_ref[0, side, :, cols])
            hcs = [_dot3(h, w4) * jnp.exp(-t * adl[:, cols]) for h, t in zip(hs, ts)]
            part = sum(jnp.sum(jnp.abs(hc), axis=0, keepdims=True) for hc in hcs)

            if side == 0:
                @pl.when(i % nsteps == 0)
                def _():
                    sum_ref[0, :, cols] = part

                @pl.when(i % nsteps != 0)
                def _():
                    sum_ref[0, :, cols] = sum_ref[0, :, cols] + part
            else:
                sum_ref[0, :, cols] = sum_ref[0, :, cols] + part

            for b, hc in enumerate(hcs):
                if side == 0 and b == 0:
                    rows = lax.broadcasted_iota(jnp.int32, hc.shape, 0)
                    flipped = jnp.where(jnp.logical_and(rows == 0, i == nsteps), 0.0, -hc)
                else:
                    flipped = -hc
                r0 = b * FILT_ROWS + side * half
                o_ref[r0:r0 + half, cols] = jnp.where(backward, flipped, hc)


def _kspec_kernel(k_ref, s_ref, m1_ref, m2_ref, o_ref, a_ref, kp_ref, *, n1):
    n2 = FFT_N2
    h1 = n1 // 2
    inv_f = 1.0 / s_ref[0]
    inv_b = 1.0 / s_ref[1]

    def pad(t1, carry):
        src = pl.multiple_of(t1 * n2, n2)
        dst = pl.multiple_of(t1 * T_PITCH, PITCH_ALIGN)
        kp_ref[pl.ds(dst, n2), :] = k_ref[pl.ds(src, n2), :] * jnp.where(t1 < h1, inv_f, inv_b)
        return carry

    lax.fori_loop(0, n1, pad, 0, unroll=8)

    wide = MXU_COLUMNS // LANES

    def stage1(gi, carry):
        sets = [[(gi * OUTER_GROUP + u) * wide + j for j in range(wide)] for u in range(OUTER_GROUP)]
        kts = [jnp.concatenate([kp_ref[pl.ds(t2, n1, stride=T_PITCH), :] for t2 in t2s], axis=1)
               for t2s in sets]
        prods = [_bdot(m1_ref[...], kt.astype(BF16)) for kt in kts]
        for t2s, a in zip(sets, prods):
            for j, t2 in enumerate(t2s):
                cols = slice(j * LANES, (j + 1) * LANES)
                a_ref[pl.ds(t2, h1, stride=A_PITCH), :] = a[:h1, cols]
                a_ref[pl.ds(t2 + n2, h1, stride=A_PITCH), :] = a[h1:, cols]
        return carry

    lax.fori_loop(0, n2 // (wide * OUTER_GROUP), stage1, 0)

    def stage2(gi, carry):
        f1s = [gi * SPECTRUM_GROUP + j for j in range(SPECTRUM_GROUP)]
        srcs = [pl.ds(pl.multiple_of(f1 * A_PITCH, PITCH_ALIGN), 2 * n2) for f1 in f1s]
        specs = [_bdot(m2_ref[f1], a_ref[src, :].astype(BF16)) for f1, src in zip(f1s, srcs)]
        for f1, spec in zip(f1s, specs):
            dst = pl.multiple_of(f1 * 2 * n2, 2 * n2)
            o_ref[pl.ds(dst, 2 * n2), :] = spec.astype(o_ref.dtype)
        return carry

    lax.fori_loop(0, h1 // SPECTRUM_GROUP, stage2, 0)


def _filter_spectra(seq, p):
    n = 2 * seq
    tb = _fft_tables(seq)
    nblk = seq // FILT_ROWS
    half = FILT_ROWS // 2
    pack = FILT_PACK

    def packed(v):
        return jnp.transpose(v.reshape(2 * nblk, 2, half), (0, 2, 1)).reshape(n // 2, 2, 1)

    fwd_pos = jnp.arange(seq, dtype=jnp.int32)
    pos = jnp.concatenate([fwd_pos, (seq - fwd_pos) % seq])
    t_fwd = jnp.linspace(0.0, 1.0, seq, dtype=F32)
    t = packed(jnp.concatenate([t_fwd, jnp.roll(t_fwd[::-1], 1)]))
    w = packed(2.0 * math.pi * pos.astype(F32) / seq)
    fb = jnp.linspace(1e-4, FILT_BANDS - 1, FILT_BANDS, dtype=F32)[None, None, :]
    zero = jnp.zeros((n // 2, 2, pack - FILT_EMB), F32)
    zz = jnp.concatenate([t, jnp.cos(fb * w), -jnp.sin(fb * w), zero], axis=-1).reshape(n // 2, LANES)

    def padw(a):
        a = jnp.pad(a, ((0, pack - a.shape[0]), (0, pack - a.shape[1])))
        zero = jnp.zeros_like(a)
        return jnp.concatenate([jnp.concatenate([a, zero], axis=1),
                                jnp.concatenate([zero, a], axis=1)], axis=0)

    def padv(a):
        return jnp.tile(jnp.pad(a, (0, pack - a.shape[0])), 2)[None, :]

    w1 = padw(p['filt_w1'])
    w2 = padw(p['filt_w2'])
    w3 = padw(p['filt_w3'])
    w4 = jnp.pad(p['filt_w4'], ((0, pack - FILT_HIDDEN), (0, 0))).reshape(pack, HYENA_ORDER, 2, D_HYENA)
    w4 = jnp.transpose(w4, (2, 0, 1, 3)).reshape(2, pack, HYENA_ORDER * D_HYENA)
    w4 = jnp.stack([jnp.pad(w4, ((0, 0), (0, pack), (0, 0))),
                    jnp.pad(w4, ((0, 0), (pack, 0), (0, 0)))], axis=1)
    min_decay = math.log(HYENA_TARGET) / SLOW_DECAY_PCT
    max_decay = math.log(HYENA_TARGET) / FAST_DECAY_PCT
    deltas = jnp.linspace(min_decay, max_decay, D_HYENA, dtype=F32)
    deltas = jnp.tile(deltas, HYENA_ORDER)[None, :]
    width = HYENA_ORDER * D_HYENA
    sq = _const_spec((LANES, LANES))
    vec = _const_spec((1, LANES))
    nsteps = nblk // FILT_BLOCKS
    filt, sums = pl.pallas_call(
        functools.partial(_filter_kernel, nsteps=nsteps),
        grid=(2 * nsteps,),
        in_specs=[pl.BlockSpec((FILT_BLOCKS * half, LANES), lambda i: (i, 0)),
                  sq, vec, sq, vec, sq, vec, vec,
                  pl.BlockSpec((1, 2, LANES, width), lambda i: (i // nsteps, 0, 0, 0)),
                  _const_spec((1, width))],
        out_specs=[pl.BlockSpec((FILT_BLOCKS * FILT_ROWS, width), lambda i: (i, 0)),
                   pl.BlockSpec((1, 1, width), lambda i: (i // nsteps, 0, 0))],
        out_shape=[jax.ShapeDtypeStruct((n, width), F32),
                   jax.ShapeDtypeStruct((2, 1, width), F32)],
        compiler_params=pltpu.CompilerParams(
            dimension_semantics=("arbitrary",), vmem_limit_bytes=VMEM_LIMIT),
        name="hyena_filter",
    )(zz, w1, padv(p['filt_b1']), w2, padv(p['filt_b2']), w3, padv(p['filt_b3']),
      padv(p['filt_freq']), w4, deltas)
    n1, h1 = tb['n1'], tb['h1']
    return pl.pallas_call(
        functools.partial(_kspec_kernel, n1=n1),
        grid=(width // LANES,),
        in_specs=[pl.BlockSpec((n, LANES), lambda j: (0, j)),
                  pl.BlockSpec((2, 1, LANES), lambda j: (0, 0, j)),
                  _const_spec((n1, n1)), _const_spec((h1, 2 * FFT_N2, 2 * FFT_N2))],
        out_specs=pl.BlockSpec((n, LANES), lambda j: (0, j)),
        out_shape=jax.ShapeDtypeStruct((n, width), BF16),
        scratch_shapes=[pltpu.VMEM((h1 * A_PITCH, LANES), F32),
                        pltpu.VMEM((n1 * T_PITCH, LANES), F32)],
        compiler_params=pltpu.CompilerParams(
            dimension_semantics=("parallel",), vmem_limit_bytes=VMEM_LIMIT),
        name="hyena_filter_spectrum",
    )(filt, sums, _bf16_const(tb['m1_full']), _bf16_const(tb['m2']))


CHUNK_UNROLL = 8


def _for_chunks(count, body):
    body(0, True, False)

    def step(t1, carry):
        body(t1, False, False)
        return carry

    lax.fori_loop(1, count - CHUNK_UNROLL + 1, step, 0, unroll=CHUNK_UNROLL)
    for t1 in range(count - CHUNK_UNROLL + 1, count):
        body(t1, False, t1 == count - 1)


def _short_conv_rows(ref, w_ref, b_ref, r0, nrows, first, last):
    x = ref[pl.ds(r0, nrows), :]
    rid = lax.broadcasted_iota(jnp.int32, x.shape, 0)
    if first:
        prev = jnp.where(rid == 0, 0.0, pltpu.roll(x, 1, axis=0))
    else:
        prev = ref[pl.ds(r0 - 1, nrows), :]
    if last:
        nxt = jnp.where(rid == nrows - 1, 0.0, pltpu.roll(x, nrows - 1, axis=0))
    else:
        nxt = ref[pl.ds(r0 + 1, nrows), :]
    return prev * w_ref[0:1, :] + x * w_ref[1:2, :] + nxt * w_ref[2:3, :] + b_ref[...]


def _long_conv_kernel(*refs, n1):
    refs = list(refs)
    a_ref, tp_ref = refs[-2:]
    slabs = a_ref.shape[0]

    def take_operand():
        return [tuple(refs.pop(0) for _ in range(3)) for _ in range(slabs)]

    v_ops = take_operand()
    g_ops = take_operand()
    k_ref, bias_ref, m1_ref, m1i_ref, m2_ref, m2i_ref, o_ref = refs[:-2]
    n2 = FFT_N2
    h1 = n1 // 2
    order = pl.program_id(2)

    def put(ref, idx, x):
        for s in range(slabs):
            ref[s, idx, :] = x[:, s * LANES:(s + 1) * LANES]

    def get(ref, idx):
        return jnp.concatenate([ref[s, idx, :] for s in range(slabs)], axis=1)

    def conv_rows(ops, t1, first, last):
        r0 = t1 * n2 if isinstance(t1, int) else pl.multiple_of(t1 * n2, n2)
        parts = [_short_conv_rows(ref, w_ref, b_ref, r0, n2, first, last) for ref, w_ref, b_ref in ops]
        return jnp.concatenate(parts, axis=1)

    def pitched(t1):
        r0 = t1 * T_PITCH
        return pl.ds(r0 if isinstance(t1, int) else pl.multiple_of(r0, PITCH_ALIGN), n2)

    def natural(t1):
        r0 = t1 * n2 if isinstance(t1, int) else pl.multiple_of(t1 * n2, n2)
        return pl.ds(r0, n2)

    def pad_first(t1, first, last):
        z = conv_rows(v_ops, t1, first, last)
        put(tp_ref, pitched(t1), z)
        o_ref[natural(t1), :] = z

    def pad_next(t1, first, last):
        put(tp_ref, pitched(t1), o_ref[natural(t1), :])

    @pl.when(order == 0)
    def _():
        _for_chunks(h1, pad_first)

    @pl.when(order > 0)
    def _():
        _for_chunks(h1, pad_next)

    wide = max(1, MXU_COLUMNS // (slabs * LANES))
    width = slabs * LANES

    def t2_sets(gi):
        return [[(gi * OUTER_GROUP + u) * wide + j for j in range(wide)] for u in range(OUTER_GROUP)]

    def stage1(gi, carry):
        sets = t2_sets(gi)
        zts = [jnp.concatenate([get(tp_ref, pl.ds(t2, h1, stride=T_PITCH)) for t2 in t2s], axis=1)
               for t2s in sets]
        prods = [_bdot(m1_ref[...], zt.astype(BF16)) for zt in zts]
        for t2s, a in zip(sets, prods):
            for j, t2 in enumerate(t2s):
                cols = slice(j * width, (j + 1) * width)
                put(a_ref, pl.ds(t2, h1, stride=A_PITCH), a[:h1, cols])
                put(a_ref, pl.ds(t2 + n2, h1, stride=A_PITCH), a[h1:, cols])
        return carry

    lax.fori_loop(0, n2 // (wide * OUTER_GROUP), stage1, 0)

    group = STAGE2_GROUP // slabs

    def stage2(gi, carry):
        f1s = [gi * group + j for j in range(group)]
        ras = [pl.ds(pl.multiple_of(f1 * A_PITCH, PITCH_ALIGN), 2 * n2) for f1 in f1s]
        xs = [_bdot(m2_ref[f1], get(a_ref, ra).astype(BF16)) for f1, ra in zip(f1s, ras)]
        ys = []
        for f1, x in zip(f1s, xs):
            kk = k_ref[pl.ds(pl.multiple_of(f1 * 2 * n2, 2 * n2), 2 * n2), :].astype(F32)
            xr, xi, kr, ki = x[:n2], x[n2:], kk[:n2], kk[n2:]
            ys.append(jnp.concatenate([xr * kr - xi * ki, xr * ki + xi * kr], axis=0).astype(BF16))
        outs = [_bdot(m2i_ref[f1], y) for f1, y in zip(f1s, ys)]
        for ra, out in zip(ras, outs):
            put(a_ref, ra, out)
        return carry

    lax.fori_loop(0, h1 // group, stage2, 0)

    def stage3(gi, carry):
        sets = t2_sets(gi)
        specs = []
        for t2s in sets:
            br = jnp.concatenate([get(a_ref, pl.ds(t2, h1, stride=A_PITCH)) for t2 in t2s], axis=1)
            bi = jnp.concatenate([get(a_ref, pl.ds(t2 + n2, h1, stride=A_PITCH)) for t2 in t2s], axis=1)
            specs.append(jnp.concatenate([br, bi], axis=0).astype(BF16))
        prods = [_bdot(m1i_ref[...], sp) for sp in specs]
        for t2s, y in zip(sets, prods):
            for j, t2 in enumerate(t2s):
                put(tp_ref, pl.ds(t2, h1, stride=T_PITCH), y[:, j * width:(j + 1) * width])
        return carry

    lax.fori_loop(0, n2 // (wide * OUTER_GROUP), stage3, 0)

    bias = bias_ref[0]

    def finish(t1, first, last):
        y = get(tp_ref, pitched(t1))
        z = o_ref[natural(t1), :]
        o_ref[natural(t1), :] = conv_rows(g_ops, t1, first, last) * (y + bias * z)

    _for_chunks(h1, finish)


def _bf16_const(a):
    return jnp.asarray(a, F32).astype(BF16)


def _long_conv_slabs(seq):
    n1 = 2 * seq // FFT_N2
    h1 = n1 // 2
    tables = 2 * (2 * n1 * h1 + 2 * h1 * (2 * FFT_N2) ** 2)
    for slabs in (2, 1):
        io = 3 * 2 * seq * 4
        per_lane = io + 2 * 2 * seq * 2 + h1 * (A_PITCH + T_PITCH) * 4
        if per_lane * slabs * LANES + tables <= VMEM_LIMIT - VMEM_HEADROOM:
            return slabs
    raise ValueError(f"long conv of length {seq} does not fit VMEM")


def _long_conv(hy, kspec, bias, short, batch, seq):
    n = 2 * seq
    tb = _fft_tables(seq)
    n1, h1 = tb['n1'], tb['h1']
    slabs = _long_conv_slabs(seq)
    width = slabs * LANES
    tiles = D_HYENA // LANES
    blocks = tiles // slabs

    def operand(col_block):
        specs, args = [], []
        for s in range(slabs):
            col = functools.partial(col_block, s=s)
            specs += [pl.BlockSpec((seq, LANES), lambda j, b, o, col=col: (b, col(j, o))),
                      pl.BlockSpec((3, LANES), lambda j, b, o, col=col: (0, col(j, o))),
                      pl.BlockSpec((1, LANES), lambda j, b, o, col=col: (0, col(j, o)))]
            args += [hy, short[0], short[1]]
        return specs, args

    v_specs, v_args = operand(lambda j, o, s: j * slabs + s)
    g_specs, g_args = operand(lambda j, o, s: (1 + o) * tiles + j * slabs + s)
    return pl.pallas_call(
        functools.partial(_long_conv_kernel, n1=n1),
        grid=(blocks, batch, HYENA_ORDER),
        in_specs=v_specs + g_specs + [
                  pl.BlockSpec((n, width), lambda j, b, o: (0, o * blocks + j)),
                  pl.BlockSpec((1, 1, width), lambda j, b, o: (o, 0, j)),
                  _const_spec((n1, h1)), _const_spec((h1, n1)),
                  _const_spec((h1, 2 * FFT_N2, 2 * FFT_N2)),
                  _const_spec((h1, 2 * FFT_N2, 2 * FFT_N2))],
        out_specs=pl.BlockSpec((seq, width), lambda j, b, o: (b, j)),
        out_shape=jax.ShapeDtypeStruct((batch * seq, D_HYENA), F32),
        scratch_shapes=[pltpu.VMEM((slabs, h1 * A_PITCH, LANES), F32),
                        pltpu.VMEM((slabs, h1 * T_PITCH, LANES), F32)],
        compiler_params=pltpu.CompilerParams(
            dimension_semantics=("parallel", "parallel", "arbitrary"), vmem_limit_bytes=VMEM_LIMIT),
        name="hyena_long_conv",
    )(*v_args, *g_args, kspec, bias, _bf16_const(tb['m1']), _bf16_const(tb['m1_inv']),
      _bf16_const(tb['m2']), _bf16_const(tb['m2_inv']))


def _layer(x3, p, rope):
    batch, seq, _ = x3.shape
    x = x3.reshape(batch * seq, D_MODEL)
    x = _ffn(x, p['ffn1_pre_g'], p['ffn1_w1'], p['ffn1_w3'], p['ffn1_w2'], p['ffn1_post_g'])
    hy, qkv, gate = _inproj(x, p['mix_pre_g'], p['w_in'], seq, rope)
    kspec = _filter_spectra(seq, p)
    short = (p['short_w'], p['short_b'])
    yh = _long_conv(hy, kspec, p['hyena_bias'][:, None, :], short, batch, seq)
    yr = _retention(qkv, gate, p['ret_log_decay_f'], p['ret_log_decay_b'], batch, seq)
    mix = (yh, yr, p['w_out'], p['mix_post_g'])
    x = _ffn(x, p['ffn2_pre_g'], p['ffn2_w1'], p['ffn2_w3'], p['ffn2_w2'], p['ffn2_post_g'], mix=mix)
    return x.reshape(batch, seq, D_MODEL)


_MATRIX_PARAMS = ('ffn1_w1', 'ffn1_w3', 'ffn1_w2', 'w_in', 'w_out', 'ffn2_w1', 'ffn2_w3', 'ffn2_w2')
_GAIN_PARAMS = ('ffn1_pre_g', 'ffn1_post_g', 'mix_pre_g', 'mix_post_g', 'ffn2_pre_g', 'ffn2_post_g')


def kernel(x_prompt, x_sample, ffn1_pre_g, ffn1_w1, ffn1_w3, ffn1_w2, ffn1_post_g, mix_pre_g, w_in, short_w, short_b, filt_w1, filt_b1, filt_w2, filt_b2, filt_w3, filt_b3, filt_w4, filt_freq, hyena_bias, ret_log_decay_f, ret_log_decay_b, w_out, mix_post_g, ffn2_pre_g, ffn2_w1, ffn2_w3, ffn2_w2, ffn2_post_g):
    params = dict(ffn1_pre_g=ffn1_pre_g, ffn1_w1=ffn1_w1, ffn1_w3=ffn1_w3, ffn1_w2=ffn1_w2,
                  ffn1_post_g=ffn1_post_g, mix_pre_g=mix_pre_g, w_in=w_in, short_w=short_w,
                  short_b=short_b, filt_w1=filt_w1, filt_b1=filt_b1, filt_w2=filt_w2,
                  filt_b2=filt_b2, filt_w3=filt_w3, filt_b3=filt_b3, filt_w4=filt_w4,
                  filt_freq=filt_freq, hyena_bias=hyena_bias, ret_log_decay_f=ret_log_decay_f,
                  ret_log_decay_b=ret_log_decay_b, w_out=w_out, mix_post_g=mix_post_g,
                  ffn2_pre_g=ffn2_pre_g, ffn2_w1=ffn2_w1, ffn2_w3=ffn2_w3, ffn2_w2=ffn2_w2,
                  ffn2_post_g=ffn2_post_g)
    depth = ffn1_w1.shape[0]
    rope = _rope_tables(max(x_prompt.shape[1], x_sample.shape[1]))

    def run(x):
        for l in range(depth):
            p = {k: v[l] for k, v in params.items()}
            for k in _MATRIX_PARAMS:
                p[k] = p[k].astype(BF16)
            for k in _GAIN_PARAMS:
                p[k] = p[k][None, :]
            p['short_b'] = p['short_b'][None, :]
            x = _layer(x, p, rope)
        return x

    return (run(x_prompt), run(x_sample))
```

```python
import functools
import math

import numpy as np
import jax
import jax.numpy as jnp
from jax import lax
from jax.experimental import pallas as pl
from jax.experimental.pallas import tpu as pltpu

F32 = jnp.float32
BF16 = jnp.bfloat16

D_MODEL = 1024
D_HYENA = 512
D_RET = 512
HYENA_ORDER = 2
N_RET_HEADS = 4
RET_HEAD_DIM = 128
D_FF = 2816
FILT_EMB = 33
FILT_BANDS = 16
FILT_HIDDEN = 64
ROPE_BASE = 10000.0
NORM_EPS = 1e-6
HYENA_TARGET = 1e-2
FAST_DECAY_PCT = 0.3
SLOW_DECAY_PCT = 1.5
N_HY_COLS = (HYENA_ORDER + 1) * D_HYENA
D_IN = N_HY_COLS + 4 * D_RET

LANES = 128
MXU_COLUMNS = 256
VMEM_LIMIT = 56 * 1024 * 1024
VMEM_HEADROOM = 4 * 1024 * 1024
FFT_N2 = 32
RET_CHUNK = 128
RET_GROUP = 32
TOKEN_TILE = 512
INPROJ_TILE = 1024


def _const_spec(shape):
    nd = len(shape)
    return pl.BlockSpec(shape, lambda *_: (0,) * nd, pipeline_mode=pl.Buffered(1))


def _rms(x, g):
    ms = jnp.mean(x * x, axis=-1, keepdims=True)
    return x * lax.rsqrt(ms + NORM_EPS) * g


def _bdot(a, b):
    return jnp.dot(a, b, preferred_element_type=F32)


def _ffn_core(x, pre_ref, w1_ref, w3_ref, w2_ref, post_ref, o_ref):
    h = _rms(x, pre_ref[...]).astype(BF16)
    a = _bdot(h, w1_ref[...])
    b = _bdot(h, w3_ref[...])
    g = (a * jax.nn.sigmoid(a) * b).astype(BF16)
    y = _bdot(g, w2_ref[...])
    o_ref[...] = x + 0.5 * _rms(y, post_ref[...])


def _ffn_kernel(x_ref, pre_ref, w1_ref, w3_ref, w2_ref, post_ref, o_ref):
    _ffn_core(x_ref[...], pre_ref, w1_ref, w3_ref, w2_ref, post_ref, o_ref)


def _mix_ffn_kernel(x_ref, yh_ref, yr_ref, wo_ref, mg_ref,
                    pre_ref, w1_ref, w3_ref, w2_ref, post_ref, o_ref):
    mixed = jnp.concatenate([yh_ref[...].astype(BF16), yr_ref[...].astype(BF16)], axis=1)
    y = _bdot(mixed, wo_ref[...])
    x = x_ref[...] + _rms(y, mg_ref[...])
    _ffn_core(x, pre_ref, w1_ref, w3_ref, w2_ref, post_ref, o_ref)


def _row_spec(tm, width):
    return pl.BlockSpec((tm, width), lambda i: (i, 0))


def _ffn(x, pre_g, w1, w3, w2, post_g, mix=None):
    t = x.shape[0]
    tm = TOKEN_TILE
    ffn_specs = [_const_spec((1, D_MODEL)), _const_spec((D_MODEL, D_FF)),
                 _const_spec((D_MODEL, D_FF)), _const_spec((D_FF, D_MODEL)),
                 _const_spec((1, D_MODEL))]
    ffn_args = (pre_g, w1, w3, w2, post_g)
    if mix is None:
        body, args = _ffn_kernel, (x,) + ffn_args
        specs = [_row_spec(tm, D_MODEL)] + ffn_specs
    else:
        yh, yr, w_out, mg = mix
        body, args = _mix_ffn_kernel, (x, yh, yr, w_out, mg) + ffn_args
        specs = [_row_spec(tm, D_MODEL), _row_spec(tm, D_HYENA), _row_spec(tm, D_RET),
                 _const_spec((D_HYENA + D_RET, D_MODEL)), _const_spec((1, D_MODEL))] + ffn_specs
    return pl.pallas_call(
        body,
        grid=(t // tm,),
        in_specs=specs,
        out_specs=_row_spec(tm, D_MODEL),
        out_shape=jax.ShapeDtypeStruct((t, D_MODEL), F32),
        compiler_params=pltpu.CompilerParams(
            dimension_semantics=("parallel",), vmem_limit_bytes=VMEM_LIMIT),
        name="ffn_mix" if mix is not None else "ffn",
    )(*args)


def _inproj_kernel(x_ref, g_ref, w_ref, cc_ref, ss_ref, hy_ref, qkv_ref, gate_ref):
    h = _rms(x_ref[...], g_ref[...]).astype(BF16)
    qk0 = N_HY_COLS
    v0 = qk0 + 2 * D_RET
    g0 = v0 + D_RET
    qk = _bdot(h, w_ref[:, qk0:v0])
    hy_ref[...] = _bdot(h, w_ref[:, :qk0])
    gate_ref[...] = _bdot(h, w_ref[:, g0:])
    qkv_ref[:, 2 * D_RET:] = _bdot(h, w_ref[:, v0:g0]).astype(BF16)
    cc = cc_ref[...]
    ss = ss_ref[...]
    d = RET_HEAD_DIM
    for blk in range(2 * N_RET_HEADS):
        x = qk[:, blk * d:(blk + 1) * d]
        r = x * cc + pltpu.roll(x, d // 2, axis=1) * ss
        if blk >= N_RET_HEADS:
            r = r * (d ** -0.5)
        qkv_ref[:, blk * d:(blk + 1) * d] = r.astype(BF16)


def _inproj(x, g, w, seq, rope):
    t = x.shape[0]
    tm = INPROJ_TILE
    cc, ss = rope
    assert cc.shape[0] >= seq
    pos_blocks = seq // tm
    rope_spec = pl.BlockSpec((tm, RET_HEAD_DIM), lambda i: (i % pos_blocks, 0))
    return pl.pallas_call(
        _inproj_kernel,
        grid=(t // tm,),
        in_specs=[_row_spec(tm, D_MODEL), _const_spec((1, D_MODEL)),
                  _const_spec((D_MODEL, D_IN)), rope_spec, rope_spec],
        out_specs=[_row_spec(tm, N_HY_COLS), _row_spec(tm, 3 * D_RET), _row_spec(tm, D_RET)],
        out_shape=[jax.ShapeDtypeStruct((t, N_HY_COLS), F32),
                   jax.ShapeDtypeStruct((t, 3 * D_RET), BF16),
                   jax.ShapeDtypeStruct((t, D_RET), F32)],
        compiler_params=pltpu.CompilerParams(
            dimension_semantics=("parallel",), vmem_limit_bytes=VMEM_LIMIT),
        name="inproj",
    )(x, g, w, cc, ss)


def _ret_kernel(lgf_ref, lgb_ref, q_ref, k_ref, v_ref, g_ref, o_ref, kt_ref, *, seq):
    c = RET_CHUNK
    d = RET_HEAD_DIM
    n_chunks = seq // c
    head = pl.program_id(1)
    lgf = jnp.full((c, d), lgf_ref[head], F32)
    lgb = jnp.full((c, d), lgb_ref[head], F32)
    row = lax.broadcasted_iota(jnp.int32, (c, d), 0).astype(F32)
    col = lax.broadcasted_iota(jnp.int32, (c, d), 1).astype(F32)
    diff = row - col
    dmat = jnp.where(diff >= 0.0, jnp.exp(jnp.maximum(diff, 0.0) * lgf),
                     jnp.exp(jnp.maximum(-diff, 0.0) * lgb))
    wq_f = jnp.exp((row + 1.0) * lgf)
    wk_f = jnp.exp((c - 1.0 - row) * lgf)
    wq_b = jnp.exp((c - row) * lgb)
    wk_b = jnp.exp(row * lgb)
    gc_f = jnp.exp(c * lgf)
    gc_b = jnp.exp(c * lgb)

    grp = RET_GROUP
    n_groups = n_chunks // grp

    def fwd(gi, state):
        rows = [pl.ds(pl.multiple_of((gi * grp + j) * c, c), c) for j in range(grp)]
        qbs, kts, vs = [], [], []
        for r in rows:
            ktb = k_ref[r, :].astype(F32).T.astype(BF16)
            kt_ref[r, :] = ktb
            qbs.append(q_ref[r, :])
            kts.append(ktb)
            vs.append(v_ref[r, :])
        scores = [_bdot(qb, ktb) for qb, ktb in zip(qbs, kts)]
        kvs = [_bdot(ktb, (v.astype(F32) * wk_f).astype(BF16)) for ktb, v in zip(kts, vs)]
        states = []
        for kv in kvs:
            states.append(state)
            state = state * gc_f + kv
        intras = [_bdot((s * dmat).astype(BF16), v) for s, v in zip(scores, vs)]
        crosses = [_bdot(qb, st.astype(BF16)) for qb, st in zip(qbs, states)]
        for r, intra, cross in zip(rows, intras, crosses):
            o_ref[r, :] = intra + wq_f * cross
        return state

    lax.fori_loop(0, n_groups, fwd, jnp.zeros((d, d), F32))

    def bwd(gi, state):
        rows = [pl.ds(pl.multiple_of((n_chunks - 1 - gi * grp - j) * c, c), c) for j in range(grp)]
        kvs = [_bdot(kt_ref[r, :], (v_ref[r, :].astype(F32) * wk_b).astype(BF16)) for r in rows]
        states = []
        for kv in kvs:
            states.append(state)
            state = state * gc_b + kv
        crosses = [_bdot(q_ref[r, :], st.astype(BF16)) for r, st in zip(rows, states)]
        outs = [o_ref[r, :] + wq_b * cross for r, cross in zip(rows, crosses)]
        means = [jnp.mean(o * o, axis=-1, keepdims=True) for o in outs]
        norms = [lax.rsqrt(m + NORM_EPS) for m in means]
        gates = [g_ref[r, :] for r in rows]
        gates = [g * jax.nn.sigmoid(g) for g in gates]
        for r, o, nrm, g in zip(rows, outs, norms, gates):
            o_ref[r, :] = g * (o * nrm)
        return state

    lax.fori_loop(0, n_groups, bwd, jnp.zeros((d, d), F32))


def _rope_tables(seq):
    d = RET_HEAD_DIM
    inv = 1.0 / (ROPE_BASE ** (jnp.arange(0, d, 2, dtype=F32) / d))
    ang = jnp.arange(seq, dtype=F32)[:, None] * inv[None, :]
    c, s = jnp.cos(ang), jnp.sin(ang)
    return jnp.concatenate([c, c], axis=-1), jnp.concatenate([-s, s], axis=-1)


def _retention(qkv, gate, lg_f, lg_b, batch, seq):
    assert RET_CHUNK == RET_HEAD_DIM
    heads = N_RET_HEADS

    def col(off):
        return pl.BlockSpec((seq, LANES), lambda b, h, *_: (b, off * heads + h))

    grid_spec = pltpu.PrefetchScalarGridSpec(
        num_scalar_prefetch=2,
        grid=(batch, heads),
        in_specs=[col(0), col(1), col(2), col(0)],
        out_specs=pl.BlockSpec((seq, LANES), lambda b, h, *_: (b, h)),
        scratch_shapes=[pltpu.VMEM((seq, RET_CHUNK), BF16)],
    )
    return pl.pallas_call(
        functools.partial(_ret_kernel, seq=seq),
        grid_spec=grid_spec,
        out_shape=jax.ShapeDtypeStruct((batch * seq, D_RET), F32),
        compiler_params=pltpu.CompilerParams(
            dimension_semantics=("parallel", "parallel"), vmem_limit_bytes=VMEM_LIMIT),
        name="retention",
    )(lg_f, lg_b, qkv, qkv, qkv, gate)


def _split(x):
    hi = x.astype(BF16)
    return hi, (x - hi.astype(F32)).astype(BF16)


def _dot3(a, b):
    a_hi, a_lo = a
    b_hi, b_lo = b
    return _bdot(a_hi, b_hi) + _bdot(a_hi, b_lo) + _bdot(a_lo, b_hi)


@functools.lru_cache(maxsize=None)
def _fft_tables(seq):
    n = 2 * seq
    n2 = FFT_N2
    n1 = n // n2
    h1 = n1 // 2
    f1 = np.arange(h1, dtype=np.float64) + 0.5
    th = 2.0 * np.pi * f1[:, None] * np.arange(n1, dtype=np.float64)[None, :] / n1
    m1_full = np.concatenate([np.cos(th), -np.sin(th)], axis=0)
    m1 = m1_full[:, :h1]
    m1_inv = (2.0 / n) * m1.T
    t2 = np.arange(n2, dtype=np.float64)
    phi = 2.0 * np.pi * (np.arange(n2, dtype=np.float64)[None, :, None] * t2[None, None, :] / n2
                         + f1[:, None, None] * t2[None, None, :] / n)
    gr, gi = np.cos(phi), -np.sin(phi)
    m2 = np.concatenate([np.concatenate([gr, -gi], axis=2),
                         np.concatenate([gi, gr], axis=2)], axis=1)
    m2_inv = np.transpose(m2, (0, 2, 1))
    return dict(n1=n1, h1=h1, m1_full=m1_full, m1=m1, m1_inv=m1_inv, m2=m2, m2_inv=m2_inv)


OUTER_GROUP = 16
STAGE2_GROUP = 32
SPECTRUM_GROUP = 64
FILT_ROWS = 512
FILT_BLOCKS = 8
FILT_PACK = LANES // 2
PITCH_ALIGN = 4
A_PITCH = 2 * FFT_N2 + PITCH_ALIGN
T_PITCH = FFT_N2 + PITCH_ALIGN


def _filter_kernel(z_ref, w1_ref, b1_ref, w2_ref, b2_ref, w3_ref, b3_ref, fr_ref,
                   w4_ref, dl_ref, o_ref, sum_ref, *, nsteps):
    i = pl.program_id(0)
    half = FILT_ROWS // 2
    fr = fr_ref[...]
    zs = [z_ref[b * half:(b + 1) * half, :] for b in range(FILT_BLOCKS)]

    def layer(xs, w_ref, b_ref):
        w = _split(w_ref[...])
        pre = [_dot3(_split(x), w) for x in xs]
        return [jnp.sin(fr * (v + b_ref[...])) for v in pre]

    hs = layer(layer(layer(zs, w1_ref, b1_ref), w2_ref, b2_ref), w3_ref, b3_ref)
    hs = [_split(h) for h in hs]
    adl = jnp.abs(dl_ref[...])
    backward = i >= nsteps
    for side in range(2):
        lane = side * FILT_PACK
        ts = [jnp.broadcast_to(z[:, lane:lane + 1], z.shape) for z in zs]
        for c in range(HYENA_ORDER * D_HYENA // LANES):
            cols = slice(c * LANES, (c + 1) * LANES)
            w4 = _split(w4_ref[0, side, :, cols])
            hcs = [_dot3(h, w4) * jnp.exp(-t * adl[:, cols]) for h, t in zip(hs, ts)]
            part = sum(jnp.sum(jnp.abs(hc), axis=0, keepdims=True) for hc in hcs)

            if side == 0:
                @pl.when(i % nsteps == 0)
                def _():
                    sum_ref[0, :, cols] = part

                @pl.when(i % nsteps != 0)
                def _():
                    sum_ref[0, :, cols] = sum_ref[0, :, cols] + part
            else:
                sum_ref[0, :, cols] = sum_ref[0, :, cols] + part

            for b, hc in enumerate(hcs):
                if side == 0 and b == 0:
                    rows = lax.broadcasted_iota(jnp.int32, hc.shape, 0)
                    flipped = jnp.where(jnp.logical_and(rows == 0, i == nsteps), 0.0, -hc)
                else:
                    flipped = -hc
                r0 = b * FILT_ROWS + side * half
                o_ref[r0:r0 + half, cols] = jnp.where(backward, flipped, hc)


def _kspec_kernel(k_ref, s_ref, m1_ref, m2_ref, o_ref, a_ref, kp_ref, *, n1):
    n2 = FFT_N2
    h1 = n1 // 2
    inv_f = 1.0 / s_ref[0]
    inv_b = 1.0 / s_ref[1]

    def pad(t1, carry):
        src = pl.multiple_of(t1 * n2, n2)
        dst = pl.multiple_of(t1 * T_PITCH, PITCH_ALIGN)
        kp_ref[pl.ds(dst, n2), :] = k_ref[pl.ds(src, n2), :] * jnp.where(t1 < h1, inv_f, inv_b)
        return carry

    lax.fori_loop(0, n1, pad, 0, unroll=8)

    wide = MXU_COLUMNS // LANES

    def stage1(gi, carry):
        sets = [[(gi * OUTER_GROUP + u) * wide + j for j in range(wide)] for u in range(OUTER_GROUP)]
        kts = [jnp.concatenate([kp_ref[pl.ds(t2, n1, stride=T_PITCH), :] for t2 in t2s], axis=1)
               for t2s in sets]
        prods = [_bdot(m1_ref[...], kt.astype(BF16)) for kt in kts]
        for t2s, a in zip(sets, prods):
            for j, t2 in enumerate(t2s):
                cols = slice(j * LANES, (j + 1) * LANES)
                a_ref[pl.ds(t2, h1, stride=A_PITCH), :] = a[:h1, cols]
                a_ref[pl.ds(t2 + n2, h1, stride=A_PITCH), :] = a[h1:, cols]
        return carry

    lax.fori_loop(0, n2 // (wide * OUTER_GROUP), stage1, 0)

    def stage2(gi, carry):
        f1s = [gi * SPECTRUM_GROUP + j for j in range(SPECTRUM_GROUP)]
        srcs = [pl.ds(pl.multiple_of(f1 * A_PITCH, PITCH_ALIGN), 2 * n2) for f1 in f1s]
        specs = [_bdot(m2_ref[f1], a_ref[src, :].astype(BF16)) for f1, src in zip(f1s, srcs)]
        for f1, spec in zip(f1s, specs):
            dst = pl.multiple_of(f1 * 2 * n2, 2 * n2)
            o_ref[pl.ds(dst, 2 * n2), :] = spec.astype(o_ref.dtype)
        return carry

    lax.fori_loop(0, h1 // SPECTRUM_GROUP, stage2, 0)


def _filter_spectra(seq, p):
    n = 2 * seq
    tb = _fft_tables(seq)
    nblk = seq // FILT_ROWS
    half = FILT_ROWS // 2
    pack = FILT_PACK

    def packed(v):
        return jnp.transpose(v.reshape(2 * nblk, 2, half), (0, 2, 1)).reshape(n // 2, 2, 1)

    fwd_pos = jnp.arange(seq, dtype=jnp.int32)
    pos = jnp.concatenate([fwd_pos, (seq - fwd_pos) % seq])
    t_fwd = jnp.linspace(0.0, 1.0, seq, dtype=F32)
    t = packed(jnp.concatenate([t_fwd, jnp.roll(t_fwd[::-1], 1)]))
    w = packed(2.0 * math.pi * pos.astype(F32) / seq)
    fb = jnp.linspace(1e-4, FILT_BANDS - 1, FILT_BANDS, dtype=F32)[None, None, :]
    zero = jnp.zeros((n // 2, 2, pack - FILT_EMB), F32)
    zz = jnp.concatenate([t, jnp.cos(fb * w), -jnp.sin(fb * w), zero], axis=-1).reshape(n // 2, LANES)

    def padw(a):
        a = jnp.pad(a, ((0, pack - a.shape[0]), (0, pack - a.shape[1])))
        zero = jnp.zeros_like(a)
        return jnp.concatenate([jnp.concatenate([a, zero], axis=1),
                                jnp.concatenate([zero, a], axis=1)], axis=0)

    def padv(a):
        return jnp.tile(jnp.pad(a, (0, pack - a.shape[0])), 2)[None, :]

    w1 = padw(p['filt_w1'])
    w2 = padw(p['filt_w2'])
    w3 = padw(p['filt_w3'])
    w4 = jnp.pad(p['filt_w4'], ((0, pack - FILT_HIDDEN), (0, 0))).reshape(pack, HYENA_ORDER, 2, D_HYENA)
    w4 = jnp.transpose(w4, (2, 0, 1, 3)).reshape(2, pack, HYENA_ORDER * D_HYENA)
    w4 = jnp.stack([jnp.pad(w4, ((0, 0), (0, pack), (0, 0))),
                    jnp.pad(w4, ((0, 0), (pack, 0), (0, 0)))], axis=1)
    min_decay = math.log(HYENA_TARGET) / SLOW_DECAY_PCT
    max_decay = math.log(HYENA_TARGET) / FAST_DECAY_PCT
    deltas = jnp.linspace(min_decay, max_decay, D_HYENA, dtype=F32)
    deltas = jnp.tile(deltas, HYENA_ORDER)[None, :]
    width = HYENA_ORDER * D_HYENA
    sq = _const_spec((LANES, LANES))
    vec = _const_spec((1, LANES))
    nsteps = nblk // FILT_BLOCKS
    filt, sums = pl.pallas_call(
        functools.partial(_filter_kernel, nsteps=nsteps),
        grid=(2 * nsteps,),
        in_specs=[pl.BlockSpec((FILT_BLOCKS * half, LANES), lambda i: (i, 0)),
                  sq, vec, sq, vec, sq, vec, vec,
                  pl.BlockSpec((1, 2, LANES, width), lambda i: (i // nsteps, 0, 0, 0)),
                  _const_spec((1, width))],
        out_specs=[pl.BlockSpec((FILT_BLOCKS * FILT_ROWS, width), lambda i: (i, 0)),
                   pl.BlockSpec((1, 1, width), lambda i: (i // nsteps, 0, 0))],
        out_shape=[jax.ShapeDtypeStruct((n, width), F32),
                   jax.ShapeDtypeStruct((2, 1, width), F32)],
        compiler_params=pltpu.CompilerParams(
            dimension_semantics=("arbitrary",), vmem_limit_bytes=VMEM_LIMIT),
        name="hyena_filter",
    )(zz, w1, padv(p['filt_b1']), w2, padv(p['filt_b2']), w3, padv(p['filt_b3']),
      padv(p['filt_freq']), w4, deltas)
    n1, h1 = tb['n1'], tb['h1']
    return pl.pallas_call(
        functools.partial(_kspec_kernel, n1=n1),
        grid=(width // LANES,),
        in_specs=[pl.BlockSpec((n, LANES), lambda j: (0, j)),
                  pl.BlockSpec((2, 1, LANES), lambda j: (0, 0, j)),
                  _const_spec((n1, n1)), _const_spec((h1, 2 * FFT_N2, 2 * FFT_N2))],
        out_specs=pl.BlockSpec((n, LANES), lambda j: (0, j)),
        out_shape=jax.ShapeDtypeStruct((n, width), BF16),
        scratch_shapes=[pltpu.VMEM((h1 * A_PITCH, LANES), F32),
                        pltpu.VMEM((n1 * T_PITCH, LANES), F32)],
        compiler_params=pltpu.CompilerParams(
            dimension_semantics=("parallel",), vmem_limit_bytes=VMEM_LIMIT),
        name="hyena_filter_spectrum",
    )(filt, sums, _bf16_const(tb['m1_full']), _bf16_const(tb['m2']))


CHUNK_UNROLL = 8


def _for_chunks(count, body):
    body(0, True, False)

    def step(t1, carry):
        body(t1, False, False)
        return carry

    lax.fori_loop(1, count - CHUNK_UNROLL + 1, step, 0, unroll=CHUNK_UNROLL)
    for t1 in range(count - CHUNK_UNROLL + 1, count):
        body(t1, False, t1 == count - 1)


def _short_conv_rows(ref, w_ref, b_ref, r0, nrows, first, last):
    x = ref[pl.ds(r0, nrows), :]
    rid = lax.broadcasted_iota(jnp.int32, x.shape, 0)
    if first:
        prev = jnp.where(rid == 0, 0.0, pltpu.roll(x, 1, axis=0))
    else:
        prev = ref[pl.ds(r0 - 1, nrows), :]
    if last:
        nxt = jnp.where(rid == nrows - 1, 0.0, pltpu.roll(x, nrows - 1, axis=0))
    else:
        nxt = ref[pl.ds(r0 + 1, nrows), :]
    return prev * w_ref[0:1, :] + x * w_ref[1:2, :] + nxt * w_ref[2:3, :] + b_ref[...]


def _long_conv_kernel(*refs, n1):
    refs = list(refs)
    a_ref, tp_ref = refs[-2:]
    slabs = a_ref.shape[0]

    def take_operand():
        return [tuple(refs.pop(0) for _ in range(3)) for _ in range(slabs)]

    v_ops = take_operand()
    g_ops = take_operand()
    k_ref, bias_ref, m1_ref, m1i_ref, m2_ref, m2i_ref, o_ref = refs[:-2]
    n2 = FFT_N2
    h1 = n1 // 2
    order = pl.program_id(2)

    def put(ref, idx, x):
        for s in range(slabs):
            ref[s, idx, :] = x[:, s * LANES:(s + 1) * LANES]

    def get(ref, idx):
        return jnp.concatenate([ref[s, idx, :] for s in range(slabs)], axis=1)

    def conv_rows(ops, t1, first, last):
        r0 = t1 * n2 if isinstance(t1, int) else pl.multiple_of(t1 * n2, n2)
        parts = [_short_conv_rows(ref, w_ref, b_ref, r0, n2, first, last) for ref, w_ref, b_ref in ops]
        return jnp.concatenate(parts, axis=1)

    def pitched(t1):
        r0 = t1 * T_PITCH
        return pl.ds(r0 if isinstance(t1, int) else pl.multiple_of(r0, PITCH_ALIGN), n2)

    def natural(t1):
        r0 = t1 * n2 if isinstance(t1, int) else pl.multiple_of(t1 * n2, n2)
        return pl.ds(r0, n2)

    def pad_first(t1, first, last):
        z = conv_rows(v_ops, t1, first, last)
        put(tp_ref, pitched(t1), z)
        o_ref[natural(t1), :] = z

    def pad_next(t1, first, last):
        put(tp_ref, pitched(t1), o_ref[natural(t1), :])

    @pl.when(order == 0)
    def _():
        _for_chunks(h1, pad_first)

    @pl.when(order > 0)
    def _():
        _for_chunks(h1, pad_next)

    wide = max(1, MXU_COLUMNS // (slabs * LANES))
    width = slabs * LANES

    def t2_sets(gi):
        return [[(gi * OUTER_GROUP + u) * wide + j for j in range(wide)] for u in range(OUTER_GROUP)]

    def stage1(gi, carry):
        sets = t2_sets(gi)
        zts = [jnp.concatenate([get(tp_ref, pl.ds(t2, h1, stride=T_PITCH)) for t2 in t2s], axis=1)
               for t2s in sets]
        prods = [_bdot(m1_ref[...], zt.astype(BF16)) for zt in zts]
        for t2s, a in zip(sets, prods):
            for j, t2 in enumerate(t2s):
                cols = slice(j * width, (j + 1) * width)
                put(a_ref, pl.ds(t2, h1, stride=A_PITCH), a[:h1, cols])
                put(a_ref, pl.ds(t2 + n2, h1, stride=A_PITCH), a[h1:, cols])
        return carry

    lax.fori_loop(0, n2 // (wide * OUTER_GROUP), stage1, 0)

    group = STAGE2_GROUP // slabs

    def stage2(gi, carry):
        f1s = [gi * group + j for j in range(group)]
        ras = [pl.ds(pl.multiple_of(f1 * A_PITCH, PITCH_ALIGN), 2 * n2) for f1 in f1s]
        xs = [_bdot(m2_ref[f1], get(a_ref, ra).astype(BF16)) for f1, ra in zip(f1s, ras)]
        ys = []
        for f1, x in zip(f1s, xs):
            kk = k_ref[pl.ds(pl.multiple_of(f1 * 2 * n2, 2 * n2), 2 * n2), :].astype(F32)
            xr, xi, kr, ki = x[:n2], x[n2:], kk[:n2], kk[n2:]
            ys.append(jnp.concatenate([xr * kr - xi * ki, xr * ki + xi * kr], axis=0).astype(BF16))
        outs = [_bdot(m2i_ref[f1], y) for f1, y in zip(f1s, ys)]
        for ra, out in zip(ras, outs):
            put(a_ref, ra, out)
        return carry

    lax.fori_loop(0, h1 // group, stage2, 0)

    def stage3(gi, carry):
        sets = t2_sets(gi)
        specs = []
        for t2s in sets:
            br = jnp.concatenate([get(a_ref, pl.ds(t2, h1, stride=A_PITCH)) for t2 in t2s], axis=1)
            bi = jnp.concatenate([get(a_ref, pl.ds(t2 + n2, h1, stride=A_PITCH)) for t2 in t2s], axis=1)
            specs.append(jnp.concatenate([br, bi], axis=0).astype(BF16))
        prods = [_bdot(m1i_ref[...], sp) for sp in specs]
        for t2s, y in zip(sets, prods):
            for j, t2 in enumerate(t2s):
                put(tp_ref, pl.ds(t2, h1, stride=T_PITCH), y[:, j * width:(j + 1) * width])
        return carry

    lax.fori_loop(0, n2 // (wide * OUTER_GROUP), stage3, 0)

    bias = bias_ref[0]

    def finish(t1, first, last):
        y = get(tp_ref, pitched(t1))
        z = o_ref[natural(t1), :]
        o_ref[natural(t1), :] = conv_rows(g_ops, t1, first, last) * (y + bias * z)

    _for_chunks(h1, finish)


def _bf16_const(a):
    return jnp.asarray(a, F32).astype(BF16)


def _long_conv_slabs(seq):
    n1 = 2 * seq // FFT_N2
    h1 = n1 // 2
    tables = 2 * (2 * n1 * h1 + 2 * h1 * (2 * FFT_N2) ** 2)
    for slabs in (2, 1):
        io = 3 * 2 * seq * 4
        per_lane = io + 2 * 2 * seq * 2 + h1 * (A_PITCH + T_PITCH) * 4
        if per_lane * slabs * LANES + tables <= VMEM_LIMIT - VMEM_HEADROOM:
            return slabs
    raise ValueError(f"long conv of length {seq} does not fit VMEM")


def _long_conv(hy, kspec, bias, short, batch, seq):
    n = 2 * seq
    tb = _fft_tables(seq)
    n1, h1 = tb['n1'], tb['h1']
    slabs = _long_conv_slabs(seq)
    width = slabs * LANES
    tiles = D_HYENA // LANES
    blocks = tiles // slabs

    def operand(col_block):
        specs, args = [], []
        for s in range(slabs):
            col = functools.partial(col_block, s=s)
            specs += [pl.BlockSpec((seq, LANES), lambda j, b, o, col=col: (b, col(j, o))),
                      pl.BlockSpec((3, LANES), lambda j, b, o, col=col: (0, col(j, o))),
                      pl.BlockSpec((1, LANES), lambda j, b, o, col=col: (0, col(j, o)))]
            args += [hy, short[0], short[1]]
        return specs, args

    v_specs, v_args = operand(lambda j, o, s: j * slabs + s)
    g_specs, g_args = operand(lambda j, o, s: (1 + o) * tiles + j * slabs + s)
    return pl.pallas_call(
        functools.partial(_long_conv_kernel, n1=n1),
        grid=(blocks, batch, HYENA_ORDER),
        in_specs=v_specs + g_specs + [
                  pl.BlockSpec((n, width), lambda j, b, o: (0, o * blocks + j)),
                  pl.BlockSpec((1, 1, width), lambda j, b, o: (o, 0, j)),
                  _const_spec((n1, h1)), _const_spec((h1, n1)),
                  _const_spec((h1, 2 * FFT_N2, 2 * FFT_N2)),
                  _const_spec((h1, 2 * FFT_N2, 2 * FFT_N2))],
        out_specs=pl.BlockSpec((seq, width), lambda j, b, o: (b, j)),
        out_shape=jax.ShapeDtypeStruct((batch * seq, D_HYENA), F32),
        scratch_shapes=[pltpu.VMEM((slabs, h1 * A_PITCH, LANES), F32),
                        pltpu.VMEM((slabs, h1 * T_PITCH, LANES), F32)],
        compiler_params=pltpu.CompilerParams(
            dimension_semantics=("parallel", "parallel", "arbitrary"), vmem_limit_bytes=VMEM_LIMIT),
        name="hyena_long_conv",
    )(*v_args, *g_args, kspec, bias, _bf16_const(tb['m1']), _bf16_const(tb['m1_inv']),
      _bf16_const(tb['m2']), _bf16_const(tb['m2_inv']))


def _layer(x3, p, rope):
    batch, seq, _ = x3.shape
    x = x3.reshape(batch * seq, D_MODEL)
    x = _ffn(x, p['ffn1_pre_g'], p['ffn1_w1'], p['ffn1_w3'], p['ffn1_w2'], p['ffn1_post_g'])
    hy, qkv, gate = _inproj(x, p['mix_pre_g'], p['w_in'], seq, rope)
    kspec = _filter_spectra(seq, p)
    short = (p['short_w'], p['short_b'])
    yh = _long_conv(hy, kspec, p['hyena_bias'][:, None, :], short, batch, seq)
    yr = _retention(qkv, gate, p['ret_log_decay_f'], p['ret_log_decay_b'], batch, seq)
    mix = (yh, yr, p['w_out'], p['mix_post_g'])
    x = _ffn(x, p['ffn2_pre_g'], p['ffn2_w1'], p['ffn2_w3'], p['ffn2_w2'], p['ffn2_post_g'], mix=mix)
    return x.reshape(batch, seq, D_MODEL)


_MATRIX_PARAMS = ('ffn1_w1', 'ffn1_w3', 'ffn1_w2', 'w_in', 'w_out', 'ffn2_w1', 'ffn2_w3', 'ffn2_w2')
_GAIN_PARAMS = ('ffn1_pre_g', 'ffn1_post_g', 'mix_pre_g', 'mix_post_g', 'ffn2_pre_g', 'ffn2_post_g')


def kernel(x_prompt, x_sample, ffn1_pre_g, ffn1_w1, ffn1_w3, ffn1_w2, ffn1_post_g, mix_pre_g, w_in, short_w, short_b, filt_w1, filt_b1, filt_w2, filt_b2, filt_w3, filt_b3, filt_w4, filt_freq, hyena_bias, ret_log_decay_f, ret_log_decay_b, w_out, mix_post_g, ffn2_pre_g, ffn2_w1, ffn2_w3, ffn2_w2, ffn2_post_g):
    params = dict(ffn1_pre_g=ffn1_pre_g, ffn1_w1=ffn1_w1, ffn1_w3=ffn1_w3, ffn1_w2=ffn1_w2,
                  ffn1_post_g=ffn1_post_g, mix_pre_g=mix_pre_g, w_in=w_in, short_w=short_w,
                  short_b=short_b, filt_w1=filt_w1, filt_b1=filt_b1, filt_w2=filt_w2,
                  filt_b2=filt_b2, filt_w3=filt_w3, filt_b3=filt_b3, filt_w4=filt_w4,
                  filt_freq=filt_freq, hyena_bias=hyena_bias, ret_log_decay_f=ret_log_decay_f,
                  ret_log_decay_b=ret_log_decay_b, w_out=w_out, mix_post_g=mix_post_g,
                  ffn2_pre_g=ffn2_pre_g, ffn2_w1=ffn2_w1, ffn2_w3=ffn2_w3, ffn2_w2=ffn2_w2,
                  ffn2_post_g=ffn2_post_g)
    depth = ffn1_w1.shape[0]
    rope = _rope_tables(max(x_prompt.shape[1], x_sample.shape[1]))

    def run(x):
        for l in range(depth):
            p = {k: v[l] for k, v in params.items()}
            for k in _MATRIX_PARAMS:
                p[k] = p[k].astype(BF16)
            for k in _GAIN_PARAMS:
                p[k] = p[k][None, :]
            p['short_b'] = p['short_b'][None, :]
            x = _layer(x, p, rope)
        return x

    return (run(x_prompt), run(x_sample))
```
